```python
import jax
import jax.numpy as jnp
from jax import lax
import numpy as np

D_MODEL = 1024
BATCH = 8
SEQ = 2048
DEPTH = 1

CHUNK = 64
Q_BLOCK = 128
EPS = 1e-6
H_M = 4
W_M = D_MODEL // 2
DH_M = W_M // H_M
CONV_K = 4
H_F = 8
W_F = D_MODEL // 2
DH_F = W_F // H_F
D_FF = (8 * D_MODEL + 3 * 256 - 1) // (3 * 256) * 256
IN_SIZES = (W_M, W_M, W_M, W_M, H_M, H_M, W_F, W_F, W_F, H_F, D_MODEL, D_MODEL)
N_IN = sum(IN_SIZES)
MF_OFF = 4 * W_M + H_M
FF_OFF = 4 * W_M + 2 * H_M + 3 * W_F

kernel_name = "hybrid_mlstm_fox_block"


def rmsnorm(x, g):
    xf = x.astype(jnp.float32)
    y = xf * lax.rsqrt(jnp.mean(xf * xf, axis=-1, keepdims=True) + EPS)
    return (y * g.astype(jnp.float32)).astype(x.dtype)


def split_heads(t, n_heads):
    b, s, _ = t.shape
    return t.reshape(b, s, n_heads, -1).transpose(0, 2, 1, 3)


def merge_heads(t):
    b, h, s, d = t.shape
    return t.transpose(0, 2, 1, 3).reshape(b, s, h * d)


def causal_depthwise_conv(u, w, b):
    y = lax.conv_general_dilated(
        u, w[:, None, :].astype(u.dtype), window_strides=(1,),
        padding=[(CONV_K - 1, 0)], dimension_numbers=("NWC", "WIO", "NWC"),
        feature_group_count=u.shape[-1])
    return y + b.astype(u.dtype)


def mlstm_chunkwise(q, k, v, log_i, log_f):
    bsz, nh, s, d = q.shape
    nc = s // CHUNK

    def to_chunks(t):
        t = t.reshape(bsz, nh, nc, CHUNK, *t.shape[3:])
        return jnp.moveaxis(t, 2, 0)

    causal = jnp.tril(jnp.ones((CHUNK, CHUNK), dtype=bool))

    def step(carry, inp):
        c_mat, n_vec, m_prev = carry
        qc, kc, vc, ic, fc = inp
        b = jnp.cumsum(fc, axis=-1)
        log_d = jnp.where(causal, b[..., :, None] - b[..., None, :] + ic[..., None, :], -jnp.inf)
        m_inter = b + m_prev[..., None]
        m_t = jnp.maximum(m_inter, jnp.max(log_d, axis=-1))
        a = jnp.exp(m_inter - m_t)
        s_qk = jnp.einsum("bhtd,bhsd->bhts", qc, kc) * jnp.exp(log_d - m_t[..., None])
        num = a[..., None] * jnp.einsum("bhtk,bhkv->bhtv", qc, c_mat) + jnp.einsum("bhts,bhsv->bhtv", s_qk, vc)
        den = a * jnp.einsum("bhtk,bhk->bht", qc, n_vec) + jnp.sum(s_qk, axis=-1)
        h = num / jnp.maximum(jnp.abs(den), jnp.exp(-m_t))[..., None]
        b_last = b[..., -1]
        log_w = b_last[..., None] - b + ic
        m_new = jnp.maximum(b_last + m_prev, jnp.max(log_w, axis=-1))
        decay = jnp.exp(b_last + m_prev - m_new)
        w = jnp.exp(log_w - m_new[..., None])
        c_new = decay[..., None, None] * c_mat + jnp.einsum("bhs,bhsk,bhsv->bhkv", w, kc, vc)
        n_new = decay[..., None] * n_vec + jnp.einsum("bhs,bhsk->bhk", w, kc)
        return (c_new, n_new, m_new), h

    init = (jnp.zeros((bsz, nh, d, d), jnp.float32),
            jnp.zeros((bsz, nh, d), jnp.float32),
            jnp.zeros((bsz, nh), jnp.float32))
    _, hs = lax.scan(step, init, (to_chunks(q), to_chunks(k), to_chunks(v),
                                  to_chunks(log_i), to_chunks(log_f)))
    return jnp.moveaxis(hs, 0, 2).reshape(bsz, nh, s, d)


def forgetting_attention(q, k, v, log_f):
    s = q.shape[2]
    scale = q.shape[-1] ** -0.5
    cum = jnp.cumsum(log_f, axis=-1)
    outs = []
    for blk in range(s // Q_BLOCK):
        lo = blk * Q_BLOCK
        hi = lo + Q_BLOCK
        logits = (jnp.einsum("bhtd,bhsd->bhts", q[:, :, lo:hi], k[:, :, :hi]) * scale
                  + (cum[:, :, lo:hi, None] - cum[:, :, None, :hi]))
        mask = (lo + jnp.arange(Q_BLOCK))[:, None] >= jnp.arange(hi)[None, :]
        p = jax.nn.softmax(jnp.where(mask, logits, -jnp.inf), axis=-1)
        outs.append(jnp.einsum("bhts,bhsd->bhtd", p, v[:, :, :hi]))
    return jnp.concatenate(outs, axis=2)


def swiglu(h, w_gate, w_up, w_down):
    return (jax.nn.silu(h @ w_gate) * (h @ w_up)) @ w_down


def setup_inputs(seed: int = 0) -> dict:
    key = jax.random.key(seed)
    ks = jax.random.split(key, 16)

    def dense(k, shape, fan_in):
        return jax.random.normal(k, shape, jnp.float32) * fan_in ** -0.5

    def gain(k, shape):
        return 1.0 + 0.02 * jax.random.normal(k, shape, jnp.float32)

    x = jax.random.normal(ks[0], (BATCH, SEQ, D_MODEL), jnp.float32)
    norm1_g = gain(ks[1], (DEPTH, D_MODEL))
    w_in = dense(ks[2], (DEPTH, D_MODEL, N_IN), D_MODEL)
    b_in = 0.02 * jax.random.normal(ks[3], (DEPTH, N_IN), jnp.float32)
    b_in = b_in.at[:, MF_OFF:MF_OFF + H_M].add(jnp.linspace(3.0, 6.0, H_M))
    b_in = b_in.at[:, FF_OFF:FF_OFF + H_F].add(jnp.linspace(2.0, 6.0, H_F))
    conv_w = dense(ks[4], (DEPTH, CONV_K, 2 * W_M), CONV_K)
    conv_b = 0.02 * jax.random.normal(ks[5], (DEPTH, 2 * W_M), jnp.float32)
    mlstm_norm_g = gain(ks[6], (DEPTH, W_M))
    w_br_mlstm = dense(ks[7], (DEPTH, W_M, D_MODEL), W_M)
    w_br_fox = dense(ks[8], (DEPTH, W_F, D_MODEL), W_F)
    w_out = dense(ks[9], (DEPTH, D_MODEL, D_MODEL), D_MODEL)
    norm2_g = gain(ks[10], (DEPTH, D_MODEL))
    w_gate = dense(ks[11], (DEPTH, D_MODEL, D_FF), D_MODEL)
    w_up = dense(ks[12], (DEPTH, D_MODEL, D_FF), D_MODEL)
    w_down = dense(ks[13], (DEPTH, D_FF, D_MODEL), D_FF)
    norm_f_g = gain(ks[14], (D_MODEL,))
    return {"x": x, "norm1_g": norm1_g, "w_in": w_in, "b_in": b_in,
            "conv_w": conv_w, "conv_b": conv_b, "mlstm_norm_g": mlstm_norm_g,
            "w_br_mlstm": w_br_mlstm, "w_br_fox": w_br_fox, "w_out": w_out,
            "norm2_g": norm2_g, "w_gate": w_gate, "w_up": w_up, "w_down": w_down,
            "norm_f_g": norm_f_g}


def reference(x, norm1_g, w_in, b_in, conv_w, conv_b, mlstm_norm_g, w_br_mlstm,
              w_br_fox, w_out, norm2_g, w_gate, w_up, w_down, norm_f_g):
    f32 = jnp.float32
    split_points = np.cumsum(IN_SIZES)[:-1].tolist()
    for l in range(DEPTH):
        h = rmsnorm(x, norm1_g[l])
        z = h @ w_in[l] + b_in[l]
        mq, mk, mv, mo, mi, mf, fq, fk, fv, ff, gm, gf = jnp.split(z, split_points, axis=-1)

        qk = jax.nn.silu(causal_depthwise_conv(jnp.concatenate([mq, mk], axis=-1), conv_w[l], conv_b[l]))
        mq, mk = qk[..., :W_M], qk[..., W_M:]
        hm = mlstm_chunkwise(
            split_heads(mq, H_M).astype(f32),
            split_heads(mk, H_M).astype(f32) * DH_M ** -0.5,
            split_heads(mv, H_M).astype(f32),
            mi.astype(f32).transpose(0, 2, 1),
            jax.nn.log_sigmoid(mf.astype(f32)).transpose(0, 2, 1))
        hm = hm * lax.rsqrt(jnp.mean(hm * hm, axis=-1, keepdims=True) + EPS)
        y_m = (merge_heads(hm) * mlstm_norm_g[l].astype(f32)
               * jax.nn.sigmoid(mo.astype(f32))).astype(x.dtype)

        hf = forgetting_attention(
            split_heads(fq, H_F).astype(f32),
            split_heads(fk, H_F).astype(f32),
            split_heads(fv, H_F).astype(f32),
            jax.nn.log_sigmoid(ff.astype(f32)).transpose(0, 2, 1))
        y_f = merge_heads(hf).astype(x.dtype)

        mix = jax.nn.sigmoid(gm) * (y_m @ w_br_mlstm[l]) + jax.nn.sigmoid(gf) * (y_f @ w_br_fox[l])
        x = x + mix @ w_out[l]

        x = x + swiglu(rmsnorm(x, norm2_g[l]), w_gate[l], w_up[l], w_down[l])
    return rmsnorm(x, norm_f_g)
```

```python
import functools

import jax
import jax.numpy as jnp
from jax import lax
from jax.experimental import pallas as pl
from jax.experimental.pallas import tpu as pltpu

EPS = 1e-6
H_M = 4
H_F = 8
CONV_K = 4

LANES = 128
GATE_GROUP = 8
N_GATE_ROWS = 3 * GATE_GROUP
V7X_VMEM_BYTES = 64 * 1024 * 1024
NEG_BIG = -1e30

F32 = jnp.float32
BF16 = jnp.bfloat16


def _cfg(batch, seq, d_model, d_ff):
    return dict(
        tm_in=512,
        n_chunk=512,
        chunk=256,
        tq=256,
        tk=256,
        tm_post=512,
        tf=256,
        vmem_limit=V7X_VMEM_BYTES - 8 * 1024 * 1024,
    )


def _const_spec(shape):
    nd = len(shape)
    return pl.BlockSpec(shape, lambda *_: (0,) * nd, pipeline_mode=pl.Buffered(1))


def _rms(x, g):
    return x * lax.rsqrt(jnp.mean(x * x, axis=-1, keepdims=True) + EPS) * g


def _log_sigmoid(x):
    return jnp.minimum(x, 0.0) - jnp.log1p(jnp.exp(-jnp.abs(x)))


def _inproj_kernel(x_ref, g_ref, wa_ref, ba_ref, wb_ref, bb_ref, wc_ref, bc_ref,
                   wd_ref, bd_ref, za_ref, zb_ref, zc_ref, zdt_ref, *, w_m, w_f, n_chunk, q_scale):
    hb = _rms(x_ref[...], g_ref[...]).astype(BF16)

    def proj(w_ref, b_ref, c0, c1):
        return jnp.dot(hb, w_ref[:, c0:c1], preferred_element_type=F32) + b_ref[:, c0:c1]

    for c0 in range(0, 4 * w_m, n_chunk):
        z = proj(wa_ref, ba_ref, c0, c0 + n_chunk)
        if c0 >= 3 * w_m:
            z = jax.nn.sigmoid(z)
        za_ref[:, c0:c0 + n_chunk] = z.astype(BF16)
    for c0 in range(0, 3 * w_f, n_chunk):
        z = proj(wb_ref, bb_ref, c0, c0 + n_chunk)
        if c0 < w_f:
            z = z * q_scale
        zb_ref[:, c0:c0 + n_chunk] = z.astype(BF16)
    for c0 in range(0, wc_ref.shape[1], n_chunk):
        z = proj(wc_ref, bc_ref, c0, c0 + n_chunk)
        zc_ref[:, c0:c0 + n_chunk] = jax.nn.sigmoid(z).astype(BF16)
    zd = proj(wd_ref, bd_ref, 0, LANES)
    zdt_ref[...] = zd.T[:N_GATE_ROWS, :]


def _scan_lanes(x, op, fill):
    n = x.shape[-1]
    lane = lax.broadcasted_iota(jnp.int32, x.shape, 1)
    s = 1
    while s < n:
        shifted = pltpu.roll(x, s, axis=1)
        x = op(x, jnp.where(lane >= s, shifted, fill))
        s *= 2
    return x


def _gates_kernel(zdt_ref, rows_ref, cols_ref):
    z = zdt_ref[...]
    seq = z.shape[1]
    i8 = z[0:GATE_GROUP]
    f8 = _scan_lanes(_log_sigmoid(z[GATE_GROUP:2 * GATE_GROUP]), jnp.add, 0.0)
    cf8 = _scan_lanes(_log_sigmoid(z[2 * GATE_GROUP:3 * GATE_GROUP]), jnp.add, 0.0)
    g8 = i8 - f8
    m8 = jnp.maximum(_scan_lanes(g8, jnp.maximum, NEG_BIG), 0.0)
    en8 = jnp.exp(-(f8 + m8))
    rows_ref[0] = jnp.concatenate([g8, cf8], axis=0)
    stack = jnp.concatenate(
        [m8, en8, cf8, jnp.zeros((LANES - 3 * GATE_GROUP, seq), F32)], axis=0)
    cols_ref[...] = stack.T


def _mlstm_kernel(za_ref, rows_ref, cols_ref, cw_ref, cb_ref, ng_ref, ym_ref,
                  q_sc, kt_sc, va_sc, c_sc, *, w_m, chunk):
    seq = za_ref.shape[0]
    dh = w_m // H_M
    nc = seq // chunk
    k_scale = dh ** -0.5
    row = lax.broadcasted_iota(jnp.int32, (seq, dh), 0)

    def conv_silu(u, w, b):
        y = u * w[CONV_K - 1:CONV_K] + b
        for s in range(1, CONV_K):
            us = jnp.where(row >= s, pltpu.roll(u, s, axis=0), 0.0)
            y = y + us * w[CONV_K - 1 - s:CONV_K - s]
        return y * jax.nn.sigmoid(y)

    one_col = jnp.where(lax.broadcasted_iota(jnp.int32, (seq, dh), 1) == 0, 1.0, 0.0).astype(BF16)
    for h in range(H_M):
        cq = slice(h * dh, (h + 1) * dh)
        ck = slice(w_m + h * dh, w_m + (h + 1) * dh)
        cv = slice(2 * w_m + h * dh, 2 * w_m + (h + 1) * dh)
        q_sc[:, cq] = conv_silu(za_ref[:, cq].astype(F32), cw_ref[:, cq], cb_ref[:, cq]).astype(BF16)
        kk = conv_silu(za_ref[:, ck].astype(F32), cw_ref[:, ck], cb_ref[:, ck]) * k_scale
        for c in range(nc):
            kt_sc[h, c] = kk[c * chunk:(c + 1) * chunk, :].T.astype(BF16)
        va_sc[h, :, 0:dh] = za_ref[:, cv]
        va_sc[h, :, dh:2 * dh] = one_col
    c_sc[...] = jnp.zeros(c_sc.shape, F32)

    tri = (lax.broadcasted_iota(jnp.int32, (chunk, chunk), 0)
           >= lax.broadcasted_iota(jnp.int32, (chunk, chunk), 1))

    def body(c, carry):
        r0 = pl.multiple_of(c * chunk, chunk)
        colsc = cols_ref[pl.ds(r0, chunk), :]
        last = cols_ref[pl.ds(r0 + chunk - 1, 1), :]
        prev = cols_ref[pl.ds(jnp.maximum(r0 - 1, 0), 1), :]
        prev = jnp.where(c > 0, prev, 0.0)
        for h in range(H_M):
            ch = slice(h * dh, (h + 1) * dh)
            g_row = rows_ref[0, h, pl.ds(c, 1), :]
            m_col = colsc[:, h:h + 1]
            en_col = colsc[:, GATE_GROUP + h:GATE_GROUP + h + 1]
            m_e = last[:, h:h + 1]
            m_p = prev[:, h:h + 1]
            qc = q_sc[pl.ds(r0, chunk), ch]
            ktc = kt_sc[h, c]
            vc = va_sc[h, pl.ds(r0, chunk), :]
            dmat = jnp.where(tri, jnp.exp(g_row - m_col), 0.0)
            sqk = (jnp.dot(qc, ktc, preferred_element_type=F32) * dmat).astype(BF16)
            c_old = c_sc[h]
            a = jnp.exp(m_p - m_col)
            nd = (a * jnp.dot(qc, c_old.astype(BF16), preferred_element_type=F32)
                  + jnp.dot(sqk, vc, preferred_element_type=F32))
            num = nd[:, 0:dh]
            den = nd[:, dh:dh + 1]
            hh = num * (1.0 / jnp.maximum(jnp.abs(den), en_col))
            hn = hh * lax.rsqrt(jnp.mean(hh * hh, axis=-1, keepdims=True) + EPS)
            og = za_ref[pl.ds(r0, chunk), 3 * w_m + h * dh:3 * w_m + (h + 1) * dh].astype(F32)
            ym_ref[pl.ds(r0, chunk), ch] = (hn * ng_ref[:, ch] * og).astype(BF16)
            w_row = jnp.exp(g_row - m_e)
            decay = jnp.exp(m_p - m_e)
            ktw = (ktc.astype(F32) * w_row).astype(BF16)
            c_sc[h] = decay * c_old + jnp.dot(ktw, vc, preferred_element_type=F32)
        return carry

    lax.fori_loop(0, nc, body, 0)


def _fox_kernel(q_ref, k_ref, v_ref, rows_ref, cols_ref, yf_ref, *, tq, tk):
    qi = pl.program_id(1)
    dhp = LANES
    half = dhp // 2
    lane = lax.broadcasted_iota(jnp.int32, (tq, dhp), 1)
    r_io = lax.broadcasted_iota(jnp.int32, (tq, tk), 0)
    c_io = lax.broadcasted_iota(jnp.int32, (tq, tk), 1)
    colsq = cols_ref[...]
    nt = (((1,), (1,)), ((), ()))

    for p in range(H_F // 2):
        cp = slice(p * dhp, (p + 1) * dhp)
        qp = q_ref[:, cp]
        res = []
        for hh in range(2):
            head = 2 * p + hh
            sel = (lane >= half) if hh else (lane < half)
            qm = jnp.where(sel, qp, jnp.zeros_like(qp))
            ct = colsq[:, 2 * GATE_GROUP + head:2 * GATE_GROUP + head + 1]

            def step(j, carry, masked, cp=cp, qm=qm, ct=ct, head=head):
                m, l, acc = carry
                k0 = pl.multiple_of(j * tk, tk)
                kj = k_ref[pl.ds(k0, tk), cp]
                vj = v_ref[pl.ds(k0, tk), cp]
                cs = rows_ref[0, GATE_GROUP + head, pl.ds(j, 1), :]
                s = lax.dot_general(qm, kj, nt, preferred_element_type=F32) + (ct - cs)
                if masked:
                    s = jnp.where(c_io <= r_io, s, NEG_BIG)
                m_new = jnp.maximum(m, jnp.max(s, axis=1, keepdims=True))
                alpha = jnp.exp(m - m_new)
                pr = jnp.exp(s - m_new)
                l = alpha * l + jnp.sum(pr, axis=1, keepdims=True)
                acc = alpha * acc + jnp.dot(pr.astype(BF16), vj, preferred_element_type=F32)
                return m_new, l, acc

            init = (jnp.full((tq, 1), NEG_BIG, F32), jnp.zeros((tq, 1), F32),
                    jnp.zeros((tq, dhp), F32))
            carry = lax.fori_loop(0, qi, functools.partial(step, masked=False), init)
            m, l, acc = step(qi, carry, True)
            res.append(acc * (1.0 / l))
        yf_ref[:, cp] = jnp.where(lane < half, res[0], res[1]).astype(BF16)


def _post_kernel(x_ref, ym_ref, yf_ref, zc_ref, wbm_ref, wbf_ref, wo_ref, g2_ref,
                 wg_ref, wu_ref, wd_ref, gfin_ref, o_ref, *, tf):
    d = x_ref.shape[1]
    d_ff = wg_ref.shape[1]
    bm = jnp.dot(ym_ref[...], wbm_ref[...], preferred_element_type=F32)
    bf = jnp.dot(yf_ref[...], wbf_ref[...], preferred_element_type=F32)
    mix = zc_ref[:, 0:d].astype(F32) * bm + zc_ref[:, d:2 * d].astype(F32) * bf
    x1 = x_ref[...] + jnp.dot(mix.astype(BF16), wo_ref[...], preferred_element_type=F32)
    h2 = _rms(x1, g2_ref[...]).astype(BF16)
    acc = jnp.zeros(x1.shape, F32)
    for f0 in range(0, d_ff, tf):
        g = jnp.dot(h2, wg_ref[:, f0:f0 + tf], preferred_element_type=F32)
        u = jnp.dot(h2, wu_ref[:, f0:f0 + tf], preferred_element_type=F32)
        act = (g * jax.nn.sigmoid(g) * u).astype(BF16)
        acc = acc + jnp.dot(act, wd_ref[f0:f0 + tf, :], preferred_element_type=F32)
    o_ref[...] = _rms(x1 + acc, gfin_ref[...])


def _layer(x2d, batch, seq, p, cfg):
    t, d = x2d.shape
    w_m = p["w_br_mlstm"].shape[0]
    w_f = p["w_br_fox"].shape[0]
    d_ff = p["w_gate"].shape[1]
    dh_f = w_f // H_F
    vmem = cfg["vmem_limit"]

    o_mi = 4 * w_m
    o_mf = o_mi + H_M
    o_fq = o_mf + H_M
    o_ff = o_fq + 3 * w_f
    o_g = o_ff + H_F
    w_in, b_in = p["w_in"], p["b_in"]
    wa, ba = w_in[:, :o_mi].astype(BF16), b_in[None, :o_mi]
    wb, bb = w_in[:, o_fq:o_ff].astype(BF16), b_in[None, o_fq:o_ff]
    wc, bc = w_in[:, o_g:].astype(BF16), b_in[None, o_g:]
    wd = jnp.zeros((d, LANES), F32)
    bd = jnp.zeros((1, LANES), F32)
    for dst, src, n in ((0, o_mi, H_M), (GATE_GROUP, o_mf, H_M), (2 * GATE_GROUP, o_ff, H_F)):
        wd = wd.at[:, dst:dst + n].set(w_in[:, src:src + n])
        bd = bd.at[:, dst:dst + n].set(b_in[None, src:src + n])
    wd = wd.astype(BF16)

    tm = cfg["tm_in"]
    za, zb, zc, zdt = pl.pallas_call(
        functools.partial(_inproj_kernel, w_m=w_m, w_f=w_f, n_chunk=cfg["n_chunk"],
                          q_scale=dh_f ** -0.5),
        grid=(t // tm,),
        in_specs=[pl.BlockSpec((tm, d), lambda i: (i, 0)), _const_spec((1, d)),
                  _const_spec(wa.shape), _const_spec(ba.shape),
                  _const_spec(wb.shape), _const_spec(bb.shape),
                  _const_spec(wc.shape), _const_spec(bc.shape),
                  _const_spec(wd.shape), _const_spec(bd.shape)],
        out_specs=[pl.BlockSpec((tm, 4 * w_m), lambda i: (i, 0)),
                   pl.BlockSpec((tm, 3 * w_f), lambda i: (i, 0)),
                   pl.BlockSpec((tm, 2 * d), lambda i: (i, 0)),
                   pl.BlockSpec((N_GATE_ROWS, tm), lambda i: (0, i))],
        out_shape=[jax.ShapeDtypeStruct((t, 4 * w_m), BF16),
                   jax.ShapeDtypeStruct((t, 3 * w_f), BF16),
                   jax.ShapeDtypeStruct((t, 2 * d), BF16),
                   jax.ShapeDtypeStruct((N_GATE_ROWS, t), F32)],
        compiler_params=pltpu.CompilerParams(dimension_semantics=("parallel",), vmem_limit_bytes=vmem),
        name="inproj",
    )(x2d, p["norm1_g"][None, :], wa, ba, wb, bb, wc, bc, wd, bd)

    rows, cols = pl.pallas_call(
        _gates_kernel,
        grid=(batch,),
        in_specs=[pl.BlockSpec((N_GATE_ROWS, seq), lambda b: (0, b))],
        out_specs=[pl.BlockSpec((1, 2 * GATE_GROUP, seq), lambda b: (b, 0, 0)),
                   pl.BlockSpec((seq, LANES), lambda b: (b, 0))],
        out_shape=[jax.ShapeDtypeStruct((batch, 2 * GATE_GROUP, seq), F32),
                   jax.ShapeDtypeStruct((t, LANES), F32)],
        compiler_params=pltpu.CompilerParams(dimension_semantics=("parallel",), vmem_limit_bytes=vmem),
        name="gates",
    )(zdt)

    chunk = cfg["chunk"]
    nc = seq // chunk
    dh_m = w_m // H_M
    rows_m = rows.reshape(batch, 2 * GATE_GROUP, nc, chunk)
    ym = pl.pallas_call(
        functools.partial(_mlstm_kernel, w_m=w_m, chunk=chunk),
        grid=(batch,),
        in_specs=[pl.BlockSpec((seq, 4 * w_m), lambda b: (b, 0)),
                  pl.BlockSpec((1, 2 * GATE_GROUP, nc, chunk), lambda b: (b, 0, 0, 0)),
                  pl.BlockSpec((seq, LANES), lambda b: (b, 0)),
                  _const_spec((CONV_K, 2 * w_m)), _const_spec((1, 2 * w_m)), _const_spec((1, w_m))],
        out_specs=pl.BlockSpec((seq, w_m), lambda b: (b, 0)),
        out_shape=jax.ShapeDtypeStruct((t, w_m), BF16),
        scratch_shapes=[pltpu.VMEM((seq, w_m), BF16),
                        pltpu.VMEM((H_M, nc, dh_m, chunk), BF16),
                        pltpu.VMEM((H_M, seq, 2 * dh_m), BF16),
                        pltpu.VMEM((H_M, dh_m, 2 * dh_m), F32)],
        compiler_params=pltpu.CompilerParams(dimension_semantics=("parallel",), vmem_limit_bytes=vmem),
        name="mlstm",
    )(za, rows_m, cols, p["conv_w"], p["conv_b"][None, :], p["mlstm_norm_g"][None, :])

    tq, tk = cfg["tq"], cfg["tk"]
    assert tq == tk
    nq = seq // tq
    rows_f = rows.reshape(batch, 2 * GATE_GROUP, seq // tk, tk)
    yf = pl.pallas_call(
        functools.partial(_fox_kernel, tq=tq, tk=tk),
        grid=(batch, nq),
        in_specs=[pl.BlockSpec((tq, w_f), lambda b, i: (b * nq + i, 0)),
                  pl.BlockSpec((seq, w_f), lambda b, i: (b, 1)),
                  pl.BlockSpec((seq, w_f), lambda b, i: (b, 2)),
                  pl.BlockSpec((1, 2 * GATE_GROUP, seq // tk, tk), lambda b, i: (b, 0, 0, 0)),
                  pl.BlockSpec((tq, LANES), lambda b, i: (b * nq + i, 0))],
        out_specs=pl.BlockSpec((tq, w_f), lambda b, i: (b * nq + i, 0)),
        out_shape=jax.ShapeDtypeStruct((t, w_f), BF16),
        compiler_params=pltpu.CompilerParams(dimension_semantics=("parallel", "arbitrary"),
                                             vmem_limit_bytes=vmem),
        name="fox",
    )(zb, zb, zb, rows_f, cols)

    tmp = cfg["tm_post"]
    wbm, wbf = p["w_br_mlstm"].astype(BF16), p["w_br_fox"].astype(BF16)
    wo = p["w_out"].astype(BF16)
    wg, wu, wdn = p["w_gate"].astype(BF16), p["w_up"].astype(BF16), p["w_down"].astype(BF16)
    return pl.pallas_call(
        functools.partial(_post_kernel, tf=cfg["tf"]),
        grid=(t // tmp,),
        in_specs=[pl.BlockSpec((tmp, d), lambda i: (i, 0)),
                  pl.BlockSpec((tmp, w_m), lambda i: (i, 0)),
                  pl.BlockSpec((tmp, w_f), lambda i: (i, 0)),
                  pl.BlockSpec((tmp, 2 * d), lambda i: (i, 0)),
                  _const_spec(wbm.shape), _const_spec(wbf.shape), _const_spec(wo.shape),
                  _const_spec((1, d)), _const_spec(wg.shape), _const_spec(wu.shape),
                  _const_spec(wdn.shape), _const_spec((1, d))],
        out_specs=pl.BlockSpec((tmp, d), lambda i: (i, 0)),
        out_shape=jax.ShapeDtypeStruct((t, d), F32),
        compiler_params=pltpu.CompilerParams(dimension_semantics=("parallel",), vmem_limit_bytes=vmem),
        name="post",
    )(x2d, ym, yf, zc, wbm, wbf, wo, p["norm2_g"][None, :], wg, wu, wdn, p["norm_f_g"][None, :])


def kernel(x, norm1_g, w_in, b_in, conv_w, conv_b, mlstm_norm_g, w_br_mlstm, w_br_fox, w_out,
           norm2_g, w_gate, w_up, w_down, norm_f_g):
    batch, seq, d = x.shape
    depth = w_in.shape[0]
    assert depth == 1, "the final norm is fused into the single layer's last call"
    cfg = _cfg(batch, seq, d, w_gate.shape[-1])
    p = dict(norm1_g=norm1_g[0], w_in=w_in[0], b_in=b_in[0], conv_w=conv_w[0], conv_b=conv_b[0],
             mlstm_norm_g=mlstm_norm_g[0], w_br_mlstm=w_br_mlstm[0], w_br_fox=w_br_fox[0],
             w_out=w_out[0], norm2_g=norm2_g[0], w_gate=w_gate[0], w_up=w_up[0], w_down=w_down[0],
             norm_f_g=norm_f_g)
    out = _layer(x.reshape(batch * seq, d), batch, seq, p, cfg)
    return out.reshape(batch, seq, d)
```

```python
import functools

import jax
import jax.numpy as jnp
import numpy as np
from jax import lax
from jax.experimental import pallas as pl
from jax.experimental.pallas import tpu as pltpu

EPS = 1e-6
H_M = 4
H_F = 8
CONV_K = 4

LANES = 128
GATE_GROUP = 8
N_GATE_ROWS = 3 * GATE_GROUP
V7X_VMEM_BYTES = 64 * 1024 * 1024
NEG_BIG = -1e30

F32 = jnp.float32
BF16 = jnp.bfloat16


def _cfg(batch, seq, d_model, d_ff):
    return dict(
        tm_in=512,
        n_chunk=512,
        chunk=256,
        tq=256,
        tk=256,
        tm_post=512,
        tf=256,
        vmem_limit=V7X_VMEM_BYTES - 8 * 1024 * 1024,
    )


def _const_spec(shape):
    nd = len(shape)
    return pl.BlockSpec(shape, lambda *_: (0,) * nd, pipeline_mode=pl.Buffered(1))


def _rms(x, g):
    return x * lax.rsqrt(jnp.mean(x * x, axis=-1, keepdims=True) + EPS) * g


def _log_sigmoid(x):
    return jnp.minimum(x, 0.0) - jnp.log1p(jnp.exp(-jnp.abs(x)))


def _inproj_kernel(x_ref, g_ref, wa_ref, ba_ref, wb_ref, bb_ref, wc_ref, bc_ref,
                   wd_ref, bd_ref, za_ref, zb_ref, zc_ref, zdt_ref, *, w_m, w_f, n_chunk, q_scale):
    hb = _rms(x_ref[...], g_ref[...]).astype(BF16)

    def proj(w_ref, b_ref, c0, c1):
        return jnp.dot(hb, w_ref[:, c0:c1], preferred_element_type=F32) + b_ref[:, c0:c1]

    for c0 in range(0, 4 * w_m, n_chunk):
        z = proj(wa_ref, ba_ref, c0, c0 + n_chunk)
        if c0 >= 3 * w_m:
            z = jax.nn.sigmoid(z)
        za_ref[:, c0:c0 + n_chunk] = z.astype(BF16)
    for c0 in range(0, 3 * w_f, n_chunk):
        z = proj(wb_ref, bb_ref, c0, c0 + n_chunk)
        if c0 < w_f:
            z = z * q_scale
        zb_ref[:, c0:c0 + n_chunk] = z.astype(BF16)
    for c0 in range(0, wc_ref.shape[1], n_chunk):
        z = proj(wc_ref, bc_ref, c0, c0 + n_chunk)
        zc_ref[:, c0:c0 + n_chunk] = jax.nn.sigmoid(z).astype(BF16)
    zd = proj(wd_ref, bd_ref, 0, LANES)
    zdt_ref[...] = zd.T[:N_GATE_ROWS, :]


def _scan_lanes(x, op, fill):
    n = x.shape[-1]
    lane = lax.broadcasted_iota(jnp.int32, x.shape, 1)
    s = 1
    while s < n:
        shifted = pltpu.roll(x, s, axis=1)
        x = op(x, jnp.where(lane >= s, shifted, fill))
        s *= 2
    return x


def _gates_kernel(zdt_ref, rows_ref, cols_ref, csplit_ref):
    z = zdt_ref[...]
    seq = z.shape[1]
    i8 = z[0:GATE_GROUP]
    f8 = _scan_lanes(_log_sigmoid(z[GATE_GROUP:2 * GATE_GROUP]), jnp.add, 0.0)
    cf8 = _scan_lanes(_log_sigmoid(z[2 * GATE_GROUP:3 * GATE_GROUP]), jnp.add, 0.0)
    g8 = i8 - f8
    m8 = jnp.maximum(_scan_lanes(g8, jnp.maximum, NEG_BIG), 0.0)
    en8 = jnp.exp(-(f8 + m8))
    rows_ref[0] = g8
    stack = jnp.concatenate(
        [m8, en8, jnp.zeros((LANES - 2 * GATE_GROUP, seq), F32)], axis=0)
    cols_ref[...] = stack.T
    hi = cf8.astype(BF16).astype(F32)
    r1 = cf8 - hi
    lo = r1.astype(BF16).astype(F32)
    lo2 = (r1 - lo).astype(BF16).astype(F32)
    ones = jnp.where(lax.broadcasted_iota(jnp.int32, (GATE_GROUP, seq), 0) == 0, 1.0, 0.0)
    split = jnp.concatenate(
        [hi, lo, lo2, ones, jnp.zeros((LANES - 4 * GATE_GROUP, seq), F32)], axis=0)
    csplit_ref[...] = split.T.astype(BF16)


def _mlstm_kernel(za_ref, rows_ref, cols_ref, cw_ref, cb_ref, ng_ref, ym_ref,
                  q_sc, kt_sc, va_sc, c_sc, *, w_m, chunk):
    seq = za_ref.shape[0]
    dh = w_m // H_M
    nc = seq // chunk
    k_scale = dh ** -0.5
    row = lax.broadcasted_iota(jnp.int32, (seq, dh), 0)

    def conv_silu(u, w, b):
        y = u * w[CONV_K - 1:CONV_K] + b
        for s in range(1, CONV_K):
            us = jnp.where(row >= s, pltpu.roll(u, s, axis=0), 0.0)
            y = y + us * w[CONV_K - 1 - s:CONV_K - s]
        return y * jax.nn.sigmoid(y)

    one_col = jnp.where(lax.broadcasted_iota(jnp.int32, (seq, dh), 1) == 0, 1.0, 0.0).astype(BF16)
    for h in range(H_M):
        cq = slice(h * dh, (h + 1) * dh)
        ck = slice(w_m + h * dh, w_m + (h + 1) * dh)
        cv = slice(2 * w_m + h * dh, 2 * w_m + (h + 1) * dh)
        q_sc[:, cq] = conv_silu(za_ref[:, cq].astype(F32), cw_ref[:, cq], cb_ref[:, cq]).astype(BF16)
        kk = conv_silu(za_ref[:, ck].astype(F32), cw_ref[:, ck], cb_ref[:, ck]) * k_scale
        for c in range(nc):
            kt_sc[h, c] = kk[c * chunk:(c + 1) * chunk, :].T.astype(BF16)
        va_sc[h, :, 0:dh] = za_ref[:, cv]
        va_sc[h, :, dh:2 * dh] = one_col
    c_sc[...] = jnp.zeros(c_sc.shape, F32)

    tri = (lax.broadcasted_iota(jnp.int32, (chunk, chunk), 0)
           >= lax.broadcasted_iota(jnp.int32, (chunk, chunk), 1))

    def body(c, carry):
        r0 = pl.multiple_of(c * chunk, chunk)
        colsc = cols_ref[pl.ds(r0, chunk), :]
        last = cols_ref[pl.ds(r0 + chunk - 1, 1), :]
        prev = cols_ref[pl.ds(jnp.maximum(r0 - 1, 0), 1), :]
        prev = jnp.where(c > 0, prev, 0.0)
        for h in range(H_M):
            ch = slice(h * dh, (h + 1) * dh)
            g_row = rows_ref[0, h, pl.ds(c, 1), :]
            m_col = colsc[:, h:h + 1]
            en_col = colsc[:, GATE_GROUP + h:GATE_GROUP + h + 1]
            m_e = last[:, h:h + 1]
            m_p = prev[:, h:h + 1]
            qc = q_sc[pl.ds(r0, chunk), ch]
            ktc = kt_sc[h, c]
            vc = va_sc[h, pl.ds(r0, chunk), :]
            dmat = jnp.where(tri, jnp.exp(g_row - m_col), 0.0)
            sqk = (jnp.dot(qc, ktc, preferred_element_type=F32) * dmat).astype(BF16)
            c_old = c_sc[h]
            a = jnp.exp(m_p - m_col)
            nd = (a * jnp.dot(qc, c_old.astype(BF16), preferred_element_type=F32)
                  + jnp.dot(sqk, vc, preferred_element_type=F32))
            num = nd[:, 0:dh]
            den = nd[:, dh:dh + 1]
            hh = num * (1.0 / jnp.maximum(jnp.abs(den), en_col))
            hn = hh * lax.rsqrt(jnp.mean(hh * hh, axis=-1, keepdims=True) + EPS)
            og = za_ref[pl.ds(r0, chunk), 3 * w_m + h * dh:3 * w_m + (h + 1) * dh].astype(F32)
            ym_ref[pl.ds(r0, chunk), ch] = (hn * ng_ref[:, ch] * og).astype(BF16)
            w_row = jnp.exp(g_row - m_e)
            decay = jnp.exp(m_p - m_e)
            ktw = (ktc.astype(F32) * w_row).astype(BF16)
            c_sc[h] = decay * c_old + jnp.dot(ktw, vc, preferred_element_type=F32)
        return carry

    lax.fori_loop(0, nc, body, 0)


V_ROWS = 80


def _fox_kernel(q_ref, k_ref, v_ref, csq_ref, csk_ref, qsel_ref, ksel_ref, yf_ref,
                kaug_sc, vt_sc, qaug_sc, m_sc, acc_sc, *, tq, tk):
    qi = pl.program_id(1)
    seq = k_ref.shape[0]
    dhp = LANES
    half = dhp // 2
    nt = (((1,), (1,)), ((), ()))

    def own_lanes(head, rows):
        lane = lax.broadcasted_iota(jnp.int32, (rows, dhp), 1)
        return (lane >= half) if head % 2 else (lane < half)

    @pl.when(qi == 0)
    def _():
        csk = csk_ref[...]
        ones_rows = jnp.where(
            lax.broadcasted_iota(jnp.int32, (V_ROWS - half, tk), 0) == 0, 1.0, 0.0).astype(BF16)
        for p in range(H_F // 2):
            cp = slice(p * dhp, (p + 1) * dhp)
            kp = k_ref[:, cp]
            vt = v_ref[:, cp].astype(F32).T
            for head in (2 * p, 2 * p + 1):
                kb = jnp.dot(csk, ksel_ref[head], preferred_element_type=F32).astype(BF16)
                kaug_sc[head] = jnp.where(own_lanes(head, seq), kp, kb)
                r0 = (head % 2) * half
                for jb in range(seq // tk):
                    vt_sc[head, jb, 0:half, :] = vt[r0:r0 + half, jb * tk:(jb + 1) * tk].astype(BF16)
                    vt_sc[head, jb, half:V_ROWS, :] = ones_rows

    csq = csq_ref[...]
    for head in range(H_F):
        qp = q_ref[:, (head // 2) * dhp:(head // 2 + 1) * dhp]
        qb = jnp.dot(csq, qsel_ref[head], preferred_element_type=F32).astype(BF16)
        qaug_sc[head] = jnp.where(own_lanes(head, tq), qp, qb)
    m_sc[...] = jnp.full(m_sc.shape, NEG_BIG, F32)
    acc_sc[...] = jnp.zeros(acc_sc.shape, F32)

    key_io = lax.broadcasted_iota(jnp.int32, (tk, tq), 0)
    qry_io = lax.broadcasted_iota(jnp.int32, (tk, tq), 1)

    def step(j, masked):
        k0 = pl.multiple_of(j * tk, tk)
        sts = [lax.dot_general(kaug_sc[head, pl.ds(k0, tk), :], qaug_sc[head], nt,
                               preferred_element_type=F32) for head in range(H_F)]
        for head in range(H_F):
            st = sts[head]
            if masked:
                st = jnp.where(key_io <= qry_io, st, NEG_BIG)
            m = m_sc[head]
            m_new = jnp.maximum(m, jnp.max(st, axis=0, keepdims=True))
            alpha = jnp.exp(m - m_new)
            pt = jnp.exp(st - m_new).astype(BF16)
            m_sc[head] = m_new
            acc_sc[head] = alpha * acc_sc[head] + jnp.dot(vt_sc[head, j], pt,
                                                          preferred_element_type=F32)

    def loop_body(j, carry):
        step(j, False)
        return carry

    lax.fori_loop(0, qi, loop_body, 0)
    step(qi, True)
    for p in range(H_F // 2):
        outs = []
        for head in (2 * p, 2 * p + 1):
            acc = acc_sc[head]
            outs.append(acc[0:half] * (1.0 / acc[half:half + 1]))
        yf_ref[:, p * dhp:(p + 1) * dhp] = jnp.concatenate(outs, axis=0).T.astype(BF16)


def _post_kernel(x_ref, ym_ref, yf_ref, zc_ref, wbm_ref, wbf_ref, wo_ref, g2_ref,
                 wg_ref, wu_ref, wd_ref, gfin_ref, o_ref, *, tf):
    d = x_ref.shape[1]
    d_ff = wg_ref.shape[1]
    bm = jnp.dot(ym_ref[...], wbm_ref[...], preferred_element_type=F32)
    bf = jnp.dot(yf_ref[...], wbf_ref[...], preferred_element_type=F32)
    mix = zc_ref[:, 0:d].astype(F32) * bm + zc_ref[:, d:2 * d].astype(F32) * bf
    x1 = x_ref[...] + jnp.dot(mix.astype(BF16), wo_ref[...], preferred_element_type=F32)
    h2 = _rms(x1, g2_ref[...]).astype(BF16)
    acc = jnp.zeros(x1.shape, F32)
    for f0 in range(0, d_ff, tf):
        g = jnp.dot(h2, wg_ref[:, f0:f0 + tf], preferred_element_type=F32)
        u = jnp.dot(h2, wu_ref[:, f0:f0 + tf], preferred_element_type=F32)
        act = (g * jax.nn.sigmoid(g) * u).astype(BF16)
        acc = acc + jnp.dot(act, wd_ref[f0:f0 + tf, :], preferred_element_type=F32)
    o_ref[...] = _rms(x1 + acc, gfin_ref[...])


def _bias_selectors():
    ones_lane = 3 * GATE_GROUP
    qsel = np.zeros((H_F, LANES, LANES), np.float32)
    ksel = np.zeros((H_F, LANES, LANES), np.float32)
    for h in range(H_F):
        p0 = LANES // 2 if h % 2 == 0 else 0
        for c in range(3):
            qsel[h, GATE_GROUP * c + h, p0 + c] = 1.0
            qsel[h, ones_lane, p0 + 3 + c] = 1.0
            ksel[h, ones_lane, p0 + c] = 1.0
            ksel[h, GATE_GROUP * c + h, p0 + 3 + c] = -1.0
    return jnp.asarray(qsel, BF16), jnp.asarray(ksel, BF16)
def _layer(x2d, batch, seq, p, cfg):
    t, d = x2d.shape
    w_m = p["w_br_mlstm"].shape[0]
    w_f = p["w_br_fox"].shape[0]
    d_ff = p["w_gate"].shape[1]
    dh_f = w_f // H_F
    vmem = cfg["vmem_limit"]

    o_mi = 4 * w_m
    o_mf = o_mi + H_M
    o_fq = o_mf + H_M
    o_ff = o_fq + 3 * w_f
    o_g = o_ff + H_F
    w_in, b_in = p["w_in"], p["b_in"]
    wa, ba = w_in[:, :o_mi].astype(BF16), b_in[None, :o_mi]
    wb, bb = w_in[:, o_fq:o_ff].astype(BF16), b_in[None, o_fq:o_ff]
    wc, bc = w_in[:, o_g:].astype(BF16), b_in[None, o_g:]
    wd = jnp.zeros((d, LANES), F32)
    bd = jnp.zeros((1, LANES), F32)
    for dst, src, n in ((0, o_mi, H_M), (GATE_GROUP, o_mf, H_M), (2 * GATE_GROUP, o_ff, H_F)):
        wd = wd.at[:, dst:dst + n].set(w_in[:, src:src + n])
        bd = bd.at[:, dst:dst + n].set(b_in[None, src:src + n])
    wd = wd.astype(BF16)

    tm = cfg["tm_in"]
    za, zb, zc, zdt = pl.pallas_call(
        functools.partial(_inproj_kernel, w_m=w_m, w_f=w_f, n_chunk=cfg["n_chunk"],
                          q_scale=dh_f ** -0.5),
        grid=(t // tm,),
        in_specs=[pl.BlockSpec((tm, d), lambda i: (i, 0)), _const_spec((1, d)),
                  _const_spec(wa.shape), _const_spec(ba.shape),
                  _const_spec(wb.shape), _const_spec(bb.shape),
                  _const_spec(wc.shape), _const_spec(bc.shape),
                  _const_spec(wd.shape), _const_spec(bd.shape)],
        out_specs=[pl.BlockSpec((tm, 4 * w_m), lambda i: (i, 0)),
                   pl.BlockSpec((tm, 3 * w_f), lambda i: (i, 0)),
                   pl.BlockSpec((tm, 2 * d), lambda i: (i, 0)),
                   pl.BlockSpec((N_GATE_ROWS, tm), lambda i: (0, i))],
        out_shape=[jax.ShapeDtypeStruct((t, 4 * w_m), BF16),
                   jax.ShapeDtypeStruct((t, 3 * w_f), BF16),
                   jax.ShapeDtypeStruct((t, 2 * d), BF16),
                   jax.ShapeDtypeStruct((N_GATE_ROWS, t), F32)],
        compiler_params=pltpu.CompilerParams(dimension_semantics=("parallel",), vmem_limit_bytes=vmem),
        name="inproj",
    )(x2d, p["norm1_g"][None, :], wa, ba, wb, bb, wc, bc, wd, bd)

    rows, cols, csplit = pl.pallas_call(
        _gates_kernel,
        grid=(batch,),
        in_specs=[pl.BlockSpec((N_GATE_ROWS, seq), lambda b: (0, b))],
        out_specs=[pl.BlockSpec((1, GATE_GROUP, seq), lambda b: (b, 0, 0)),
                   pl.BlockSpec((seq, LANES), lambda b: (b, 0)),
                   pl.BlockSpec((seq, LANES), lambda b: (b, 0))],
        out_shape=[jax.ShapeDtypeStruct((batch, GATE_GROUP, seq), F32),
                   jax.ShapeDtypeStruct((t, LANES), F32),
                   jax.ShapeDtypeStruct((t, LANES), BF16)],
        compiler_params=pltpu.CompilerParams(dimension_semantics=("parallel",), vmem_limit_bytes=vmem),
        name="gates",
    )(zdt)

    chunk = cfg["chunk"]
    nc = seq // chunk
    dh_m = w_m // H_M
    rows_m = rows.reshape(batch, GATE_GROUP, nc, chunk)
    ym = pl.pallas_call(
        functools.partial(_mlstm_kernel, w_m=w_m, chunk=chunk),
        grid=(batch,),
        in_specs=[pl.BlockSpec((seq, 4 * w_m), lambda b: (b, 0)),
                  pl.BlockSpec((1, GATE_GROUP, nc, chunk), lambda b: (b, 0, 0, 0)),
                  pl.BlockSpec((seq, LANES), lambda b: (b, 0)),
                  _const_spec((CONV_K, 2 * w_m)), _const_spec((1, 2 * w_m)), _const_spec((1, w_m))],
        out_specs=pl.BlockSpec((seq, w_m), lambda b: (b, 0)),
        out_shape=jax.ShapeDtypeStruct((t, w_m), BF16),
        scratch_shapes=[pltpu.VMEM((seq, w_m), BF16),
                        pltpu.VMEM((H_M, nc, dh_m, chunk), BF16),
                        pltpu.VMEM((H_M, seq, 2 * dh_m), BF16),
                        pltpu.VMEM((H_M, dh_m, 2 * dh_m), F32)],
        compiler_params=pltpu.CompilerParams(dimension_semantics=("parallel",), vmem_limit_bytes=vmem),
        name="mlstm",
    )(za, rows_m, cols, p["conv_w"], p["conv_b"][None, :], p["mlstm_norm_g"][None, :])

    tq, tk = cfg["tq"], cfg["tk"]
    assert tq == tk
    nq = seq // tq
    qsel, ksel = _bias_selectors()
    yf = pl.pallas_call(
        functools.partial(_fox_kernel, tq=tq, tk=tk),
        grid=(batch, nq),
        in_specs=[pl.BlockSpec((tq, w_f), lambda b, i: (b * nq + i, 0)),
                  pl.BlockSpec((seq, w_f), lambda b, i: (b, 1)),
                  pl.BlockSpec((seq, w_f), lambda b, i: (b, 2)),
                  pl.BlockSpec((tq, LANES), lambda b, i: (b * nq + i, 0)),
                  pl.BlockSpec((seq, LANES), lambda b, i: (b, 0)),
                  _const_spec(qsel.shape), _const_spec(ksel.shape)],
        out_specs=pl.BlockSpec((tq, w_f), lambda b, i: (b * nq + i, 0)),
        out_shape=jax.ShapeDtypeStruct((t, w_f), BF16),
        scratch_shapes=[pltpu.VMEM((H_F, seq, LANES), BF16),
                        pltpu.VMEM((H_F, seq // tk, V_ROWS, tk), BF16),
                        pltpu.VMEM((H_F, tq, LANES), BF16),
                        pltpu.VMEM((H_F, 1, tq), F32),
                        pltpu.VMEM((H_F, V_ROWS, tq), F32)],
        compiler_params=pltpu.CompilerParams(dimension_semantics=("arbitrary", "arbitrary"),
                                             vmem_limit_bytes=vmem),
        name="fox",
    )(zb, zb, zb, csplit, csplit, qsel, ksel)

    tmp = cfg["tm_post"]
    wbm, wbf = p["w_br_mlstm"].astype(BF16), p["w_br_fox"].astype(BF16)
    wo = p["w_out"].astype(BF16)
    wg, wu, wdn = p["w_gate"].astype(BF16), p["w_up"].astype(BF16), p["w_down"].astype(BF16)
    return pl.pallas_call(
        functools.partial(_post_kernel, tf=cfg["tf"]),
        grid=(t // tmp,),
        in_specs=[pl.BlockSpec((tmp, d), lambda i: (i, 0)),
                  pl.BlockSpec((tmp, w_m), lambda i: (i, 0)),
                  pl.BlockSpec((tmp, w_f), lambda i: (i, 0)),
                  pl.BlockSpec((tmp, 2 * d), lambda i: (i, 0)),
                  _const_spec(wbm.shape), _const_spec(wbf.shape), _const_spec(wo.shape),
                  _const_spec((1, d)), _const_spec(wg.shape), _const_spec(wu.shape),
                  _const_spec(wdn.shape), _const_spec((1, d))],
        out_specs=pl.BlockSpec((tmp, d), lambda i: (i, 0)),
        out_shape=jax.ShapeDtypeStruct((t, d), F32),
        compiler_params=pltpu.CompilerParams(dimension_semantics=("parallel",), vmem_limit_bytes=vmem),
        name="post",
    )(x2d, ym, yf, zc, wbm, wbf, wo, p["norm2_g"][None, :], wg, wu, wdn, p["norm_f_g"][None, :])


def kernel(x, norm1_g, w_in, b_in, conv_w, conv_b, mlstm_norm_g, w_br_mlstm, w_br_fox, w_out,
           norm2_g, w_gate, w_up, w_down, norm_f_g):
    batch, seq, d = x.shape
    depth = w_in.shape[0]
    assert depth == 1, "the final norm is fused into the single layer's last call"
    cfg = _cfg(batch, seq, d, w_gate.shape[-1])
    p = dict(norm1_g=norm1_g[0], w_in=w_in[0], b_in=b_in[0], conv_w=conv_w[0], conv_b=conv_b[0],
             mlstm_norm_g=mlstm_norm_g[0], w_br_mlstm=w_br_mlstm[0], w_br_fox=w_br_fox[0],
             w_out=w_out[0], norm2_g=norm2_g[0], w_gate=w_gate[0], w_up=w_up[0], w_down=w_down[0],
             norm_f_g=norm_f_g)
    out = _layer(x.reshape(batch * seq, d), batch, seq, p, cfg)
    return out.reshape(batch, seq, d)
```

```python
import functools

import jax
import jax.numpy as jnp
import numpy as np
from jax import lax
from jax.experimental import pallas as pl
from jax.experimental.pallas import tpu as pltpu

EPS = 1e-6
H_M = 4
H_F = 8
CONV_K = 4

LANES = 128
SUBLANES = 8
BF16_ROWS = 16
GATE_GROUP = 8
N_GATE_ROWS = 3 * GATE_GROUP
V7X_VMEM_BYTES = 64 * 1024 * 1024
NEG_BIG = -1e30

F32 = jnp.float32
BF16 = jnp.bfloat16
NT_DIMS = (((1,), (1,)), ((), ()))


def _cfg(batch, seq, d_model, d_ff):
    return dict(
        tm_in=512,
        n_chunk=512,
        chunk=256,
        tq=256,
        tk=256,
        tm_post=512,
        tf=256,
        vmem_limit=V7X_VMEM_BYTES - 8 * 1024 * 1024,
    )


def _const_spec(shape):
    nd = len(shape)
    return pl.BlockSpec(shape, lambda *_: (0,) * nd, pipeline_mode=pl.Buffered(1))


def _rms(x, g):
    return x * lax.rsqrt(jnp.mean(x * x, axis=-1, keepdims=True) + EPS) * g


def _log_sigmoid(x):
    return jnp.minimum(x, 0.0) - jnp.log1p(jnp.exp(-jnp.abs(x)))


def _inproj_kernel(x_ref, g_ref, wa_ref, ba_ref, wb_ref, bb_ref, wc_ref, bc_ref, wd_ref, bd_ref,
                   cw_ref, cb_ref, zqk_ref, vt_ref, og_ref, zb_ref, zc_ref, zdt_ref, prev_sc,
                   *, w_m, w_f, n_chunk, q_scale, k_scale, chunk, tiles_per_seq):
    tm = x_ref.shape[0]
    seq_start = (pl.program_id(0) % tiles_per_seq) == 0
    hb = _rms(x_ref[...], g_ref[...]).astype(BF16)
    row8 = lax.broadcasted_iota(jnp.int32, (SUBLANES, n_chunk), 0)

    def proj(w_ref, b_ref, c0, c1):
        return jnp.dot(hb, w_ref[:, c0:c1], preferred_element_type=F32) + b_ref[:, c0:c1]

    for c0 in range(0, 2 * w_m, n_chunk):
        cs = slice(c0, c0 + n_chunk)
        z = proj(wa_ref, ba_ref, c0, c0 + n_chunk)
        prev = jnp.where(seq_start, 0.0, prev_sc[:, cs])
        prev_sc[:, cs] = z[tm - SUBLANES:tm]
        w = cw_ref[:, cs]
        y = z * w[CONV_K - 1:CONV_K] + cb_ref[:, cs]
        for s in range(1, CONV_K):
            rolled = pltpu.roll(z, s, axis=0)
            top = jnp.where(row8 < s, pltpu.roll(prev, s, axis=0), rolled[0:SUBLANES])
            shifted = jnp.concatenate([top, rolled[SUBLANES:]], axis=0)
            y = y + shifted * w[CONV_K - 1 - s:CONV_K - s]
        act = y * jax.nn.sigmoid(y)
        if c0 >= w_m:
            act = act * k_scale
        zqk_ref[:, cs] = act.astype(BF16)
    for c0 in range(2 * w_m, 3 * w_m, n_chunk):
        z = proj(wa_ref, ba_ref, c0, c0 + n_chunk)
        for cc in range(tm // chunk):
            for f0 in range(0, n_chunk, LANES):
                blk = z[cc * chunk:(cc + 1) * chunk, f0:f0 + LANES]
                r0 = c0 - 2 * w_m + f0
                vt_ref[cc, r0:r0 + LANES, :] = blk.T.astype(BF16)
    for c0 in range(3 * w_m, 4 * w_m, n_chunk):
        z = proj(wa_ref, ba_ref, c0, c0 + n_chunk)
        og_ref[:, c0 - 3 * w_m:c0 - 3 * w_m + n_chunk] = jax.nn.sigmoid(z).astype(BF16)
    for c0 in range(0, 3 * w_f, n_chunk):
        z = proj(wb_ref, bb_ref, c0, c0 + n_chunk)
        if c0 < w_f:
            z = z * q_scale
        zb_ref[:, c0:c0 + n_chunk] = z.astype(BF16)
    for c0 in range(0, wc_ref.shape[1], n_chunk):
        z = proj(wc_ref, bc_ref, c0, c0 + n_chunk)
        zc_ref[:, c0:c0 + n_chunk] = jax.nn.sigmoid(z).astype(BF16)
    zd = proj(wd_ref, bd_ref, 0, LANES)
    zdt_ref[...] = zd.T[:N_GATE_ROWS, :]


def _scan_lanes(x, op, fill):
    n = x.shape[-1]
    lane = lax.broadcasted_iota(jnp.int32, x.shape, 1)
    s = 1
    while s < n:
        shifted = pltpu.roll(x, s, axis=1)
        x = op(x, jnp.where(lane >= s, shifted, fill))
        s *= 2
    return x


def _gates_kernel(zdt_ref, rows_ref, cols_ref, csplit_ref):
    z = zdt_ref[...]
    seq = z.shape[1]
    i8 = z[0:GATE_GROUP]
    f8 = _scan_lanes(_log_sigmoid(z[GATE_GROUP:2 * GATE_GROUP]), jnp.add, 0.0)
    cf8 = _scan_lanes(_log_sigmoid(z[2 * GATE_GROUP:3 * GATE_GROUP]), jnp.add, 0.0)
    g8 = i8 - f8
    m8 = jnp.maximum(_scan_lanes(g8, jnp.maximum, NEG_BIG), 0.0)
    en8 = jnp.exp(-(f8 + m8))
    rows_ref[0] = jnp.concatenate([g8, m8, en8], axis=0)
    stack = jnp.concatenate(
        [g8, m8, jnp.zeros((LANES - 2 * GATE_GROUP, seq), F32)], axis=0)
    cols_ref[...] = stack.T
    hi = cf8.astype(BF16).astype(F32)
    r1 = cf8 - hi
    lo = r1.astype(BF16).astype(F32)
    lo2 = (r1 - lo).astype(BF16).astype(F32)
    ones = jnp.where(lax.broadcasted_iota(jnp.int32, (GATE_GROUP, seq), 0) == 0, 1.0, 0.0)
    split = jnp.concatenate(
        [hi, lo, lo2, ones, jnp.zeros((LANES - 4 * GATE_GROUP, seq), F32)], axis=0)
    csplit_ref[...] = split.T.astype(BF16)


def _mlstm_kernel(zqk_ref, vt_ref, og_ref, rows_ref, cols_ref, ng_ref, ym_ref, ct_sc, *, w_m, chunk):
    seq = zqk_ref.shape[0]
    dh = w_m // H_M
    nc = seq // chunk
    ct_sc[...] = jnp.zeros(ct_sc.shape, F32)
    causal = (lax.broadcasted_iota(jnp.int32, (chunk, chunk), 0)
              <= lax.broadcasted_iota(jnp.int32, (chunk, chunk), 1))
    ones_rows = jnp.where(
        lax.broadcasted_iota(jnp.int32, (BF16_ROWS, chunk), 0) == 0, 1.0, 0.0).astype(BF16)

    def body(c, carry):
        r0 = pl.multiple_of(c * chunk, chunk)
        colsc = cols_ref[pl.ds(r0, chunk), :]
        last = cols_ref[pl.ds(r0 + chunk - 1, 1), :]
        prev = cols_ref[pl.ds(jnp.maximum(r0 - 1, 0), 1), :]
        prev = jnp.where(c > 0, prev, 0.0)
        first, vas = [], []
        for h in range(H_M):
            qc = zqk_ref[pl.ds(r0, chunk), h * dh:(h + 1) * dh]
            kc = zqk_ref[pl.ds(r0, chunk), w_m + h * dh:w_m + (h + 1) * dh]
            lhs = jnp.concatenate([kc, ct_sc[h].astype(BF16)], axis=0)
            first.append(lax.dot_general(lhs, qc, NT_DIMS, preferred_element_type=F32))
            vas.append(jnp.concatenate([vt_ref[c, h * dh:(h + 1) * dh, :], ones_rows], axis=0))
        for h in range(H_M):
            kc = zqk_ref[pl.ds(r0, chunk), w_m + h * dh:w_m + (h + 1) * dh]
            g_row = rows_ref[0, h, pl.ds(c, 1), :]
            m_e = last[:, GATE_GROUP + h:GATE_GROUP + h + 1]
            m_p = prev[:, GATE_GROUP + h:GATE_GROUP + h + 1]
            vaw = (vas[h].astype(F32) * jnp.exp(g_row - m_e)).astype(BF16)
            ct_sc[h] = jnp.exp(m_p - m_e) * ct_sc[h] + jnp.dot(vaw, kc, preferred_element_type=F32)
        for h in range(H_M):
            ch = slice(h * dh, (h + 1) * dh)
            g_col = colsc[:, h:h + 1]
            m_row = rows_ref[0, GATE_GROUP + h, pl.ds(c, 1), :]
            en_row = rows_ref[0, 2 * GATE_GROUP + h, pl.ds(c, 1), :]
            m_p = prev[:, GATE_GROUP + h:GATE_GROUP + h + 1]
            dmat = jnp.where(causal, jnp.exp(g_col - m_row), 0.0)
            sqk = (first[h][0:chunk] * dmat).astype(BF16)
            nd = (jnp.exp(m_p - m_row) * first[h][chunk:]
                  + jnp.dot(vas[h], sqk, preferred_element_type=F32))
            den = nd[dh:dh + 1]
            ht = nd[0:dh] * (1.0 / jnp.maximum(jnp.abs(den), en_row))
            hn = ht * lax.rsqrt(jnp.mean(ht * ht, axis=0, keepdims=True) + EPS)
            og = og_ref[pl.ds(r0, chunk), ch].astype(F32)
            ym_ref[pl.ds(r0, chunk), ch] = (hn.T * ng_ref[:, ch] * og).astype(BF16)
        return carry

    lax.fori_loop(0, nc, body, 0)


def _fox_kernel(q_ref, k_ref, v_ref, csq_ref, csk_ref, qsel_ref, ksel_ref, yf_ref,
                kaug_sc, vt_sc, qaug_sc, m_sc, acc_sc, *, tq, tk):
    qi = pl.program_id(1)
    seq = k_ref.shape[0]
    dhp = LANES
    half = dhp // 2
    v_rows = half + BF16_ROWS

    def own_lanes(head, rows):
        lane = lax.broadcasted_iota(jnp.int32, (rows, dhp), 1)
        return (lane >= half) if head % 2 else (lane < half)

    @pl.when(qi == 0)
    def _():
        csk = csk_ref[...]
        ones_rows = jnp.where(
            lax.broadcasted_iota(jnp.int32, (BF16_ROWS, tk), 0) == 0, 1.0, 0.0).astype(BF16)
        for p in range(H_F // 2):
            cp = slice(p * dhp, (p + 1) * dhp)
            kp = k_ref[:, cp]
            vt = v_ref[:, cp].astype(F32).T
            for head in (2 * p, 2 * p + 1):
                kb = jnp.dot(csk, ksel_ref[head], preferred_element_type=F32).astype(BF16)
                kaug_sc[head] = jnp.where(own_lanes(head, seq), kp, kb)
                r0 = (head % 2) * half
                for jb in range(seq // tk):
                    vt_sc[head, jb, 0:half, :] = vt[r0:r0 + half, jb * tk:(jb + 1) * tk].astype(BF16)
                    vt_sc[head, jb, half:v_rows, :] = ones_rows

    csq = csq_ref[...]
    for head in range(H_F):
        qp = q_ref[:, (head // 2) * dhp:(head // 2 + 1) * dhp]
        qb = jnp.dot(csq, qsel_ref[head], preferred_element_type=F32).astype(BF16)
        qaug_sc[head] = jnp.where(own_lanes(head, tq), qp, qb)
    m_sc[...] = jnp.full(m_sc.shape, NEG_BIG, F32)
    acc_sc[...] = jnp.zeros(acc_sc.shape, F32)

    key_io = lax.broadcasted_iota(jnp.int32, (tk, tq), 0)
    qry_io = lax.broadcasted_iota(jnp.int32, (tk, tq), 1)

    def step(j, masked):
        k0 = pl.multiple_of(j * tk, tk)
        sts = [lax.dot_general(kaug_sc[head, pl.ds(k0, tk), :], qaug_sc[head], NT_DIMS,
                               preferred_element_type=F32) for head in range(H_F)]
        for head in range(H_F):
            st = sts[head]
            if masked:
                st = jnp.where(key_io <= qry_io, st, NEG_BIG)
            m = m_sc[head]
            m_new = jnp.maximum(m, jnp.max(st, axis=0, keepdims=True))
            alpha = jnp.exp(m - m_new)
            pt = jnp.exp(st - m_new).astype(BF16)
            m_sc[head] = m_new
            acc_sc[head] = alpha * acc_sc[head] + jnp.dot(vt_sc[head, j], pt,
                                                          preferred_element_type=F32)

    def loop_body(j, carry):
        step(j, False)
        return carry

    lax.fori_loop(0, qi, loop_body, 0)
    step(qi, True)
    for p in range(H_F // 2):
        outs = []
        for head in (2 * p, 2 * p + 1):
            acc = acc_sc[head]
            outs.append(acc[0:half] * (1.0 / acc[half:half + 1]))
        yf_ref[:, p * dhp:(p + 1) * dhp] = jnp.concatenate(outs, axis=0).T.astype(BF16)


def _post_kernel(x_ref, ym_ref, yf_ref, zc_ref, wbm_ref, wbf_ref, wo_ref, g2_ref,
                 wg_ref, wu_ref, wd_ref, gfin_ref, o_ref, *, tf):
    d = x_ref.shape[1]
    d_ff = wg_ref.shape[1]
    bm = jnp.dot(ym_ref[...], wbm_ref[...], preferred_element_type=F32)
    bf = jnp.dot(yf_ref[...], wbf_ref[...], preferred_element_type=F32)
    mix = zc_ref[:, 0:d].astype(F32) * bm + zc_ref[:, d:2 * d].astype(F32) * bf
    x1 = x_ref[...] + jnp.dot(mix.astype(BF16), wo_ref[...], preferred_element_type=F32)
    h2 = _rms(x1, g2_ref[...]).astype(BF16)
    acc = jnp.zeros(x1.shape, F32)
    for f0 in range(0, d_ff, tf):
        g = jnp.dot(h2, wg_ref[:, f0:f0 + tf], preferred_element_type=F32)
        u = jnp.dot(h2, wu_ref[:, f0:f0 + tf], preferred_element_type=F32)
        act = (g * jax.nn.sigmoid(g) * u).astype(BF16)
        acc = acc + jnp.dot(act, wd_ref[f0:f0 + tf, :], preferred_element_type=F32)
    o_ref[...] = _rms(x1 + acc, gfin_ref[...])


def _bias_selectors():
    ones_lane = 3 * GATE_GROUP
    qsel = np.zeros((H_F, LANES, LANES), np.float32)
    ksel = np.zeros((H_F, LANES, LANES), np.float32)
    for h in range(H_F):
        p0 = LANES // 2 if h % 2 == 0 else 0
        for c in range(3):
            qsel[h, GATE_GROUP * c + h, p0 + c] = 1.0
            qsel[h, ones_lane, p0 + 3 + c] = 1.0
            ksel[h, ones_lane, p0 + c] = 1.0
            ksel[h, GATE_GROUP * c + h, p0 + 3 + c] = -1.0
    return jnp.asarray(qsel, BF16), jnp.asarray(ksel, BF16)


def _layer(x2d, batch, seq, p, cfg):
    t, d = x2d.shape
    w_m = p["w_br_mlstm"].shape[0]
    w_f = p["w_br_fox"].shape[0]
    dh_m = w_m // H_M
    dh_f = w_f // H_F
    vmem = cfg["vmem_limit"]

    o_mi = 4 * w_m
    o_mf = o_mi + H_M
    o_fq = o_mf + H_M
    o_ff = o_fq + 3 * w_f
    o_g = o_ff + H_F
    w_in, b_in = p["w_in"], p["b_in"]
    wa, ba = w_in[:, :o_mi].astype(BF16), b_in[None, :o_mi]
    wb, bb = w_in[:, o_fq:o_ff].astype(BF16), b_in[None, o_fq:o_ff]
    wc, bc = w_in[:, o_g:].astype(BF16), b_in[None, o_g:]
    wd = jnp.zeros((d, LANES), F32)
    bd = jnp.zeros((1, LANES), F32)
    for dst, src, n in ((0, o_mi, H_M), (GATE_GROUP, o_mf, H_M), (2 * GATE_GROUP, o_ff, H_F)):
        wd = wd.at[:, dst:dst + n].set(w_in[:, src:src + n])
        bd = bd.at[:, dst:dst + n].set(b_in[None, src:src + n])
    wd = wd.astype(BF16)

    tm = cfg["tm_in"]
    chunk = cfg["chunk"]
    nc = seq // chunk
    assert seq % tm == 0 and tm % chunk == 0
    zqk, vt, og, zb, zc, zdt = pl.pallas_call(
        functools.partial(_inproj_kernel, w_m=w_m, w_f=w_f, n_chunk=cfg["n_chunk"],
                          q_scale=dh_f ** -0.5, k_scale=dh_m ** -0.5, chunk=chunk,
                          tiles_per_seq=seq // tm),
        grid=(t // tm,),
        in_specs=[pl.BlockSpec((tm, d), lambda i: (i, 0)), _const_spec((1, d)),
                  _const_spec(wa.shape), _const_spec(ba.shape),
                  _const_spec(wb.shape), _const_spec(bb.shape),
                  _const_spec(wc.shape), _const_spec(bc.shape),
                  _const_spec(wd.shape), _const_spec(bd.shape),
                  _const_spec((CONV_K, 2 * w_m)), _const_spec((1, 2 * w_m))],
        out_specs=[pl.BlockSpec((tm, 2 * w_m), lambda i: (i, 0)),
                   pl.BlockSpec((tm // chunk, w_m, chunk), lambda i: (i, 0, 0)),
                   pl.BlockSpec((tm, w_m), lambda i: (i, 0)),
                   pl.BlockSpec((tm, 3 * w_f), lambda i: (i, 0)),
                   pl.BlockSpec((tm, 2 * d), lambda i: (i, 0)),
                   pl.BlockSpec((N_GATE_ROWS, tm), lambda i: (0, i))],
        out_shape=[jax.ShapeDtypeStruct((t, 2 * w_m), BF16),
                   jax.ShapeDtypeStruct((t // chunk, w_m, chunk), BF16),
                   jax.ShapeDtypeStruct((t, w_m), BF16),
                   jax.ShapeDtypeStruct((t, 3 * w_f), BF16),
                   jax.ShapeDtypeStruct((t, 2 * d), BF16),
                   jax.ShapeDtypeStruct((N_GATE_ROWS, t), F32)],
        scratch_shapes=[pltpu.VMEM((SUBLANES, 2 * w_m), F32)],
        compiler_params=pltpu.CompilerParams(dimension_semantics=("arbitrary",), vmem_limit_bytes=vmem),
        name="inproj",
    )(x2d, p["norm1_g"][None, :], wa, ba, wb, bb, wc, bc, wd, bd, p["conv_w"], p["conv_b"][None, :])

    rows, cols, csplit = pl.pallas_call(
        _gates_kernel,
        grid=(batch,),
        in_specs=[pl.BlockSpec((N_GATE_ROWS, seq), lambda b: (0, b))],
        out_specs=[pl.BlockSpec((1, 3 * GATE_GROUP, seq), lambda b: (b, 0, 0)),
                   pl.BlockSpec((seq, LANES), lambda b: (b, 0)),
                   pl.BlockSpec((seq, LANES), lambda b: (b, 0))],
        out_shape=[jax.ShapeDtypeStruct((batch, 3 * GATE_GROUP, seq), F32),
                   jax.ShapeDtypeStruct((t, LANES), F32),
                   jax.ShapeDtypeStruct((t, LANES), BF16)],
        compiler_params=pltpu.CompilerParams(dimension_semantics=("parallel",), vmem_limit_bytes=vmem),
        name="gates",
    )(zdt)

    rows_m = rows.reshape(batch, 3 * GATE_GROUP, nc, chunk)
    ym = pl.pallas_call(
        functools.partial(_mlstm_kernel, w_m=w_m, chunk=chunk),
        grid=(batch,),
        in_specs=[pl.BlockSpec((seq, 2 * w_m), lambda b: (b, 0)),
                  pl.BlockSpec((nc, w_m, chunk), lambda b: (b, 0, 0)),
                  pl.BlockSpec((seq, w_m), lambda b: (b, 0)),
                  pl.BlockSpec((1, 3 * GATE_GROUP, nc, chunk), lambda b: (b, 0, 0, 0)),
                  pl.BlockSpec((seq, LANES), lambda b: (b, 0)),
                  _const_spec((1, w_m))],
        out_specs=pl.BlockSpec((seq, w_m), lambda b: (b, 0)),
        out_shape=jax.ShapeDtypeStruct((t, w_m), BF16),
        scratch_shapes=[pltpu.VMEM((H_M, dh_m + BF16_ROWS, dh_m), F32)],
        compiler_params=pltpu.CompilerParams(dimension_semantics=("parallel",), vmem_limit_bytes=vmem),
        name="mlstm",
    )(zqk, vt, og, rows_m, cols, p["mlstm_norm_g"][None, :])

    tq, tk = cfg["tq"], cfg["tk"]
    assert tq == tk
    nq = seq // tq
    qsel, ksel = _bias_selectors()
    v_rows = LANES // 2 + BF16_ROWS
    yf = pl.pallas_call(
        functools.partial(_fox_kernel, tq=tq, tk=tk),
        grid=(batch, nq),
        in_specs=[pl.BlockSpec((tq, w_f), lambda b, i: (b * nq + i, 0)),
                  pl.BlockSpec((seq, w_f), lambda b, i: (b, 1)),
                  pl.BlockSpec((seq, w_f), lambda b, i: (b, 2)),
                  pl.BlockSpec((tq, LANES), lambda b, i: (b * nq + i, 0)),
                  pl.BlockSpec((seq, LANES), lambda b, i: (b, 0)),
                  _const_spec(qsel.shape), _const_spec(ksel.shape)],
        out_specs=pl.BlockSpec((tq, w_f), lambda b, i: (b * nq + i, 0)),
        out_shape=jax.ShapeDtypeStruct((t, w_f), BF16),
        scratch_shapes=[pltpu.VMEM((H_F, seq, LANES), BF16),
                        pltpu.VMEM((H_F, seq // tk, v_rows, tk), BF16),
                        pltpu.VMEM((H_F, tq, LANES), BF16),
                        pltpu.VMEM((H_F, 1, tq), F32),
                        pltpu.VMEM((H_F, v_rows, tq), F32)],
        compiler_params=pltpu.CompilerParams(dimension_semantics=("arbitrary", "arbitrary"),
                                             vmem_limit_bytes=vmem),
        name="fox",
    )(zb, zb, zb, csplit, csplit, qsel, ksel)

    tmp = cfg["tm_post"]
    wbm, wbf = p["w_br_mlstm"].astype(BF16), p["w_br_fox"].astype(BF16)
    wo = p["w_out"].astype(BF16)
    wg, wu, wdn = p["w_gate"].astype(BF16), p["w_up"].astype(BF16), p["w_down"].astype(BF16)
    return pl.pallas_call(
        functools.partial(_post_kernel, tf=cfg["tf"]),
        grid=(t // tmp,),
        in_specs=[pl.BlockSpec((tmp, d), lambda i: (i, 0)),
                  pl.BlockSpec((tmp, w_m), lambda i: (i, 0)),
                  pl.BlockSpec((tmp, w_f), lambda i: (i, 0)),
                  pl.BlockSpec((tmp, 2 * d), lambda i: (i, 0)),
                  _const_spec(wbm.shape), _const_spec(wbf.shape), _const_spec(wo.shape),
                  _const_spec((1, d)), _const_spec(wg.shape), _const_spec(wu.shape),
                  _const_spec(wdn.shape), _const_spec((1, d))],
        out_specs=pl.BlockSpec((tmp, d), lambda i: (i, 0)),
        out_shape=jax.ShapeDtypeStruct((t, d), F32),
        compiler_params=pltpu.CompilerParams(dimension_semantics=("parallel",), vmem_limit_bytes=vmem),
        name="post",
    )(x2d, ym, yf, zc, wbm, wbf, wo, p["norm2_g"][None, :], wg, wu, wdn, p["norm_f_g"][None, :])


def kernel(x, norm1_g, w_in, b_in, conv_w, conv_b, mlstm_norm_g, w_br_mlstm, w_br_fox, w_out,
           norm2_g, w_gate, w_up, w_down, norm_f_g):
    batch, seq, d = x.shape
    depth = w_in.shape[0]
    assert depth == 1, "the final norm is fused into the single layer's last call"
    cfg = _cfg(batch, seq, d, w_gate.shape[-1])
    p = dict(norm1_g=norm1_g[0], w_in=w_in[0], b_in=b_in[0], conv_w=conv_w[0], conv_b=conv_b[0],
             mlstm_norm_g=mlstm_norm_g[0], w_br_mlstm=w_br_mlstm[0], w_br_fox=w_br_fox[0],
             w_out=w_out[0], norm2_g=norm2_g[0], w_gate=w_gate[0], w_up=w_up[0], w_down=w_down[0],
             norm_f_g=norm_f_g)
    out = _layer(x.reshape(batch * seq, d), batch, seq, p, cfg)
    return out.reshape(batch, seq, d)
```

```python
import functools

import jax
import jax.numpy as jnp
import numpy as np
from jax import lax
from jax.experimental import pallas as pl
from jax.experimental.pallas import tpu as pltpu

EPS = 1e-6
H_M = 4
H_F = 8
CONV_K = 4

LANES = 128
SUBLANES = 8
BF16_ROWS = 16
GATE_GROUP = 8
N_GATE_ROWS = 3 * GATE_GROUP
V7X_VMEM_BYTES = 64 * 1024 * 1024
NEG_BIG = -1e30
LOG2E = 1.4426950408889634

F32 = jnp.float32
BF16 = jnp.bfloat16
NT_DIMS = (((1,), (1,)), ((), ()))


def _cfg(batch, seq, d_model, d_ff):
    return dict(
        tm_in=512,
        n_chunk=256,
        chunk=256,
        tq=512,
        tk=256,
        tm_post=512,
        tf=256,
        vmem_limit=V7X_VMEM_BYTES - 8 * 1024 * 1024,
    )


def _const_spec(shape):
    nd = len(shape)
    return pl.BlockSpec(shape, lambda *_: (0,) * nd, pipeline_mode=pl.Buffered(1))


def _rms(x, g):
    return x * lax.rsqrt(jnp.mean(x * x, axis=-1, keepdims=True) + EPS) * g


def _log_sigmoid(x):
    return jnp.minimum(x, 0.0) - jnp.log1p(jnp.exp(-jnp.abs(x)))


def _inproj_kernel(x_ref, g_ref, wa_ref, ba_ref, wb_ref, bb_ref, wc_ref, bc_ref, wd_ref, bd_ref,
                   cw_ref, cb_ref, zqk_ref, vt_ref, og_ref, zb_ref, zc_ref, zdt_ref, zs_sc,
                   *, w_m, w_f, n_chunk, q_scale, k_scale, chunk, tiles_per_seq):
    tm = x_ref.shape[0]
    seq_start = (pl.program_id(0) % tiles_per_seq) == 0
    hb = _rms(x_ref[...], g_ref[...]).astype(BF16)

    @pl.when(seq_start)
    def _():
        zs_sc[:, 0:SUBLANES, :] = jnp.zeros((zs_sc.shape[0], SUBLANES, n_chunk), F32)

    @pl.when(jnp.logical_not(seq_start))
    def _():
        zs_sc[:, 0:SUBLANES, :] = zs_sc[:, tm:tm + SUBLANES, :]

    def proj(w_ref, b_ref, c0, c1):
        return jnp.dot(hb, w_ref[:, c0:c1], preferred_element_type=F32) + b_ref[:, c0:c1]

    for ci, c0 in enumerate(range(0, 2 * w_m, n_chunk)):
        cs = slice(c0, c0 + n_chunk)
        z = proj(wa_ref, ba_ref, c0, c0 + n_chunk)
        zs = zs_sc.at[ci]
        zs[SUBLANES:tm + SUBLANES] = z
        w = cw_ref[:, cs]
        y = z * w[CONV_K - 1:CONV_K] + cb_ref[:, cs]
        for s in range(1, CONV_K):
            y = y + zs[SUBLANES - s:SUBLANES - s + tm] * w[CONV_K - 1 - s:CONV_K - s]
        act = y * jax.nn.sigmoid(y)
        if c0 >= w_m:
            act = act * k_scale
        zqk_ref[:, cs] = act.astype(BF16)
    for c0 in range(2 * w_m, 3 * w_m, n_chunk):
        z = proj(wa_ref, ba_ref, c0, c0 + n_chunk)
        for cc in range(tm // chunk):
            for f0 in range(0, n_chunk, LANES):
                blk = z[cc * chunk:(cc + 1) * chunk, f0:f0 + LANES]
                r0 = c0 - 2 * w_m + f0
                vt_ref[cc, r0:r0 + LANES, :] = blk.T.astype(BF16)
    for c0 in range(3 * w_m, 4 * w_m, n_chunk):
        z = proj(wa_ref, ba_ref, c0, c0 + n_chunk)
        og_ref[:, c0 - 3 * w_m:c0 - 3 * w_m + n_chunk] = jax.nn.sigmoid(z).astype(BF16)
    for c0 in range(0, 3 * w_f, n_chunk):
        z = proj(wb_ref, bb_ref, c0, c0 + n_chunk)
        if c0 < w_f:
            z = z * q_scale
        zb_ref[:, c0:c0 + n_chunk] = z.astype(BF16)
    for c0 in range(0, wc_ref.shape[1], n_chunk):
        z = proj(wc_ref, bc_ref, c0, c0 + n_chunk)
        zc_ref[:, c0:c0 + n_chunk] = jax.nn.sigmoid(z).astype(BF16)
    zd = proj(wd_ref, bd_ref, 0, LANES)
    zdt_ref[...] = zd.T[:N_GATE_ROWS, :]


def _scan_lanes(x, op, fill):
    n = x.shape[-1]
    lane = lax.broadcasted_iota(jnp.int32, x.shape, 1)
    s = 1
    while s < n:
        shifted = pltpu.roll(x, s, axis=1)
        x = op(x, jnp.where(lane >= s, shifted, fill))
        s *= 2
    return x


def _gates_kernel(zdt_ref, rows_ref, cols_ref, csplit_ref):
    z = zdt_ref[...]
    seq = z.shape[1]
    i8 = z[0:GATE_GROUP]
    f8 = _scan_lanes(_log_sigmoid(z[GATE_GROUP:2 * GATE_GROUP]), jnp.add, 0.0)
    cf8 = _scan_lanes(_log_sigmoid(z[2 * GATE_GROUP:3 * GATE_GROUP]), jnp.add, 0.0) * LOG2E
    g8 = i8 - f8
    m8 = jnp.maximum(_scan_lanes(g8, jnp.maximum, NEG_BIG), 0.0)
    en8 = jnp.exp(-(f8 + m8))
    rows_ref[0] = jnp.concatenate([g8, m8, en8], axis=0)
    stack = jnp.concatenate(
        [g8, m8, jnp.zeros((LANES - 2 * GATE_GROUP, seq), F32)], axis=0)
    cols_ref[...] = stack.T
    hi = cf8.astype(BF16).astype(F32)
    r1 = cf8 - hi
    lo = r1.astype(BF16).astype(F32)
    lo2 = (r1 - lo).astype(BF16).astype(F32)
    ones = jnp.where(lax.broadcasted_iota(jnp.int32, (GATE_GROUP, seq), 0) == 0, 1.0, 0.0)
    split = jnp.concatenate(
        [hi, lo, lo2, ones, jnp.zeros((LANES - 4 * GATE_GROUP, seq), F32)], axis=0)
    csplit_ref[...] = split.T.astype(BF16)


def _mlstm_kernel(zqk_ref, vt_ref, og_ref, rows_ref, cols_ref, ng_ref, ym_ref, ct_sc, *, w_m, chunk):
    seq = zqk_ref.shape[0]
    dh = w_m // H_M
    nc = seq // chunk
    ct_sc[...] = jnp.zeros(ct_sc.shape, F32)
    causal = (lax.broadcasted_iota(jnp.int32, (chunk, chunk), 0)
              <= lax.broadcasted_iota(jnp.int32, (chunk, chunk), 1))
    ones_rows = jnp.where(
        lax.broadcasted_iota(jnp.int32, (BF16_ROWS, chunk), 0) == 0, 1.0, 0.0).astype(BF16)

    def body(c, carry):
        r0 = pl.multiple_of(c * chunk, chunk)
        colsc = cols_ref[pl.ds(r0, chunk), :]
        last = cols_ref[pl.ds(r0 + chunk - 1, 1), :]
        prev = cols_ref[pl.ds(jnp.maximum(r0 - 1, 0), 1), :]
        prev = jnp.where(c > 0, prev, 0.0)
        first, vas = [], []
        for h in range(H_M):
            qc = zqk_ref[pl.ds(r0, chunk), h * dh:(h + 1) * dh]
            kc = zqk_ref[pl.ds(r0, chunk), w_m + h * dh:w_m + (h + 1) * dh]
            lhs = jnp.concatenate([kc, ct_sc[h].astype(BF16)], axis=0)
            first.append(lax.dot_general(lhs, qc, NT_DIMS, preferred_element_type=F32))
            vas.append(jnp.concatenate([vt_ref[c, h * dh:(h + 1) * dh, :], ones_rows], axis=0))
        for h in range(H_M):
            kc = zqk_ref[pl.ds(r0, chunk), w_m + h * dh:w_m + (h + 1) * dh]
            g_row = rows_ref[0, h, pl.ds(c, 1), :]
            m_e = last[:, GATE_GROUP + h:GATE_GROUP + h + 1]
            m_p = prev[:, GATE_GROUP + h:GATE_GROUP + h + 1]
            vaw = (vas[h].astype(F32) * jnp.exp(g_row - m_e)).astype(BF16)
            ct_sc[h] = jnp.exp(m_p - m_e) * ct_sc[h] + jnp.dot(vaw, kc, preferred_element_type=F32)
        for h in range(H_M):
            ch = slice(h * dh, (h + 1) * dh)
            g_col = colsc[:, h:h + 1]
            m_row = rows_ref[0, GATE_GROUP + h, pl.ds(c, 1), :]
            en_row = rows_ref[0, 2 * GATE_GROUP + h, pl.ds(c, 1), :]
            m_p = prev[:, GATE_GROUP + h:GATE_GROUP + h + 1]
            dmat = jnp.where(causal, jnp.exp(g_col - m_row), 0.0)
            sqk = (first[h][0:chunk] * dmat).astype(BF16)
            nd = (jnp.exp(m_p - m_row) * first[h][chunk:]
                  + jnp.dot(vas[h], sqk, preferred_element_type=F32))
            den = nd[dh:dh + 1]
            ht = nd[0:dh] * (1.0 / jnp.maximum(jnp.abs(den), en_row))
            hn = ht * lax.rsqrt(jnp.mean(ht * ht, axis=0, keepdims=True) + EPS)
            og = og_ref[pl.ds(r0, chunk), ch].astype(F32)
            ym_ref[pl.ds(r0, chunk), ch] = (hn.T * ng_ref[:, ch] * og).astype(BF16)
        return carry

    lax.fori_loop(0, nc, body, 0)


def _fox_kernel(q_ref, k_ref, v_ref, csq_ref, csk_ref, qsel_ref, ksel_ref, yf_ref,
                kaug_sc, vt_sc, qaug_sc, m_sc, acc_sc, *, tq, tk):
    qi = pl.program_id(1)

    seq = k_ref.shape[0]
    dhp = LANES
    half = dhp // 2
    v_rows = half + BF16_ROWS

    def own_lanes(head, rows):
        lane = lax.broadcasted_iota(jnp.int32, (rows, dhp), 1)
        return (lane >= half) if head % 2 else (lane < half)

    @pl.when(qi == 0)
    def _():
        csk = csk_ref[...]
        ones_rows = jnp.where(
            lax.broadcasted_iota(jnp.int32, (BF16_ROWS, tk), 0) == 0, 1.0, 0.0).astype(BF16)
        for p in range(H_F // 2):
            cp = slice(p * dhp, (p + 1) * dhp)
            kp = k_ref[:, cp]
            vt = v_ref[:, cp].astype(F32).T
            for head in (2 * p, 2 * p + 1):
                kb = jnp.dot(csk, ksel_ref[head], preferred_element_type=F32).astype(BF16)
                kaug_sc[head] = jnp.where(own_lanes(head, seq), kp, kb)
                r0 = (head % 2) * half
                for jb in range(seq // tk):
                    vt_sc[head, jb, 0:half, :] = vt[r0:r0 + half, jb * tk:(jb + 1) * tk].astype(BF16)
                    vt_sc[head, jb, half:v_rows, :] = ones_rows

    csq = csq_ref[...]
    for head in range(H_F):
        qp = q_ref[:, (head // 2) * dhp:(head // 2 + 1) * dhp]
        qb = jnp.dot(csq, qsel_ref[head], preferred_element_type=F32).astype(BF16)
        qaug_sc[head] = jnp.where(own_lanes(head, tq), qp, qb)
    m_sc[...] = jnp.full(m_sc.shape, NEG_BIG, F32)
    acc_sc[...] = jnp.zeros(acc_sc.shape, F32)

    lead_io = (lax.broadcasted_iota(jnp.int32, (tk, tq), 1)
               - lax.broadcasted_iota(jnp.int32, (tk, tq), 0))
    ratio = tq // tk

    def step(j, diag):
        k0 = pl.multiple_of(j * tk, tk)
        sts = [lax.dot_general(kaug_sc[head, pl.ds(k0, tk), :], qaug_sc[head], NT_DIMS,
                               preferred_element_type=F32) for head in range(H_F)]
        for head in range(H_F):
            st = sts[head]
            if diag is not None:
                st = jnp.where(lead_io >= diag * tk, st, NEG_BIG)
            m = m_sc[head]
            m_new = jnp.maximum(m, jnp.max(st, axis=0, keepdims=True))
            alpha = jnp.exp2(m - m_new)
            pt = jnp.exp2(st - m_new).astype(BF16)
            m_sc[head] = m_new
            acc_sc[head] = alpha * acc_sc[head] + jnp.dot(vt_sc[head, j], pt,
                                                          preferred_element_type=F32)

    def loop_body(j, carry):
        step(j, None)
        return carry

    lax.fori_loop(0, qi * ratio, loop_body, 0)
    for diag in range(ratio):
        step(qi * ratio + diag, diag)
    for p in range(H_F // 2):
        outs = []
        for head in (2 * p, 2 * p + 1):
            acc = acc_sc[head]
            outs.append(acc[0:half] * (1.0 / acc[half:half + 1]))
        yf_ref[:, p * dhp:(p + 1) * dhp] = jnp.concatenate(outs, axis=0).T.astype(BF16)


def _post_kernel(x_ref, ym_ref, yf_ref, zc_ref, wbm_ref, wbf_ref, wo_ref, g2_ref,
                 wg_ref, wu_ref, wd_ref, gfin_ref, o_ref, *, tf):
    d = x_ref.shape[1]
    d_ff = wg_ref.shape[1]
    bm = jnp.dot(ym_ref[...], wbm_ref[...], preferred_element_type=F32)
    bf = jnp.dot(yf_ref[...], wbf_ref[...], preferred_element_type=F32)
    mix = zc_ref[:, 0:d].astype(F32) * bm + zc_ref[:, d:2 * d].astype(F32) * bf
    x1 = x_ref[...] + jnp.dot(mix.astype(BF16), wo_ref[...], preferred_element_type=F32)
    h2 = _rms(x1, g2_ref[...]).astype(BF16)
    acc = jnp.zeros(x1.shape, F32)
    for f0 in range(0, d_ff, tf):
        g = jnp.dot(h2, wg_ref[:, f0:f0 + tf], preferred_element_type=F32)
        u = jnp.dot(h2, wu_ref[:, f0:f0 + tf], preferred_element_type=F32)
        act = (g * jax.nn.sigmoid(g) * u).astype(BF16)
        acc = acc + jnp.dot(act, wd_ref[f0:f0 + tf, :], preferred_element_type=F32)
    o_ref[...] = _rms(x1 + acc, gfin_ref[...])


def _bias_selectors():
    ones_lane = 3 * GATE_GROUP
    qsel = np.zeros((H_F, LANES, LANES), np.float32)
    ksel = np.zeros((H_F, LANES, LANES), np.float32)
    for h in range(H_F):
        p0 = LANES // 2 if h % 2 == 0 else 0
        for c in range(3):
            qsel[h, GATE_GROUP * c + h, p0 + c] = 1.0
            qsel[h, ones_lane, p0 + 3 + c] = 1.0
            ksel[h, ones_lane, p0 + c] = 1.0
            ksel[h, GATE_GROUP * c + h, p0 + 3 + c] = -1.0
    return jnp.asarray(qsel, BF16), jnp.asarray(ksel, BF16)


def _layer(x2d, batch, seq, p, cfg):
    t, d = x2d.shape
    w_m = p["w_br_mlstm"].shape[0]
    w_f = p["w_br_fox"].shape[0]
    dh_m = w_m // H_M
    dh_f = w_f // H_F
    vmem = cfg["vmem_limit"]

    o_mi = 4 * w_m
    o_mf = o_mi + H_M
    o_fq = o_mf + H_M
    o_ff = o_fq + 3 * w_f
    o_g = o_ff + H_F
    w_in, b_in = p["w_in"], p["b_in"]
    wa, ba = w_in[:, :o_mi].astype(BF16), b_in[None, :o_mi]
    wb, bb = w_in[:, o_fq:o_ff].astype(BF16), b_in[None, o_fq:o_ff]
    wc, bc = w_in[:, o_g:].astype(BF16), b_in[None, o_g:]
    wd = jnp.zeros((d, LANES), F32)
    bd = jnp.zeros((1, LANES), F32)
    for dst, src, n in ((0, o_mi, H_M), (GATE_GROUP, o_mf, H_M), (2 * GATE_GROUP, o_ff, H_F)):
        wd = wd.at[:, dst:dst + n].set(w_in[:, src:src + n])
        bd = bd.at[:, dst:dst + n].set(b_in[None, src:src + n])
    wd = wd.astype(BF16)

    tm = cfg["tm_in"]
    chunk = cfg["chunk"]
    nc = seq // chunk
    assert seq % tm == 0 and tm % chunk == 0
    zqk, vt, og, zb, zc, zdt = pl.pallas_call(
        functools.partial(_inproj_kernel, w_m=w_m, w_f=w_f, n_chunk=cfg["n_chunk"],
                          q_scale=dh_f ** -0.5 * LOG2E, k_scale=dh_m ** -0.5, chunk=chunk,
                          tiles_per_seq=seq // tm),
        grid=(t // tm,),
        in_specs=[pl.BlockSpec((tm, d), lambda i: (i, 0)), _const_spec((1, d)),
                  _const_spec(wa.shape), _const_spec(ba.shape),
                  _const_spec(wb.shape), _const_spec(bb.shape),
                  _const_spec(wc.shape), _const_spec(bc.shape),
                  _const_spec(wd.shape), _const_spec(bd.shape),
                  _const_spec((CONV_K, 2 * w_m)), _const_spec((1, 2 * w_m))],
        out_specs=[pl.BlockSpec((tm, 2 * w_m), lambda i: (i, 0)),
                   pl.BlockSpec((tm // chunk, w_m, chunk), lambda i: (i, 0, 0)),
                   pl.BlockSpec((tm, w_m), lambda i: (i, 0)),
                   pl.BlockSpec((tm, 3 * w_f), lambda i: (i, 0)),
                   pl.BlockSpec((tm, 2 * d), lambda i: (i, 0)),
                   pl.BlockSpec((N_GATE_ROWS, tm), lambda i: (0, i))],
        out_shape=[jax.ShapeDtypeStruct((t, 2 * w_m), BF16),
                   jax.ShapeDtypeStruct((t // chunk, w_m, chunk), BF16),
                   jax.ShapeDtypeStruct((t, w_m), BF16),
                   jax.ShapeDtypeStruct((t, 3 * w_f), BF16),
                   jax.ShapeDtypeStruct((t, 2 * d), BF16),
                   jax.ShapeDtypeStruct((N_GATE_ROWS, t), F32)],
        scratch_shapes=[pltpu.VMEM((2 * w_m // cfg["n_chunk"], tm + SUBLANES, cfg["n_chunk"]), F32)],
        compiler_params=pltpu.CompilerParams(dimension_semantics=("arbitrary",), vmem_limit_bytes=vmem),
        name="inproj",
    )(x2d, p["norm1_g"][None, :], wa, ba, wb, bb, wc, bc, wd, bd, p["conv_w"], p["conv_b"][None, :])

    rows, cols, csplit = pl.pallas_call(
        _gates_kernel,
        grid=(batch,),
        in_specs=[pl.BlockSpec((N_GATE_ROWS, seq), lambda b: (0, b))],
        out_specs=[pl.BlockSpec((1, 3 * GATE_GROUP, seq), lambda b: (b, 0, 0)),
                   pl.BlockSpec((seq, LANES), lambda b: (b, 0)),
                   pl.BlockSpec((seq, LANES), lambda b: (b, 0))],
        out_shape=[jax.ShapeDtypeStruct((batch, 3 * GATE_GROUP, seq), F32),
                   jax.ShapeDtypeStruct((t, LANES), F32),
                   jax.ShapeDtypeStruct((t, LANES), BF16)],
        compiler_params=pltpu.CompilerParams(dimension_semantics=("parallel",), vmem_limit_bytes=vmem),
        name="gates",
    )(zdt)

    rows_m = rows.reshape(batch, 3 * GATE_GROUP, nc, chunk)
    ym = pl.pallas_call(
        functools.partial(_mlstm_kernel, w_m=w_m, chunk=chunk),
        grid=(batch,),
        in_specs=[pl.BlockSpec((seq, 2 * w_m), lambda b: (b, 0)),
                  pl.BlockSpec((nc, w_m, chunk), lambda b: (b, 0, 0)),
                  pl.BlockSpec((seq, w_m), lambda b: (b, 0)),
                  pl.BlockSpec((1, 3 * GATE_GROUP, nc, chunk), lambda b: (b, 0, 0, 0)),
                  pl.BlockSpec((seq, LANES), lambda b: (b, 0)),
                  _const_spec((1, w_m))],
        out_specs=pl.BlockSpec((seq, w_m), lambda b: (b, 0)),
        out_shape=jax.ShapeDtypeStruct((t, w_m), BF16),
        scratch_shapes=[pltpu.VMEM((H_M, dh_m + BF16_ROWS, dh_m), F32)],
        compiler_params=pltpu.CompilerParams(dimension_semantics=("parallel",), vmem_limit_bytes=vmem),
        name="mlstm",
    )(zqk, vt, og, rows_m, cols, p["mlstm_norm_g"][None, :])

    tq, tk = cfg["tq"], cfg["tk"]
    assert tq % tk == 0 and seq % tq == 0
    nq = seq // tq
    qsel, ksel = _bias_selectors()
    v_rows = LANES // 2 + BF16_ROWS
    yf = pl.pallas_call(
        functools.partial(_fox_kernel, tq=tq, tk=tk),
        grid=(batch, nq),
        in_specs=[pl.BlockSpec((tq, w_f), lambda b, i: (b * nq + i, 0)),
                  pl.BlockSpec((seq, w_f), lambda b, i: (b, 1)),
                  pl.BlockSpec((seq, w_f), lambda b, i: (b, 2)),
                  pl.BlockSpec((tq, LANES), lambda b, i: (b * nq + i, 0)),
                  pl.BlockSpec((seq, LANES), lambda b, i: (b, 0)),
                  _const_spec(qsel.shape), _const_spec(ksel.shape)],
        out_specs=pl.BlockSpec((tq, w_f), lambda b, i: (b * nq + i, 0)),
        out_shape=jax.ShapeDtypeStruct((t, w_f), BF16),
        scratch_shapes=[pltpu.VMEM((H_F, seq, LANES), BF16),
                        pltpu.VMEM((H_F, seq // tk, v_rows, tk), BF16),
                        pltpu.VMEM((H_F, tq, LANES), BF16),
                        pltpu.VMEM((H_F, 1, tq), F32),
                        pltpu.VMEM((H_F, v_rows, tq), F32)],
        compiler_params=pltpu.CompilerParams(dimension_semantics=("arbitrary", "arbitrary"),
                                             vmem_limit_bytes=vmem),
        name="fox",
    )(zb, zb, zb, csplit, csplit, qsel, ksel)

    tmp = cfg["tm_post"]
    wbm, wbf = p["w_br_mlstm"].astype(BF16), p["w_br_fox"].astype(BF16)
    wo = p["w_out"].astype(BF16)
    wg, wu, wdn = p["w_gate"].astype(BF16), p["w_up"].astype(BF16), p["w_down"].astype(BF16)
    return pl.pallas_call(
        functools.partial(_post_kernel, tf=cfg["tf"]),
        grid=(t // tmp,),
        in_specs=[pl.BlockSpec((tmp, d), lambda i: (i, 0)),
                  pl.BlockSpec((tmp, w_m), lambda i: (i, 0)),
                  pl.BlockSpec((tmp, w_f), lambda i: (i, 0)),
                  pl.BlockSpec((tmp, 2 * d), lambda i: (i, 0)),
                  _const_spec(wbm.shape), _const_spec(wbf.shape), _const_spec(wo.shape),
                  _const_spec((1, d)), _const_spec(wg.shape), _const_spec(wu.shape),
                  _const_spec(wdn.shape), _const_spec((1, d))],
        out_specs=pl.BlockSpec((tmp, d), lambda i: (i, 0)),
        out_shape=jax.ShapeDtypeStruct((t, d), F32),
        compiler_params=pltpu.CompilerParams(dimension_semantics=("parallel",), vmem_limit_bytes=vmem),
        name="post",
    )(x2d, ym, yf, zc, wbm, wbf, wo, p["norm2_g"][None, :], wg, wu, wdn, p["norm_f_g"][None, :])


def kernel(x, norm1_g, w_in, b_in, conv_w, conv_b, mlstm_norm_g, w_br_mlstm, w_br_fox, w_out,
           norm2_g, w_gate, w_up, w_down, norm_f_g):
    batch, seq, d = x.shape
    depth = w_in.shape[0]
    assert depth == 1, "the final norm is fused into the single layer's last call"
    cfg = _cfg(batch, seq, d, w_gate.shape[-1])
    p = dict(norm1_g=norm1_g[0], w_in=w_in[0], b_in=b_in[0], conv_w=conv_w[0], conv_b=conv_b[0],
             mlstm_norm_g=mlstm_norm_g[0], w_br_mlstm=w_br_mlstm[0], w_br_fox=w_br_fox[0],
             w_out=w_out[0], norm2_g=norm2_g[0], w_gate=w_gate[0], w_up=w_up[0], w_down=w_down[0],
             norm_f_g=norm_f_g)
    out = _layer(x.reshape(batch * seq, d), batch, seq, p, cfg)
    return out.reshape(batch, seq, d)
```

```python
import functools

import jax
import jax.numpy as jnp
import numpy as np
from jax import lax
from jax.experimental import pallas as pl
from jax.experimental.pallas import tpu as pltpu

EPS = 1e-6
H_M = 4
H_F = 8
CONV_K = 4

LANES = 128
SUBLANES = 8
BF16_ROWS = 16
GATE_GROUP = 8
N_GATE_ROWS = 3 * GATE_GROUP
V7X_VMEM_BYTES = 64 * 1024 * 1024
NEG_BIG = -1e30
LOG2E = 1.4426950408889634

F32 = jnp.float32
BF16 = jnp.bfloat16
NT_DIMS = (((1,), (1,)), ((), ()))


def _cfg(batch, seq, d_model, d_ff):
    return dict(
        tm_in=512,
        n_chunk=256,
        chunk=256,
        tq=512,
        tk=256,
        tm_post=512,
        tf=256,
        vmem_limit=V7X_VMEM_BYTES - 8 * 1024 * 1024,
    )


def _const_spec(shape):
    nd = len(shape)
    return pl.BlockSpec(shape, lambda *_: (0,) * nd, pipeline_mode=pl.Buffered(1))


def _rms(x, g):
    return x * lax.rsqrt(jnp.mean(x * x, axis=-1, keepdims=True) + EPS) * g


def _log_sigmoid(x):
    return jnp.minimum(x, 0.0) - jnp.log1p(jnp.exp(-jnp.abs(x)))


def _inproj_kernel(x_ref, g_ref, wa_ref, ba_ref, wb_ref, bb_ref, wc_ref, bc_ref, wd_ref, bd_ref,
                   cw_ref, cb_ref, zqk_ref, vt_ref, og_ref, zb_ref, zc_ref, zdt_ref, zs_sc,
                   *, w_m, w_f, n_chunk, q_scale, k_scale, chunk, tiles_per_seq):
    tm = x_ref.shape[0]
    seq_start = (pl.program_id(0) % tiles_per_seq) == 0
    hb = _rms(x_ref[...], g_ref[...]).astype(BF16)

    @pl.when(seq_start)
    def _():
        zs_sc[:, 0:SUBLANES, :] = jnp.zeros((zs_sc.shape[0], SUBLANES, n_chunk), F32)

    @pl.when(jnp.logical_not(seq_start))
    def _():
        zs_sc[:, 0:SUBLANES, :] = zs_sc[:, tm:tm + SUBLANES, :]

    def proj(w_ref, b_ref, c0, c1):
        return jnp.dot(hb, w_ref[:, c0:c1], preferred_element_type=F32) + b_ref[:, c0:c1]

    def qk_chunk(ci):
        c0 = ci * n_chunk
        cs = slice(c0, c0 + n_chunk)
        zs_sc[ci, SUBLANES:tm + SUBLANES, :] = proj(wa_ref, ba_ref, c0, c0 + n_chunk)
        w = cw_ref[:, cs]
        y = cb_ref[:, cs]
        for s in range(CONV_K):
            y = y + zs_sc[ci, SUBLANES - s:SUBLANES - s + tm, :] * w[CONV_K - 1 - s:CONV_K - s]
        act = y * jax.nn.sigmoid(y)
        if c0 >= w_m:
            act = act * k_scale
        zqk_ref[:, cs] = act.astype(BF16)

    def v_chunk(i):
        c0 = 2 * w_m + i * n_chunk
        z = proj(wa_ref, ba_ref, c0, c0 + n_chunk)
        for cc in range(tm // chunk):
            for f0 in range(0, n_chunk, LANES):
                blk = z[cc * chunk:(cc + 1) * chunk, f0:f0 + LANES]
                r0 = i * n_chunk + f0
                vt_ref[cc, r0:r0 + LANES, :] = blk.T.astype(BF16)

    def og_chunk(i):
        c0 = 3 * w_m + i * n_chunk
        z = proj(wa_ref, ba_ref, c0, c0 + n_chunk)
        og_ref[:, i * n_chunk:(i + 1) * n_chunk] = jax.nn.sigmoid(z).astype(BF16)

    def b_chunk(i):
        c0 = i * n_chunk
        z = proj(wb_ref, bb_ref, c0, c0 + n_chunk)
        if c0 < w_f:
            z = z * q_scale
        zb_ref[:, c0:c0 + n_chunk] = z.astype(BF16)

    def c_chunk(i):
        c0 = i * n_chunk
        z = proj(wc_ref, bc_ref, c0, c0 + n_chunk)
        zc_ref[:, c0:c0 + n_chunk] = jax.nn.sigmoid(z).astype(BF16)

    def d_chunk(_):
        zd = proj(wd_ref, bd_ref, 0, LANES)
        zdt_ref[...] = zd.T[:N_GATE_ROWS, :]

    heavy = [(qk_chunk, i) for i in range(2 * w_m // n_chunk)]
    light = ([(c_chunk, i) for i in range(wc_ref.shape[1] // n_chunk)]
             + [(b_chunk, i) for i in range(3 * w_f // n_chunk)]
             + [(v_chunk, i) for i in range(w_m // n_chunk)]
             + [(og_chunk, i) for i in range(w_m // n_chunk)] + [(d_chunk, 0)])
    per_heavy = len(light) // len(heavy)
    order = []
    for hi, task in enumerate(heavy):
        order.append(task)
        order.extend(light[hi * per_heavy:(hi + 1) * per_heavy])
    order.extend(light[len(heavy) * per_heavy:])
    for fn, i in order:
        fn(i)


def _scan_lanes(x, op, fill):
    n = x.shape[-1]
    lane = lax.broadcasted_iota(jnp.int32, x.shape, 1)
    s = 1
    while s < n:
        shifted = pltpu.roll(x, s, axis=1)
        x = op(x, jnp.where(lane >= s, shifted, fill))
        s *= 2
    return x


def _gates_kernel(zdt_ref, rows_ref, cols_ref, csplit_ref):
    z = zdt_ref[...]
    seq = z.shape[1]
    i8 = z[0:GATE_GROUP]
    f8 = _scan_lanes(_log_sigmoid(z[GATE_GROUP:2 * GATE_GROUP]), jnp.add, 0.0)
    cf8 = _scan_lanes(_log_sigmoid(z[2 * GATE_GROUP:3 * GATE_GROUP]), jnp.add, 0.0) * LOG2E
    g8 = i8 - f8
    m8 = jnp.maximum(_scan_lanes(g8, jnp.maximum, NEG_BIG), 0.0)
    en8 = jnp.exp(-(f8 + m8))
    rows_ref[0] = jnp.concatenate([g8, m8, en8], axis=0)
    stack = jnp.concatenate(
        [g8, m8, jnp.zeros((LANES - 2 * GATE_GROUP, seq), F32)], axis=0)
    cols_ref[...] = stack.T
    hi = cf8.astype(BF16).astype(F32)
    r1 = cf8 - hi
    lo = r1.astype(BF16).astype(F32)
    lo2 = (r1 - lo).astype(BF16).astype(F32)
    ones = jnp.where(lax.broadcasted_iota(jnp.int32, (GATE_GROUP, seq), 0) == 0, 1.0, 0.0)
    split = jnp.concatenate(
        [hi, lo, lo2, ones, jnp.zeros((LANES - 4 * GATE_GROUP, seq), F32)], axis=0)
    csplit_ref[...] = split.T.astype(BF16)


def _mlstm_kernel(zqk_ref, vt_ref, og_ref, rows_ref, cols_ref, ng_ref, ym_ref, ct_sc, *, w_m, chunk):
    seq = zqk_ref.shape[0]
    dh = w_m // H_M
    nc = seq // chunk
    ct_sc[...] = jnp.zeros(ct_sc.shape, F32)
    causal = (lax.broadcasted_iota(jnp.int32, (chunk, chunk), 0)
              <= lax.broadcasted_iota(jnp.int32, (chunk, chunk), 1))
    ones_rows = jnp.where(
        lax.broadcasted_iota(jnp.int32, (BF16_ROWS, chunk), 0) == 0, 1.0, 0.0).astype(BF16)

    def body(c, carry):
        r0 = pl.multiple_of(c * chunk, chunk)
        colsc = cols_ref[pl.ds(r0, chunk), :]
        last = cols_ref[pl.ds(r0 + chunk - 1, 1), :]
        prev = cols_ref[pl.ds(jnp.maximum(r0 - 1, 0), 1), :]
        prev = jnp.where(c > 0, prev, 0.0)
        first, vas = [], []
        for h in range(H_M):
            qc = zqk_ref[pl.ds(r0, chunk), h * dh:(h + 1) * dh]
            kc = zqk_ref[pl.ds(r0, chunk), w_m + h * dh:w_m + (h + 1) * dh]
            lhs = jnp.concatenate([kc, ct_sc[h].astype(BF16)], axis=0)
            first.append(lax.dot_general(lhs, qc, NT_DIMS, preferred_element_type=F32))
            vas.append(jnp.concatenate([vt_ref[c, h * dh:(h + 1) * dh, :], ones_rows], axis=0))
        for h in range(H_M):
            kc = zqk_ref[pl.ds(r0, chunk), w_m + h * dh:w_m + (h + 1) * dh]
            g_row = rows_ref[0, h, pl.ds(c, 1), :]
            m_e = last[:, GATE_GROUP + h:GATE_GROUP + h + 1]
            m_p = prev[:, GATE_GROUP + h:GATE_GROUP + h + 1]
            vaw = (vas[h].astype(F32) * jnp.exp(g_row - m_e)).astype(BF16)
            ct_sc[h] = jnp.exp(m_p - m_e) * ct_sc[h] + jnp.dot(vaw, kc, preferred_element_type=F32)
        for h in range(H_M):
            ch = slice(h * dh, (h + 1) * dh)
            g_col = colsc[:, h:h + 1]
            m_row = rows_ref[0, GATE_GROUP + h, pl.ds(c, 1), :]
            en_row = rows_ref[0, 2 * GATE_GROUP + h, pl.ds(c, 1), :]
            m_p = prev[:, GATE_GROUP + h:GATE_GROUP + h + 1]
            dmat = jnp.where(causal, jnp.exp(g_col - m_row), 0.0)
            sqk = (first[h][0:chunk] * dmat).astype(BF16)
            nd = (jnp.exp(m_p - m_row) * first[h][chunk:]
                  + jnp.dot(vas[h], sqk, preferred_element_type=F32))
            den = nd[dh:dh + 1]
            ht = nd[0:dh] * (1.0 / jnp.maximum(jnp.abs(den), en_row))
            hn = ht * lax.rsqrt(jnp.mean(ht * ht, axis=0, keepdims=True) + EPS)
            og = og_ref[pl.ds(r0, chunk), ch].astype(F32)
            ym_ref[pl.ds(r0, chunk), ch] = (hn.T * ng_ref[:, ch] * og).astype(BF16)
        return carry

    lax.fori_loop(0, nc, body, 0, unroll=True)


def _fox_kernel(q_ref, k_ref, v_ref, csq_ref, csk_ref, qsel_ref, ksel_ref, yf_ref,
                kaug_sc, vt_sc, qaug_sc, m_sc, acc_sc, *, tq, tk):
    qi = pl.program_id(1)

    seq = k_ref.shape[0]
    dhp = LANES
    half = dhp // 2
    v_rows = half + BF16_ROWS

    def own_lanes(head, rows):
        lane = lax.broadcasted_iota(jnp.int32, (rows, dhp), 1)
        return (lane >= half) if head % 2 else (lane < half)

    @pl.when(qi == 0)
    def _():
        csk = csk_ref[...]
        ones_rows = jnp.where(
            lax.broadcasted_iota(jnp.int32, (BF16_ROWS, tk), 0) == 0, 1.0, 0.0).astype(BF16)
        for p in range(H_F // 2):
            cp = slice(p * dhp, (p + 1) * dhp)
            kp = k_ref[:, cp]
            vt = v_ref[:, cp].astype(F32).T
            for head in (2 * p, 2 * p + 1):
                kb = jnp.dot(csk, ksel_ref[head], preferred_element_type=F32).astype(BF16)
                kaug_sc[head] = jnp.where(own_lanes(head, seq), kp, kb)
                r0 = (head % 2) * half
                for jb in range(seq // tk):
                    vt_sc[head, jb, 0:half, :] = vt[r0:r0 + half, jb * tk:(jb + 1) * tk].astype(BF16)
                    vt_sc[head, jb, half:v_rows, :] = ones_rows

    csq = csq_ref[...]
    for head in range(H_F):
        qp = q_ref[:, (head // 2) * dhp:(head // 2 + 1) * dhp]
        qb = jnp.dot(csq, qsel_ref[head], preferred_element_type=F32).astype(BF16)
        qaug_sc[head] = jnp.where(own_lanes(head, tq), qp, qb)
    m_sc[...] = jnp.full(m_sc.shape, NEG_BIG, F32)
    acc_sc[...] = jnp.zeros(acc_sc.shape, F32)

    lead_io = (lax.broadcasted_iota(jnp.int32, (tk, tq), 1)
               - lax.broadcasted_iota(jnp.int32, (tk, tq), 0))
    ratio = tq // tk

    def step(j, diag):
        k0 = pl.multiple_of(j * tk, tk)
        sts = [lax.dot_general(kaug_sc[head, pl.ds(k0, tk), :], qaug_sc[head], NT_DIMS,
                               preferred_element_type=F32) for head in range(H_F)]
        for head in range(H_F):
            st = sts[head]
            if diag is not None:
                st = jnp.where(lead_io >= diag * tk, st, NEG_BIG)
            m = m_sc[head]
            m_new = jnp.maximum(m, jnp.max(st, axis=0, keepdims=True))
            alpha = jnp.exp2(m - m_new)
            pt = jnp.exp2(st - m_new).astype(BF16)
            m_sc[head] = m_new
            acc_sc[head] = alpha * acc_sc[head] + jnp.dot(vt_sc[head, j], pt,
                                                          preferred_element_type=F32)

    def loop_body(jj, carry):
        for r in range(ratio):
            step(jj * ratio + r, None)
        return carry

    lax.fori_loop(0, qi, loop_body, 0)
    for diag in range(ratio):
        step(qi * ratio + diag, diag)
    for p in range(H_F // 2):
        outs = []
        for head in (2 * p, 2 * p + 1):
            acc = acc_sc[head]
            outs.append(acc[0:half] * (1.0 / acc[half:half + 1]))
        yf_ref[:, p * dhp:(p + 1) * dhp] = jnp.concatenate(outs, axis=0).T.astype(BF16)


def _post_kernel(x_ref, ym_ref, yf_ref, zc_ref, wbm_ref, wbf_ref, wo_ref, g2_ref,
                 wg_ref, wu_ref, wd_ref, gfin_ref, o_ref, *, tf):
    d = x_ref.shape[1]
    d_ff = wg_ref.shape[1]
    bm = jnp.dot(ym_ref[...], wbm_ref[...], preferred_element_type=F32)
    bf = jnp.dot(yf_ref[...], wbf_ref[...], preferred_element_type=F32)
    mix = zc_ref[:, 0:d].astype(F32) * bm + zc_ref[:, d:2 * d].astype(F32) * bf
    x1 = x_ref[...] + jnp.dot(mix.astype(BF16), wo_ref[...], preferred_element_type=F32)
    h2 = _rms(x1, g2_ref[...]).astype(BF16)
    acc = jnp.zeros(x1.shape, F32)
    for f0 in range(0, d_ff, tf):
        g = jnp.dot(h2, wg_ref[:, f0:f0 + tf], preferred_element_type=F32)
        u = jnp.dot(h2, wu_ref[:, f0:f0 + tf], preferred_element_type=F32)
        act = (g * jax.nn.sigmoid(g) * u).astype(BF16)
        acc = acc + jnp.dot(act, wd_ref[f0:f0 + tf, :], preferred_element_type=F32)
    o_ref[...] = _rms(x1 + acc, gfin_ref[...])


def _bias_selectors():
    ones_lane = 3 * GATE_GROUP
    qsel = np.zeros((H_F, LANES, LANES), np.float32)
    ksel = np.zeros((H_F, LANES, LANES), np.float32)
    for h in range(H_F):
        p0 = LANES // 2 if h % 2 == 0 else 0
        for c in range(3):
            qsel[h, GATE_GROUP * c + h, p0 + c] = 1.0
            qsel[h, ones_lane, p0 + 3 + c] = 1.0
            ksel[h, ones_lane, p0 + c] = 1.0
            ksel[h, GATE_GROUP * c + h, p0 + 3 + c] = -1.0
    return jnp.asarray(qsel, BF16), jnp.asarray(ksel, BF16)


def _layer(x2d, batch, seq, p, cfg):
    t, d = x2d.shape
    w_m = p["w_br_mlstm"].shape[0]
    w_f = p["w_br_fox"].shape[0]
    dh_m = w_m // H_M
    dh_f = w_f // H_F
    vmem = cfg["vmem_limit"]

    o_mi = 4 * w_m
    o_mf = o_mi + H_M
    o_fq = o_mf + H_M
    o_ff = o_fq + 3 * w_f
    o_g = o_ff + H_F
    w_in, b_in = p["w_in"], p["b_in"]
    wa, ba = w_in[:, :o_mi].astype(BF16), b_in[None, :o_mi]
    wb, bb = w_in[:, o_fq:o_ff].astype(BF16), b_in[None, o_fq:o_ff]
    wc, bc = w_in[:, o_g:].astype(BF16), b_in[None, o_g:]
    wd = jnp.zeros((d, LANES), F32)
    bd = jnp.zeros((1, LANES), F32)
    for dst, src, n in ((0, o_mi, H_M), (GATE_GROUP, o_mf, H_M), (2 * GATE_GROUP, o_ff, H_F)):
        wd = wd.at[:, dst:dst + n].set(w_in[:, src:src + n])
        bd = bd.at[:, dst:dst + n].set(b_in[None, src:src + n])
    wd = wd.astype(BF16)

    tm = cfg["tm_in"]
    chunk = cfg["chunk"]
    nc = seq // chunk
    assert seq % tm == 0 and tm % chunk == 0
    zqk, vt, og, zb, zc, zdt = pl.pallas_call(
        functools.partial(_inproj_kernel, w_m=w_m, w_f=w_f, n_chunk=cfg["n_chunk"],
                          q_scale=dh_f ** -0.5 * LOG2E, k_scale=dh_m ** -0.5, chunk=chunk,
                          tiles_per_seq=seq // tm),
        grid=(t // tm,),
        in_specs=[pl.BlockSpec((tm, d), lambda i: (i, 0)), _const_spec((1, d)),
                  _const_spec(wa.shape), _const_spec(ba.shape),
                  _const_spec(wb.shape), _const_spec(bb.shape),
                  _const_spec(wc.shape), _const_spec(bc.shape),
                  _const_spec(wd.shape), _const_spec(bd.shape),
                  _const_spec((CONV_K, 2 * w_m)), _const_spec((1, 2 * w_m))],
        out_specs=[pl.BlockSpec((tm, 2 * w_m), lambda i: (i, 0)),
                   pl.BlockSpec((tm // chunk, w_m, chunk), lambda i: (i, 0, 0)),
                   pl.BlockSpec((tm, w_m), lambda i: (i, 0)),
                   pl.BlockSpec((tm, 3 * w_f), lambda i: (i, 0)),
                   pl.BlockSpec((tm, 2 * d), lambda i: (i, 0)),
                   pl.BlockSpec((N_GATE_ROWS, tm), lambda i: (0, i))],
        out_shape=[jax.ShapeDtypeStruct((t, 2 * w_m), BF16),
                   jax.ShapeDtypeStruct((t // chunk, w_m, chunk), BF16),
                   jax.ShapeDtypeStruct((t, w_m), BF16),
                   jax.ShapeDtypeStruct((t, 3 * w_f), BF16),
                   jax.ShapeDtypeStruct((t, 2 * d), BF16),
                   jax.ShapeDtypeStruct((N_GATE_ROWS, t), F32)],
        scratch_shapes=[pltpu.VMEM((2 * w_m // cfg["n_chunk"], tm + SUBLANES, cfg["n_chunk"]), F32)],
        compiler_params=pltpu.CompilerParams(dimension_semantics=("arbitrary",), vmem_limit_bytes=vmem),
        name="inproj",
    )(x2d, p["norm1_g"][None, :], wa, ba, wb, bb, wc, bc, wd, bd, p["conv_w"], p["conv_b"][None, :])

    rows, cols, csplit = pl.pallas_call(
        _gates_kernel,
        grid=(batch,),
        in_specs=[pl.BlockSpec((N_GATE_ROWS, seq), lambda b: (0, b))],
        out_specs=[pl.BlockSpec((1, 3 * GATE_GROUP, seq), lambda b: (b, 0, 0)),
                   pl.BlockSpec((seq, LANES), lambda b: (b, 0)),
                   pl.BlockSpec((seq, LANES), lambda b: (b, 0))],
        out_shape=[jax.ShapeDtypeStruct((batch, 3 * GATE_GROUP, seq), F32),
                   jax.ShapeDtypeStruct((t, LANES), F32),
                   jax.ShapeDtypeStruct((t, LANES), BF16)],
        compiler_params=pltpu.CompilerParams(dimension_semantics=("parallel",), vmem_limit_bytes=vmem),
        name="gates",
    )(zdt)

    rows_m = rows.reshape(batch, 3 * GATE_GROUP, nc, chunk)
    ym = pl.pallas_call(
        functools.partial(_mlstm_kernel, w_m=w_m, chunk=chunk),
        grid=(batch,),
        in_specs=[pl.BlockSpec((seq, 2 * w_m), lambda b: (b, 0)),
                  pl.BlockSpec((nc, w_m, chunk), lambda b: (b, 0, 0)),
                  pl.BlockSpec((seq, w_m), lambda b: (b, 0)),
                  pl.BlockSpec((1, 3 * GATE_GROUP, nc, chunk), lambda b: (b, 0, 0, 0)),
                  pl.BlockSpec((seq, LANES), lambda b: (b, 0)),
                  _const_spec((1, w_m))],
        out_specs=pl.BlockSpec((seq, w_m), lambda b: (b, 0)),
        out_shape=jax.ShapeDtypeStruct((t, w_m), BF16),
        scratch_shapes=[pltpu.VMEM((H_M, dh_m + BF16_ROWS, dh_m), F32)],
        compiler_params=pltpu.CompilerParams(dimension_semantics=("parallel",), vmem_limit_bytes=vmem),
        name="mlstm",
    )(zqk, vt, og, rows_m, cols, p["mlstm_norm_g"][None, :])

    tq, tk = cfg["tq"], cfg["tk"]
    assert tq % tk == 0 and seq % tq == 0
    nq = seq // tq
    qsel, ksel = _bias_selectors()
    v_rows = LANES // 2 + BF16_ROWS
    yf = pl.pallas_call(
        functools.partial(_fox_kernel, tq=tq, tk=tk),
        grid=(batch, nq),
        in_specs=[pl.BlockSpec((tq, w_f), lambda b, i: (b * nq + i, 0)),
                  pl.BlockSpec((seq, w_f), lambda b, i: (b, 1)),
                  pl.BlockSpec((seq, w_f), lambda b, i: (b, 2)),
                  pl.BlockSpec((tq, LANES), lambda b, i: (b * nq + i, 0)),
                  pl.BlockSpec((seq, LANES), lambda b, i: (b, 0)),
                  _const_spec(qsel.shape), _const_spec(ksel.shape)],
        out_specs=pl.BlockSpec((tq, w_f), lambda b, i: (b * nq + i, 0)),
        out_shape=jax.ShapeDtypeStruct((t, w_f), BF16),
        scratch_shapes=[pltpu.VMEM((H_F, seq, LANES), BF16),
                        pltpu.VMEM((H_F, seq // tk, v_rows, tk), BF16),
                        pltpu.VMEM((H_F, tq, LANES), BF16),
                        pltpu.VMEM((H_F, 1, tq), F32),
                        pltpu.VMEM((H_F, v_rows, tq), F32)],
        compiler_params=pltpu.CompilerParams(dimension_semantics=("arbitrary", "arbitrary"),
                                             vmem_limit_bytes=vmem),
        name="fox",
    )(zb, zb, zb, csplit, csplit, qsel, ksel)

    tmp = cfg["tm_post"]
    wbm, wbf = p["w_br_mlstm"].astype(BF16), p["w_br_fox"].astype(BF16)
    wo = p["w_out"].astype(BF16)
    wg, wu, wdn = p["w_gate"].astype(BF16), p["w_up"].astype(BF16), p["w_down"].astype(BF16)
    return pl.pallas_call(
        functools.partial(_post_kernel, tf=cfg["tf"]),
        grid=(t // tmp,),
        in_specs=[pl.BlockSpec((tmp, d), lambda i: (i, 0)),
                  pl.BlockSpec((tmp, w_m), lambda i: (i, 0)),
                  pl.BlockSpec((tmp, w_f), lambda i: (i, 0)),
                  pl.BlockSpec((tmp, 2 * d), lambda i: (i, 0)),
                  _const_spec(wbm.shape), _const_spec(wbf.shape), _const_spec(wo.shape),
                  _const_spec((1, d)), _const_spec(wg.shape), _const_spec(wu.shape),
                  _const_spec(wdn.shape), _const_spec((1, d))],
        out_specs=pl.BlockSpec((tmp, d), lambda i: (i, 0)),
        out_shape=jax.ShapeDtypeStruct((t, d), F32),
        compiler_params=pltpu.CompilerParams(dimension_semantics=("parallel",), vmem_limit_bytes=vmem),
        name="post",
    )(x2d, ym, yf, zc, wbm, wbf, wo, p["norm2_g"][None, :], wg, wu, wdn, p["norm_f_g"][None, :])


def kernel(x, norm1_g, w_in, b_in, conv_w, conv_b, mlstm_norm_g, w_br_mlstm, w_br_fox, w_out,
           norm2_g, w_gate, w_up, w_down, norm_f_g):
    batch, seq, d = x.shape
    depth = w_in.shape[0]
    assert depth == 1, "the final norm is fused into the single layer's last call"
    cfg = _cfg(batch, seq, d, w_gate.shape[-1])
    p = dict(norm1_g=norm1_g[0], w_in=w_in[0], b_in=b_in[0], conv_w=conv_w[0], conv_b=conv_b[0],
             mlstm_norm_g=mlstm_norm_g[0], w_br_mlstm=w_br_mlstm[0], w_br_fox=w_br_fox[0],
             w_out=w_out[0], norm2_g=norm2_g[0], w_gate=w_gate[0], w_up=w_up[0], w_down=w_down[0],
             norm_f_g=norm_f_g)
    out = _layer(x.reshape(batch * seq, d), batch, seq, p, cfg)
    return out.reshape(batch, seq, d)
```

```python
import functools

import jax
import jax.numpy as jnp
import numpy as np
from jax import lax
from jax.experimental import pallas as pl
from jax.experimental.pallas import tpu as pltpu

EPS = 1e-6
H_M = 4
H_F = 8
CONV_K = 4

LANES = 128
SUBLANES = 8
BF16_ROWS = 16
GATE_GROUP = 8
N_GATE_ROWS = 3 * GATE_GROUP
V7X_VMEM_BYTES = 64 * 1024 * 1024
NEG_BIG = -1e30
LOG2E = 1.4426950408889634

F32 = jnp.float32
BF16 = jnp.bfloat16
NT_DIMS = (((1,), (1,)), ((), ()))


def _cfg(batch, seq, d_model, d_ff):
    return dict(
        tm_in=512,
        n_chunk=256,
        chunk=256,
        tq=512,
        tk=256,
        tm_post=512,
        tf=256,
        vmem_limit=V7X_VMEM_BYTES - 8 * 1024 * 1024,
    )


def _const_spec(shape):
    nd = len(shape)
    return pl.BlockSpec(shape, lambda *_: (0,) * nd, pipeline_mode=pl.Buffered(1))


def _rms(x, g):
    return x * lax.rsqrt(jnp.mean(x * x, axis=-1, keepdims=True) + EPS) * g


def _log_sigmoid(x):
    return jnp.minimum(x, 0.0) - jnp.log1p(jnp.exp(-jnp.abs(x)))


def _wprep_kernel(win_ref, bin_ref, wbm_ref, wbf_ref, wo_ref, wg_ref, wu_ref, wdn_ref,
                  wa_o, wb_o, wc_o, wd_o, ba_o, bb_o, bc_o, bd_o,
                  wbm_o, wbf_o, wo_o, wg_o, wu_o, wdn_o, *, w_m, w_f):
    o_mi = 4 * w_m
    o_mf = o_mi + H_M
    o_fq = o_mf + H_M
    o_ff = o_fq + 3 * w_f
    o_g = o_ff + H_F

    def gate_block(src):
        rows = src.shape[0]
        lane = lax.broadcasted_iota(jnp.int32, (rows, LANES), 1)
        blk_m = src[:, o_mi:o_mi + LANES]
        f0 = (o_ff // LANES) * LANES
        blk_f = src[:, f0:f0 + LANES]
        mi = jnp.where(lane < H_M, blk_m, 0.0)
        mf = jnp.where((lane >= GATE_GROUP) & (lane < GATE_GROUP + H_M),
                       pltpu.roll(blk_m, GATE_GROUP - H_M, axis=1), 0.0)
        ff = jnp.where((lane >= 2 * GATE_GROUP) & (lane < 2 * GATE_GROUP + H_F),
                       pltpu.roll(blk_f, 2 * GATE_GROUP - (o_ff - f0), axis=1), 0.0)
        return mi + mf + ff

    for src_ref, dsts, dt in ((win_ref, (wa_o, wb_o, wc_o, wd_o), BF16),
                              (bin_ref, (ba_o, bb_o, bc_o, bd_o), F32)):
        src = src_ref[...]
        dsts[0][...] = src[:, 0:o_mi].astype(dt)
        dsts[1][...] = src[:, o_fq:o_ff].astype(dt)
        dsts[2][...] = src[:, o_g:].astype(dt)
        dsts[3][...] = gate_block(src).astype(dt)
    for src_ref, dst in ((wbm_ref, wbm_o), (wbf_ref, wbf_o), (wo_ref, wo_o), (wg_ref, wg_o),
                         (wu_ref, wu_o), (wdn_ref, wdn_o)):
        dst[...] = src_ref[...].astype(BF16)


def _prep_weights(p, vmem):
    d, n_in = p["w_in"].shape
    w_m = p["w_br_mlstm"].shape[0]
    w_f = p["w_br_fox"].shape[0]
    d_ff = p["w_gate"].shape[1]
    steps = 8
    o_ff = 4 * w_m + 2 * H_M + 3 * w_f
    assert (4 * w_m) % LANES == 0 and 2 * H_M <= GATE_GROUP + H_M <= LANES
    assert o_ff % LANES <= 2 * GATE_GROUP and o_ff % LANES + H_F <= LANES
    assert n_in == o_ff + H_F + 2 * d
    srcs = [p["w_in"], p["b_in"][None, :], p["w_br_mlstm"], p["w_br_fox"], p["w_out"],
            p["w_gate"], p["w_up"], p["w_down"]]

    def row_spec(shape, tiled=True):
        if not tiled:
            return pl.BlockSpec(shape, lambda i: (0, 0))
        assert shape[0] % (steps * BF16_ROWS) == 0
        return pl.BlockSpec((shape[0] // steps, shape[1]), lambda i: (i, 0))

    out_shapes = [((d, 4 * w_m), BF16), ((d, 3 * w_f), BF16), ((d, 2 * d), BF16), ((d, LANES), BF16),
                  ((1, 4 * w_m), F32), ((1, 3 * w_f), F32), ((1, 2 * d), F32), ((1, LANES), F32),
                  ((w_m, d), BF16), ((w_f, d), BF16), ((d, d), BF16),
                  ((d, d_ff), BF16), ((d, d_ff), BF16), ((d_ff, d), BF16)]
    return pl.pallas_call(
        functools.partial(_wprep_kernel, w_m=w_m, w_f=w_f),
        grid=(steps,),
        in_specs=[row_spec(s.shape, tiled=s.shape[0] > 1) for s in srcs],
        out_specs=[row_spec(s, tiled=s[0] > 1) for s, _ in out_shapes],
        out_shape=[jax.ShapeDtypeStruct(s, dt) for s, dt in out_shapes],
        compiler_params=pltpu.CompilerParams(dimension_semantics=("arbitrary",), vmem_limit_bytes=vmem),
        name="wprep",
    )(*srcs)


def _inproj_kernel(x_ref, g_ref, wa_ref, ba_ref, wb_ref, bb_ref, wc_ref, bc_ref, wd_ref, bd_ref,
                   cw_ref, cb_ref, zqk_ref, vt_ref, og_ref, zb_ref, zc_ref, zdt_ref, zs_sc,
                   *, w_m, w_f, n_chunk, q_scale, k_scale, chunk, tiles_per_seq):
    tm = x_ref.shape[0]
    seq_start = (pl.program_id(0) % tiles_per_seq) == 0
    hb = _rms(x_ref[...], g_ref[...]).astype(BF16)

    @pl.when(seq_start)
    def _():
        zs_sc[:, 0:SUBLANES, :] = jnp.zeros((zs_sc.shape[0], SUBLANES, n_chunk), F32)

    @pl.when(jnp.logical_not(seq_start))
    def _():
        zs_sc[:, 0:SUBLANES, :] = zs_sc[:, tm:tm + SUBLANES, :]

    def proj(w_ref, b_ref, c0, c1):
        return jnp.dot(hb, w_ref[:, c0:c1], preferred_element_type=F32) + b_ref[:, c0:c1]

    def qk_chunk(ci):
        c0 = ci * n_chunk
        cs = slice(c0, c0 + n_chunk)
        zs_sc[ci, SUBLANES:tm + SUBLANES, :] = proj(wa_ref, ba_ref, c0, c0 + n_chunk)
        w = cw_ref[:, cs]
        y = cb_ref[:, cs]
        for s in range(CONV_K):
            y = y + zs_sc[ci, SUBLANES - s:SUBLANES - s + tm, :] * w[CONV_K - 1 - s:CONV_K - s]
        act = y * jax.nn.sigmoid(y)
        if c0 >= w_m:
            act = act * k_scale
        zqk_ref[:, cs] = act.astype(BF16)

    def v_chunk(i):
        c0 = 2 * w_m + i * n_chunk
        z = proj(wa_ref, ba_ref, c0, c0 + n_chunk)
        for cc in range(tm // chunk):
            for f0 in range(0, n_chunk, LANES):
                blk = z[cc * chunk:(cc + 1) * chunk, f0:f0 + LANES]
                r0 = i * n_chunk + f0
                vt_ref[cc, r0:r0 + LANES, :] = blk.T.astype(BF16)

    def og_chunk(i):
        c0 = 3 * w_m + i * n_chunk
        z = proj(wa_ref, ba_ref, c0, c0 + n_chunk)
        og_ref[:, i * n_chunk:(i + 1) * n_chunk] = jax.nn.sigmoid(z).astype(BF16)

    def b_chunk(i):
        c0 = i * n_chunk
        z = proj(wb_ref, bb_ref, c0, c0 + n_chunk)
        if c0 < w_f:
            z = z * q_scale
        zb_ref[:, c0:c0 + n_chunk] = z.astype(BF16)

    def c_chunk(i):
        c0 = i * n_chunk
        z = proj(wc_ref, bc_ref, c0, c0 + n_chunk)
        zc_ref[:, c0:c0 + n_chunk] = jax.nn.sigmoid(z).astype(BF16)

    def d_chunk(_):
        zd = proj(wd_ref, bd_ref, 0, LANES)
        zdt_ref[...] = zd.T[:N_GATE_ROWS, :]

    heavy = [(qk_chunk, i) for i in range(2 * w_m // n_chunk)]
    light = ([(c_chunk, i) for i in range(wc_ref.shape[1] // n_chunk)]
             + [(b_chunk, i) for i in range(3 * w_f // n_chunk)]
             + [(v_chunk, i) for i in range(w_m // n_chunk)]
             + [(og_chunk, i) for i in range(w_m // n_chunk)] + [(d_chunk, 0)])
    per_heavy = len(light) // len(heavy)
    order = []
    for hi, task in enumerate(heavy):
        order.append(task)
        order.extend(light[hi * per_heavy:(hi + 1) * per_heavy])
    order.extend(light[len(heavy) * per_heavy:])
    for fn, i in order:
        fn(i)


def _scan_lanes(x, op, fill):
    n = x.shape[-1]
    lane = lax.broadcasted_iota(jnp.int32, x.shape, 1)
    s = 1
    while s < n:
        shifted = pltpu.roll(x, s, axis=1)
        x = op(x, jnp.where(lane >= s, shifted, fill))
        s *= 2
    return x


def _gates_kernel(zdt_ref, rows_ref, cols_ref, csplit_ref):
    z = zdt_ref[...]
    seq = z.shape[1]
    i8 = z[0:GATE_GROUP]
    f8 = _scan_lanes(_log_sigmoid(z[GATE_GROUP:2 * GATE_GROUP]), jnp.add, 0.0)
    cf8 = _scan_lanes(_log_sigmoid(z[2 * GATE_GROUP:3 * GATE_GROUP]), jnp.add, 0.0) * LOG2E
    g8 = i8 - f8
    m8 = jnp.maximum(_scan_lanes(g8, jnp.maximum, NEG_BIG), 0.0)
    en8 = jnp.exp(-(f8 + m8))
    rows_ref[0] = jnp.concatenate([g8, m8, en8], axis=0)
    stack = jnp.concatenate(
        [g8, m8, jnp.zeros((LANES - 2 * GATE_GROUP, seq), F32)], axis=0)
    cols_ref[...] = stack.T
    hi = cf8.astype(BF16).astype(F32)
    r1 = cf8 - hi
    lo = r1.astype(BF16).astype(F32)
    lo2 = (r1 - lo).astype(BF16).astype(F32)
    ones = jnp.where(lax.broadcasted_iota(jnp.int32, (GATE_GROUP, seq), 0) == 0, 1.0, 0.0)
    split = jnp.concatenate(
        [hi, lo, lo2, ones, jnp.zeros((LANES - 4 * GATE_GROUP, seq), F32)], axis=0)
    csplit_ref[...] = split.T.astype(BF16)


def _mlstm_kernel(zqk_ref, vt_ref, og_ref, rows_ref, cols_ref, ng_ref, ym_ref, ct_sc, *, w_m, chunk):
    seq = zqk_ref.shape[0]
    dh = w_m // H_M
    nc = seq // chunk
    ct_sc[...] = jnp.zeros(ct_sc.shape, F32)
    causal = (lax.broadcasted_iota(jnp.int32, (chunk, chunk), 0)
              <= lax.broadcasted_iota(jnp.int32, (chunk, chunk), 1))
    ones_rows = jnp.where(
        lax.broadcasted_iota(jnp.int32, (BF16_ROWS, chunk), 0) == 0, 1.0, 0.0).astype(BF16)

    def body(c, carry):
        r0 = pl.multiple_of(c * chunk, chunk)
        colsc = cols_ref[pl.ds(r0, chunk), :]
        last = cols_ref[pl.ds(r0 + chunk - 1, 1), :]
        prev = cols_ref[pl.ds(jnp.maximum(r0 - 1, 0), 1), :]
        prev = jnp.where(c > 0, prev, 0.0)
        first, vas = [], []
        for h in range(H_M):
            qc = zqk_ref[pl.ds(r0, chunk), h * dh:(h + 1) * dh]
            kc = zqk_ref[pl.ds(r0, chunk), w_m + h * dh:w_m + (h + 1) * dh]
            lhs = jnp.concatenate([kc, ct_sc[h].astype(BF16)], axis=0)
            first.append(lax.dot_general(lhs, qc, NT_DIMS, preferred_element_type=F32))
            vas.append(jnp.concatenate([vt_ref[c, h * dh:(h + 1) * dh, :], ones_rows], axis=0))
        for h in range(H_M):
            kc = zqk_ref[pl.ds(r0, chunk), w_m + h * dh:w_m + (h + 1) * dh]
            g_row = rows_ref[0, h, pl.ds(c, 1), :]
            m_e = last[:, GATE_GROUP + h:GATE_GROUP + h + 1]
            m_p = prev[:, GATE_GROUP + h:GATE_GROUP + h + 1]
            vaw = (vas[h].astype(F32) * jnp.exp(g_row - m_e)).astype(BF16)
            ct_sc[h] = jnp.exp(m_p - m_e) * ct_sc[h] + jnp.dot(vaw, kc, preferred_element_type=F32)
        for h in range(H_M):
            ch = slice(h * dh, (h + 1) * dh)
            g_col = colsc[:, h:h + 1]
            m_row = rows_ref[0, GATE_GROUP + h, pl.ds(c, 1), :]
            en_row = rows_ref[0, 2 * GATE_GROUP + h, pl.ds(c, 1), :]
            m_p = prev[:, GATE_GROUP + h:GATE_GROUP + h + 1]
            dmat = jnp.where(causal, jnp.exp(g_col - m_row), 0.0)
            sqk = (first[h][0:chunk] * dmat).astype(BF16)
            nd = (jnp.exp(m_p - m_row) * first[h][chunk:]
                  + jnp.dot(vas[h], sqk, preferred_element_type=F32))
            den = nd[dh:dh + 1]
            ht = nd[0:dh] * (1.0 / jnp.maximum(jnp.abs(den), en_row))
            hn = ht * lax.rsqrt(jnp.mean(ht * ht, axis=0, keepdims=True) + EPS)
            og = og_ref[pl.ds(r0, chunk), ch].astype(F32)
            ym_ref[pl.ds(r0, chunk), ch] = (hn.T * ng_ref[:, ch] * og).astype(BF16)
        return carry

    lax.fori_loop(0, nc, body, 0, unroll=True)


def _fox_kernel(q_ref, k_ref, v_ref, csq_ref, csk_ref, qsel_ref, ksel_ref, yf_ref,
                kaug_sc, vt_sc, qaug_sc, m_sc, acc_sc, *, tq, tk):
    qi = pl.program_id(1)

    seq = k_ref.shape[0]
    dhp = LANES
    half = dhp // 2
    v_rows = half + BF16_ROWS

    def own_lanes(head, rows):
        lane = lax.broadcasted_iota(jnp.int32, (rows, dhp), 1)
        return (lane >= half) if head % 2 else (lane < half)

    @pl.when(qi == 0)
    def _():
        csk = csk_ref[...]
        ones_rows = jnp.where(
            lax.broadcasted_iota(jnp.int32, (BF16_ROWS, tk), 0) == 0, 1.0, 0.0).astype(BF16)
        for p in range(H_F // 2):
            cp = slice(p * dhp, (p + 1) * dhp)
            kp = k_ref[:, cp]
            vt = v_ref[:, cp].astype(F32).T
            for head in (2 * p, 2 * p + 1):
                kb = jnp.dot(csk, ksel_ref[head], preferred_element_type=F32).astype(BF16)
                kaug_sc[head] = jnp.where(own_lanes(head, seq), kp, kb)
                r0 = (head % 2) * half
                for jb in range(seq // tk):
                    vt_sc[head, jb, 0:half, :] = vt[r0:r0 + half, jb * tk:(jb + 1) * tk].astype(BF16)
                    vt_sc[head, jb, half:v_rows, :] = ones_rows

    csq = csq_ref[...]
    for head in range(H_F):
        qp = q_ref[:, (head // 2) * dhp:(head // 2 + 1) * dhp]
        qb = jnp.dot(csq, qsel_ref[head], preferred_element_type=F32).astype(BF16)
        qaug_sc[head] = jnp.where(own_lanes(head, tq), qp, qb)
    m_sc[...] = jnp.full(m_sc.shape, NEG_BIG, F32)
    acc_sc[...] = jnp.zeros(acc_sc.shape, F32)

    lead_io = (lax.broadcasted_iota(jnp.int32, (tk, tq), 1)
               - lax.broadcasted_iota(jnp.int32, (tk, tq), 0))
    ratio = tq // tk

    def step(j, diag):
        k0 = pl.multiple_of(j * tk, tk)
        sts = [lax.dot_general(kaug_sc[head, pl.ds(k0, tk), :], qaug_sc[head], NT_DIMS,
                               preferred_element_type=F32) for head in range(H_F)]
        for head in range(H_F):
            st = sts[head]
            if diag is not None:
                st = jnp.where(lead_io >= diag * tk, st, NEG_BIG)
            m = m_sc[head]
            m_new = jnp.maximum(m, jnp.max(st, axis=0, keepdims=True))
            alpha = jnp.exp2(m - m_new)
            pt = jnp.exp2(st - m_new).astype(BF16)
            m_sc[head] = m_new
            acc_sc[head] = alpha * acc_sc[head] + jnp.dot(vt_sc[head, j], pt,
                                                          preferred_element_type=F32)

    def loop_body(jj, carry):
        for r in range(ratio):
            step(jj * ratio + r, None)
        return carry

    lax.fori_loop(0, qi, loop_body, 0)
    for diag in range(ratio):
        step(qi * ratio + diag, diag)
    for p in range(H_F // 2):
        outs = []
        for head in (2 * p, 2 * p + 1):
            acc = acc_sc[head]
            outs.append(acc[0:half] * (1.0 / acc[half:half + 1]))
        yf_ref[:, p * dhp:(p + 1) * dhp] = jnp.concatenate(outs, axis=0).T.astype(BF16)


def _post_kernel(x_ref, ym_ref, yf_ref, zc_ref, wbm_ref, wbf_ref, wo_ref, g2_ref,
                 wg_ref, wu_ref, wd_ref, gfin_ref, o_ref, *, tf):
    d = x_ref.shape[1]
    d_ff = wg_ref.shape[1]
    bm = jnp.dot(ym_ref[...], wbm_ref[...], preferred_element_type=F32)
    bf = jnp.dot(yf_ref[...], wbf_ref[...], preferred_element_type=F32)
    mix = zc_ref[:, 0:d].astype(F32) * bm + zc_ref[:, d:2 * d].astype(F32) * bf
    x1 = x_ref[...] + jnp.dot(mix.astype(BF16), wo_ref[...], preferred_element_type=F32)
    h2 = _rms(x1, g2_ref[...]).astype(BF16)
    acc = jnp.zeros(x1.shape, F32)
    for f0 in range(0, d_ff, tf):
        g = jnp.dot(h2, wg_ref[:, f0:f0 + tf], preferred_element_type=F32)
        u = jnp.dot(h2, wu_ref[:, f0:f0 + tf], preferred_element_type=F32)
        act = (g * jax.nn.sigmoid(g) * u).astype(BF16)
        acc = acc + jnp.dot(act, wd_ref[f0:f0 + tf, :], preferred_element_type=F32)
    o_ref[...] = _rms(x1 + acc, gfin_ref[...])


def _bias_selectors():
    ones_lane = 3 * GATE_GROUP
    qsel = np.zeros((H_F, LANES, LANES), np.float32)
    ksel = np.zeros((H_F, LANES, LANES), np.float32)
    for h in range(H_F):
        p0 = LANES // 2 if h % 2 == 0 else 0
        for c in range(3):
            qsel[h, GATE_GROUP * c + h, p0 + c] = 1.0
            qsel[h, ones_lane, p0 + 3 + c] = 1.0
            ksel[h, ones_lane, p0 + c] = 1.0
            ksel[h, GATE_GROUP * c + h, p0 + 3 + c] = -1.0
    return jnp.asarray(qsel, BF16), jnp.asarray(ksel, BF16)


def _layer(x2d, batch, seq, p, cfg):
    t, d = x2d.shape
    w_m = p["w_br_mlstm"].shape[0]
    w_f = p["w_br_fox"].shape[0]
    dh_m = w_m // H_M
    dh_f = w_f // H_F
    vmem = cfg["vmem_limit"]

    wa, wb, wc, wd, ba, bb, bc, bd, wbm, wbf, wo, wg, wu, wdn = _prep_weights(p, vmem)

    tm = cfg["tm_in"]
    chunk = cfg["chunk"]
    nc = seq // chunk
    assert seq % tm == 0 and tm % chunk == 0
    zqk, vt, og, zb, zc, zdt = pl.pallas_call(
        functools.partial(_inproj_kernel, w_m=w_m, w_f=w_f, n_chunk=cfg["n_chunk"],
                          q_scale=dh_f ** -0.5 * LOG2E, k_scale=dh_m ** -0.5, chunk=chunk,
                          tiles_per_seq=seq // tm),
        grid=(t // tm,),
        in_specs=[pl.BlockSpec((tm, d), lambda i: (i, 0)), _const_spec((1, d)),
                  _const_spec(wa.shape), _const_spec(ba.shape),
                  _const_spec(wb.shape), _const_spec(bb.shape),
                  _const_spec(wc.shape), _const_spec(bc.shape),
                  _const_spec(wd.shape), _const_spec(bd.shape),
                  _const_spec((CONV_K, 2 * w_m)), _const_spec((1, 2 * w_m))],
        out_specs=[pl.BlockSpec((tm, 2 * w_m), lambda i: (i, 0)),
                   pl.BlockSpec((tm // chunk, w_m, chunk), lambda i: (i, 0, 0)),
                   pl.BlockSpec((tm, w_m), lambda i: (i, 0)),
                   pl.BlockSpec((tm, 3 * w_f), lambda i: (i, 0)),
                   pl.BlockSpec((tm, 2 * d), lambda i: (i, 0)),
                   pl.BlockSpec((N_GATE_ROWS, tm), lambda i: (0, i))],
        out_shape=[jax.ShapeDtypeStruct((t, 2 * w_m), BF16),
                   jax.ShapeDtypeStruct((t // chunk, w_m, chunk), BF16),
                   jax.ShapeDtypeStruct((t, w_m), BF16),
                   jax.ShapeDtypeStruct((t, 3 * w_f), BF16),
                   jax.ShapeDtypeStruct((t, 2 * d), BF16),
                   jax.ShapeDtypeStruct((N_GATE_ROWS, t), F32)],
        scratch_shapes=[pltpu.VMEM((2 * w_m // cfg["n_chunk"], tm + SUBLANES, cfg["n_chunk"]), F32)],
        compiler_params=pltpu.CompilerParams(dimension_semantics=("arbitrary",), vmem_limit_bytes=vmem),
        name="inproj",
    )(x2d, p["norm1_g"][None, :], wa, ba, wb, bb, wc, bc, wd, bd, p["conv_w"], p["conv_b"][None, :])

    rows, cols, csplit = pl.pallas_call(
        _gates_kernel,
        grid=(batch,),
        in_specs=[pl.BlockSpec((N_GATE_ROWS, seq), lambda b: (0, b))],
        out_specs=[pl.BlockSpec((1, 3 * GATE_GROUP, seq), lambda b: (b, 0, 0)),
                   pl.BlockSpec((seq, LANES), lambda b: (b, 0)),
                   pl.BlockSpec((seq, LANES), lambda b: (b, 0))],
        out_shape=[jax.ShapeDtypeStruct((batch, 3 * GATE_GROUP, seq), F32),
                   jax.ShapeDtypeStruct((t, LANES), F32),
                   jax.ShapeDtypeStruct((t, LANES), BF16)],
        compiler_params=pltpu.CompilerParams(dimension_semantics=("parallel",), vmem_limit_bytes=vmem),
        name="gates",
    )(zdt)

    rows_m = rows.reshape(batch, 3 * GATE_GROUP, nc, chunk)
    ym = pl.pallas_call(
        functools.partial(_mlstm_kernel, w_m=w_m, chunk=chunk),
        grid=(batch,),
        in_specs=[pl.BlockSpec((seq, 2 * w_m), lambda b: (b, 0)),
                  pl.BlockSpec((nc, w_m, chunk), lambda b: (b, 0, 0)),
                  pl.BlockSpec((seq, w_m), lambda b: (b, 0)),
                  pl.BlockSpec((1, 3 * GATE_GROUP, nc, chunk), lambda b: (b, 0, 0, 0)),
                  pl.BlockSpec((seq, LANES), lambda b: (b, 0)),
                  _const_spec((1, w_m))],
        out_specs=pl.BlockSpec((seq, w_m), lambda b: (b, 0)),
        out_shape=jax.ShapeDtypeStruct((t, w_m), BF16),
        scratch_shapes=[pltpu.VMEM((H_M, dh_m + BF16_ROWS, dh_m), F32)],
        compiler_params=pltpu.CompilerParams(dimension_semantics=("parallel",), vmem_limit_bytes=vmem),
        name="mlstm",
    )(zqk, vt, og, rows_m, cols, p["mlstm_norm_g"][None, :])

    tq, tk = cfg["tq"], cfg["tk"]
    assert tq % tk == 0 and seq % tq == 0
    nq = seq // tq
    qsel, ksel = _bias_selectors()
    v_rows = LANES // 2 + BF16_ROWS
    yf = pl.pallas_call(
        functools.partial(_fox_kernel, tq=tq, tk=tk),
        grid=(batch, nq),
        in_specs=[pl.BlockSpec((tq, w_f), lambda b, i: (b * nq + i, 0)),
                  pl.BlockSpec((seq, w_f), lambda b, i: (b, 1)),
                  pl.BlockSpec((seq, w_f), lambda b, i: (b, 2)),
                  pl.BlockSpec((tq, LANES), lambda b, i: (b * nq + i, 0)),
                  pl.BlockSpec((seq, LANES), lambda b, i: (b, 0)),
                  _const_spec(qsel.shape), _const_spec(ksel.shape)],
        out_specs=pl.BlockSpec((tq, w_f), lambda b, i: (b * nq + i, 0)),
        out_shape=jax.ShapeDtypeStruct((t, w_f), BF16),
        scratch_shapes=[pltpu.VMEM((H_F, seq, LANES), BF16),
                        pltpu.VMEM((H_F, seq // tk, v_rows, tk), BF16),
                        pltpu.VMEM((H_F, tq, LANES), BF16),
                        pltpu.VMEM((H_F, 1, tq), F32),
                        pltpu.VMEM((H_F, v_rows, tq), F32)],
        compiler_params=pltpu.CompilerParams(dimension_semantics=("arbitrary", "arbitrary"),
                                             vmem_limit_bytes=vmem),
        name="fox",
    )(zb, zb, zb, csplit, csplit, qsel, ksel)

    tmp = cfg["tm_post"]
    return pl.pallas_call(
        functools.partial(_post_kernel, tf=cfg["tf"]),
        grid=(t // tmp,),
        in_specs=[pl.BlockSpec((tmp, d), lambda i: (i, 0)),
                  pl.BlockSpec((tmp, w_m), lambda i: (i, 0)),
                  pl.BlockSpec((tmp, w_f), lambda i: (i, 0)),
                  pl.BlockSpec((tmp, 2 * d), lambda i: (i, 0)),
                  _const_spec(wbm.shape), _const_spec(wbf.shape), _const_spec(wo.shape),
                  _const_spec((1, d)), _const_spec(wg.shape), _const_spec(wu.shape),
                  _const_spec(wdn.shape), _const_spec((1, d))],
        out_specs=pl.BlockSpec((tmp, d), lambda i: (i, 0)),
        out_shape=jax.ShapeDtypeStruct((t, d), F32),
        compiler_params=pltpu.CompilerParams(dimension_semantics=("parallel",), vmem_limit_bytes=vmem),
        name="post",
    )(x2d, ym, yf, zc, wbm, wbf, wo, p["norm2_g"][None, :], wg, wu, wdn, p["norm_f_g"][None, :])


def kernel(x, norm1_g, w_in, b_in, conv_w, conv_b, mlstm_norm_g, w_br_mlstm, w_br_fox, w_out,
           norm2_g, w_gate, w_up, w_down, norm_f_g):
    batch, seq, d = x.shape
    depth = w_in.shape[0]
    assert depth == 1, "the final norm is fused into the single layer's last call"
    cfg = _cfg(batch, seq, d, w_gate.shape[-1])
    p = dict(norm1_g=norm1_g[0], w_in=w_in[0], b_in=b_in[0], conv_w=conv_w[0], conv_b=conv_b[0],
             mlstm_norm_g=mlstm_norm_g[0], w_br_mlstm=w_br_mlstm[0], w_br_fox=w_br_fox[0],
             w_out=w_out[0], norm2_g=norm2_g[0], w_gate=w_gate[0], w_up=w_up[0], w_down=w_down[0],
             norm_f_g=norm_f_g)
    out = _layer(x.reshape(batch * seq, d), batch, seq, p, cfg)
    return out.reshape(batch, seq, d)
```

```python
import functools

import jax
import jax.numpy as jnp
import numpy as np
from jax import lax
from jax.experimental import pallas as pl
from jax.experimental.pallas import tpu as pltpu

EPS = 1e-6
H_M = 4
H_F = 8
CONV_K = 4

LANES = 128
SUBLANES = 8
BF16_ROWS = 16
GATE_GROUP = 8
N_GATE_ROWS = 3 * GATE_GROUP
V7X_VMEM_BYTES = 64 * 1024 * 1024
NEG_BIG = -1e30
LOG2E = 1.4426950408889634

F32 = jnp.float32
BF16 = jnp.bfloat16
NT_DIMS = (((1,), (1,)), ((), ()))


def _cfg(batch, seq, d_model, d_ff):
    return dict(
        tm_in=512,
        n_chunk=256,
        chunk=256,
        tq=512,
        tk=256,
        tm_post=512,
        tf=256,
        vmem_limit=V7X_VMEM_BYTES - 8 * 1024 * 1024,
    )


def _const_spec(shape):
    nd = len(shape)
    return pl.BlockSpec(shape, lambda *_: (0,) * nd, pipeline_mode=pl.Buffered(1))


def _rms(x, g):
    return x * lax.rsqrt(jnp.mean(x * x, axis=-1, keepdims=True) + EPS) * g


def _log_sigmoid(x):
    return jnp.minimum(x, 0.0) - jnp.log1p(jnp.exp(-jnp.abs(x)))


def _wprep_kernel(wint_ref, bin_ref, wbm_ref, wbf_ref, wo_ref, wg_ref, wu_ref, wdn_ref,
                  wa_o, wb_o, wc_o, wd_o, ba_o, bb_o, bc_o, bd_o,
                  wbm_o, wbf_o, wo_o, wg_o, wu_o, wdn_o, *, w_m, w_f):
    o_mi = 4 * w_m
    o_mf = o_mi + H_M
    o_fq = o_mf + H_M
    o_ff = o_fq + 3 * w_f
    o_g = o_ff + H_F

    def gate_block(src):
        rows = src.shape[0]
        lane = lax.broadcasted_iota(jnp.int32, (rows, LANES), 1)
        blk_m = src[:, o_mi:o_mi + LANES]
        f0 = (o_ff // LANES) * LANES
        blk_f = src[:, f0:f0 + LANES]
        mi = jnp.where(lane < H_M, blk_m, 0.0)
        mf = jnp.where((lane >= GATE_GROUP) & (lane < GATE_GROUP + H_M),
                       pltpu.roll(blk_m, GATE_GROUP - H_M, axis=1), 0.0)
        ff = jnp.where((lane >= 2 * GATE_GROUP) & (lane < 2 * GATE_GROUP + H_F),
                       pltpu.roll(blk_f, 2 * GATE_GROUP - (o_ff - f0), axis=1), 0.0)
        return mi + mf + ff

    slab = wint_ref[...]
    wa_o[...] = slab[0:o_mi].T.astype(BF16)
    wb_o[...] = slab[o_fq:o_ff].T.astype(BF16)
    wc_o[...] = slab[o_g:].T.astype(BF16)
    g_m = slab[o_mi:o_mi + GATE_GROUP]
    row = lax.broadcasted_iota(jnp.int32, g_m.shape, 0)
    gates = jnp.concatenate(
        [jnp.where(row < H_M, g_m, 0.0),
         jnp.where(row < H_M, pltpu.roll(g_m, GATE_GROUP - H_M, axis=0), 0.0),
         slab[o_ff:o_ff + GATE_GROUP],
         jnp.zeros((LANES - 3 * GATE_GROUP, slab.shape[1]), F32)], axis=0)
    wd_o[...] = gates.T.astype(BF16)
    b = bin_ref[...]
    ba_o[...] = b[:, 0:o_mi]
    bb_o[...] = b[:, o_fq:o_ff]
    bc_o[...] = b[:, o_g:]
    bd_o[...] = gate_block(b)
    for src_ref, dst in ((wbm_ref, wbm_o), (wbf_ref, wbf_o), (wo_ref, wo_o), (wg_ref, wg_o),
                         (wu_ref, wu_o), (wdn_ref, wdn_o)):
        dst[...] = src_ref[...].astype(BF16)


def _prep_weights(p, vmem):
    d, n_in = p["w_in"].shape
    w_m = p["w_br_mlstm"].shape[0]
    w_f = p["w_br_fox"].shape[0]
    d_ff = p["w_gate"].shape[1]
    steps = 8
    o_ff = 4 * w_m + 2 * H_M + 3 * w_f
    assert (4 * w_m) % LANES == 0 and 2 * H_M <= GATE_GROUP + H_M <= LANES
    assert o_ff % LANES <= 2 * GATE_GROUP and o_ff % LANES + H_F <= LANES
    assert n_in == o_ff + H_F + 2 * d
    srcs = [p["w_in"].T, p["b_in"][None, :], p["w_br_mlstm"], p["w_br_fox"], p["w_out"],
            p["w_gate"], p["w_up"], p["w_down"]]

    def row_spec(shape, tiled=True):
        if not tiled:
            return pl.BlockSpec(shape, lambda i: (0, 0))
        assert shape[0] % (steps * BF16_ROWS) == 0
        return pl.BlockSpec((shape[0] // steps, shape[1]), lambda i: (i, 0))

    out_shapes = [((d, 4 * w_m), BF16), ((d, 3 * w_f), BF16), ((d, 2 * d), BF16), ((d, LANES), BF16),
                  ((1, 4 * w_m), F32), ((1, 3 * w_f), F32), ((1, 2 * d), F32), ((1, LANES), F32),
                  ((w_m, d), BF16), ((w_f, d), BF16), ((d, d), BF16),
                  ((d, d_ff), BF16), ((d, d_ff), BF16), ((d_ff, d), BF16)]
    return pl.pallas_call(
        functools.partial(_wprep_kernel, w_m=w_m, w_f=w_f),
        grid=(steps,),
        in_specs=([pl.BlockSpec((n_in, d // steps), lambda i: (0, i))]
                  + [row_spec(s.shape, tiled=s.shape[0] > 1) for s in srcs[1:]]),
        out_specs=[row_spec(s, tiled=s[0] > 1) for s, _ in out_shapes],
        out_shape=[jax.ShapeDtypeStruct(s, dt) for s, dt in out_shapes],
        compiler_params=pltpu.CompilerParams(dimension_semantics=("arbitrary",), vmem_limit_bytes=vmem),
        name="wprep",
    )(*srcs)


def _inproj_kernel(x_ref, g_ref, wa_ref, ba_ref, wb_ref, bb_ref, wc_ref, bc_ref, wd_ref, bd_ref,
                   cw_ref, cb_ref, zqk_ref, vt_ref, og_ref, zb_ref, zc_ref, zdt_ref, zs_sc,
                   *, w_m, w_f, n_chunk, q_scale, k_scale, chunk, tiles_per_seq):
    tm = x_ref.shape[0]
    seq_start = (pl.program_id(0) % tiles_per_seq) == 0
    hb = _rms(x_ref[...], g_ref[...]).astype(BF16)

    @pl.when(seq_start)
    def _():
        zs_sc[:, 0:SUBLANES, :] = jnp.zeros((zs_sc.shape[0], SUBLANES, n_chunk), F32)

    @pl.when(jnp.logical_not(seq_start))
    def _():
        zs_sc[:, 0:SUBLANES, :] = zs_sc[:, tm:tm + SUBLANES, :]

    def proj(w_ref, b_ref, c0, c1):
        return jnp.dot(hb, w_ref[:, c0:c1], preferred_element_type=F32) + b_ref[:, c0:c1]

    def qk_chunk(ci):
        c0 = ci * n_chunk
        cs = slice(c0, c0 + n_chunk)
        zs_sc[ci, SUBLANES:tm + SUBLANES, :] = proj(wa_ref, ba_ref, c0, c0 + n_chunk)
        w = cw_ref[:, cs]
        y = cb_ref[:, cs]
        for s in range(CONV_K):
            y = y + zs_sc[ci, SUBLANES - s:SUBLANES - s + tm, :] * w[CONV_K - 1 - s:CONV_K - s]
        act = y * jax.nn.sigmoid(y)
        if c0 >= w_m:
            act = act * k_scale
        zqk_ref[:, cs] = act.astype(BF16)

    def v_chunk(i):
        c0 = 2 * w_m + i * n_chunk
        z = proj(wa_ref, ba_ref, c0, c0 + n_chunk)
        for cc in range(tm // chunk):
            for f0 in range(0, n_chunk, LANES):
                blk = z[cc * chunk:(cc + 1) * chunk, f0:f0 + LANES]
                r0 = i * n_chunk + f0
                vt_ref[cc, r0:r0 + LANES, :] = blk.T.astype(BF16)

    def og_chunk(i):
        c0 = 3 * w_m + i * n_chunk
        z = proj(wa_ref, ba_ref, c0, c0 + n_chunk)
        og_ref[:, i * n_chunk:(i + 1) * n_chunk] = jax.nn.sigmoid(z).astype(BF16)

    def b_chunk(i):
        c0 = i * n_chunk
        z = proj(wb_ref, bb_ref, c0, c0 + n_chunk)
        if c0 < w_f:
            z = z * q_scale
        zb_ref[:, c0:c0 + n_chunk] = z.astype(BF16)

    def c_chunk(i):
        c0 = i * n_chunk
        z = proj(wc_ref, bc_ref, c0, c0 + n_chunk)
        zc_ref[:, c0:c0 + n_chunk] = jax.nn.sigmoid(z).astype(BF16)

    def d_chunk(_):
        zd = proj(wd_ref, bd_ref, 0, LANES)
        zdt_ref[...] = zd.T[:N_GATE_ROWS, :]

    heavy = [(qk_chunk, i) for i in range(2 * w_m // n_chunk)]
    light = ([(c_chunk, i) for i in range(wc_ref.shape[1] // n_chunk)]
             + [(b_chunk, i) for i in range(3 * w_f // n_chunk)]
             + [(v_chunk, i) for i in range(w_m // n_chunk)]
             + [(og_chunk, i) for i in range(w_m // n_chunk)] + [(d_chunk, 0)])
    per_heavy = len(light) // len(heavy)
    order = []
    for hi, task in enumerate(heavy):
        order.append(task)
        order.extend(light[hi * per_heavy:(hi + 1) * per_heavy])
    order.extend(light[len(heavy) * per_heavy:])
    for fn, i in order:
        fn(i)


def _scan_lanes(x, op, fill):
    n = x.shape[-1]
    lane = lax.broadcasted_iota(jnp.int32, x.shape, 1)
    s = 1
    while s < n:
        shifted = pltpu.roll(x, s, axis=1)
        x = op(x, jnp.where(lane >= s, shifted, fill))
        s *= 2
    return x


def _gates_kernel(zdt_ref, rows_ref, cols_ref, csplit_ref):
    z = zdt_ref[...]
    seq = z.shape[1]
    i8 = z[0:GATE_GROUP]
    f8 = _scan_lanes(_log_sigmoid(z[GATE_GROUP:2 * GATE_GROUP]), jnp.add, 0.0)
    cf8 = _scan_lanes(_log_sigmoid(z[2 * GATE_GROUP:3 * GATE_GROUP]), jnp.add, 0.0) * LOG2E
    g8 = i8 - f8
    m8 = jnp.maximum(_scan_lanes(g8, jnp.maximum, NEG_BIG), 0.0)
    en8 = jnp.exp(-(f8 + m8))
    rows_ref[0] = jnp.concatenate([g8, m8, en8], axis=0)
    stack = jnp.concatenate(
        [g8, m8, jnp.zeros((LANES - 2 * GATE_GROUP, seq), F32)], axis=0)
    cols_ref[...] = stack.T
    hi = cf8.astype(BF16).astype(F32)
    r1 = cf8 - hi
    lo = r1.astype(BF16).astype(F32)
    lo2 = (r1 - lo).astype(BF16).astype(F32)
    ones = jnp.where(lax.broadcasted_iota(jnp.int32, (GATE_GROUP, seq), 0) == 0, 1.0, 0.0)
    split = jnp.concatenate(
        [hi, lo, lo2, ones, jnp.zeros((LANES - 4 * GATE_GROUP, seq), F32)], axis=0)
    csplit_ref[...] = split.T.astype(BF16)


def _mlstm_kernel(zqk_ref, vt_ref, og_ref, rows_ref, cols_ref, ng_ref, ym_ref, ct_sc, *, w_m, chunk):
    seq = zqk_ref.shape[0]
    dh = w_m // H_M
    nc = seq // chunk
    ct_sc[...] = jnp.zeros(ct_sc.shape, F32)
    causal = (lax.broadcasted_iota(jnp.int32, (chunk, chunk), 0)
              <= lax.broadcasted_iota(jnp.int32, (chunk, chunk), 1))
    ones_rows = jnp.where(
        lax.broadcasted_iota(jnp.int32, (BF16_ROWS, chunk), 0) == 0, 1.0, 0.0).astype(BF16)

    def body(c, carry):
        r0 = pl.multiple_of(c * chunk, chunk)
        colsc = cols_ref[pl.ds(r0, chunk), :]
        last = cols_ref[pl.ds(r0 + chunk - 1, 1), :]
        prev = cols_ref[pl.ds(jnp.maximum(r0 - 1, 0), 1), :]
        prev = jnp.where(c > 0, prev, 0.0)
        first, vas = [], []
        for h in range(H_M):
            qc = zqk_ref[pl.ds(r0, chunk), h * dh:(h + 1) * dh]
            kc = zqk_ref[pl.ds(r0, chunk), w_m + h * dh:w_m + (h + 1) * dh]
            lhs = jnp.concatenate([kc, ct_sc[h].astype(BF16)], axis=0)
            first.append(lax.dot_general(lhs, qc, NT_DIMS, preferred_element_type=F32))
            vas.append(jnp.concatenate([vt_ref[c, h * dh:(h + 1) * dh, :], ones_rows], axis=0))
        for h in range(H_M):
            kc = zqk_ref[pl.ds(r0, chunk), w_m + h * dh:w_m + (h + 1) * dh]
            g_row = rows_ref[0, h, pl.ds(c, 1), :]
            m_e = last[:, GATE_GROUP + h:GATE_GROUP + h + 1]
            m_p = prev[:, GATE_GROUP + h:GATE_GROUP + h + 1]
            vaw = (vas[h].astype(F32) * jnp.exp(g_row - m_e)).astype(BF16)
            ct_sc[h] = jnp.exp(m_p - m_e) * ct_sc[h] + jnp.dot(vaw, kc, preferred_element_type=F32)
        for h in range(H_M):
            ch = slice(h * dh, (h + 1) * dh)
            g_col = colsc[:, h:h + 1]
            m_row = rows_ref[0, GATE_GROUP + h, pl.ds(c, 1), :]
            en_row = rows_ref[0, 2 * GATE_GROUP + h, pl.ds(c, 1), :]
            m_p = prev[:, GATE_GROUP + h:GATE_GROUP + h + 1]
            dmat = jnp.where(causal, jnp.exp(g_col - m_row), 0.0)
            sqk = (first[h][0:chunk] * dmat).astype(BF16)
            nd = (jnp.exp(m_p - m_row) * first[h][chunk:]
                  + jnp.dot(vas[h], sqk, preferred_element_type=F32))
            den = nd[dh:dh + 1]
            ht = nd[0:dh] * (1.0 / jnp.maximum(jnp.abs(den), en_row))
            hn = ht * lax.rsqrt(jnp.mean(ht * ht, axis=0, keepdims=True) + EPS)
            og = og_ref[pl.ds(r0, chunk), ch].astype(F32)
            ym_ref[pl.ds(r0, chunk), ch] = (hn.T * ng_ref[:, ch] * og).astype(BF16)
        return carry

    lax.fori_loop(0, nc, body, 0, unroll=True)


def _fox_kernel(q_ref, k_ref, v_ref, csq_ref, csk_ref, qsel_ref, ksel_ref, yf_ref,
                kaug_sc, vt_sc, qaug_sc, m_sc, acc_sc, *, tq, tk):
    qi = pl.program_id(1)

    seq = k_ref.shape[0]
    dhp = LANES
    half = dhp // 2
    v_rows = half + BF16_ROWS

    def own_lanes(head, rows):
        lane = lax.broadcasted_iota(jnp.int32, (rows, dhp), 1)
        return (lane >= half) if head % 2 else (lane < half)

    @pl.when(qi == 0)
    def _():
        csk = csk_ref[...]
        ones_rows = jnp.where(
            lax.broadcasted_iota(jnp.int32, (BF16_ROWS, tk), 0) == 0, 1.0, 0.0).astype(BF16)
        for p in range(H_F // 2):
            cp = slice(p * dhp, (p + 1) * dhp)
            kp = k_ref[:, cp]
            vt = v_ref[:, cp].astype(F32).T
            for head in (2 * p, 2 * p + 1):
                kb = jnp.dot(csk, ksel_ref[head], preferred_element_type=F32).astype(BF16)
                kaug_sc[head] = jnp.where(own_lanes(head, seq), kp, kb)
                r0 = (head % 2) * half
                for jb in range(seq // tk):
                    vt_sc[head, jb, 0:half, :] = vt[r0:r0 + half, jb * tk:(jb + 1) * tk].astype(BF16)
                    vt_sc[head, jb, half:v_rows, :] = ones_rows

    csq = csq_ref[...]
    for head in range(H_F):
        qp = q_ref[:, (head // 2) * dhp:(head // 2 + 1) * dhp]
        qb = jnp.dot(csq, qsel_ref[head], preferred_element_type=F32).astype(BF16)
        qaug_sc[head] = jnp.where(own_lanes(head, tq), qp, qb)
    m_sc[...] = jnp.full(m_sc.shape, NEG_BIG, F32)
    acc_sc[...] = jnp.zeros(acc_sc.shape, F32)

    lead_io = (lax.broadcasted_iota(jnp.int32, (tk, tq), 1)
               - lax.broadcasted_iota(jnp.int32, (tk, tq), 0))
    ratio = tq // tk

    def step(j, diag):
        k0 = pl.multiple_of(j * tk, tk)
        sts = [lax.dot_general(kaug_sc[head, pl.ds(k0, tk), :], qaug_sc[head], NT_DIMS,
                               preferred_element_type=F32) for head in range(H_F)]
        for head in range(H_F):
            st = sts[head]
            if diag is not None:
                st = jnp.where(lead_io >= diag * tk, st, NEG_BIG)
            m = m_sc[head]
            m_new = jnp.maximum(m, jnp.max(st, axis=0, keepdims=True))
            alpha = jnp.exp2(m - m_new)
            pt = jnp.exp2(st - m_new).astype(BF16)
            m_sc[head] = m_new
            acc_sc[head] = alpha * acc_sc[head] + jnp.dot(vt_sc[head, j], pt,
                                                          preferred_element_type=F32)

    def loop_body(jj, carry):
        for r in range(ratio):
            step(jj * ratio + r, None)
        return carry

    lax.fori_loop(0, qi, loop_body, 0)
    for diag in range(ratio):
        step(qi * ratio + diag, diag)
    for p in range(H_F // 2):
        outs = []
        for head in (2 * p, 2 * p + 1):
            acc = acc_sc[head]
            outs.append(acc[0:half] * (1.0 / acc[half:half + 1]))
        yf_ref[:, p * dhp:(p + 1) * dhp] = jnp.concatenate(outs, axis=0).T.astype(BF16)


def _post_kernel(x_ref, ym_ref, yf_ref, zc_ref, wbm_ref, wbf_ref, wo_ref, g2_ref,
                 wg_ref, wu_ref, wd_ref, gfin_ref, o_ref, *, tf):
    d = x_ref.shape[1]
    d_ff = wg_ref.shape[1]
    bm = jnp.dot(ym_ref[...], wbm_ref[...], preferred_element_type=F32)
    bf = jnp.dot(yf_ref[...], wbf_ref[...], preferred_element_type=F32)
    mix = zc_ref[:, 0:d].astype(F32) * bm + zc_ref[:, d:2 * d].astype(F32) * bf
    x1 = x_ref[...] + jnp.dot(mix.astype(BF16), wo_ref[...], preferred_element_type=F32)
    h2 = _rms(x1, g2_ref[...]).astype(BF16)
    acc = jnp.zeros(x1.shape, F32)
    for f0 in range(0, d_ff, tf):
        g = jnp.dot(h2, wg_ref[:, f0:f0 + tf], preferred_element_type=F32)
        u = jnp.dot(h2, wu_ref[:, f0:f0 + tf], preferred_element_type=F32)
        act = (g * jax.nn.sigmoid(g) * u).astype(BF16)
        acc = acc + jnp.dot(act, wd_ref[f0:f0 + tf, :], preferred_element_type=F32)
    o_ref[...] = _rms(x1 + acc, gfin_ref[...])


def _bias_selectors():
    ones_lane = 3 * GATE_GROUP
    qsel = np.zeros((H_F, LANES, LANES), np.float32)
    ksel = np.zeros((H_F, LANES, LANES), np.float32)
    for h in range(H_F):
        p0 = LANES // 2 if h % 2 == 0 else 0
        for c in range(3):
            qsel[h, GATE_GROUP * c + h, p0 + c] = 1.0
            qsel[h, ones_lane, p0 + 3 + c] = 1.0
            ksel[h, ones_lane, p0 + c] = 1.0
            ksel[h, GATE_GROUP * c + h, p0 + 3 + c] = -1.0
    return jnp.asarray(qsel, BF16), jnp.asarray(ksel, BF16)


def _layer(x2d, batch, seq, p, cfg):
    t, d = x2d.shape
    w_m = p["w_br_mlstm"].shape[0]
    w_f = p["w_br_fox"].shape[0]
    dh_m = w_m // H_M
    dh_f = w_f // H_F
    vmem = cfg["vmem_limit"]

    wa, wb, wc, wd, ba, bb, bc, bd, wbm, wbf, wo, wg, wu, wdn = _prep_weights(p, vmem)

    tm = cfg["tm_in"]
    chunk = cfg["chunk"]
    nc = seq // chunk
    assert seq % tm == 0 and tm % chunk == 0
    zqk, vt, og, zb, zc, zdt = pl.pallas_call(
        functools.partial(_inproj_kernel, w_m=w_m, w_f=w_f, n_chunk=cfg["n_chunk"],
                          q_scale=dh_f ** -0.5 * LOG2E, k_scale=dh_m ** -0.5, chunk=chunk,
                          tiles_per_seq=seq // tm),
        grid=(t // tm,),
        in_specs=[pl.BlockSpec((tm, d), lambda i: (i, 0)), _const_spec((1, d)),
                  _const_spec(wa.shape), _const_spec(ba.shape),
                  _const_spec(wb.shape), _const_spec(bb.shape),
                  _const_spec(wc.shape), _const_spec(bc.shape),
                  _const_spec(wd.shape), _const_spec(bd.shape),
                  _const_spec((CONV_K, 2 * w_m)), _const_spec((1, 2 * w_m))],
        out_specs=[pl.BlockSpec((tm, 2 * w_m), lambda i: (i, 0)),
                   pl.BlockSpec((tm // chunk, w_m, chunk), lambda i: (i, 0, 0)),
                   pl.BlockSpec((tm, w_m), lambda i: (i, 0)),
                   pl.BlockSpec((tm, 3 * w_f), lambda i: (i, 0)),
                   pl.BlockSpec((tm, 2 * d), lambda i: (i, 0)),
                   pl.BlockSpec((N_GATE_ROWS, tm), lambda i: (0, i))],
        out_shape=[jax.ShapeDtypeStruct((t, 2 * w_m), BF16),
                   jax.ShapeDtypeStruct((t // chunk, w_m, chunk), BF16),
                   jax.ShapeDtypeStruct((t, w_m), BF16),
                   jax.ShapeDtypeStruct((t, 3 * w_f), BF16),
                   jax.ShapeDtypeStruct((t, 2 * d), BF16),
                   jax.ShapeDtypeStruct((N_GATE_ROWS, t), F32)],
        scratch_shapes=[pltpu.VMEM((2 * w_m // cfg["n_chunk"], tm + SUBLANES, cfg["n_chunk"]), F32)],
        compiler_params=pltpu.CompilerParams(dimension_semantics=("arbitrary",), vmem_limit_bytes=vmem),
        name="inproj",
    )(x2d, p["norm1_g"][None, :], wa, ba, wb, bb, wc, bc, wd, bd, p["conv_w"], p["conv_b"][None, :])

    rows, cols, csplit = pl.pallas_call(
        _gates_kernel,
        grid=(batch,),
        in_specs=[pl.BlockSpec((N_GATE_ROWS, seq), lambda b: (0, b))],
        out_specs=[pl.BlockSpec((1, 3 * GATE_GROUP, seq), lambda b: (b, 0, 0)),
                   pl.BlockSpec((seq, LANES), lambda b: (b, 0)),
                   pl.BlockSpec((seq, LANES), lambda b: (b, 0))],
        out_shape=[jax.ShapeDtypeStruct((batch, 3 * GATE_GROUP, seq), F32),
                   jax.ShapeDtypeStruct((t, LANES), F32),
                   jax.ShapeDtypeStruct((t, LANES), BF16)],
        compiler_params=pltpu.CompilerParams(dimension_semantics=("parallel",), vmem_limit_bytes=vmem),
        name="gates",
    )(zdt)

    rows_m = rows.reshape(batch, 3 * GATE_GROUP, nc, chunk)
    ym = pl.pallas_call(
        functools.partial(_mlstm_kernel, w_m=w_m, chunk=chunk),
        grid=(batch,),
        in_specs=[pl.BlockSpec((seq, 2 * w_m), lambda b: (b, 0)),
                  pl.BlockSpec((nc, w_m, chunk), lambda b: (b, 0, 0)),
                  pl.BlockSpec((seq, w_m), lambda b: (b, 0)),
                  pl.BlockSpec((1, 3 * GATE_GROUP, nc, chunk), lambda b: (b, 0, 0, 0)),
                  pl.BlockSpec((seq, LANES), lambda b: (b, 0)),
                  _const_spec((1, w_m))],
        out_specs=pl.BlockSpec((seq, w_m), lambda b: (b, 0)),
        out_shape=jax.ShapeDtypeStruct((t, w_m), BF16),
        scratch_shapes=[pltpu.VMEM((H_M, dh_m + BF16_ROWS, dh_m), F32)],
        compiler_params=pltpu.CompilerParams(dimension_semantics=("parallel",), vmem_limit_bytes=vmem),
        name="mlstm",
    )(zqk, vt, og, rows_m, cols, p["mlstm_norm_g"][None, :])

    tq, tk = cfg["tq"], cfg["tk"]
    assert tq % tk == 0 and seq % tq == 0
    nq = seq // tq
    qsel, ksel = _bias_selectors()
    v_rows = LANES // 2 + BF16_ROWS
    yf = pl.pallas_call(
        functools.partial(_fox_kernel, tq=tq, tk=tk),
        grid=(batch, nq),
        in_specs=[pl.BlockSpec((tq, w_f), lambda b, i: (b * nq + i, 0)),
                  pl.BlockSpec((seq, w_f), lambda b, i: (b, 1)),
                  pl.BlockSpec((seq, w_f), lambda b, i: (b, 2)),
                  pl.BlockSpec((tq, LANES), lambda b, i: (b * nq + i, 0)),
                  pl.BlockSpec((seq, LANES), lambda b, i: (b, 0)),
                  _const_spec(qsel.shape), _const_spec(ksel.shape)],
        out_specs=pl.BlockSpec((tq, w_f), lambda b, i: (b * nq + i, 0)),
        out_shape=jax.ShapeDtypeStruct((t, w_f), BF16),
        scratch_shapes=[pltpu.VMEM((H_F, seq, LANES), BF16),
                        pltpu.VMEM((H_F, seq // tk, v_rows, tk), BF16),
                        pltpu.VMEM((H_F, tq, LANES), BF16),
                        pltpu.VMEM((H_F, 1, tq), F32),
                        pltpu.VMEM((H_F, v_rows, tq), F32)],
        compiler_params=pltpu.CompilerParams(dimension_semantics=("arbitrary", "arbitrary"),
                                             vmem_limit_bytes=vmem),
        name="fox",
    )(zb, zb, zb, csplit, csplit, qsel, ksel)

    tmp = cfg["tm_post"]
    return pl.pallas_call(
        functools.partial(_post_kernel, tf=cfg["tf"]),
        grid=(t // tmp,),
        in_specs=[pl.BlockSpec((tmp, d), lambda i: (i, 0)),
                  pl.BlockSpec((tmp, w_m), lambda i: (i, 0)),
                  pl.BlockSpec((tmp, w_f), lambda i: (i, 0)),
                  pl.BlockSpec((tmp, 2 * d), lambda i: (i, 0)),
                  _const_spec(wbm.shape), _const_spec(wbf.shape), _const_spec(wo.shape),
                  _const_spec((1, d)), _const_spec(wg.shape), _const_spec(wu.shape),
                  _const_spec(wdn.shape), _const_spec((1, d))],
        out_specs=pl.BlockSpec((tmp, d), lambda i: (i, 0)),
        out_shape=jax.ShapeDtypeStruct((t, d), F32),
        compiler_params=pltpu.CompilerParams(dimension_semantics=("parallel",), vmem_limit_bytes=vmem),
        name="post",
    )(x2d, ym, yf, zc, wbm, wbf, wo, p["norm2_g"][None, :], wg, wu, wdn, p["norm_f_g"][None, :])


def kernel(x, norm1_g, w_in, b_in, conv_w, conv_b, mlstm_norm_g, w_br_mlstm, w_br_fox, w_out,
           norm2_g, w_gate, w_up, w_down, norm_f_g):
    batch, seq, d = x.shape
    depth = w_in.shape[0]
    assert depth == 1, "the final norm is fused into the single layer's last call"
    cfg = _cfg(batch, seq, d, w_gate.shape[-1])
    p = dict(norm1_g=norm1_g[0], w_in=w_in[0], b_in=b_in[0], conv_w=conv_w[0], conv_b=conv_b[0],
             mlstm_norm_g=mlstm_norm_g[0], w_br_mlstm=w_br_mlstm[0], w_br_fox=w_br_fox[0],
             w_out=w_out[0], norm2_g=norm2_g[0], w_gate=w_gate[0], w_up=w_up[0], w_down=w_down[0],
             norm_f_g=norm_f_g)
    out = _layer(x.reshape(batch * seq, d), batch, seq, p, cfg)
    return out.reshape(batch, seq, d)
```

```python
import functools

import jax
import jax.numpy as jnp
import numpy as np
from jax import lax
from jax.experimental import pallas as pl
from jax.experimental.pallas import tpu as pltpu

EPS = 1e-6
H_M = 4
H_F = 8
CONV_K = 4

LANES = 128
SUBLANES = 8
BF16_ROWS = 16
GATE_GROUP = 8
N_GATE_ROWS = 3 * GATE_GROUP
V7X_VMEM_BYTES = 64 * 1024 * 1024
NEG_BIG = -1e30
LOG2E = 1.4426950408889634

F32 = jnp.float32
BF16 = jnp.bfloat16
NT_DIMS = (((1,), (1,)), ((), ()))


def _cfg(batch, seq, d_model, d_ff):
    return dict(
        tm_in=512,
        n_chunk=256,
        m_split=4,
        chunk=256,
        tq=512,
        tk=256,
        tm_post=512,
        tf=256,
        vmem_limit=V7X_VMEM_BYTES - 8 * 1024 * 1024,
    )


def _const_spec(shape):
    nd = len(shape)
    return pl.BlockSpec(shape, lambda *_: (0,) * nd, pipeline_mode=pl.Buffered(1))


def _rms(x, g):
    return x * lax.rsqrt(jnp.mean(x * x, axis=-1, keepdims=True) + EPS) * g


def _log_sigmoid(x):
    return jnp.minimum(x, 0.0) - jnp.log1p(jnp.exp(-jnp.abs(x)))


def _wprep_kernel(wint_ref, bin_ref, wbm_ref, wbf_ref, wo_ref, wg_ref, wu_ref, wdn_ref,
                  wa_o, wb_o, wc_o, wd_o, ba_o, bb_o, bc_o, bd_o,
                  wbm_o, wbf_o, wo_o, wg_o, wu_o, wdn_o, *, w_m, w_f):
    o_mi = 4 * w_m
    o_mf = o_mi + H_M
    o_fq = o_mf + H_M
    o_ff = o_fq + 3 * w_f
    o_g = o_ff + H_F

    def gate_block(src):
        rows = src.shape[0]
        lane = lax.broadcasted_iota(jnp.int32, (rows, LANES), 1)
        blk_m = src[:, o_mi:o_mi + LANES]
        f0 = (o_ff // LANES) * LANES
        blk_f = src[:, f0:f0 + LANES]
        mi = jnp.where(lane < H_M, blk_m, 0.0)
        mf = jnp.where((lane >= GATE_GROUP) & (lane < GATE_GROUP + H_M),
                       pltpu.roll(blk_m, GATE_GROUP - H_M, axis=1), 0.0)
        ff = jnp.where((lane >= 2 * GATE_GROUP) & (lane < 2 * GATE_GROUP + H_F),
                       pltpu.roll(blk_f, 2 * GATE_GROUP - (o_ff - f0), axis=1), 0.0)
        return mi + mf + ff

    slab = wint_ref[...]
    wa_o[...] = slab[0:o_mi].T.astype(BF16)
    wb_o[...] = slab[o_fq:o_ff].T.astype(BF16)
    wc_o[...] = slab[o_g:].T.astype(BF16)
    g_m = slab[o_mi:o_mi + GATE_GROUP]
    row = lax.broadcasted_iota(jnp.int32, g_m.shape, 0)
    gates = jnp.concatenate(
        [jnp.where(row < H_M, g_m, 0.0),
         jnp.where(row < H_M, pltpu.roll(g_m, GATE_GROUP - H_M, axis=0), 0.0),
         slab[o_ff:o_ff + GATE_GROUP],
         jnp.zeros((LANES - 3 * GATE_GROUP, slab.shape[1]), F32)], axis=0)
    wd_o[...] = gates.T.astype(BF16)
    b = bin_ref[...]
    ba_o[...] = b[:, 0:o_mi]
    bb_o[...] = b[:, o_fq:o_ff]
    bc_o[...] = b[:, o_g:]
    bd_o[...] = gate_block(b)
    for src_ref, dst in ((wbm_ref, wbm_o), (wbf_ref, wbf_o), (wo_ref, wo_o), (wg_ref, wg_o),
                         (wu_ref, wu_o), (wdn_ref, wdn_o)):
        dst[...] = src_ref[...].astype(BF16)


def _prep_weights(p, vmem):
    d, n_in = p["w_in"].shape
    w_m = p["w_br_mlstm"].shape[0]
    w_f = p["w_br_fox"].shape[0]
    d_ff = p["w_gate"].shape[1]
    steps = 8
    o_ff = 4 * w_m + 2 * H_M + 3 * w_f
    assert (4 * w_m) % LANES == 0 and 2 * H_M <= GATE_GROUP + H_M <= LANES
    assert o_ff % LANES <= 2 * GATE_GROUP and o_ff % LANES + H_F <= LANES
    assert n_in == o_ff + H_F + 2 * d
    srcs = [p["w_in"].T, p["b_in"][None, :], p["w_br_mlstm"], p["w_br_fox"], p["w_out"],
            p["w_gate"], p["w_up"], p["w_down"]]

    def row_spec(shape, tiled=True):
        if not tiled:
            return pl.BlockSpec(shape, lambda i: (0, 0))
        assert shape[0] % (steps * BF16_ROWS) == 0
        return pl.BlockSpec((shape[0] // steps, shape[1]), lambda i: (i, 0))

    out_shapes = [((d, 4 * w_m), BF16), ((d, 3 * w_f), BF16), ((d, 2 * d), BF16), ((d, LANES), BF16),
                  ((1, 4 * w_m), F32), ((1, 3 * w_f), F32), ((1, 2 * d), F32), ((1, LANES), F32),
                  ((w_m, d), BF16), ((w_f, d), BF16), ((d, d), BF16),
                  ((d, d_ff), BF16), ((d, d_ff), BF16), ((d_ff, d), BF16)]
    return pl.pallas_call(
        functools.partial(_wprep_kernel, w_m=w_m, w_f=w_f),
        grid=(steps,),
        in_specs=([pl.BlockSpec((n_in, d // steps), lambda i: (0, i))]
                  + [row_spec(s.shape, tiled=s.shape[0] > 1) for s in srcs[1:]]),
        out_specs=[row_spec(s, tiled=s[0] > 1) for s, _ in out_shapes],
        out_shape=[jax.ShapeDtypeStruct(s, dt) for s, dt in out_shapes],
        compiler_params=pltpu.CompilerParams(dimension_semantics=("arbitrary",), vmem_limit_bytes=vmem),
        name="wprep",
    )(*srcs)


def _inproj_kernel(x_ref, g_ref, wa_ref, ba_ref, wb_ref, bb_ref, wc_ref, bc_ref, wd_ref, bd_ref,
                   cw_ref, cb_ref, zqk_ref, vt_ref, og_ref, zb_ref, zc_ref, zdt_ref, zs_sc,
                   *, w_m, w_f, n_chunk, m_split, q_scale, k_scale, chunk, tiles_per_seq):
    tm = x_ref.shape[0]
    seq_start = (pl.program_id(0) % tiles_per_seq) == 0
    hb = _rms(x_ref[...], g_ref[...]).astype(BF16)

    @pl.when(seq_start)
    def _():
        zs_sc[:, 0:SUBLANES, :] = jnp.zeros((zs_sc.shape[0], SUBLANES, n_chunk), F32)

    @pl.when(jnp.logical_not(seq_start))
    def _():
        zs_sc[:, 0:SUBLANES, :] = zs_sc[:, tm:tm + SUBLANES, :]

    def proj(w_ref, b_ref, c0, c1):
        w = w_ref[:, c0:c1]
        rows = tm // m_split
        parts = [jnp.dot(hb[r0:r0 + rows], w, preferred_element_type=F32)
                 for r0 in range(0, tm, rows)]
        return jnp.concatenate(parts, axis=0) + b_ref[:, c0:c1]

    def qk_chunk(ci):
        c0 = ci * n_chunk
        cs = slice(c0, c0 + n_chunk)
        zs_sc[ci, SUBLANES:tm + SUBLANES, :] = proj(wa_ref, ba_ref, c0, c0 + n_chunk)
        w = cw_ref[:, cs]
        y = cb_ref[:, cs]
        for s in range(CONV_K):
            y = y + zs_sc[ci, SUBLANES - s:SUBLANES - s + tm, :] * w[CONV_K - 1 - s:CONV_K - s]
        act = y * jax.nn.sigmoid(y)
        if c0 >= w_m:
            act = act * k_scale
        zqk_ref[:, cs] = act.astype(BF16)

    def v_chunk(i):
        c0 = 2 * w_m + i * n_chunk
        z = proj(wa_ref, ba_ref, c0, c0 + n_chunk)
        for cc in range(tm // chunk):
            for f0 in range(0, n_chunk, LANES):
                blk = z[cc * chunk:(cc + 1) * chunk, f0:f0 + LANES]
                r0 = i * n_chunk + f0
                vt_ref[cc, r0:r0 + LANES, :] = blk.T.astype(BF16)

    def og_chunk(i):
        c0 = 3 * w_m + i * n_chunk
        z = proj(wa_ref, ba_ref, c0, c0 + n_chunk)
        og_ref[:, i * n_chunk:(i + 1) * n_chunk] = jax.nn.sigmoid(z).astype(BF16)

    def b_chunk(i):
        c0 = i * n_chunk
        z = proj(wb_ref, bb_ref, c0, c0 + n_chunk)
        if c0 < w_f:
            z = z * q_scale
        zb_ref[:, c0:c0 + n_chunk] = z.astype(BF16)

    def c_chunk(i):
        c0 = i * n_chunk
        z = proj(wc_ref, bc_ref, c0, c0 + n_chunk)
        zc_ref[:, c0:c0 + n_chunk] = jax.nn.sigmoid(z).astype(BF16)

    def d_chunk(_):
        zd = proj(wd_ref, bd_ref, 0, LANES)
        zdt_ref[...] = zd.T[:N_GATE_ROWS, :]

    heavy = [(qk_chunk, i) for i in range(2 * w_m // n_chunk)]
    light = ([(c_chunk, i) for i in range(wc_ref.shape[1] // n_chunk)]
             + [(b_chunk, i) for i in range(3 * w_f // n_chunk)]
             + [(v_chunk, i) for i in range(w_m // n_chunk)]
             + [(og_chunk, i) for i in range(w_m // n_chunk)] + [(d_chunk, 0)])
    per_heavy = len(light) // len(heavy)
    order = []
    for hi, task in enumerate(heavy):
        order.append(task)
        order.extend(light[hi * per_heavy:(hi + 1) * per_heavy])
    order.extend(light[len(heavy) * per_heavy:])
    for fn, i in order:
        fn(i)


def _scan_lanes(x, op, fill):
    n = x.shape[-1]
    lane = lax.broadcasted_iota(jnp.int32, x.shape, 1)
    s = 1
    while s < n:
        shifted = pltpu.roll(x, s, axis=1)
        x = op(x, jnp.where(lane >= s, shifted, fill))
        s *= 2
    return x


def _gates_kernel(zdt_ref, rows_ref, cols_ref, csplit_ref):
    z = zdt_ref[...]
    seq = z.shape[1]
    i8 = z[0:GATE_GROUP]
    f8 = _scan_lanes(_log_sigmoid(z[GATE_GROUP:2 * GATE_GROUP]), jnp.add, 0.0)
    cf8 = _scan_lanes(_log_sigmoid(z[2 * GATE_GROUP:3 * GATE_GROUP]), jnp.add, 0.0) * LOG2E
    g8 = i8 - f8
    m8 = jnp.maximum(_scan_lanes(g8, jnp.maximum, NEG_BIG), 0.0)
    en8 = jnp.exp(-(f8 + m8))
    rows_ref[0] = jnp.concatenate([g8, m8, en8], axis=0)
    stack = jnp.concatenate(
        [g8, m8, jnp.zeros((LANES - 2 * GATE_GROUP, seq), F32)], axis=0)
    cols_ref[...] = stack.T
    hi = cf8.astype(BF16).astype(F32)
    r1 = cf8 - hi
    lo = r1.astype(BF16).astype(F32)
    lo2 = (r1 - lo).astype(BF16).astype(F32)
    ones = jnp.where(lax.broadcasted_iota(jnp.int32, (GATE_GROUP, seq), 0) == 0, 1.0, 0.0)
    split = jnp.concatenate(
        [hi, lo, lo2, ones, jnp.zeros((LANES - 4 * GATE_GROUP, seq), F32)], axis=0)
    csplit_ref[...] = split.T.astype(BF16)


def _mlstm_kernel(zqk_ref, vt_ref, og_ref, rows_ref, cols_ref, ng_ref, ym_ref, ct_sc, *, w_m, chunk):
    seq = zqk_ref.shape[0]
    dh = w_m // H_M
    nc = seq // chunk
    ct_sc[...] = jnp.zeros(ct_sc.shape, F32)
    causal = (lax.broadcasted_iota(jnp.int32, (chunk, chunk), 0)
              <= lax.broadcasted_iota(jnp.int32, (chunk, chunk), 1))
    ones_rows = jnp.where(
        lax.broadcasted_iota(jnp.int32, (BF16_ROWS, chunk), 0) == 0, 1.0, 0.0).astype(BF16)

    def body(c, carry):
        r0 = pl.multiple_of(c * chunk, chunk)
        colsc = cols_ref[pl.ds(r0, chunk), :]
        last = cols_ref[pl.ds(r0 + chunk - 1, 1), :]
        prev = cols_ref[pl.ds(jnp.maximum(r0 - 1, 0), 1), :]
        prev = jnp.where(c > 0, prev, 0.0)
        first, vas = [], []
        for h in range(H_M):
            qc = zqk_ref[pl.ds(r0, chunk), h * dh:(h + 1) * dh]
            kc = zqk_ref[pl.ds(r0, chunk), w_m + h * dh:w_m + (h + 1) * dh]
            lhs = jnp.concatenate([kc, ct_sc[h].astype(BF16)], axis=0)
            first.append(lax.dot_general(lhs, qc, NT_DIMS, preferred_element_type=F32))
            vas.append(jnp.concatenate([vt_ref[c, h * dh:(h + 1) * dh, :], ones_rows], axis=0))
        for h in range(H_M):
            kc = zqk_ref[pl.ds(r0, chunk), w_m + h * dh:w_m + (h + 1) * dh]
            g_row = rows_ref[0, h, pl.ds(c, 1), :]
            m_e = last[:, GATE_GROUP + h:GATE_GROUP + h + 1]
            m_p = prev[:, GATE_GROUP + h:GATE_GROUP + h + 1]
            vaw = (vas[h].astype(F32) * jnp.exp(g_row - m_e)).astype(BF16)
            ct_sc[h] = jnp.exp(m_p - m_e) * ct_sc[h] + jnp.dot(vaw, kc, preferred_element_type=F32)
        for h in range(H_M):
            ch = slice(h * dh, (h + 1) * dh)
            g_col = colsc[:, h:h + 1]
            m_row = rows_ref[0, GATE_GROUP + h, pl.ds(c, 1), :]
            en_row = rows_ref[0, 2 * GATE_GROUP + h, pl.ds(c, 1), :]
            m_p = prev[:, GATE_GROUP + h:GATE_GROUP + h + 1]
            dmat = jnp.where(causal, jnp.exp(g_col - m_row), 0.0)
            sqk = (first[h][0:chunk] * dmat).astype(BF16)
            nd = (jnp.exp(m_p - m_row) * first[h][chunk:]
                  + jnp.dot(vas[h], sqk, preferred_element_type=F32))
            den = nd[dh:dh + 1]
            ht = nd[0:dh] * (1.0 / jnp.maximum(jnp.abs(den), en_row))
            hn = ht * lax.rsqrt(jnp.mean(ht * ht, axis=0, keepdims=True) + EPS)
            og = og_ref[pl.ds(r0, chunk), ch].astype(F32)
            ym_ref[pl.ds(r0, chunk), ch] = (hn.T * ng_ref[:, ch] * og).astype(BF16)
        return carry

    lax.fori_loop(0, nc, body, 0, unroll=True)


def _fox_kernel(q_ref, k_ref, v_ref, csq_ref, csk_ref, qsel_ref, ksel_ref, yf_ref,
                kaug_sc, vt_sc, qaug_sc, m_sc, acc_sc, *, tq, tk):
    qi = pl.program_id(1)

    seq = k_ref.shape[0]
    dhp = LANES
    half = dhp // 2
    v_rows = half + BF16_ROWS

    def own_lanes(head, rows):
        lane = lax.broadcasted_iota(jnp.int32, (rows, dhp), 1)
        return (lane >= half) if head % 2 else (lane < half)

    @pl.when(qi == 0)
    def _():
        csk = csk_ref[...]
        ones_rows = jnp.where(
            lax.broadcasted_iota(jnp.int32, (BF16_ROWS, tk), 0) == 0, 1.0, 0.0).astype(BF16)
        for p in range(H_F // 2):
            cp = slice(p * dhp, (p + 1) * dhp)
            kp = k_ref[:, cp]
            vt = v_ref[:, cp].astype(F32).T
            for head in (2 * p, 2 * p + 1):
                kb = jnp.dot(csk, ksel_ref[head], preferred_element_type=F32).astype(BF16)
                kaug_sc[head] = jnp.where(own_lanes(head, seq), kp, kb)
                r0 = (head % 2) * half
                for jb in range(seq // tk):
                    vt_sc[head, jb, 0:half, :] = vt[r0:r0 + half, jb * tk:(jb + 1) * tk].astype(BF16)
                    vt_sc[head, jb, half:v_rows, :] = ones_rows

    csq = csq_ref[...]
    for head in range(H_F):
        qp = q_ref[:, (head // 2) * dhp:(head // 2 + 1) * dhp]
        qb = jnp.dot(csq, qsel_ref[head], preferred_element_type=F32).astype(BF16)
        qaug_sc[head] = jnp.where(own_lanes(head, tq), qp, qb)
    m_sc[...] = jnp.full(m_sc.shape, NEG_BIG, F32)
    acc_sc[...] = jnp.zeros(acc_sc.shape, F32)

    ratio = tq // tk

    def step(j, diag):
        q0 = 0 if diag is None else diag * tk
        nq = tq - q0
        k0 = pl.multiple_of(j * tk, tk)
        sts = [lax.dot_general(kaug_sc[head, pl.ds(k0, tk), :], qaug_sc[head, q0:tq, :], NT_DIMS,
                               preferred_element_type=F32) for head in range(H_F)]
        if diag is not None:
            causal = (lax.broadcasted_iota(jnp.int32, (tk, nq), 0)
                      <= lax.broadcasted_iota(jnp.int32, (tk, nq), 1))
        for head in range(H_F):
            st = sts[head]
            if diag is not None:
                st = jnp.where(causal, st, NEG_BIG)
            m = m_sc[head, :, q0:tq]
            m_new = jnp.maximum(m, jnp.max(st, axis=0, keepdims=True))
            alpha = jnp.exp2(m - m_new)
            pt = jnp.exp2(st - m_new).astype(BF16)
            m_sc[head, :, q0:tq] = m_new
            acc_sc[head, :, q0:tq] = (alpha * acc_sc[head, :, q0:tq]
                                      + jnp.dot(vt_sc[head, j], pt, preferred_element_type=F32))

    def loop_body(jj, carry):
        for r in range(ratio):
            step(jj * ratio + r, None)
        return carry

    lax.fori_loop(0, qi, loop_body, 0)
    for diag in range(ratio):
        step(qi * ratio + diag, diag)
    for p in range(H_F // 2):
        outs = []
        for head in (2 * p, 2 * p + 1):
            acc = acc_sc[head]
            outs.append(acc[0:half] * (1.0 / acc[half:half + 1]))
        yf_ref[:, p * dhp:(p + 1) * dhp] = jnp.concatenate(outs, axis=0).T.astype(BF16)


def _post_kernel(x_ref, ym_ref, yf_ref, zc_ref, wbm_ref, wbf_ref, wo_ref, g2_ref,
                 wg_ref, wu_ref, wd_ref, gfin_ref, o_ref, *, tf):
    d = x_ref.shape[1]
    d_ff = wg_ref.shape[1]
    bm = jnp.dot(ym_ref[...], wbm_ref[...], preferred_element_type=F32)
    bf = jnp.dot(yf_ref[...], wbf_ref[...], preferred_element_type=F32)
    mix = zc_ref[:, 0:d].astype(F32) * bm + zc_ref[:, d:2 * d].astype(F32) * bf
    x1 = x_ref[...] + jnp.dot(mix.astype(BF16), wo_ref[...], preferred_element_type=F32)
    h2 = _rms(x1, g2_ref[...]).astype(BF16)
    acc = jnp.zeros(x1.shape, F32)
    for f0 in range(0, d_ff, tf):
        g = jnp.dot(h2, wg_ref[:, f0:f0 + tf], preferred_element_type=F32)
        u = jnp.dot(h2, wu_ref[:, f0:f0 + tf], preferred_element_type=F32)
        act = (g * jax.nn.sigmoid(g) * u).astype(BF16)
        acc = acc + jnp.dot(act, wd_ref[f0:f0 + tf, :], preferred_element_type=F32)
    o_ref[...] = _rms(x1 + acc, gfin_ref[...])


def _bias_selectors():
    ones_lane = 3 * GATE_GROUP
    qsel = np.zeros((H_F, LANES, LANES), np.float32)
    ksel = np.zeros((H_F, LANES, LANES), np.float32)
    for h in range(H_F):
        p0 = LANES // 2 if h % 2 == 0 else 0
        for c in range(3):
            qsel[h, GATE_GROUP * c + h, p0 + c] = 1.0
            qsel[h, ones_lane, p0 + 3 + c] = 1.0
            ksel[h, ones_lane, p0 + c] = 1.0
            ksel[h, GATE_GROUP * c + h, p0 + 3 + c] = -1.0
    return jnp.asarray(qsel, BF16), jnp.asarray(ksel, BF16)


def _layer(x2d, batch, seq, p, cfg):
    t, d = x2d.shape
    w_m = p["w_br_mlstm"].shape[0]
    w_f = p["w_br_fox"].shape[0]
    dh_m = w_m // H_M
    dh_f = w_f // H_F
    vmem = cfg["vmem_limit"]

    wa, wb, wc, wd, ba, bb, bc, bd, wbm, wbf, wo, wg, wu, wdn = _prep_weights(p, vmem)

    tm = cfg["tm_in"]
    chunk = cfg["chunk"]
    nc = seq // chunk
    assert seq % tm == 0 and tm % chunk == 0
    zqk, vt, og, zb, zc, zdt = pl.pallas_call(
        functools.partial(_inproj_kernel, w_m=w_m, w_f=w_f, n_chunk=cfg["n_chunk"],
                          m_split=cfg["m_split"],
                          q_scale=dh_f ** -0.5 * LOG2E, k_scale=dh_m ** -0.5, chunk=chunk,
                          tiles_per_seq=seq // tm),
        grid=(t // tm,),
        in_specs=[pl.BlockSpec((tm, d), lambda i: (i, 0)), _const_spec((1, d)),
                  _const_spec(wa.shape), _const_spec(ba.shape),
                  _const_spec(wb.shape), _const_spec(bb.shape),
                  _const_spec(wc.shape), _const_spec(bc.shape),
                  _const_spec(wd.shape), _const_spec(bd.shape),
                  _const_spec((CONV_K, 2 * w_m)), _const_spec((1, 2 * w_m))],
        out_specs=[pl.BlockSpec((tm, 2 * w_m), lambda i: (i, 0)),
                   pl.BlockSpec((tm // chunk, w_m, chunk), lambda i: (i, 0, 0)),
                   pl.BlockSpec((tm, w_m), lambda i: (i, 0)),
                   pl.BlockSpec((tm, 3 * w_f), lambda i: (i, 0)),
                   pl.BlockSpec((tm, 2 * d), lambda i: (i, 0)),
                   pl.BlockSpec((N_GATE_ROWS, tm), lambda i: (0, i))],
        out_shape=[jax.ShapeDtypeStruct((t, 2 * w_m), BF16),
                   jax.ShapeDtypeStruct((t // chunk, w_m, chunk), BF16),
                   jax.ShapeDtypeStruct((t, w_m), BF16),
                   jax.ShapeDtypeStruct((t, 3 * w_f), BF16),
                   jax.ShapeDtypeStruct((t, 2 * d), BF16),
                   jax.ShapeDtypeStruct((N_GATE_ROWS, t), F32)],
        scratch_shapes=[pltpu.VMEM((2 * w_m // cfg["n_chunk"], tm + SUBLANES, cfg["n_chunk"]), F32)],
        compiler_params=pltpu.CompilerParams(dimension_semantics=("arbitrary",), vmem_limit_bytes=vmem),
        name="inproj",
    )(x2d, p["norm1_g"][None, :], wa, ba, wb, bb, wc, bc, wd, bd, p["conv_w"], p["conv_b"][None, :])

    rows, cols, csplit = pl.pallas_call(
        _gates_kernel,
        grid=(batch,),
        in_specs=[pl.BlockSpec((N_GATE_ROWS, seq), lambda b: (0, b))],
        out_specs=[pl.BlockSpec((1, 3 * GATE_GROUP, seq), lambda b: (b, 0, 0)),
                   pl.BlockSpec((seq, LANES), lambda b: (b, 0)),
                   pl.BlockSpec((seq, LANES), lambda b: (b, 0))],
        out_shape=[jax.ShapeDtypeStruct((batch, 3 * GATE_GROUP, seq), F32),
                   jax.ShapeDtypeStruct((t, LANES), F32),
                   jax.ShapeDtypeStruct((t, LANES), BF16)],
        compiler_params=pltpu.CompilerParams(dimension_semantics=("parallel",), vmem_limit_bytes=vmem),
        name="gates",
    )(zdt)

    rows_m = rows.reshape(batch, 3 * GATE_GROUP, nc, chunk)
    ym = pl.pallas_call(
        functools.partial(_mlstm_kernel, w_m=w_m, chunk=chunk),
        grid=(batch,),
        in_specs=[pl.BlockSpec((seq, 2 * w_m), lambda b: (b, 0)),
                  pl.BlockSpec((nc, w_m, chunk), lambda b: (b, 0, 0)),
                  pl.BlockSpec((seq, w_m), lambda b: (b, 0)),
                  pl.BlockSpec((1, 3 * GATE_GROUP, nc, chunk), lambda b: (b, 0, 0, 0)),
                  pl.BlockSpec((seq, LANES), lambda b: (b, 0)),
                  _const_spec((1, w_m))],
        out_specs=pl.BlockSpec((seq, w_m), lambda b: (b, 0)),
        out_shape=jax.ShapeDtypeStruct((t, w_m), BF16),
        scratch_shapes=[pltpu.VMEM((H_M, dh_m + BF16_ROWS, dh_m), F32)],
        compiler_params=pltpu.CompilerParams(dimension_semantics=("parallel",), vmem_limit_bytes=vmem),
        name="mlstm",
    )(zqk, vt, og, rows_m, cols, p["mlstm_norm_g"][None, :])

    tq, tk = cfg["tq"], cfg["tk"]
    assert tq % tk == 0 and seq % tq == 0
    nq = seq // tq
    qsel, ksel = _bias_selectors()
    v_rows = LANES // 2 + BF16_ROWS
    yf = pl.pallas_call(
        functools.partial(_fox_kernel, tq=tq, tk=tk),
        grid=(batch, nq),
        in_specs=[pl.BlockSpec((tq, w_f), lambda b, i: (b * nq + i, 0)),
                  pl.BlockSpec((seq, w_f), lambda b, i: (b, 1)),
                  pl.BlockSpec((seq, w_f), lambda b, i: (b, 2)),
                  pl.BlockSpec((tq, LANES), lambda b, i: (b * nq + i, 0)),
                  pl.BlockSpec((seq, LANES), lambda b, i: (b, 0)),
                  _const_spec(qsel.shape), _const_spec(ksel.shape)],
        out_specs=pl.BlockSpec((tq, w_f), lambda b, i: (b * nq + i, 0)),
        out_shape=jax.ShapeDtypeStruct((t, w_f), BF16),
        scratch_shapes=[pltpu.VMEM((H_F, seq, LANES), BF16),
                        pltpu.VMEM((H_F, seq // tk, v_rows, tk), BF16),
                        pltpu.VMEM((H_F, tq, LANES), BF16),
                        pltpu.VMEM((H_F, 1, tq), F32),
                        pltpu.VMEM((H_F, v_rows, tq), F32)],
        compiler_params=pltpu.CompilerParams(dimension_semantics=("arbitrary", "arbitrary"),
                                             vmem_limit_bytes=vmem),
        name="fox",
    )(zb, zb, zb, csplit, csplit, qsel, ksel)

    tmp = cfg["tm_post"]
    return pl.pallas_call(
        functools.partial(_post_kernel, tf=cfg["tf"]),
        grid=(t // tmp,),
        in_specs=[pl.BlockSpec((tmp, d), lambda i: (i, 0)),
                  pl.BlockSpec((tmp, w_m), lambda i: (i, 0)),
                  pl.BlockSpec((tmp, w_f), lambda i: (i, 0)),
                  pl.BlockSpec((tmp, 2 * d), lambda i: (i, 0)),
                  _const_spec(wbm.shape), _const_spec(wbf.shape), _const_spec(wo.shape),
                  _const_spec((1, d)), _const_spec(wg.shape), _const_spec(wu.shape),
                  _const_spec(wdn.shape), _const_spec((1, d))],
        out_specs=pl.BlockSpec((tmp, d), lambda i: (i, 0)),
        out_shape=jax.ShapeDtypeStruct((t, d), F32),
        compiler_params=pltpu.CompilerParams(dimension_semantics=("parallel",), vmem_limit_bytes=vmem),
        name="post",
    )(x2d, ym, yf, zc, wbm, wbf, wo, p["norm2_g"][None, :], wg, wu, wdn, p["norm_f_g"][None, :])


def kernel(x, norm1_g, w_in, b_in, conv_w, conv_b, mlstm_norm_g, w_br_mlstm, w_br_fox, w_out,
           norm2_g, w_gate, w_up, w_down, norm_f_g):
    batch, seq, d = x.shape
    depth = w_in.shape[0]
    assert depth == 1, "the final norm is fused into the single layer's last call"
    cfg = _cfg(batch, seq, d, w_gate.shape[-1])
    p = dict(norm1_g=norm1_g[0], w_in=w_in[0], b_in=b_in[0], conv_w=conv_w[0], conv_b=conv_b[0],
             mlstm_norm_g=mlstm_norm_g[0], w_br_mlstm=w_br_mlstm[0], w_br_fox=w_br_fox[0],
             w_out=w_out[0], norm2_g=norm2_g[0], w_gate=w_gate[0], w_up=w_up[0], w_down=w_down[0],
             norm_f_g=norm_f_g)
    out = _layer(x.reshape(batch * seq, d), batch, seq, p, cfg)
    return out.reshape(batch, seq, d)
```

```python
import functools

import jax
import jax.numpy as jnp
import numpy as np
from jax import lax
from jax.experimental import pallas as pl
from jax.experimental.pallas import tpu as pltpu

EPS = 1e-6
H_M = 4
H_F = 8
CONV_K = 4

LANES = 128
SUBLANES = 8
BF16_ROWS = 16
GATE_GROUP = 8
N_GATE_ROWS = 3 * GATE_GROUP
V7X_VMEM_BYTES = 64 * 1024 * 1024
NEG_BIG = -1e30
LOG2E = 1.4426950408889634

F32 = jnp.float32
BF16 = jnp.bfloat16
NT_DIMS = (((1,), (1,)), ((), ()))


def _cfg(batch, seq, d_model, d_ff):
    return dict(
        tm_in=512,
        n_chunk=256,
        m_split=4,
        chunk=256,
        tq=512,
        tk=256,
        tm_post=512,
        tf=256,
        vmem_limit=V7X_VMEM_BYTES - 8 * 1024 * 1024,
    )


def _const_spec(shape):
    nd = len(shape)
    return pl.BlockSpec(shape, lambda *_: (0,) * nd, pipeline_mode=pl.Buffered(1))


def _rms(x, g):
    return x * lax.rsqrt(jnp.mean(x * x, axis=-1, keepdims=True) + EPS) * g


def _log_sigmoid(x):
    return jnp.minimum(x, 0.0) - jnp.log1p(jnp.exp(-jnp.abs(x)))


def _wprep_kernel(wint_ref, bin_ref, wbm_ref, wbf_ref, wo_ref, wg_ref, wu_ref, wdn_ref,
                  wa_o, wb_o, wc_o, wd_o, ba_o, bb_o, bc_o, bd_o,
                  wbm_o, wbf_o, wo_o, wg_o, wu_o, wdn_o, *, w_m, w_f):
    o_mi = 4 * w_m
    o_mf = o_mi + H_M
    o_fq = o_mf + H_M
    o_ff = o_fq + 3 * w_f
    o_g = o_ff + H_F

    def gate_block(src):
        rows = src.shape[0]
        lane = lax.broadcasted_iota(jnp.int32, (rows, LANES), 1)
        blk_m = src[:, o_mi:o_mi + LANES]
        f0 = (o_ff // LANES) * LANES
        blk_f = src[:, f0:f0 + LANES]
        mi = jnp.where(lane < H_M, blk_m, 0.0)
        mf = jnp.where((lane >= GATE_GROUP) & (lane < GATE_GROUP + H_M),
                       pltpu.roll(blk_m, GATE_GROUP - H_M, axis=1), 0.0)
        ff = jnp.where((lane >= 2 * GATE_GROUP) & (lane < 2 * GATE_GROUP + H_F),
                       pltpu.roll(blk_f, 2 * GATE_GROUP - (o_ff - f0), axis=1), 0.0)
        return mi + mf + ff

    slab = wint_ref[...]
    wa_o[...] = slab[0:o_mi].T.astype(BF16)
    wb_o[...] = slab[o_fq:o_ff].T.astype(BF16)
    wc_o[...] = slab[o_g:].T.astype(BF16)
    g_m = slab[o_mi:o_mi + GATE_GROUP]
    row = lax.broadcasted_iota(jnp.int32, g_m.shape, 0)
    gates = jnp.concatenate(
        [jnp.where(row < H_M, g_m, 0.0),
         jnp.where(row < H_M, pltpu.roll(g_m, GATE_GROUP - H_M, axis=0), 0.0),
         slab[o_ff:o_ff + GATE_GROUP],
         jnp.zeros((LANES - 3 * GATE_GROUP, slab.shape[1]), F32)], axis=0)
    wd_o[...] = gates.T.astype(BF16)
    b = bin_ref[...]
    ba_o[...] = b[:, 0:o_mi]
    bb_o[...] = b[:, o_fq:o_ff]
    bc_o[...] = b[:, o_g:]
    bd_o[...] = gate_block(b)
    for src_ref, dst in ((wbm_ref, wbm_o), (wbf_ref, wbf_o), (wo_ref, wo_o), (wg_ref, wg_o),
                         (wu_ref, wu_o), (wdn_ref, wdn_o)):
        dst[...] = src_ref[...].astype(BF16)


def _prep_weights(p, vmem):
    d, n_in = p["w_in"].shape
    w_m = p["w_br_mlstm"].shape[0]
    w_f = p["w_br_fox"].shape[0]
    d_ff = p["w_gate"].shape[1]
    steps = 8
    o_ff = 4 * w_m + 2 * H_M + 3 * w_f
    assert (4 * w_m) % LANES == 0 and 2 * H_M <= GATE_GROUP + H_M <= LANES
    assert o_ff % LANES <= 2 * GATE_GROUP and o_ff % LANES + H_F <= LANES
    assert n_in == o_ff + H_F + 2 * d
    srcs = [p["w_in"].T, p["b_in"][None, :], p["w_br_mlstm"], p["w_br_fox"], p["w_out"],
            p["w_gate"], p["w_up"], p["w_down"]]

    def row_spec(shape, tiled=True):
        if not tiled:
            return pl.BlockSpec(shape, lambda i: (0, 0))
        assert shape[0] % (steps * BF16_ROWS) == 0
        return pl.BlockSpec((shape[0] // steps, shape[1]), lambda i: (i, 0))

    out_shapes = [((d, 4 * w_m), BF16), ((d, 3 * w_f), BF16), ((d, 2 * d), BF16), ((d, LANES), BF16),
                  ((1, 4 * w_m), F32), ((1, 3 * w_f), F32), ((1, 2 * d), F32), ((1, LANES), F32),
                  ((w_m, d), BF16), ((w_f, d), BF16), ((d, d), BF16),
                  ((d, d_ff), BF16), ((d, d_ff), BF16), ((d_ff, d), BF16)]
    return pl.pallas_call(
        functools.partial(_wprep_kernel, w_m=w_m, w_f=w_f),
        grid=(steps,),
        in_specs=([pl.BlockSpec((n_in, d // steps), lambda i: (0, i))]
                  + [row_spec(s.shape, tiled=s.shape[0] > 1) for s in srcs[1:]]),
        out_specs=[row_spec(s, tiled=s[0] > 1) for s, _ in out_shapes],
        out_shape=[jax.ShapeDtypeStruct(s, dt) for s, dt in out_shapes],
        compiler_params=pltpu.CompilerParams(dimension_semantics=("arbitrary",), vmem_limit_bytes=vmem),
        name="wprep",
    )(*srcs)


def _inproj_kernel(x_ref, g_ref, wa_ref, ba_ref, wb_ref, bb_ref, wc_ref, bc_ref, wd_ref, bd_ref,
                   cw_ref, cb_ref, zqk_ref, vt_ref, og_ref, zb_ref, zc_ref, zdt_ref, zs_sc,
                   *, w_m, w_f, n_chunk, m_split, q_scale, k_scale, chunk, tiles_per_seq):
    tm = x_ref.shape[0]
    seq_start = (pl.program_id(0) % tiles_per_seq) == 0

    @pl.when(seq_start)
    def _():
        zs_sc[:, 0:SUBLANES, :] = jnp.zeros((zs_sc.shape[0], SUBLANES, n_chunk), F32)

    @pl.when(jnp.logical_not(seq_start))
    def _():
        zs_sc[:, 0:SUBLANES, :] = zs_sc[:, tm:tm + SUBLANES, :]

    rows = tm // m_split
    hb = [_rms(x_ref[r0:r0 + rows, :], g_ref[...]).astype(BF16) for r0 in range(0, tm, rows)]

    def proj(w_ref, b_ref, c0, c1):
        w = w_ref[:, c0:c1]
        parts = [jnp.dot(h, w, preferred_element_type=F32) for h in hb]
        return jnp.concatenate(parts, axis=0) + b_ref[:, c0:c1]

    def qk_chunk(ci):
        c0 = ci * n_chunk
        cs = slice(c0, c0 + n_chunk)
        zs_sc[ci, SUBLANES:tm + SUBLANES, :] = proj(wa_ref, ba_ref, c0, c0 + n_chunk)
        w = cw_ref[:, cs]
        y = cb_ref[:, cs]
        for s in range(CONV_K):
            y = y + zs_sc[ci, SUBLANES - s:SUBLANES - s + tm, :] * w[CONV_K - 1 - s:CONV_K - s]
        act = y * jax.nn.sigmoid(y)
        if c0 >= w_m:
            act = act * k_scale
        zqk_ref[:, cs] = act.astype(BF16)

    def v_chunk(i):
        c0 = 2 * w_m + i * n_chunk
        z = proj(wa_ref, ba_ref, c0, c0 + n_chunk)
        for cc in range(tm // chunk):
            for f0 in range(0, n_chunk, LANES):
                blk = z[cc * chunk:(cc + 1) * chunk, f0:f0 + LANES]
                r0 = i * n_chunk + f0
                vt_ref[cc, r0:r0 + LANES, :] = blk.T.astype(BF16)

    def og_chunk(i):
        c0 = 3 * w_m + i * n_chunk
        z = proj(wa_ref, ba_ref, c0, c0 + n_chunk)
        og_ref[:, i * n_chunk:(i + 1) * n_chunk] = jax.nn.sigmoid(z).astype(BF16)

    def b_chunk(i):
        c0 = i * n_chunk
        z = proj(wb_ref, bb_ref, c0, c0 + n_chunk)
        if c0 < w_f:
            z = z * q_scale
        zb_ref[:, c0:c0 + n_chunk] = z.astype(BF16)

    def c_chunk(i):
        c0 = i * n_chunk
        z = proj(wc_ref, bc_ref, c0, c0 + n_chunk)
        zc_ref[:, c0:c0 + n_chunk] = jax.nn.sigmoid(z).astype(BF16)

    def d_chunk(_):
        zd = proj(wd_ref, bd_ref, 0, LANES)
        zdt_ref[...] = zd.T[:N_GATE_ROWS, :]

    heavy = [(qk_chunk, i) for i in range(2 * w_m // n_chunk)]
    light = ([(c_chunk, i) for i in range(wc_ref.shape[1] // n_chunk)]
             + [(b_chunk, i) for i in range(3 * w_f // n_chunk)]
             + [(v_chunk, i) for i in range(w_m // n_chunk)]
             + [(og_chunk, i) for i in range(w_m // n_chunk)] + [(d_chunk, 0)])
    per_heavy = len(light) // len(heavy)
    order = []
    for hi, task in enumerate(heavy):
        order.append(task)
        order.extend(light[hi * per_heavy:(hi + 1) * per_heavy])
    order.extend(light[len(heavy) * per_heavy:])
    for fn, i in order:
        fn(i)


def _scan_lanes(x, op, fill):
    n = x.shape[-1]
    lane = lax.broadcasted_iota(jnp.int32, x.shape, 1)
    s = 1
    while s < n:
        shifted = pltpu.roll(x, s, axis=1)
        x = op(x, jnp.where(lane >= s, shifted, fill))
        s *= 2
    return x


def _gates_kernel(zdt_ref, rows_ref, cols_ref, csplit_ref):
    z = zdt_ref[...]
    seq = z.shape[1]
    i8 = z[0:GATE_GROUP]
    f8 = _scan_lanes(_log_sigmoid(z[GATE_GROUP:2 * GATE_GROUP]), jnp.add, 0.0)
    cf8 = _scan_lanes(_log_sigmoid(z[2 * GATE_GROUP:3 * GATE_GROUP]), jnp.add, 0.0) * LOG2E
    g8 = i8 - f8
    m8 = jnp.maximum(_scan_lanes(g8, jnp.maximum, NEG_BIG), 0.0)
    en8 = jnp.exp(-(f8 + m8))
    rows_ref[0] = jnp.concatenate([g8, m8, en8], axis=0)
    stack = jnp.concatenate(
        [g8, m8, jnp.zeros((LANES - 2 * GATE_GROUP, seq), F32)], axis=0)
    cols_ref[...] = stack.T
    hi = cf8.astype(BF16).astype(F32)
    r1 = cf8 - hi
    lo = r1.astype(BF16).astype(F32)
    lo2 = (r1 - lo).astype(BF16).astype(F32)
    ones = jnp.where(lax.broadcasted_iota(jnp.int32, (GATE_GROUP, seq), 0) == 0, 1.0, 0.0)
    split = jnp.concatenate(
        [hi, lo, lo2, ones, jnp.zeros((LANES - 4 * GATE_GROUP, seq), F32)], axis=0)
    csplit_ref[...] = split.T.astype(BF16)


def _mlstm_kernel(zqk_ref, vt_ref, og_ref, rows_ref, cols_ref, ng_ref, ym_ref, ct_sc, *, w_m, chunk):
    seq = zqk_ref.shape[0]
    dh = w_m // H_M
    nc = seq // chunk
    ct_sc[...] = jnp.zeros(ct_sc.shape, F32)
    causal = (lax.broadcasted_iota(jnp.int32, (chunk, chunk), 0)
              <= lax.broadcasted_iota(jnp.int32, (chunk, chunk), 1))
    ones_rows = jnp.where(
        lax.broadcasted_iota(jnp.int32, (BF16_ROWS, chunk), 0) == 0, 1.0, 0.0).astype(BF16)

    def body(c, carry):
        r0 = pl.multiple_of(c * chunk, chunk)
        colsc = cols_ref[pl.ds(r0, chunk), :]
        last = cols_ref[pl.ds(r0 + chunk - 1, 1), :]
        prev = cols_ref[pl.ds(jnp.maximum(r0 - 1, 0), 1), :]
        prev = jnp.where(c > 0, prev, 0.0)
        first, vas = [], []
        for h in range(H_M):
            qc = zqk_ref[pl.ds(r0, chunk), h * dh:(h + 1) * dh]
            kc = zqk_ref[pl.ds(r0, chunk), w_m + h * dh:w_m + (h + 1) * dh]
            lhs = jnp.concatenate([kc, ct_sc[h].astype(BF16)], axis=0)
            first.append(lax.dot_general(lhs, qc, NT_DIMS, preferred_element_type=F32))
            vas.append(jnp.concatenate([vt_ref[c, h * dh:(h + 1) * dh, :], ones_rows], axis=0))
        for h in range(H_M):
            kc = zqk_ref[pl.ds(r0, chunk), w_m + h * dh:w_m + (h + 1) * dh]
            g_row = rows_ref[0, h, pl.ds(c, 1), :]
            m_e = last[:, GATE_GROUP + h:GATE_GROUP + h + 1]
            m_p = prev[:, GATE_GROUP + h:GATE_GROUP + h + 1]
            vaw = (vas[h].astype(F32) * jnp.exp(g_row - m_e)).astype(BF16)
            ct_sc[h] = jnp.exp(m_p - m_e) * ct_sc[h] + jnp.dot(vaw, kc, preferred_element_type=F32)
        for h in range(H_M):
            ch = slice(h * dh, (h + 1) * dh)
            g_col = colsc[:, h:h + 1]
            m_row = rows_ref[0, GATE_GROUP + h, pl.ds(c, 1), :]
            en_row = rows_ref[0, 2 * GATE_GROUP + h, pl.ds(c, 1), :]
            m_p = prev[:, GATE_GROUP + h:GATE_GROUP + h + 1]
            dmat = jnp.where(causal, jnp.exp(g_col - m_row), 0.0)
            sqk = (first[h][0:chunk] * dmat).astype(BF16)
            nd = (jnp.exp(m_p - m_row) * first[h][chunk:]
                  + jnp.dot(vas[h], sqk, preferred_element_type=F32))
            den = nd[dh:dh + 1]
            ht = nd[0:dh] * (1.0 / jnp.maximum(jnp.abs(den), en_row))
            hn = ht * lax.rsqrt(jnp.mean(ht * ht, axis=0, keepdims=True) + EPS)
            og = og_ref[pl.ds(r0, chunk), ch].astype(F32)
            ym_ref[pl.ds(r0, chunk), ch] = (hn.T * ng_ref[:, ch] * og).astype(BF16)
        return carry

    lax.fori_loop(0, nc, body, 0, unroll=True)


def _fox_kernel(q_ref, k_ref, v_ref, csq_ref, csk_ref, qsel_ref, ksel_ref, yf_ref,
                kaug_sc, vt_sc, qaug_sc, m_sc, acc_sc, *, tq, tk):
    qi = pl.program_id(1)

    seq = k_ref.shape[0]
    dhp = LANES
    half = dhp // 2
    v_rows = half + BF16_ROWS

    def own_lanes(head, rows):
        lane = lax.broadcasted_iota(jnp.int32, (rows, dhp), 1)
        return (lane >= half) if head % 2 else (lane < half)

    @pl.when(qi == 0)
    def _():
        csk = csk_ref[...]
        ones_rows = jnp.where(
            lax.broadcasted_iota(jnp.int32, (BF16_ROWS, tk), 0) == 0, 1.0, 0.0).astype(BF16)
        for p in range(H_F // 2):
            cp = slice(p * dhp, (p + 1) * dhp)
            kp = k_ref[:, cp]
            vt = v_ref[:, cp].astype(F32).T
            for head in (2 * p, 2 * p + 1):
                kb = jnp.dot(csk, ksel_ref[head], preferred_element_type=F32).astype(BF16)
                kaug_sc[head] = jnp.where(own_lanes(head, seq), kp, kb)
                r0 = (head % 2) * half
                for jb in range(seq // tk):
                    vt_sc[head, jb, 0:half, :] = vt[r0:r0 + half, jb * tk:(jb + 1) * tk].astype(BF16)
                    vt_sc[head, jb, half:v_rows, :] = ones_rows

    csq = csq_ref[...]
    for head in range(H_F):
        qp = q_ref[:, (head // 2) * dhp:(head // 2 + 1) * dhp]
        qb = jnp.dot(csq, qsel_ref[head], preferred_element_type=F32).astype(BF16)
        qaug_sc[head] = jnp.where(own_lanes(head, tq), qp, qb)
    m_sc[...] = jnp.full(m_sc.shape, NEG_BIG, F32)
    acc_sc[...] = jnp.zeros(acc_sc.shape, F32)

    ratio = tq // tk

    def step(j, diag):
        q0 = 0 if diag is None else diag * tk
        nq = tq - q0
        k0 = pl.multiple_of(j * tk, tk)
        sts = [lax.dot_general(kaug_sc[head, pl.ds(k0, tk), :], qaug_sc[head, q0:tq, :], NT_DIMS,
                               preferred_element_type=F32) for head in range(H_F)]
        if diag is not None:
            causal = (lax.broadcasted_iota(jnp.int32, (tk, nq), 0)
                      <= lax.broadcasted_iota(jnp.int32, (tk, nq), 1))
        for head in range(H_F):
            st = sts[head]
            if diag is not None:
                st = jnp.where(causal, st, NEG_BIG)
            m = m_sc[head, :, q0:tq]
            m_new = jnp.maximum(m, jnp.max(st, axis=0, keepdims=True))
            alpha = jnp.exp2(m - m_new)
            pt = jnp.exp2(st - m_new).astype(BF16)
            m_sc[head, :, q0:tq] = m_new
            acc_sc[head, :, q0:tq] = (alpha * acc_sc[head, :, q0:tq]
                                      + jnp.dot(vt_sc[head, j], pt, preferred_element_type=F32))

    def loop_body(jj, carry):
        for r in range(ratio):
            step(jj * ratio + r, None)
        return carry

    lax.fori_loop(0, qi, loop_body, 0)
    for diag in range(ratio):
        step(qi * ratio + diag, diag)
    for p in range(H_F // 2):
        outs = []
        for head in (2 * p, 2 * p + 1):
            acc = acc_sc[head]
            outs.append(acc[0:half] * (1.0 / acc[half:half + 1]))
        yf_ref[:, p * dhp:(p + 1) * dhp] = jnp.concatenate(outs, axis=0).T.astype(BF16)


def _post_kernel(x_ref, ym_ref, yf_ref, zc_ref, wbm_ref, wbf_ref, wo_ref, g2_ref,
                 wg_ref, wu_ref, wd_ref, gfin_ref, o_ref, *, tf):
    d = x_ref.shape[1]
    d_ff = wg_ref.shape[1]
    bm = jnp.dot(ym_ref[...], wbm_ref[...], preferred_element_type=F32)
    bf = jnp.dot(yf_ref[...], wbf_ref[...], preferred_element_type=F32)
    mix = zc_ref[:, 0:d].astype(F32) * bm + zc_ref[:, d:2 * d].astype(F32) * bf
    x1 = x_ref[...] + jnp.dot(mix.astype(BF16), wo_ref[...], preferred_element_type=F32)
    h2 = _rms(x1, g2_ref[...]).astype(BF16)
    acc = jnp.zeros(x1.shape, F32)
    for f0 in range(0, d_ff, tf):
        g = jnp.dot(h2, wg_ref[:, f0:f0 + tf], preferred_element_type=F32)
        u = jnp.dot(h2, wu_ref[:, f0:f0 + tf], preferred_element_type=F32)
        act = (g * jax.nn.sigmoid(g) * u).astype(BF16)
        acc = acc + jnp.dot(act, wd_ref[f0:f0 + tf, :], preferred_element_type=F32)
    o_ref[...] = _rms(x1 + acc, gfin_ref[...])


def _bias_selectors():
    ones_lane = 3 * GATE_GROUP
    qsel = np.zeros((H_F, LANES, LANES), np.float32)
    ksel = np.zeros((H_F, LANES, LANES), np.float32)
    for h in range(H_F):
        p0 = LANES // 2 if h % 2 == 0 else 0
        for c in range(3):
            qsel[h, GATE_GROUP * c + h, p0 + c] = 1.0
            qsel[h, ones_lane, p0 + 3 + c] = 1.0
            ksel[h, ones_lane, p0 + c] = 1.0
            ksel[h, GATE_GROUP * c + h, p0 + 3 + c] = -1.0
    return jnp.asarray(qsel, BF16), jnp.asarray(ksel, BF16)


def _layer(x2d, batch, seq, p, cfg):
    t, d = x2d.shape
    w_m = p["w_br_mlstm"].shape[0]
    w_f = p["w_br_fox"].shape[0]
    dh_m = w_m // H_M
    dh_f = w_f // H_F
    vmem = cfg["vmem_limit"]

    wa, wb, wc, wd, ba, bb, bc, bd, wbm, wbf, wo, wg, wu, wdn = _prep_weights(p, vmem)

    tm = cfg["tm_in"]
    chunk = cfg["chunk"]
    nc = seq // chunk
    assert seq % tm == 0 and tm % chunk == 0
    zqk, vt, og, zb, zc, zdt = pl.pallas_call(
        functools.partial(_inproj_kernel, w_m=w_m, w_f=w_f, n_chunk=cfg["n_chunk"],
                          m_split=cfg["m_split"],
                          q_scale=dh_f ** -0.5 * LOG2E, k_scale=dh_m ** -0.5, chunk=chunk,
                          tiles_per_seq=seq // tm),
        grid=(t // tm,),
        in_specs=[pl.BlockSpec((tm, d), lambda i: (i, 0)), _const_spec((1, d)),
                  _const_spec(wa.shape), _const_spec(ba.shape),
                  _const_spec(wb.shape), _const_spec(bb.shape),
                  _const_spec(wc.shape), _const_spec(bc.shape),
                  _const_spec(wd.shape), _const_spec(bd.shape),
                  _const_spec((CONV_K, 2 * w_m)), _const_spec((1, 2 * w_m))],
        out_specs=[pl.BlockSpec((tm, 2 * w_m), lambda i: (i, 0)),
                   pl.BlockSpec((tm // chunk, w_m, chunk), lambda i: (i, 0, 0)),
                   pl.BlockSpec((tm, w_m), lambda i: (i, 0)),
                   pl.BlockSpec((tm, 3 * w_f), lambda i: (i, 0)),
                   pl.BlockSpec((tm, 2 * d), lambda i: (i, 0)),
                   pl.BlockSpec((N_GATE_ROWS, tm), lambda i: (0, i))],
        out_shape=[jax.ShapeDtypeStruct((t, 2 * w_m), BF16),
                   jax.ShapeDtypeStruct((t // chunk, w_m, chunk), BF16),
                   jax.ShapeDtypeStruct((t, w_m), BF16),
                   jax.ShapeDtypeStruct((t, 3 * w_f), BF16),
                   jax.ShapeDtypeStruct((t, 2 * d), BF16),
                   jax.ShapeDtypeStruct((N_GATE_ROWS, t), F32)],
        scratch_shapes=[pltpu.VMEM((2 * w_m // cfg["n_chunk"], tm + SUBLANES, cfg["n_chunk"]), F32)],
        compiler_params=pltpu.CompilerParams(dimension_semantics=("arbitrary",), vmem_limit_bytes=vmem),
        name="inproj",
    )(x2d, p["norm1_g"][None, :], wa, ba, wb, bb, wc, bc, wd, bd, p["conv_w"], p["conv_b"][None, :])

    rows, cols, csplit = pl.pallas_call(
        _gates_kernel,
        grid=(batch,),
        in_specs=[pl.BlockSpec((N_GATE_ROWS, seq), lambda b: (0, b))],
        out_specs=[pl.BlockSpec((1, 3 * GATE_GROUP, seq), lambda b: (b, 0, 0)),
                   pl.BlockSpec((seq, LANES), lambda b: (b, 0)),
                   pl.BlockSpec((seq, LANES), lambda b: (b, 0))],
        out_shape=[jax.ShapeDtypeStruct((batch, 3 * GATE_GROUP, seq), F32),
                   jax.ShapeDtypeStruct((t, LANES), F32),
                   jax.ShapeDtypeStruct((t, LANES), BF16)],
        compiler_params=pltpu.CompilerParams(dimension_semantics=("parallel",), vmem_limit_bytes=vmem),
        name="gates",
    )(zdt)

    rows_m = rows.reshape(batch, 3 * GATE_GROUP, nc, chunk)
    ym = pl.pallas_call(
        functools.partial(_mlstm_kernel, w_m=w_m, chunk=chunk),
        grid=(batch,),
        in_specs=[pl.BlockSpec((seq, 2 * w_m), lambda b: (b, 0)),
                  pl.BlockSpec((nc, w_m, chunk), lambda b: (b, 0, 0)),
                  pl.BlockSpec((seq, w_m), lambda b: (b, 0)),
                  pl.BlockSpec((1, 3 * GATE_GROUP, nc, chunk), lambda b: (b, 0, 0, 0)),
                  pl.BlockSpec((seq, LANES), lambda b: (b, 0)),
                  _const_spec((1, w_m))],
        out_specs=pl.BlockSpec((seq, w_m), lambda b: (b, 0)),
        out_shape=jax.ShapeDtypeStruct((t, w_m), BF16),
        scratch_shapes=[pltpu.VMEM((H_M, dh_m + BF16_ROWS, dh_m), F32)],
        compiler_params=pltpu.CompilerParams(dimension_semantics=("parallel",), vmem_limit_bytes=vmem),
        name="mlstm",
    )(zqk, vt, og, rows_m, cols, p["mlstm_norm_g"][None, :])

    tq, tk = cfg["tq"], cfg["tk"]
    assert tq % tk == 0 and seq % tq == 0
    nq = seq // tq
    qsel, ksel = _bias_selectors()
    v_rows = LANES // 2 + BF16_ROWS
    yf = pl.pallas_call(
        functools.partial(_fox_kernel, tq=tq, tk=tk),
        grid=(batch, nq),
        in_specs=[pl.BlockSpec((tq, w_f), lambda b, i: (b * nq + i, 0)),
                  pl.BlockSpec((seq, w_f), lambda b, i: (b, 1)),
                  pl.BlockSpec((seq, w_f), lambda b, i: (b, 2)),
                  pl.BlockSpec((tq, LANES), lambda b, i: (b * nq + i, 0)),
                  pl.BlockSpec((seq, LANES), lambda b, i: (b, 0)),
                  _const_spec(qsel.shape), _const_spec(ksel.shape)],
        out_specs=pl.BlockSpec((tq, w_f), lambda b, i: (b * nq + i, 0)),
        out_shape=jax.ShapeDtypeStruct((t, w_f), BF16),
        scratch_shapes=[pltpu.VMEM((H_F, seq, LANES), BF16),
                        pltpu.VMEM((H_F, seq // tk, v_rows, tk), BF16),
                        pltpu.VMEM((H_F, tq, LANES), BF16),
                        pltpu.VMEM((H_F, 1, tq), F32),
                        pltpu.VMEM((H_F, v_rows, tq), F32)],
        compiler_params=pltpu.CompilerParams(dimension_semantics=("arbitrary", "arbitrary"),
                                             vmem_limit_bytes=vmem),
        name="fox",
    )(zb, zb, zb, csplit, csplit, qsel, ksel)

    tmp = cfg["tm_post"]
    return pl.pallas_call(
        functools.partial(_post_kernel, tf=cfg["tf"]),
        grid=(t // tmp,),
        in_specs=[pl.BlockSpec((tmp, d), lambda i: (i, 0)),
                  pl.BlockSpec((tmp, w_m), lambda i: (i, 0)),
                  pl.BlockSpec((tmp, w_f), lambda i: (i, 0)),
                  pl.BlockSpec((tmp, 2 * d), lambda i: (i, 0)),
                  _const_spec(wbm.shape), _const_spec(wbf.shape), _const_spec(wo.shape),
                  _const_spec((1, d)), _const_spec(wg.shape), _const_spec(wu.shape),
                  _const_spec(wdn.shape), _const_spec((1, d))],
        out_specs=pl.BlockSpec((tmp, d), lambda i: (i, 0)),
        out_shape=jax.ShapeDtypeStruct((t, d), F32),
        compiler_params=pltpu.CompilerParams(dimension_semantics=("parallel",), vmem_limit_bytes=vmem),
        name="post",
    )(x2d, ym, yf, zc, wbm, wbf, wo, p["norm2_g"][None, :], wg, wu, wdn, p["norm_f_g"][None, :])


def kernel(x, norm1_g, w_in, b_in, conv_w, conv_b, mlstm_norm_g, w_br_mlstm, w_br_fox, w_out,
           norm2_g, w_gate, w_up, w_down, norm_f_g):
    batch, seq, d = x.shape
    depth = w_in.shape[0]
    assert depth == 1, "the final norm is fused into the single layer's last call"
    cfg = _cfg(batch, seq, d, w_gate.shape[-1])
    p = dict(norm1_g=norm1_g[0], w_in=w_in[0], b_in=b_in[0], conv_w=conv_w[0], conv_b=conv_b[0],
             mlstm_norm_g=mlstm_norm_g[0], w_br_mlstm=w_br_mlstm[0], w_br_fox=w_br_fox[0],
             w_out=w_out[0], norm2_g=norm2_g[0], w_gate=w_gate[0], w_up=w_up[0], w_down=w_down[0],
             norm_f_g=norm_f_g)
    out = _layer(x.reshape(batch * seq, d), batch, seq, p, cfg)
    return out.reshape(batch, seq, d)
```

```python
import functools

import jax
import jax.numpy as jnp
import numpy as np
from jax import lax
from jax.experimental import pallas as pl
from jax.experimental.pallas import tpu as pltpu

EPS = 1e-6
H_M = 4
H_F = 8
CONV_K = 4

LANES = 128
SUBLANES = 8
BF16_ROWS = 16
GATE_GROUP = 8
N_GATE_ROWS = 3 * GATE_GROUP
V7X_VMEM_BYTES = 64 * 1024 * 1024
NEG_BIG = -1e30
LOG2E = 1.4426950408889634

F32 = jnp.float32
BF16 = jnp.bfloat16
NT_DIMS = (((1,), (1,)), ((), ()))


def _cfg(batch, seq, d_model, d_ff):
    return dict(
        tm_in=512,
        n_chunk=256,
        m_split=4,
        chunk=256,
        tq=512,
        tk=256,
        tm_post=512,
        tf=256,
        vmem_limit=V7X_VMEM_BYTES - 8 * 1024 * 1024,
    )


def _const_spec(shape):
    nd = len(shape)
    return pl.BlockSpec(shape, lambda *_: (0,) * nd, pipeline_mode=pl.Buffered(1))


def _rms(x, g):
    return x * lax.rsqrt(jnp.mean(x * x, axis=-1, keepdims=True) + EPS) * g


def _log_sigmoid(x):
    return jnp.minimum(x, 0.0) - jnp.log1p(jnp.exp(-jnp.abs(x)))


def _wprep_kernel(wint_ref, bin_ref, wa_o, wb_o, wc_o, wd_o, ba_o, bb_o, bc_o, bd_o, *, w_m, w_f):
    o_mi = 4 * w_m
    o_mf = o_mi + H_M
    o_fq = o_mf + H_M
    o_ff = o_fq + 3 * w_f
    o_g = o_ff + H_F

    def gate_block(src):
        rows = src.shape[0]
        lane = lax.broadcasted_iota(jnp.int32, (rows, LANES), 1)
        blk_m = src[:, o_mi:o_mi + LANES]
        f0 = (o_ff // LANES) * LANES
        blk_f = src[:, f0:f0 + LANES]
        mi = jnp.where(lane < H_M, blk_m, 0.0)
        mf = jnp.where((lane >= GATE_GROUP) & (lane < GATE_GROUP + H_M),
                       pltpu.roll(blk_m, GATE_GROUP - H_M, axis=1), 0.0)
        ff = jnp.where((lane >= 2 * GATE_GROUP) & (lane < 2 * GATE_GROUP + H_F),
                       pltpu.roll(blk_f, 2 * GATE_GROUP - (o_ff - f0), axis=1), 0.0)
        return mi + mf + ff

    slab = wint_ref[...]
    wa_o[...] = slab[0:o_mi].T.astype(BF16)
    wb_o[...] = slab[o_fq:o_ff].T.astype(BF16)
    wc_o[...] = slab[o_g:].T.astype(BF16)
    g_m = slab[o_mi:o_mi + GATE_GROUP]
    row = lax.broadcasted_iota(jnp.int32, g_m.shape, 0)
    gates = jnp.concatenate(
        [jnp.where(row < H_M, g_m, 0.0),
         jnp.where(row < H_M, pltpu.roll(g_m, GATE_GROUP - H_M, axis=0), 0.0),
         slab[o_ff:o_ff + GATE_GROUP],
         jnp.zeros((LANES - 3 * GATE_GROUP, slab.shape[1]), F32)], axis=0)
    wd_o[...] = gates.T.astype(BF16)
    b = bin_ref[...]
    ba_o[...] = b[:, 0:o_mi]
    bb_o[...] = b[:, o_fq:o_ff]
    bc_o[...] = b[:, o_g:]
    bd_o[...] = gate_block(b)


def _prep_in_weights(p, vmem):
    d, n_in = p["w_in"].shape
    w_m = p["w_br_mlstm"].shape[0]
    w_f = p["w_br_fox"].shape[0]
    steps = 8
    o_ff = 4 * w_m + 2 * H_M + 3 * w_f
    assert (4 * w_m) % LANES == 0 and 2 * H_M <= GATE_GROUP + H_M <= LANES
    assert o_ff % LANES <= 2 * GATE_GROUP and o_ff % LANES + H_F <= LANES
    assert n_in == o_ff + H_F + 2 * d
    srcs = [p["w_in"].T, p["b_in"][None, :]]

    def row_spec(shape, tiled=True):
        if not tiled:
            return pl.BlockSpec(shape, lambda i: (0, 0))
        assert shape[0] % (steps * BF16_ROWS) == 0
        return pl.BlockSpec((shape[0] // steps, shape[1]), lambda i: (i, 0))

    out_shapes = [((d, 4 * w_m), BF16), ((d, 3 * w_f), BF16), ((d, 2 * d), BF16), ((d, LANES), BF16),
                  ((1, 4 * w_m), F32), ((1, 3 * w_f), F32), ((1, 2 * d), F32), ((1, LANES), F32)]
    return pl.pallas_call(
        functools.partial(_wprep_kernel, w_m=w_m, w_f=w_f),
        grid=(steps,),
        in_specs=([pl.BlockSpec((n_in, d // steps), lambda i: (0, i))]
                  + [row_spec(s.shape, tiled=s.shape[0] > 1) for s in srcs[1:]]),
        out_specs=[row_spec(s, tiled=s[0] > 1) for s, _ in out_shapes],
        out_shape=[jax.ShapeDtypeStruct(s, dt) for s, dt in out_shapes],
        compiler_params=pltpu.CompilerParams(dimension_semantics=("arbitrary",), vmem_limit_bytes=vmem),
        name="wprep",
    )(*srcs)


def _inproj_kernel(x_ref, g_ref, wa_ref, ba_ref, wb_ref, bb_ref, wc_ref, bc_ref, wd_ref, bd_ref,
                   cw_ref, cb_ref, *rest,
                   w_m, w_f, n_chunk, m_split, q_scale, k_scale, chunk, tiles_per_seq, n_cast):
    cast_srcs, rest = rest[:n_cast], rest[n_cast:]
    (zqk_ref, vt_ref, og_ref, zb_ref, zc_ref, zdt_ref), rest = rest[:6], rest[6:]
    cast_dsts, (zs_sc,) = rest[:n_cast], rest[n_cast:]
    tm = x_ref.shape[0]
    seq_start = (pl.program_id(0) % tiles_per_seq) == 0

    @pl.when(seq_start)
    def _():
        zs_sc[:, 0:SUBLANES, :] = jnp.zeros((zs_sc.shape[0], SUBLANES, n_chunk), F32)

    @pl.when(jnp.logical_not(seq_start))
    def _():
        zs_sc[:, 0:SUBLANES, :] = zs_sc[:, tm:tm + SUBLANES, :]

    rows = tm // m_split
    hb = [_rms(x_ref[r0:r0 + rows, :], g_ref[...]).astype(BF16) for r0 in range(0, tm, rows)]

    def proj(w_ref, b_ref, c0, c1):
        w = w_ref[:, c0:c1]
        parts = [jnp.dot(h, w, preferred_element_type=F32) for h in hb]
        return jnp.concatenate(parts, axis=0) + b_ref[:, c0:c1]

    def qk_chunk(ci):
        c0 = ci * n_chunk
        cs = slice(c0, c0 + n_chunk)
        zs_sc[ci, SUBLANES:tm + SUBLANES, :] = proj(wa_ref, ba_ref, c0, c0 + n_chunk)
        w = cw_ref[:, cs]
        y = cb_ref[:, cs]
        for s in range(CONV_K):
            y = y + zs_sc[ci, SUBLANES - s:SUBLANES - s + tm, :] * w[CONV_K - 1 - s:CONV_K - s]
        act = y * jax.nn.sigmoid(y)
        if c0 >= w_m:
            act = act * k_scale
        zqk_ref[:, cs] = act.astype(BF16)

    def v_chunk(i):
        c0 = 2 * w_m + i * n_chunk
        z = proj(wa_ref, ba_ref, c0, c0 + n_chunk)
        for cc in range(tm // chunk):
            for f0 in range(0, n_chunk, LANES):
                blk = z[cc * chunk:(cc + 1) * chunk, f0:f0 + LANES]
                r0 = i * n_chunk + f0
                vt_ref[cc, r0:r0 + LANES, :] = blk.T.astype(BF16)

    def og_chunk(i):
        c0 = 3 * w_m + i * n_chunk
        z = proj(wa_ref, ba_ref, c0, c0 + n_chunk)
        og_ref[:, i * n_chunk:(i + 1) * n_chunk] = jax.nn.sigmoid(z).astype(BF16)

    def b_chunk(i):
        c0 = i * n_chunk
        z = proj(wb_ref, bb_ref, c0, c0 + n_chunk)
        if c0 < w_f:
            z = z * q_scale
        zb_ref[:, c0:c0 + n_chunk] = z.astype(BF16)

    def c_chunk(i):
        c0 = i * n_chunk
        z = proj(wc_ref, bc_ref, c0, c0 + n_chunk)
        zc_ref[:, c0:c0 + n_chunk] = jax.nn.sigmoid(z).astype(BF16)

    def d_chunk(_):
        zd = proj(wd_ref, bd_ref, 0, LANES)
        zdt_ref[...] = zd.T[:N_GATE_ROWS, :]

    heavy = [(qk_chunk, i) for i in range(2 * w_m // n_chunk)]
    light = ([(c_chunk, i) for i in range(wc_ref.shape[1] // n_chunk)]
             + [(b_chunk, i) for i in range(3 * w_f // n_chunk)]
             + [(v_chunk, i) for i in range(w_m // n_chunk)]
             + [(og_chunk, i) for i in range(w_m // n_chunk)] + [(d_chunk, 0)])
    per_heavy = len(light) // len(heavy)
    order = []
    for hi, task in enumerate(heavy):
        order.append(task)
        order.extend(light[hi * per_heavy:(hi + 1) * per_heavy])
    order.extend(light[len(heavy) * per_heavy:])
    for fn, i in order:
        fn(i)
    for src, dst in zip(cast_srcs, cast_dsts):
        dst[...] = src[...].astype(BF16)


def _scan_lanes(x, op, fill):
    n = x.shape[-1]
    lane = lax.broadcasted_iota(jnp.int32, x.shape, 1)
    s = 1
    while s < n:
        shifted = pltpu.roll(x, s, axis=1)
        x = op(x, jnp.where(lane >= s, shifted, fill))
        s *= 2
    return x


def _gates_kernel(zdt_ref, rows_ref, cols_ref, csplit_ref):
    z = zdt_ref[...]
    seq = z.shape[1]
    i8 = z[0:GATE_GROUP]
    f8 = _scan_lanes(_log_sigmoid(z[GATE_GROUP:2 * GATE_GROUP]), jnp.add, 0.0)
    cf8 = _scan_lanes(_log_sigmoid(z[2 * GATE_GROUP:3 * GATE_GROUP]), jnp.add, 0.0) * LOG2E
    g8 = i8 - f8
    m8 = jnp.maximum(_scan_lanes(g8, jnp.maximum, NEG_BIG), 0.0)
    en8 = jnp.exp(-(f8 + m8))
    rows_ref[0] = jnp.concatenate([g8, m8, en8], axis=0)
    stack = jnp.concatenate(
        [g8, m8, jnp.zeros((LANES - 2 * GATE_GROUP, seq), F32)], axis=0)
    cols_ref[...] = stack.T
    hi = cf8.astype(BF16).astype(F32)
    r1 = cf8 - hi
    lo = r1.astype(BF16).astype(F32)
    lo2 = (r1 - lo).astype(BF16).astype(F32)
    ones = jnp.where(lax.broadcasted_iota(jnp.int32, (GATE_GROUP, seq), 0) == 0, 1.0, 0.0)
    split = jnp.concatenate(
        [hi, lo, lo2, ones, jnp.zeros((LANES - 4 * GATE_GROUP, seq), F32)], axis=0)
    csplit_ref[...] = split.T.astype(BF16)


def _mlstm_kernel(zqk_ref, vt_ref, og_ref, rows_ref, cols_ref, ng_ref, ym_ref, ct_sc, *, w_m, chunk):
    seq = zqk_ref.shape[0]
    dh = w_m // H_M
    nc = seq // chunk
    ct_sc[...] = jnp.zeros(ct_sc.shape, F32)
    causal = (lax.broadcasted_iota(jnp.int32, (chunk, chunk), 0)
              <= lax.broadcasted_iota(jnp.int32, (chunk, chunk), 1))
    ones_rows = jnp.where(
        lax.broadcasted_iota(jnp.int32, (BF16_ROWS, chunk), 0) == 0, 1.0, 0.0).astype(BF16)

    def body(c, carry):
        r0 = pl.multiple_of(c * chunk, chunk)
        colsc = cols_ref[pl.ds(r0, chunk), :]
        last = cols_ref[pl.ds(r0 + chunk - 1, 1), :]
        prev = cols_ref[pl.ds(jnp.maximum(r0 - 1, 0), 1), :]
        prev = jnp.where(c > 0, prev, 0.0)
        first, vas = [], []
        for h in range(H_M):
            qc = zqk_ref[pl.ds(r0, chunk), h * dh:(h + 1) * dh]
            kc = zqk_ref[pl.ds(r0, chunk), w_m + h * dh:w_m + (h + 1) * dh]
            lhs = jnp.concatenate([kc, ct_sc[h].astype(BF16)], axis=0)
            first.append(lax.dot_general(lhs, qc, NT_DIMS, preferred_element_type=F32))
            vas.append(jnp.concatenate([vt_ref[c, h * dh:(h + 1) * dh, :], ones_rows], axis=0))
        for h in range(H_M):
            kc = zqk_ref[pl.ds(r0, chunk), w_m + h * dh:w_m + (h + 1) * dh]
            g_row = rows_ref[0, h, pl.ds(c, 1), :]
            m_e = last[:, GATE_GROUP + h:GATE_GROUP + h + 1]
            m_p = prev[:, GATE_GROUP + h:GATE_GROUP + h + 1]
            vaw = (vas[h].astype(F32) * jnp.exp(g_row - m_e)).astype(BF16)
            ct_sc[h] = jnp.exp(m_p - m_e) * ct_sc[h] + jnp.dot(vaw, kc, preferred_element_type=F32)
        for h in range(H_M):
            ch = slice(h * dh, (h + 1) * dh)
            g_col = colsc[:, h:h + 1]
            m_row = rows_ref[0, GATE_GROUP + h, pl.ds(c, 1), :]
            en_row = rows_ref[0, 2 * GATE_GROUP + h, pl.ds(c, 1), :]
            m_p = prev[:, GATE_GROUP + h:GATE_GROUP + h + 1]
            dmat = jnp.where(causal, jnp.exp(g_col - m_row), 0.0)
            sqk = (first[h][0:chunk] * dmat).astype(BF16)
            nd = (jnp.exp(m_p - m_row) * first[h][chunk:]
                  + jnp.dot(vas[h], sqk, preferred_element_type=F32))
            den = nd[dh:dh + 1]
            ht = nd[0:dh] * (1.0 / jnp.maximum(jnp.abs(den), en_row))
            hn = ht * lax.rsqrt(jnp.mean(ht * ht, axis=0, keepdims=True) + EPS)
            og = og_ref[pl.ds(r0, chunk), ch].astype(F32)
            ym_ref[pl.ds(r0, chunk), ch] = (hn.T * ng_ref[:, ch] * og).astype(BF16)
        return carry

    lax.fori_loop(0, nc, body, 0, unroll=True)


def _fox_kernel(q_ref, k_ref, v_ref, csq_ref, csk_ref, qsel_ref, ksel_ref, yf_ref,
                kaug_sc, vt_sc, qaug_sc, m_sc, acc_sc, *, tq, tk):
    qi = pl.program_id(1)

    seq = k_ref.shape[0]
    dhp = LANES
    half = dhp // 2
    v_rows = half + BF16_ROWS

    def own_lanes(head, rows):
        lane = lax.broadcasted_iota(jnp.int32, (rows, dhp), 1)
        return (lane >= half) if head % 2 else (lane < half)

    @pl.when(qi == 0)
    def _():
        csk = csk_ref[...]
        ones_rows = jnp.where(
            lax.broadcasted_iota(jnp.int32, (BF16_ROWS, tk), 0) == 0, 1.0, 0.0).astype(BF16)
        for p in range(H_F // 2):
            cp = slice(p * dhp, (p + 1) * dhp)
            kp = k_ref[:, cp]
            vt = v_ref[:, cp].astype(F32).T
            for head in (2 * p, 2 * p + 1):
                kb = jnp.dot(csk, ksel_ref[head], preferred_element_type=F32).astype(BF16)
                kaug_sc[head] = jnp.where(own_lanes(head, seq), kp, kb)
                r0 = (head % 2) * half
                for jb in range(seq // tk):
                    vt_sc[head, jb, 0:half, :] = vt[r0:r0 + half, jb * tk:(jb + 1) * tk].astype(BF16)
                    vt_sc[head, jb, half:v_rows, :] = ones_rows

    csq = csq_ref[...]
    for head in range(H_F):
        qp = q_ref[:, (head // 2) * dhp:(head // 2 + 1) * dhp]
        qb = jnp.dot(csq, qsel_ref[head], preferred_element_type=F32).astype(BF16)
        qaug_sc[head] = jnp.where(own_lanes(head, tq), qp, qb)
    m_sc[...] = jnp.full(m_sc.shape, NEG_BIG, F32)
    acc_sc[...] = jnp.zeros(acc_sc.shape, F32)

    ratio = tq // tk

    def step(j, diag):
        q0 = 0 if diag is None else diag * tk
        nq = tq - q0
        k0 = pl.multiple_of(j * tk, tk)
        sts = [lax.dot_general(kaug_sc[head, pl.ds(k0, tk), :], qaug_sc[head, q0:tq, :], NT_DIMS,
                               preferred_element_type=F32) for head in range(H_F)]
        if diag is not None:
            causal = (lax.broadcasted_iota(jnp.int32, (tk, nq), 0)
                      <= lax.broadcasted_iota(jnp.int32, (tk, nq), 1))
        for head in range(H_F):
            st = sts[head]
            if diag is not None:
                st = jnp.where(causal, st, NEG_BIG)
            m = m_sc[head, :, q0:tq]
            m_new = jnp.maximum(m, jnp.max(st, axis=0, keepdims=True))
            alpha = jnp.exp2(m - m_new)
            pt = jnp.exp2(st - m_new).astype(BF16)
            m_sc[head, :, q0:tq] = m_new
            acc_sc[head, :, q0:tq] = (alpha * acc_sc[head, :, q0:tq]
                                      + jnp.dot(vt_sc[head, j], pt, preferred_element_type=F32))

    def loop_body(jj, carry):
        for r in range(ratio):
            step(jj * ratio + r, None)
        return carry

    lax.fori_loop(0, qi, loop_body, 0)
    for diag in range(ratio):
        step(qi * ratio + diag, diag)
    for p in range(H_F // 2):
        outs = []
        for head in (2 * p, 2 * p + 1):
            acc = acc_sc[head]
            outs.append(acc[0:half] * (1.0 / acc[half:half + 1]))
        yf_ref[:, p * dhp:(p + 1) * dhp] = jnp.concatenate(outs, axis=0).T.astype(BF16)


def _post_kernel(x_ref, ym_ref, yf_ref, zc_ref, wbm_ref, wbf_ref, wo_ref, g2_ref,
                 wg_ref, wu_ref, wd_ref, gfin_ref, o_ref, *, tf):
    d = x_ref.shape[1]
    d_ff = wg_ref.shape[1]
    bm = jnp.dot(ym_ref[...], wbm_ref[...], preferred_element_type=F32)
    bf = jnp.dot(yf_ref[...], wbf_ref[...], preferred_element_type=F32)
    mix = zc_ref[:, 0:d].astype(F32) * bm + zc_ref[:, d:2 * d].astype(F32) * bf
    x1 = x_ref[...] + jnp.dot(mix.astype(BF16), wo_ref[...], preferred_element_type=F32)
    h2 = _rms(x1, g2_ref[...]).astype(BF16)
    acc = jnp.zeros(x1.shape, F32)
    for f0 in range(0, d_ff, tf):
        g = jnp.dot(h2, wg_ref[:, f0:f0 + tf], preferred_element_type=F32)
        u = jnp.dot(h2, wu_ref[:, f0:f0 + tf], preferred_element_type=F32)
        act = (g * jax.nn.sigmoid(g) * u).astype(BF16)
        acc = acc + jnp.dot(act, wd_ref[f0:f0 + tf, :], preferred_element_type=F32)
    o_ref[...] = _rms(x1 + acc, gfin_ref[...])


def _bias_selectors():
    ones_lane = 3 * GATE_GROUP
    qsel = np.zeros((H_F, LANES, LANES), np.float32)
    ksel = np.zeros((H_F, LANES, LANES), np.float32)
    for h in range(H_F):
        p0 = LANES // 2 if h % 2 == 0 else 0
        for c in range(3):
            qsel[h, GATE_GROUP * c + h, p0 + c] = 1.0
            qsel[h, ones_lane, p0 + 3 + c] = 1.0
            ksel[h, ones_lane, p0 + c] = 1.0
            ksel[h, GATE_GROUP * c + h, p0 + 3 + c] = -1.0
    return jnp.asarray(qsel, BF16), jnp.asarray(ksel, BF16)


def _layer(x2d, batch, seq, p, cfg):
    t, d = x2d.shape
    w_m = p["w_br_mlstm"].shape[0]
    w_f = p["w_br_fox"].shape[0]
    dh_m = w_m // H_M
    dh_f = w_f // H_F
    vmem = cfg["vmem_limit"]

    wa, wb, wc, wd, ba, bb, bc, bd = _prep_in_weights(p, vmem)

    tm = cfg["tm_in"]
    chunk = cfg["chunk"]
    nc = seq // chunk
    n_steps = t // tm
    assert seq % tm == 0 and tm % chunk == 0

    later = [p["w_br_mlstm"], p["w_br_fox"], p["w_out"], p["w_gate"], p["w_up"], p["w_down"]]

    def slice_spec(rows, cols):
        hold = 1
        while (rows * hold) % (n_steps * BF16_ROWS):
            hold *= 2
        assert hold <= n_steps
        return pl.BlockSpec((rows * hold // n_steps, cols), lambda i: (i // hold, 0))

    later_specs = [slice_spec(*w.shape) for w in later]
    outs = pl.pallas_call(
        functools.partial(_inproj_kernel, w_m=w_m, w_f=w_f, n_chunk=cfg["n_chunk"],
                          m_split=cfg["m_split"],
                          q_scale=dh_f ** -0.5 * LOG2E, k_scale=dh_m ** -0.5, chunk=chunk,
                          tiles_per_seq=seq // tm, n_cast=len(later)),
        grid=(n_steps,),
        in_specs=[pl.BlockSpec((tm, d), lambda i: (i, 0)), _const_spec((1, d)),
                  _const_spec(wa.shape), _const_spec(ba.shape),
                  _const_spec(wb.shape), _const_spec(bb.shape),
                  _const_spec(wc.shape), _const_spec(bc.shape),
                  _const_spec(wd.shape), _const_spec(bd.shape),
                  _const_spec((CONV_K, 2 * w_m)), _const_spec((1, 2 * w_m))] + later_specs,
        out_specs=[pl.BlockSpec((tm, 2 * w_m), lambda i: (i, 0)),
                   pl.BlockSpec((tm // chunk, w_m, chunk), lambda i: (i, 0, 0)),
                   pl.BlockSpec((tm, w_m), lambda i: (i, 0)),
                   pl.BlockSpec((tm, 3 * w_f), lambda i: (i, 0)),
                   pl.BlockSpec((tm, 2 * d), lambda i: (i, 0)),
                   pl.BlockSpec((N_GATE_ROWS, tm), lambda i: (0, i))] + later_specs,
        out_shape=[jax.ShapeDtypeStruct((t, 2 * w_m), BF16),
                   jax.ShapeDtypeStruct((t // chunk, w_m, chunk), BF16),
                   jax.ShapeDtypeStruct((t, w_m), BF16),
                   jax.ShapeDtypeStruct((t, 3 * w_f), BF16),
                   jax.ShapeDtypeStruct((t, 2 * d), BF16),
                   jax.ShapeDtypeStruct((N_GATE_ROWS, t), F32)]
        + [jax.ShapeDtypeStruct(w.shape, BF16) for w in later],
        scratch_shapes=[pltpu.VMEM((2 * w_m // cfg["n_chunk"], tm + SUBLANES, cfg["n_chunk"]), F32)],
        compiler_params=pltpu.CompilerParams(dimension_semantics=("arbitrary",), vmem_limit_bytes=vmem),
        name="inproj",
    )(x2d, p["norm1_g"][None, :], wa, ba, wb, bb, wc, bc, wd, bd, p["conv_w"], p["conv_b"][None, :],
      *later)
    zqk, vt, og, zb, zc, zdt, wbm, wbf, wo, wg, wu, wdn = outs

    rows, cols, csplit = pl.pallas_call(
        _gates_kernel,
        grid=(batch,),
        in_specs=[pl.BlockSpec((N_GATE_ROWS, seq), lambda b: (0, b))],
        out_specs=[pl.BlockSpec((1, 3 * GATE_GROUP, seq), lambda b: (b, 0, 0)),
                   pl.BlockSpec((seq, LANES), lambda b: (b, 0)),
                   pl.BlockSpec((seq, LANES), lambda b: (b, 0))],
        out_shape=[jax.ShapeDtypeStruct((batch, 3 * GATE_GROUP, seq), F32),
                   jax.ShapeDtypeStruct((t, LANES), F32),
                   jax.ShapeDtypeStruct((t, LANES), BF16)],
        compiler_params=pltpu.CompilerParams(dimension_semantics=("parallel",), vmem_limit_bytes=vmem),
        name="gates",
    )(zdt)

    rows_m = rows.reshape(batch, 3 * GATE_GROUP, nc, chunk)
    ym = pl.pallas_call(
        functools.partial(_mlstm_kernel, w_m=w_m, chunk=chunk),
        grid=(batch,),
        in_specs=[pl.BlockSpec((seq, 2 * w_m), lambda b: (b, 0)),
                  pl.BlockSpec((nc, w_m, chunk), lambda b: (b, 0, 0)),
                  pl.BlockSpec((seq, w_m), lambda b: (b, 0)),
                  pl.BlockSpec((1, 3 * GATE_GROUP, nc, chunk), lambda b: (b, 0, 0, 0)),
                  pl.BlockSpec((seq, LANES), lambda b: (b, 0)),
                  _const_spec((1, w_m))],
        out_specs=pl.BlockSpec((seq, w_m), lambda b: (b, 0)),
        out_shape=jax.ShapeDtypeStruct((t, w_m), BF16),
        scratch_shapes=[pltpu.VMEM((H_M, dh_m + BF16_ROWS, dh_m), F32)],
        compiler_params=pltpu.CompilerParams(dimension_semantics=("parallel",), vmem_limit_bytes=vmem),
        name="mlstm",
    )(zqk, vt, og, rows_m, cols, p["mlstm_norm_g"][None, :])

    tq, tk = cfg["tq"], cfg["tk"]
    assert tq % tk == 0 and seq % tq == 0
    nq = seq // tq
    qsel, ksel = _bias_selectors()
    v_rows = LANES // 2 + BF16_ROWS
    yf = pl.pallas_call(
        functools.partial(_fox_kernel, tq=tq, tk=tk),
        grid=(batch, nq),
        in_specs=[pl.BlockSpec((tq, w_f), lambda b, i: (b * nq + i, 0)),
                  pl.BlockSpec((seq, w_f), lambda b, i: (b, 1)),
                  pl.BlockSpec((seq, w_f), lambda b, i: (b, 2)),
                  pl.BlockSpec((tq, LANES), lambda b, i: (b * nq + i, 0)),
                  pl.BlockSpec((seq, LANES), lambda b, i: (b, 0)),
                  _const_spec(qsel.shape), _const_spec(ksel.shape)],
        out_specs=pl.BlockSpec((tq, w_f), lambda b, i: (b * nq + i, 0)),
        out_shape=jax.ShapeDtypeStruct((t, w_f), BF16),
        scratch_shapes=[pltpu.VMEM((H_F, seq, LANES), BF16),
                        pltpu.VMEM((H_F, seq // tk, v_rows, tk), BF16),
                        pltpu.VMEM((H_F, tq, LANES), BF16),
                        pltpu.VMEM((H_F, 1, tq), F32),
                        pltpu.VMEM((H_F, v_rows, tq), F32)],
        compiler_params=pltpu.CompilerParams(dimension_semantics=("arbitrary", "arbitrary"),
                                             vmem_limit_bytes=vmem),
        name="fox",
    )(zb, zb, zb, csplit, csplit, qsel, ksel)

    tmp = cfg["tm_post"]
    return pl.pallas_call(
        functools.partial(_post_kernel, tf=cfg["tf"]),
        grid=(t // tmp,),
        in_specs=[pl.BlockSpec((tmp, d), lambda i: (i, 0)),
                  pl.BlockSpec((tmp, w_m), lambda i: (i, 0)),
                  pl.BlockSpec((tmp, w_f), lambda i: (i, 0)),
                  pl.BlockSpec((tmp, 2 * d), lambda i: (i, 0)),
                  _const_spec(wbm.shape), _const_spec(wbf.shape), _const_spec(wo.shape),
                  _const_spec((1, d)), _const_spec(wg.shape), _const_spec(wu.shape),
                  _const_spec(wdn.shape), _const_spec((1, d))],
        out_specs=pl.BlockSpec((tmp, d), lambda i: (i, 0)),
        out_shape=jax.ShapeDtypeStruct((t, d), F32),
        compiler_params=pltpu.CompilerParams(dimension_semantics=("parallel",), vmem_limit_bytes=vmem),
        name="post",
    )(x2d, ym, yf, zc, wbm, wbf, wo, p["norm2_g"][None, :], wg, wu, wdn, p["norm_f_g"][None, :])


def kernel(x, norm1_g, w_in, b_in, conv_w, conv_b, mlstm_norm_g, w_br_mlstm, w_br_fox, w_out,
           norm2_g, w_gate, w_up, w_down, norm_f_g):
    batch, seq, d = x.shape
    depth = w_in.shape[0]
    assert depth == 1, "the final norm is fused into the single layer's last call"
    cfg = _cfg(batch, seq, d, w_gate.shape[-1])
    p = dict(norm1_g=norm1_g[0], w_in=w_in[0], b_in=b_in[0], conv_w=conv_w[0], conv_b=conv_b[0],
             mlstm_norm_g=mlstm_norm_g[0], w_br_mlstm=w_br_mlstm[0], w_br_fox=w_br_fox[0],
             w_out=w_out[0], norm2_g=norm2_g[0], w_gate=w_gate[0], w_up=w_up[0], w_down=w_down[0],
             norm_f_g=norm_f_g)
    out = _layer(x.reshape(batch * seq, d), batch, seq, p, cfg)
    return out.reshape(batch, seq, d)
```

```python
import functools

import jax
import jax.numpy as jnp
import numpy as np
from jax import lax
from jax.experimental import pallas as pl
from jax.experimental.pallas import tpu as pltpu

EPS = 1e-6
H_M = 4
H_F = 8
CONV_K = 4

LANES = 128
SUBLANES = 8
BF16_ROWS = 16
GATE_GROUP = 8
N_GATE_ROWS = 3 * GATE_GROUP
V7X_VMEM_BYTES = 64 * 1024 * 1024
NEG_BIG = -1e30
LOG2E = 1.4426950408889634

F32 = jnp.float32
BF16 = jnp.bfloat16
NT_DIMS = (((1,), (1,)), ((), ()))


def _cfg(batch, seq, d_model, d_ff):
    return dict(
        tm_in=512,
        n_chunk=256,
        m_split=4,
        gates_batch=4,
        chunk=256,
        tq=512,
        tk=256,
        tm_post=512,
        tf=256,
        vmem_limit=V7X_VMEM_BYTES - 8 * 1024 * 1024,
    )


def _const_spec(shape):
    nd = len(shape)
    return pl.BlockSpec(shape, lambda *_: (0,) * nd, pipeline_mode=pl.Buffered(1))


def _rms(x, g):
    return x * lax.rsqrt(jnp.mean(x * x, axis=-1, keepdims=True) + EPS) * g


def _log_sigmoid(x):
    return jnp.minimum(x, 0.0) - jnp.log1p(jnp.exp(-jnp.abs(x)))


def _wprep_kernel(wint_ref, bin_ref, wa_o, wb_o, wc_o, wd_o, ba_o, bb_o, bc_o, bd_o, *, w_m, w_f):
    o_mi = 4 * w_m
    o_mf = o_mi + H_M
    o_fq = o_mf + H_M
    o_ff = o_fq + 3 * w_f
    o_g = o_ff + H_F

    def gate_block(src):
        rows = src.shape[0]
        lane = lax.broadcasted_iota(jnp.int32, (rows, LANES), 1)
        blk_m = src[:, o_mi:o_mi + LANES]
        f0 = (o_ff // LANES) * LANES
        blk_f = src[:, f0:f0 + LANES]
        mi = jnp.where(lane < H_M, blk_m, 0.0)
        mf = jnp.where((lane >= GATE_GROUP) & (lane < GATE_GROUP + H_M),
                       pltpu.roll(blk_m, GATE_GROUP - H_M, axis=1), 0.0)
        ff = jnp.where((lane >= 2 * GATE_GROUP) & (lane < 2 * GATE_GROUP + H_F),
                       pltpu.roll(blk_f, 2 * GATE_GROUP - (o_ff - f0), axis=1), 0.0)
        return mi + mf + ff

    slab = wint_ref[...]
    wa_o[...] = slab[0:o_mi].T.astype(BF16)
    wb_o[...] = slab[o_fq:o_ff].T.astype(BF16)
    wc_o[...] = slab[o_g:].T.astype(BF16)
    g_m = slab[o_mi:o_mi + GATE_GROUP]
    row = lax.broadcasted_iota(jnp.int32, g_m.shape, 0)
    gates = jnp.concatenate(
        [jnp.where(row < H_M, g_m, 0.0),
         jnp.where(row < H_M, pltpu.roll(g_m, GATE_GROUP - H_M, axis=0), 0.0),
         slab[o_ff:o_ff + GATE_GROUP],
         jnp.zeros((LANES - 3 * GATE_GROUP, slab.shape[1]), F32)], axis=0)
    wd_o[...] = gates.T.astype(BF16)
    b = bin_ref[...]
    ba_o[...] = b[:, 0:o_mi]
    bb_o[...] = b[:, o_fq:o_ff]
    bc_o[...] = b[:, o_g:]
    bd_o[...] = gate_block(b)


def _prep_in_weights(p, vmem):
    d, n_in = p["w_in"].shape
    w_m = p["w_br_mlstm"].shape[0]
    w_f = p["w_br_fox"].shape[0]
    steps = 8
    o_ff = 4 * w_m + 2 * H_M + 3 * w_f
    assert (4 * w_m) % LANES == 0 and 2 * H_M <= GATE_GROUP + H_M <= LANES
    assert o_ff % LANES <= 2 * GATE_GROUP and o_ff % LANES + H_F <= LANES
    assert n_in == o_ff + H_F + 2 * d
    srcs = [p["w_in"].T, p["b_in"][None, :]]

    def row_spec(shape, tiled=True):
        if not tiled:
            return pl.BlockSpec(shape, lambda i: (0, 0))
        assert shape[0] % (steps * BF16_ROWS) == 0
        return pl.BlockSpec((shape[0] // steps, shape[1]), lambda i: (i, 0))

    out_shapes = [((d, 4 * w_m), BF16), ((d, 3 * w_f), BF16), ((d, 2 * d), BF16), ((d, LANES), BF16),
                  ((1, 4 * w_m), F32), ((1, 3 * w_f), F32), ((1, 2 * d), F32), ((1, LANES), F32)]
    return pl.pallas_call(
        functools.partial(_wprep_kernel, w_m=w_m, w_f=w_f),
        grid=(steps,),
        in_specs=([pl.BlockSpec((n_in, d // steps), lambda i: (0, i))]
                  + [row_spec(s.shape, tiled=s.shape[0] > 1) for s in srcs[1:]]),
        out_specs=[row_spec(s, tiled=s[0] > 1) for s, _ in out_shapes],
        out_shape=[jax.ShapeDtypeStruct(s, dt) for s, dt in out_shapes],
        compiler_params=pltpu.CompilerParams(dimension_semantics=("arbitrary",), vmem_limit_bytes=vmem),
        name="wprep",
    )(*srcs)


def _inproj_kernel(x_ref, g_ref, wa_ref, ba_ref, wb_ref, bb_ref, wc_ref, bc_ref, wd_ref, bd_ref,
                   cw_ref, cb_ref, *rest,
                   w_m, w_f, n_chunk, m_split, q_scale, k_scale, chunk, tiles_per_seq, n_cast):
    cast_srcs, rest = rest[:n_cast], rest[n_cast:]
    (zqk_ref, vt_ref, og_ref, zb_ref, zc_ref, zdt_ref), rest = rest[:6], rest[6:]
    cast_dsts, (zs_sc,) = rest[:n_cast], rest[n_cast:]
    tm = x_ref.shape[0]
    seq_start = (pl.program_id(0) % tiles_per_seq) == 0

    @pl.when(seq_start)
    def _():
        zs_sc[:, 0:SUBLANES, :] = jnp.zeros((zs_sc.shape[0], SUBLANES, n_chunk), F32)

    @pl.when(jnp.logical_not(seq_start))
    def _():
        zs_sc[:, 0:SUBLANES, :] = zs_sc[:, tm:tm + SUBLANES, :]

    rows = tm // m_split
    hb = [_rms(x_ref[r0:r0 + rows, :], g_ref[...]).astype(BF16) for r0 in range(0, tm, rows)]

    def proj(w_ref, b_ref, c0, c1):
        w = w_ref[:, c0:c1]
        parts = [jnp.dot(h, w, preferred_element_type=F32) for h in hb]
        return jnp.concatenate(parts, axis=0) + b_ref[:, c0:c1]

    def qk_chunk(ci):
        c0 = ci * n_chunk
        cs = slice(c0, c0 + n_chunk)
        zs_sc[ci, SUBLANES:tm + SUBLANES, :] = proj(wa_ref, ba_ref, c0, c0 + n_chunk)
        w = cw_ref[:, cs]
        y = cb_ref[:, cs]
        for s in range(CONV_K):
            y = y + zs_sc[ci, SUBLANES - s:SUBLANES - s + tm, :] * w[CONV_K - 1 - s:CONV_K - s]
        act = y * jax.nn.sigmoid(y)
        if c0 >= w_m:
            act = act * k_scale
        zqk_ref[:, cs] = act.astype(BF16)

    def v_chunk(i):
        c0 = 2 * w_m + i * n_chunk
        z = proj(wa_ref, ba_ref, c0, c0 + n_chunk)
        for cc in range(tm // chunk):
            for f0 in range(0, n_chunk, LANES):
                blk = z[cc * chunk:(cc + 1) * chunk, f0:f0 + LANES]
                r0 = i * n_chunk + f0
                vt_ref[cc, r0:r0 + LANES, :] = blk.T.astype(BF16)

    def og_chunk(i):
        c0 = 3 * w_m + i * n_chunk
        z = proj(wa_ref, ba_ref, c0, c0 + n_chunk)
        og_ref[:, i * n_chunk:(i + 1) * n_chunk] = jax.nn.sigmoid(z).astype(BF16)

    def b_chunk(i):
        c0 = i * n_chunk
        z = proj(wb_ref, bb_ref, c0, c0 + n_chunk)
        if c0 < w_f:
            z = z * q_scale
        zb_ref[:, c0:c0 + n_chunk] = z.astype(BF16)

    def c_chunk(i):
        c0 = i * n_chunk
        z = proj(wc_ref, bc_ref, c0, c0 + n_chunk)
        zc_ref[:, c0:c0 + n_chunk] = jax.nn.sigmoid(z).astype(BF16)

    def d_chunk(_):
        zd = proj(wd_ref, bd_ref, 0, LANES)
        zdt_ref[...] = zd.T[:N_GATE_ROWS, :]

    heavy = [(qk_chunk, i) for i in range(2 * w_m // n_chunk)]
    light = ([(c_chunk, i) for i in range(wc_ref.shape[1] // n_chunk)]
             + [(b_chunk, i) for i in range(3 * w_f // n_chunk)]
             + [(v_chunk, i) for i in range(w_m // n_chunk)]
             + [(og_chunk, i) for i in range(w_m // n_chunk)] + [(d_chunk, 0)])
    per_heavy = len(light) // len(heavy)
    order = []
    for hi, task in enumerate(heavy):
        order.append(task)
        order.extend(light[hi * per_heavy:(hi + 1) * per_heavy])
    order.extend(light[len(heavy) * per_heavy:])
    for fn, i in order:
        fn(i)
    for src, dst in zip(cast_srcs, cast_dsts):
        dst[...] = src[...].astype(BF16)


def _scan_lanes(x, op, fill, seg):
    pos = lax.rem(lax.broadcasted_iota(jnp.int32, x.shape, 1), seg)
    s = 1
    while s < seg:
        shifted = pltpu.roll(x, s, axis=1)
        x = op(x, jnp.where(pos >= s, shifted, fill))
        s *= 2
    return x


def _gates_kernel(zdt_ref, rows_ref, cols_ref, csplit_ref, *, seq):
    z = zdt_ref[...]
    width = z.shape[1]
    i8 = z[0:GATE_GROUP]
    cum = _scan_lanes(_log_sigmoid(z[GATE_GROUP:3 * GATE_GROUP]), jnp.add, 0.0, seq)
    f8 = cum[0:GATE_GROUP]
    cf8 = cum[GATE_GROUP:2 * GATE_GROUP] * LOG2E
    g8 = i8 - f8
    m8 = jnp.maximum(_scan_lanes(g8, jnp.maximum, NEG_BIG, seq), 0.0)
    en8 = jnp.exp(-(f8 + m8))
    g8, m8 = g8 * LOG2E, m8 * LOG2E
    rows = jnp.concatenate([g8, m8, en8], axis=0)
    for bb in range(width // seq):
        rows_ref[bb] = rows[:, bb * seq:(bb + 1) * seq]
    stack = jnp.concatenate(
        [g8, m8, jnp.zeros((LANES - 2 * GATE_GROUP, width), F32)], axis=0)
    cols_ref[...] = stack.T
    hi = cf8.astype(BF16).astype(F32)
    r1 = cf8 - hi
    lo = r1.astype(BF16).astype(F32)
    lo2 = (r1 - lo).astype(BF16).astype(F32)
    ones = jnp.where(lax.broadcasted_iota(jnp.int32, (GATE_GROUP, width), 0) == 0, 1.0, 0.0)
    split = jnp.concatenate(
        [hi, lo, lo2, ones, jnp.zeros((LANES - 4 * GATE_GROUP, width), F32)], axis=0)
    csplit_ref[...] = split.T.astype(BF16)


def _mlstm_kernel(zqk_ref, vt_ref, og_ref, rows_ref, cols_ref, ng_ref, ym_ref, ct_sc, *, w_m, chunk):
    seq = zqk_ref.shape[0]
    dh = w_m // H_M
    nc = seq // chunk
    ct_sc[...] = jnp.zeros(ct_sc.shape, F32)
    hc = chunk // 2
    causal = (lax.broadcasted_iota(jnp.int32, (hc, hc), 0)
              <= lax.broadcasted_iota(jnp.int32, (hc, hc), 1))
    ones_rows = jnp.where(
        lax.broadcasted_iota(jnp.int32, (BF16_ROWS, chunk), 0) == 0, 1.0, 0.0).astype(BF16)

    def body(c, carry):
        r0 = pl.multiple_of(c * chunk, chunk)
        colsc = cols_ref[pl.ds(r0, chunk), :]
        last = cols_ref[pl.ds(r0 + chunk - 1, 1), :]
        prev = cols_ref[pl.ds(jnp.maximum(r0 - 1, 0), 1), :]
        prev = jnp.where(c > 0, prev, 0.0)
        first, vas = [], []
        for h in range(H_M):
            qc = zqk_ref[pl.ds(r0, chunk), h * dh:(h + 1) * dh]
            kc = zqk_ref[pl.ds(r0, chunk), w_m + h * dh:w_m + (h + 1) * dh]
            lhs = jnp.concatenate([kc, ct_sc[h].astype(BF16)], axis=0)
            first.append(lax.dot_general(lhs, qc, NT_DIMS, preferred_element_type=F32))
            vas.append(jnp.concatenate([vt_ref[c, h * dh:(h + 1) * dh, :], ones_rows], axis=0))
        for h in range(H_M):
            kc = zqk_ref[pl.ds(r0, chunk), w_m + h * dh:w_m + (h + 1) * dh]
            g_row = rows_ref[0, h, pl.ds(c, 1), :]
            m_e = last[:, GATE_GROUP + h:GATE_GROUP + h + 1]
            m_p = prev[:, GATE_GROUP + h:GATE_GROUP + h + 1]
            vaw = (vas[h].astype(F32) * jnp.exp2(g_row - m_e)).astype(BF16)
            ct_sc[h] = jnp.exp2(m_p - m_e) * ct_sc[h] + jnp.dot(vaw, kc, preferred_element_type=F32)
        for h in range(H_M):
            ch = slice(h * dh, (h + 1) * dh)
            g_col = colsc[:, h:h + 1]
            m_row = rows_ref[0, GATE_GROUP + h, pl.ds(c, 1), :]
            en_row = rows_ref[0, 2 * GATE_GROUP + h, pl.ds(c, 1), :]
            m_p = prev[:, GATE_GROUP + h:GATE_GROUP + h + 1]
            sk = first[h]
            d00 = jnp.where(causal, jnp.exp2(g_col[0:hc] - m_row[:, 0:hc]), 0.0)
            d01 = jnp.exp2(g_col[0:hc] - m_row[:, hc:chunk])
            d11 = jnp.where(causal, jnp.exp2(g_col[hc:chunk] - m_row[:, hc:chunk]), 0.0)
            top = jnp.concatenate([sk[0:hc, 0:hc] * d00, sk[0:hc, hc:chunk] * d01], axis=1)
            bot = jnp.concatenate([jnp.zeros((hc, hc), F32), sk[hc:chunk, hc:chunk] * d11], axis=1)
            sqk = jnp.concatenate([top, bot], axis=0).astype(BF16)
            nd = (jnp.exp2(m_p - m_row) * first[h][chunk:]
                  + jnp.dot(vas[h], sqk, preferred_element_type=F32))
            den = nd[dh:dh + 1]
            ht = nd[0:dh] * (1.0 / jnp.maximum(jnp.abs(den), en_row))
            hn = ht * lax.rsqrt(jnp.mean(ht * ht, axis=0, keepdims=True) + EPS)
            og = og_ref[pl.ds(r0, chunk), ch].astype(F32)
            ym_ref[pl.ds(r0, chunk), ch] = (hn.T * ng_ref[:, ch] * og).astype(BF16)
        return carry

    lax.fori_loop(0, nc, body, 0, unroll=True)


def _fox_kernel(q_ref, k_ref, v_ref, csq_ref, csk_ref, qsel_ref, ksel_ref, yf_ref,
                kaug_sc, vt_sc, qaug_sc, m_sc, acc_sc, *, tq, tk):
    qi = pl.program_id(1)

    seq = k_ref.shape[0]
    dhp = LANES
    half = dhp // 2
    v_rows = half + BF16_ROWS

    def own_lanes(head, rows):
        lane = lax.broadcasted_iota(jnp.int32, (rows, dhp), 1)
        return (lane >= half) if head % 2 else (lane < half)

    @pl.when(qi == 0)
    def _():
        csk = csk_ref[...]
        ones_rows = jnp.where(
            lax.broadcasted_iota(jnp.int32, (BF16_ROWS, tk), 0) == 0, 1.0, 0.0).astype(BF16)
        for p in range(H_F // 2):
            cp = slice(p * dhp, (p + 1) * dhp)
            kp = k_ref[:, cp]
            vt = v_ref[:, cp].astype(F32).T
            for head in (2 * p, 2 * p + 1):
                kb = jnp.dot(csk, ksel_ref[head], preferred_element_type=F32).astype(BF16)
                kaug_sc[head] = jnp.where(own_lanes(head, seq), kp, kb)
                r0 = (head % 2) * half
                for jb in range(seq // tk):
                    vt_sc[head, jb, 0:half, :] = vt[r0:r0 + half, jb * tk:(jb + 1) * tk].astype(BF16)
                    vt_sc[head, jb, half:v_rows, :] = ones_rows

    csq = csq_ref[...]
    for head in range(H_F):
        qp = q_ref[:, (head // 2) * dhp:(head // 2 + 1) * dhp]
        qb = jnp.dot(csq, qsel_ref[head], preferred_element_type=F32).astype(BF16)
        qaug_sc[head] = jnp.where(own_lanes(head, tq), qp, qb)
    m_sc[...] = jnp.full(m_sc.shape, NEG_BIG, F32)
    acc_sc[...] = jnp.zeros(acc_sc.shape, F32)

    ratio = tq // tk

    def step(j, diag):
        q0 = 0 if diag is None else diag * tk
        nq = tq - q0
        k0 = pl.multiple_of(j * tk, tk)
        sts = [lax.dot_general(kaug_sc[head, pl.ds(k0, tk), :], qaug_sc[head, q0:tq, :], NT_DIMS,
                               preferred_element_type=F32) for head in range(H_F)]
        if diag is not None:
            causal = (lax.broadcasted_iota(jnp.int32, (tk, nq), 0)
                      <= lax.broadcasted_iota(jnp.int32, (tk, nq), 1))
        for head in range(H_F):
            st = sts[head]
            if diag is not None:
                st = jnp.where(causal, st, NEG_BIG)
            m = m_sc[head, :, q0:tq]
            m_new = jnp.maximum(m, jnp.max(st, axis=0, keepdims=True))
            alpha = jnp.exp2(m - m_new)
            pt = jnp.exp2(st - m_new).astype(BF16)
            m_sc[head, :, q0:tq] = m_new
            acc_sc[head, :, q0:tq] = (alpha * acc_sc[head, :, q0:tq]
                                      + jnp.dot(vt_sc[head, j], pt, preferred_element_type=F32))

    def loop_body(jj, carry):
        for r in range(ratio):
            step(jj * ratio + r, None)
        return carry

    lax.fori_loop(0, qi, loop_body, 0)
    for diag in range(ratio):
        step(qi * ratio + diag, diag)
    for p in range(H_F // 2):
        outs = []
        for head in (2 * p, 2 * p + 1):
            acc = acc_sc[head]
            outs.append(acc[0:half] * (1.0 / acc[half:half + 1]))
        yf_ref[:, p * dhp:(p + 1) * dhp] = jnp.concatenate(outs, axis=0).T.astype(BF16)


def _post_kernel(x_ref, ym_ref, yf_ref, zc_ref, wbm_ref, wbf_ref, wo_ref, g2_ref,
                 wg_ref, wu_ref, wd_ref, gfin_ref, o_ref, *, tf):
    d = x_ref.shape[1]
    d_ff = wg_ref.shape[1]
    bm = jnp.dot(ym_ref[...], wbm_ref[...], preferred_element_type=F32)
    bf = jnp.dot(yf_ref[...], wbf_ref[...], preferred_element_type=F32)
    mix = zc_ref[:, 0:d].astype(F32) * bm + zc_ref[:, d:2 * d].astype(F32) * bf
    x1 = x_ref[...] + jnp.dot(mix.astype(BF16), wo_ref[...], preferred_element_type=F32)
    h2 = _rms(x1, g2_ref[...]).astype(BF16)
    acc = jnp.zeros(x1.shape, F32)
    for f0 in range(0, d_ff, tf):
        g = jnp.dot(h2, wg_ref[:, f0:f0 + tf], preferred_element_type=F32)
        u = jnp.dot(h2, wu_ref[:, f0:f0 + tf], preferred_element_type=F32)
        act = (g * jax.nn.sigmoid(g) * u).astype(BF16)
        acc = acc + jnp.dot(act, wd_ref[f0:f0 + tf, :], preferred_element_type=F32)
    o_ref[...] = _rms(x1 + acc, gfin_ref[...])


def _bias_selectors():
    ones_lane = 3 * GATE_GROUP
    qsel = np.zeros((H_F, LANES, LANES), np.float32)
    ksel = np.zeros((H_F, LANES, LANES), np.float32)
    for h in range(H_F):
        p0 = LANES // 2 if h % 2 == 0 else 0
        for c in range(3):
            qsel[h, GATE_GROUP * c + h, p0 + c] = 1.0
            qsel[h, ones_lane, p0 + 3 + c] = 1.0
            ksel[h, ones_lane, p0 + c] = 1.0
            ksel[h, GATE_GROUP * c + h, p0 + 3 + c] = -1.0
    return jnp.asarray(qsel, BF16), jnp.asarray(ksel, BF16)


def _layer(x2d, batch, seq, p, cfg):
    t, d = x2d.shape
    w_m = p["w_br_mlstm"].shape[0]
    w_f = p["w_br_fox"].shape[0]
    dh_m = w_m // H_M
    dh_f = w_f // H_F
    vmem = cfg["vmem_limit"]

    wa, wb, wc, wd, ba, bb, bc, bd = _prep_in_weights(p, vmem)

    tm = cfg["tm_in"]
    chunk = cfg["chunk"]
    nc = seq // chunk
    n_steps = t // tm
    assert seq % tm == 0 and tm % chunk == 0

    later = [p["w_br_mlstm"], p["w_br_fox"], p["w_out"], p["w_gate"], p["w_up"], p["w_down"]]

    def slice_spec(rows, cols):
        hold = 1
        while (rows * hold) % (n_steps * BF16_ROWS):
            hold *= 2
        assert hold <= n_steps
        return pl.BlockSpec((rows * hold // n_steps, cols), lambda i: (i // hold, 0))

    later_specs = [slice_spec(*w.shape) for w in later]
    outs = pl.pallas_call(
        functools.partial(_inproj_kernel, w_m=w_m, w_f=w_f, n_chunk=cfg["n_chunk"],
                          m_split=cfg["m_split"],
                          q_scale=dh_f ** -0.5 * LOG2E, k_scale=dh_m ** -0.5, chunk=chunk,
                          tiles_per_seq=seq // tm, n_cast=len(later)),
        grid=(n_steps,),
        in_specs=[pl.BlockSpec((tm, d), lambda i: (i, 0)), _const_spec((1, d)),
                  _const_spec(wa.shape), _const_spec(ba.shape),
                  _const_spec(wb.shape), _const_spec(bb.shape),
                  _const_spec(wc.shape), _const_spec(bc.shape),
                  _const_spec(wd.shape), _const_spec(bd.shape),
                  _const_spec((CONV_K, 2 * w_m)), _const_spec((1, 2 * w_m))] + later_specs,
        out_specs=[pl.BlockSpec((tm, 2 * w_m), lambda i: (i, 0)),
                   pl.BlockSpec((tm // chunk, w_m, chunk), lambda i: (i, 0, 0)),
                   pl.BlockSpec((tm, w_m), lambda i: (i, 0)),
                   pl.BlockSpec((tm, 3 * w_f), lambda i: (i, 0)),
                   pl.BlockSpec((tm, 2 * d), lambda i: (i, 0)),
                   pl.BlockSpec((N_GATE_ROWS, tm), lambda i: (0, i))] + later_specs,
        out_shape=[jax.ShapeDtypeStruct((t, 2 * w_m), BF16),
                   jax.ShapeDtypeStruct((t // chunk, w_m, chunk), BF16),
                   jax.ShapeDtypeStruct((t, w_m), BF16),
                   jax.ShapeDtypeStruct((t, 3 * w_f), BF16),
                   jax.ShapeDtypeStruct((t, 2 * d), BF16),
                   jax.ShapeDtypeStruct((N_GATE_ROWS, t), F32)]
        + [jax.ShapeDtypeStruct(w.shape, BF16) for w in later],
        scratch_shapes=[pltpu.VMEM((2 * w_m // cfg["n_chunk"], tm + SUBLANES, cfg["n_chunk"]), F32)],
        compiler_params=pltpu.CompilerParams(dimension_semantics=("arbitrary",), vmem_limit_bytes=vmem),
        name="inproj",
    )(x2d, p["norm1_g"][None, :], wa, ba, wb, bb, wc, bc, wd, bd, p["conv_w"], p["conv_b"][None, :],
      *later)
    zqk, vt, og, zb, zc, zdt, wbm, wbf, wo, wg, wu, wdn = outs

    nb = cfg["gates_batch"]
    assert batch % nb == 0
    rows, cols, csplit = pl.pallas_call(
        functools.partial(_gates_kernel, seq=seq),
        grid=(batch // nb,),
        in_specs=[pl.BlockSpec((N_GATE_ROWS, nb * seq), lambda b: (0, b))],
        out_specs=[pl.BlockSpec((nb, 3 * GATE_GROUP, seq), lambda b: (b, 0, 0)),
                   pl.BlockSpec((nb * seq, LANES), lambda b: (b, 0)),
                   pl.BlockSpec((nb * seq, LANES), lambda b: (b, 0))],
        out_shape=[jax.ShapeDtypeStruct((batch, 3 * GATE_GROUP, seq), F32),
                   jax.ShapeDtypeStruct((t, LANES), F32),
                   jax.ShapeDtypeStruct((t, LANES), BF16)],
        compiler_params=pltpu.CompilerParams(dimension_semantics=("parallel",), vmem_limit_bytes=vmem),
        name="gates",
    )(zdt)

    rows_m = rows.reshape(batch, 3 * GATE_GROUP, nc, chunk)
    ym = pl.pallas_call(
        functools.partial(_mlstm_kernel, w_m=w_m, chunk=chunk),
        grid=(batch,),
        in_specs=[pl.BlockSpec((seq, 2 * w_m), lambda b: (b, 0)),
                  pl.BlockSpec((nc, w_m, chunk), lambda b: (b, 0, 0)),
                  pl.BlockSpec((seq, w_m), lambda b: (b, 0)),
                  pl.BlockSpec((1, 3 * GATE_GROUP, nc, chunk), lambda b: (b, 0, 0, 0)),
                  pl.BlockSpec((seq, LANES), lambda b: (b, 0)),
                  _const_spec((1, w_m))],
        out_specs=pl.BlockSpec((seq, w_m), lambda b: (b, 0)),
        out_shape=jax.ShapeDtypeStruct((t, w_m), BF16),
        scratch_shapes=[pltpu.VMEM((H_M, dh_m + BF16_ROWS, dh_m), F32)],
        compiler_params=pltpu.CompilerParams(dimension_semantics=("parallel",), vmem_limit_bytes=vmem),
        name="mlstm",
    )(zqk, vt, og, rows_m, cols, p["mlstm_norm_g"][None, :])

    tq, tk = cfg["tq"], cfg["tk"]
    assert tq % tk == 0 and seq % tq == 0
    nq = seq // tq
    qsel, ksel = _bias_selectors()
    v_rows = LANES // 2 + BF16_ROWS
    yf = pl.pallas_call(
        functools.partial(_fox_kernel, tq=tq, tk=tk),
        grid=(batch, nq),
        in_specs=[pl.BlockSpec((tq, w_f), lambda b, i: (b * nq + i, 0)),
                  pl.BlockSpec((seq, w_f), lambda b, i: (b, 1)),
                  pl.BlockSpec((seq, w_f), lambda b, i: (b, 2)),
                  pl.BlockSpec((tq, LANES), lambda b, i: (b * nq + i, 0)),
                  pl.BlockSpec((seq, LANES), lambda b, i: (b, 0)),
                  _const_spec(qsel.shape), _const_spec(ksel.shape)],
        out_specs=pl.BlockSpec((tq, w_f), lambda b, i: (b * nq + i, 0)),
        out_shape=jax.ShapeDtypeStruct((t, w_f), BF16),
        scratch_shapes=[pltpu.VMEM((H_F, seq, LANES), BF16),
                        pltpu.VMEM((H_F, seq // tk, v_rows, tk), BF16),
                        pltpu.VMEM((H_F, tq, LANES), BF16),
                        pltpu.VMEM((H_F, 1, tq), F32),
                        pltpu.VMEM((H_F, v_rows, tq), F32)],
        compiler_params=pltpu.CompilerParams(dimension_semantics=("arbitrary", "arbitrary"),
                                             vmem_limit_bytes=vmem),
        name="fox",
    )(zb, zb, zb, csplit, csplit, qsel, ksel)

    tmp = cfg["tm_post"]
    return pl.pallas_call(
        functools.partial(_post_kernel, tf=cfg["tf"]),
        grid=(t // tmp,),
        in_specs=[pl.BlockSpec((tmp, d), lambda i: (i, 0)),
                  pl.BlockSpec((tmp, w_m), lambda i: (i, 0)),
                  pl.BlockSpec((tmp, w_f), lambda i: (i, 0)),
                  pl.BlockSpec((tmp, 2 * d), lambda i: (i, 0)),
                  _const_spec(wbm.shape), _const_spec(wbf.shape), _const_spec(wo.shape),
                  _const_spec((1, d)), _const_spec(wg.shape), _const_spec(wu.shape),
                  _const_spec(wdn.shape), _const_spec((1, d))],
        out_specs=pl.BlockSpec((tmp, d), lambda i: (i, 0)),
        out_shape=jax.ShapeDtypeStruct((t, d), F32),
        compiler_params=pltpu.CompilerParams(dimension_semantics=("parallel",), vmem_limit_bytes=vmem),
        name="post",
    )(x2d, ym, yf, zc, wbm, wbf, wo, p["norm2_g"][None, :], wg, wu, wdn, p["norm_f_g"][None, :])


def kernel(x, norm1_g, w_in, b_in, conv_w, conv_b, mlstm_norm_g, w_br_mlstm, w_br_fox, w_out,
           norm2_g, w_gate, w_up, w_down, norm_f_g):
    batch, seq, d = x.shape
    depth = w_in.shape[0]
    assert depth == 1, "the final norm is fused into the single layer's last call"
    cfg = _cfg(batch, seq, d, w_gate.shape[-1])
    p = dict(norm1_g=norm1_g[0], w_in=w_in[0], b_in=b_in[0], conv_w=conv_w[0], conv_b=conv_b[0],
             mlstm_norm_g=mlstm_norm_g[0], w_br_mlstm=w_br_mlstm[0], w_br_fox=w_br_fox[0],
             w_out=w_out[0], norm2_g=norm2_g[0], w_gate=w_gate[0], w_up=w_up[0], w_down=w_down[0],
             norm_f_g=norm_f_g)
    out = _layer(x.reshape(batch * seq, d), batch, seq, p, cfg)
    return out.reshape(batch, seq, d)
```

```python
import functools

import jax
import jax.numpy as jnp
import numpy as np
from jax import lax
from jax.experimental import pallas as pl
from jax.experimental.pallas import tpu as pltpu

EPS = 1e-6
H_M = 4
H_F = 8
CONV_K = 4

LANES = 128
SUBLANES = 8
BF16_ROWS = 16
GATE_GROUP = 8
N_GATE_ROWS = 3 * GATE_GROUP
V7X_VMEM_BYTES = 64 * 1024 * 1024
NEG_BIG = -1e30
LOG2E = 1.4426950408889634

F32 = jnp.float32
BF16 = jnp.bfloat16
NT_DIMS = (((1,), (1,)), ((), ()))


def _cfg(batch, seq, d_model, d_ff):
    return dict(
        tm_in=512,
        n_chunk=256,
        m_split=4,
        gates_batch=4,
        chunk=256,
        tq=512,
        tk=256,
        tm_post=512,
        tf=256,
        vmem_limit=V7X_VMEM_BYTES - 8 * 1024 * 1024,
    )


def _const_spec(shape):
    nd = len(shape)
    return pl.BlockSpec(shape, lambda *_: (0,) * nd, pipeline_mode=pl.Buffered(1))


def _rms(x, g):
    return x * lax.rsqrt(jnp.mean(x * x, axis=-1, keepdims=True) + EPS) * g


def _log_sigmoid(x):
    return jnp.minimum(x, 0.0) - jnp.log1p(jnp.exp(-jnp.abs(x)))


def _wprep_kernel(wint_ref, bin_ref, wa_o, wb_o, wc_o, wd_o, ba_o, bb_o, bc_o, bd_o, *, w_m, w_f):
    o_mi = 4 * w_m
    o_mf = o_mi + H_M
    o_fq = o_mf + H_M
    o_ff = o_fq + 3 * w_f
    o_g = o_ff + H_F

    def gate_block(src):
        rows = src.shape[0]
        lane = lax.broadcasted_iota(jnp.int32, (rows, LANES), 1)
        blk_m = src[:, o_mi:o_mi + LANES]
        f0 = (o_ff // LANES) * LANES
        blk_f = src[:, f0:f0 + LANES]
        mi = jnp.where(lane < H_M, blk_m, 0.0)
        mf = jnp.where((lane >= GATE_GROUP) & (lane < GATE_GROUP + H_M),
                       pltpu.roll(blk_m, GATE_GROUP - H_M, axis=1), 0.0)
        ff = jnp.where((lane >= 2 * GATE_GROUP) & (lane < 2 * GATE_GROUP + H_F),
                       pltpu.roll(blk_f, 2 * GATE_GROUP - (o_ff - f0), axis=1), 0.0)
        return mi + mf + ff

    slab = wint_ref[...]
    wa_o[...] = slab[0:o_mi].T.astype(BF16)
    wb_o[...] = slab[o_fq:o_ff].T.astype(BF16)
    wc_o[...] = slab[o_g:].T.astype(BF16)
    g_m = slab[o_mi:o_mi + GATE_GROUP]
    row = lax.broadcasted_iota(jnp.int32, g_m.shape, 0)
    gates = jnp.concatenate(
        [jnp.where(row < H_M, g_m, 0.0),
         jnp.where(row < H_M, pltpu.roll(g_m, GATE_GROUP - H_M, axis=0), 0.0),
         slab[o_ff:o_ff + GATE_GROUP],
         jnp.zeros((LANES - 3 * GATE_GROUP, slab.shape[1]), F32)], axis=0)
    wd_o[...] = gates.T.astype(BF16)
    b = bin_ref[...]
    ba_o[...] = b[:, 0:o_mi]
    bb_o[...] = b[:, o_fq:o_ff]
    bc_o[...] = b[:, o_g:]
    bd_o[...] = gate_block(b)


def _prep_in_weights(p, vmem):
    d, n_in = p["w_in"].shape
    w_m = p["w_br_mlstm"].shape[0]
    w_f = p["w_br_fox"].shape[0]
    steps = 8
    o_ff = 4 * w_m + 2 * H_M + 3 * w_f
    assert (4 * w_m) % LANES == 0 and 2 * H_M <= GATE_GROUP + H_M <= LANES
    assert o_ff % LANES <= 2 * GATE_GROUP and o_ff % LANES + H_F <= LANES
    assert n_in == o_ff + H_F + 2 * d
    srcs = [p["w_in"].T, p["b_in"][None, :]]

    def row_spec(shape, tiled=True):
        if not tiled:
            return pl.BlockSpec(shape, lambda i: (0, 0))
        assert shape[0] % (steps * BF16_ROWS) == 0
        return pl.BlockSpec((shape[0] // steps, shape[1]), lambda i: (i, 0))

    out_shapes = [((d, 4 * w_m), BF16), ((d, 3 * w_f), BF16), ((d, 2 * d), BF16), ((d, LANES), BF16),
                  ((1, 4 * w_m), F32), ((1, 3 * w_f), F32), ((1, 2 * d), F32), ((1, LANES), F32)]
    return pl.pallas_call(
        functools.partial(_wprep_kernel, w_m=w_m, w_f=w_f),
        grid=(steps,),
        in_specs=([pl.BlockSpec((n_in, d // steps), lambda i: (0, i))]
                  + [row_spec(s.shape, tiled=s.shape[0] > 1) for s in srcs[1:]]),
        out_specs=[row_spec(s, tiled=s[0] > 1) for s, _ in out_shapes],
        out_shape=[jax.ShapeDtypeStruct(s, dt) for s, dt in out_shapes],
        compiler_params=pltpu.CompilerParams(dimension_semantics=("arbitrary",), vmem_limit_bytes=vmem),
        name="wprep",
    )(*srcs)


def _inproj_kernel(x_ref, g_ref, wa_ref, ba_ref, wb_ref, bb_ref, wc_ref, bc_ref, wd_ref, bd_ref,
                   cw_ref, cb_ref, *rest,
                   w_m, w_f, n_chunk, m_split, q_scale, k_scale, chunk, tiles_per_seq, n_cast):
    cast_srcs, rest = rest[:n_cast], rest[n_cast:]
    (zqk_ref, vt_ref, og_ref, zb_ref, zc_ref, zdt_ref), rest = rest[:6], rest[6:]
    cast_dsts, (zs_sc,) = rest[:n_cast], rest[n_cast:]
    tm = x_ref.shape[0]
    seq_start = (pl.program_id(0) % tiles_per_seq) == 0

    @pl.when(seq_start)
    def _():
        zs_sc[:, 0:SUBLANES, :] = jnp.zeros((zs_sc.shape[0], SUBLANES, n_chunk), F32)

    @pl.when(jnp.logical_not(seq_start))
    def _():
        zs_sc[:, 0:SUBLANES, :] = zs_sc[:, tm:tm + SUBLANES, :]

    rows = tm // m_split
    hb = [_rms(x_ref[r0:r0 + rows, :], g_ref[...]).astype(BF16) for r0 in range(0, tm, rows)]

    def proj(w_ref, b_ref, c0, c1):
        w = w_ref[:, c0:c1]
        parts = [jnp.dot(h, w, preferred_element_type=F32) for h in hb]
        return jnp.concatenate(parts, axis=0) + b_ref[:, c0:c1]

    def qk_chunk(ci):
        c0 = ci * n_chunk
        cs = slice(c0, c0 + n_chunk)
        zs_sc[ci, SUBLANES:tm + SUBLANES, :] = proj(wa_ref, ba_ref, c0, c0 + n_chunk)
        w = cw_ref[:, cs]
        y = cb_ref[:, cs]
        for s in range(CONV_K):
            y = y + zs_sc[ci, SUBLANES - s:SUBLANES - s + tm, :] * w[CONV_K - 1 - s:CONV_K - s]
        act = y * jax.nn.sigmoid(y)
        if c0 >= w_m:
            act = act * k_scale
        zqk_ref[:, cs] = act.astype(BF16)

    def v_chunk(i):
        c0 = 2 * w_m + i * n_chunk
        z = proj(wa_ref, ba_ref, c0, c0 + n_chunk)
        for cc in range(tm // chunk):
            for f0 in range(0, n_chunk, LANES):
                blk = z[cc * chunk:(cc + 1) * chunk, f0:f0 + LANES]
                r0 = i * n_chunk + f0
                vt_ref[cc, r0:r0 + LANES, :] = blk.T.astype(BF16)

    def og_chunk(i):
        c0 = 3 * w_m + i * n_chunk
        z = proj(wa_ref, ba_ref, c0, c0 + n_chunk)
        og_ref[:, i * n_chunk:(i + 1) * n_chunk] = jax.nn.sigmoid(z).astype(BF16)

    def b_chunk(i):
        c0 = i * n_chunk
        z = proj(wb_ref, bb_ref, c0, c0 + n_chunk)
        if c0 < w_f:
            z = z * q_scale
        zb_ref[:, c0:c0 + n_chunk] = z.astype(BF16)

    def c_chunk(i):
        c0 = i * n_chunk
        z = proj(wc_ref, bc_ref, c0, c0 + n_chunk)
        zc_ref[:, c0:c0 + n_chunk] = jax.nn.sigmoid(z).astype(BF16)

    def d_chunk(_):
        zd = proj(wd_ref, bd_ref, 0, LANES)
        zdt_ref[...] = zd.T[:N_GATE_ROWS, :]

    heavy = [(qk_chunk, i) for i in range(2 * w_m // n_chunk)]
    light = ([(c_chunk, i) for i in range(wc_ref.shape[1] // n_chunk)]
             + [(b_chunk, i) for i in range(3 * w_f // n_chunk)]
             + [(v_chunk, i) for i in range(w_m // n_chunk)]
             + [(og_chunk, i) for i in range(w_m // n_chunk)] + [(d_chunk, 0)])
    per_heavy = len(light) // len(heavy)
    order = []
    for hi, task in enumerate(heavy):
        order.append(task)
        order.extend(light[hi * per_heavy:(hi + 1) * per_heavy])
    order.extend(light[len(heavy) * per_heavy:])
    for fn, i in order:
        fn(i)
    for src, dst in zip(cast_srcs, cast_dsts):
        dst[...] = src[...].astype(BF16)


def _scan_lanes(x, op, fill, seg):
    pos = lax.rem(lax.broadcasted_iota(jnp.int32, x.shape, 1), seg)
    s = 1
    while s < seg:
        shifted = pltpu.roll(x, s, axis=1)
        x = op(x, jnp.where(pos >= s, shifted, fill))
        s *= 2
    return x


def _gates_kernel(zdt_ref, rows_ref, cols_ref, csplit_ref, *, seq):
    z = zdt_ref[...]
    width = z.shape[1]
    i8 = z[0:GATE_GROUP]
    cum = _scan_lanes(_log_sigmoid(z[GATE_GROUP:3 * GATE_GROUP]), jnp.add, 0.0, seq)
    f8 = cum[0:GATE_GROUP]
    cf8 = cum[GATE_GROUP:2 * GATE_GROUP] * LOG2E
    g8 = i8 - f8
    m8 = jnp.maximum(_scan_lanes(g8, jnp.maximum, NEG_BIG, seq), 0.0)
    en8 = jnp.exp(-(f8 + m8))
    g8, m8 = g8 * LOG2E, m8 * LOG2E
    rows = jnp.concatenate([g8, m8, en8], axis=0)
    for bb in range(width // seq):
        rows_ref[bb] = rows[:, bb * seq:(bb + 1) * seq]
    stack = jnp.concatenate(
        [g8, m8, jnp.zeros((LANES - 2 * GATE_GROUP, width), F32)], axis=0)
    cols_ref[...] = stack.T
    hi = cf8.astype(BF16).astype(F32)
    r1 = cf8 - hi
    lo = r1.astype(BF16).astype(F32)
    lo2 = (r1 - lo).astype(BF16).astype(F32)
    ones = jnp.where(lax.broadcasted_iota(jnp.int32, (GATE_GROUP, width), 0) == 0, 1.0, 0.0)
    split = jnp.concatenate(
        [hi, lo, lo2, ones, jnp.zeros((LANES - 4 * GATE_GROUP, width), F32)], axis=0)
    csplit_ref[...] = split.T.astype(BF16)


def _mlstm_kernel(zqk_ref, vt_ref, og_ref, rows_ref, cols_ref, ng_ref, ym_ref, ct_sc, *, w_m, chunk):
    seq = zqk_ref.shape[0]
    dh = w_m // H_M
    nc = seq // chunk
    ct_sc[...] = jnp.zeros(ct_sc.shape, F32)
    hc = chunk // 2
    causal = (lax.broadcasted_iota(jnp.int32, (hc, hc), 0)
              <= lax.broadcasted_iota(jnp.int32, (hc, hc), 1))
    ones_rows = jnp.where(
        lax.broadcasted_iota(jnp.int32, (BF16_ROWS, chunk), 0) == 0, 1.0, 0.0).astype(BF16)

    def body(c, carry):
        r0 = pl.multiple_of(c * chunk, chunk)
        colsc = cols_ref[pl.ds(r0, chunk), :]
        last = cols_ref[pl.ds(r0 + chunk - 1, 1), :]
        prev = cols_ref[pl.ds(jnp.maximum(r0 - 1, 0), 1), :]
        prev = jnp.where(c > 0, prev, 0.0)
        first, vas = [], []
        for h in range(H_M):
            qc = zqk_ref[pl.ds(r0, chunk), h * dh:(h + 1) * dh]
            kc = zqk_ref[pl.ds(r0, chunk), w_m + h * dh:w_m + (h + 1) * dh]
            lhs = jnp.concatenate([kc, ct_sc[h].astype(BF16)], axis=0)
            first.append(lax.dot_general(lhs, qc, NT_DIMS, preferred_element_type=F32))
            vas.append(jnp.concatenate([vt_ref[c, h * dh:(h + 1) * dh, :], ones_rows], axis=0))
        for h in range(H_M):
            kc = zqk_ref[pl.ds(r0, chunk), w_m + h * dh:w_m + (h + 1) * dh]
            g_row = rows_ref[0, h, pl.ds(c, 1), :]
            m_e = last[:, GATE_GROUP + h:GATE_GROUP + h + 1]
            m_p = prev[:, GATE_GROUP + h:GATE_GROUP + h + 1]
            vaw = (vas[h].astype(F32) * jnp.exp2(g_row - m_e)).astype(BF16)
            ct_sc[h] = jnp.exp2(m_p - m_e) * ct_sc[h] + jnp.dot(vaw, kc, preferred_element_type=F32)
        for h in range(H_M):
            ch = slice(h * dh, (h + 1) * dh)
            g_col = colsc[:, h:h + 1]
            m_row = rows_ref[0, GATE_GROUP + h, pl.ds(c, 1), :]
            en_row = rows_ref[0, 2 * GATE_GROUP + h, pl.ds(c, 1), :]
            m_p = prev[:, GATE_GROUP + h:GATE_GROUP + h + 1]
            sk = first[h]
            d00 = jnp.where(causal, jnp.exp2(g_col[0:hc] - m_row[:, 0:hc]), 0.0)
            d01 = jnp.exp2(g_col[0:hc] - m_row[:, hc:chunk])
            d11 = jnp.where(causal, jnp.exp2(g_col[hc:chunk] - m_row[:, hc:chunk]), 0.0)
            top = jnp.concatenate([sk[0:hc, 0:hc] * d00, sk[0:hc, hc:chunk] * d01], axis=1)
            bot = jnp.concatenate([jnp.zeros((hc, hc), F32), sk[hc:chunk, hc:chunk] * d11], axis=1)
            sqk = jnp.concatenate([top, bot], axis=0).astype(BF16)
            nd = (jnp.exp2(m_p - m_row) * first[h][chunk:]
                  + jnp.dot(vas[h], sqk, preferred_element_type=F32))
            den = nd[dh:dh + 1]
            ht = nd[0:dh] * (1.0 / jnp.maximum(jnp.abs(den), en_row))
            hn = ht * lax.rsqrt(jnp.mean(ht * ht, axis=0, keepdims=True) + EPS)
            og = og_ref[pl.ds(r0, chunk), ch].astype(F32)
            ym_ref[pl.ds(r0, chunk), ch] = (hn.T * ng_ref[:, ch] * og).astype(BF16)
        return carry

    lax.fori_loop(0, nc, body, 0, unroll=True)


def _fox_kernel(q_ref, k_ref, v_ref, csq_ref, csk_ref, qsel_ref, ksel_ref, yf_ref,
                kaug_sc, vt_sc, qaug_sc, m_sc, acc_sc, *, tq, tk):
    qi = pl.program_id(1)

    seq = k_ref.shape[0]
    dhp = LANES
    half = dhp // 2
    v_rows = half + BF16_ROWS

    def own_lanes(head, rows):
        lane = lax.broadcasted_iota(jnp.int32, (rows, dhp), 1)
        return (lane >= half) if head % 2 else (lane < half)

    @pl.when(qi == 0)
    def _():
        csk = csk_ref[...]
        ones_rows = jnp.where(
            lax.broadcasted_iota(jnp.int32, (BF16_ROWS, tk), 0) == 0, 1.0, 0.0).astype(BF16)
        for p in range(H_F // 2):
            cp = slice(p * dhp, (p + 1) * dhp)
            kp = k_ref[:, cp]
            vt = v_ref[:, cp].astype(F32).T
            for head in (2 * p, 2 * p + 1):
                kb = jnp.dot(csk, ksel_ref[head], preferred_element_type=F32).astype(BF16)
                kaug_sc[head] = jnp.where(own_lanes(head, seq), kp, kb)
                r0 = (head % 2) * half
                for jb in range(seq // tk):
                    vt_sc[head, jb, 0:half, :] = vt[r0:r0 + half, jb * tk:(jb + 1) * tk].astype(BF16)
                    vt_sc[head, jb, half:v_rows, :] = ones_rows

    csq = csq_ref[...]
    for head in range(H_F):
        qp = q_ref[:, (head // 2) * dhp:(head // 2 + 1) * dhp]
        qb = jnp.dot(csq, qsel_ref[head], preferred_element_type=F32).astype(BF16)
        qaug_sc[head] = jnp.where(own_lanes(head, tq), qp, qb)
    m_sc[...] = jnp.full(m_sc.shape, NEG_BIG, F32)
    acc_sc[...] = jnp.zeros(acc_sc.shape, F32)

    ratio = tq // tk

    def step(j, diag):
        q0 = 0 if diag is None else diag * tk
        nq = tq - q0
        k0 = pl.multiple_of(j * tk, tk)
        sts = [lax.dot_general(kaug_sc[head, pl.ds(k0, tk), :], qaug_sc[head, q0:tq, :], NT_DIMS,
                               preferred_element_type=F32) for head in range(H_F)]
        if diag is not None:
            causal = (lax.broadcasted_iota(jnp.int32, (tk, nq), 0)
                      <= lax.broadcasted_iota(jnp.int32, (tk, nq), 1))
        for head in range(H_F):
            st = sts[head]
            if diag is not None:
                st = jnp.where(causal, st, NEG_BIG)
            m = m_sc[head, :, q0:tq]
            m_new = jnp.maximum(m, jnp.max(st, axis=0, keepdims=True))
            alpha = jnp.exp2(m - m_new)
            pt = jnp.exp2(st - m_new).astype(BF16)
            m_sc[head, :, q0:tq] = m_new
            acc_sc[head, :, q0:tq] = (alpha * acc_sc[head, :, q0:tq]
                                      + jnp.dot(vt_sc[head, j], pt, preferred_element_type=F32))

    def loop_body(jj, carry):
        for r in range(ratio):
            step(jj * ratio + r, None)
        return carry

    lax.fori_loop(0, qi, loop_body, 0)
    for diag in range(ratio):
        step(qi * ratio + diag, diag)
    for p in range(H_F // 2):
        outs = []
        for head in (2 * p, 2 * p + 1):
            acc = acc_sc[head]
            outs.append(acc[0:half] * (1.0 / acc[half:half + 1]))
        yf_ref[:, p * dhp:(p + 1) * dhp] = jnp.concatenate(outs, axis=0).T.astype(BF16)


def _post_kernel(x_ref, ym_ref, yf_ref, zc_ref, wbm_ref, wbf_ref, wo_ref, g2_ref,
                 wg_ref, wu_ref, wd_ref, gfin_ref, o_ref, *, tf, m_split):
    d = x_ref.shape[1]
    d_ff = wg_ref.shape[1]
    bm = jnp.dot(ym_ref[...], wbm_ref[...], preferred_element_type=F32)
    bf = jnp.dot(yf_ref[...], wbf_ref[...], preferred_element_type=F32)
    mix = zc_ref[:, 0:d].astype(F32) * bm + zc_ref[:, d:2 * d].astype(F32) * bf
    x1 = x_ref[...] + jnp.dot(mix.astype(BF16), wo_ref[...], preferred_element_type=F32)
    h2 = _rms(x1, g2_ref[...]).astype(BF16)
    acts = []
    for f0 in range(0, d_ff, tf):
        g = jnp.dot(h2, wg_ref[:, f0:f0 + tf], preferred_element_type=F32)
        u = jnp.dot(h2, wu_ref[:, f0:f0 + tf], preferred_element_type=F32)
        acts.append((g * jax.nn.sigmoid(g) * u).astype(BF16))
    act = jnp.concatenate(acts, axis=1)
    rows = x1.shape[0] // m_split
    for r0 in range(0, x1.shape[0], rows):
        down = jnp.dot(act[r0:r0 + rows], wd_ref[...], preferred_element_type=F32)
        o_ref[r0:r0 + rows, :] = _rms(x1[r0:r0 + rows] + down, gfin_ref[...])


def _bias_selectors():
    ones_lane = 3 * GATE_GROUP
    qsel = np.zeros((H_F, LANES, LANES), np.float32)
    ksel = np.zeros((H_F, LANES, LANES), np.float32)
    for h in range(H_F):
        p0 = LANES // 2 if h % 2 == 0 else 0
        for c in range(3):
            qsel[h, GATE_GROUP * c + h, p0 + c] = 1.0
            qsel[h, ones_lane, p0 + 3 + c] = 1.0
            ksel[h, ones_lane, p0 + c] = 1.0
            ksel[h, GATE_GROUP * c + h, p0 + 3 + c] = -1.0
    return jnp.asarray(qsel, BF16), jnp.asarray(ksel, BF16)


def _layer(x2d, batch, seq, p, cfg):
    t, d = x2d.shape
    w_m = p["w_br_mlstm"].shape[0]
    w_f = p["w_br_fox"].shape[0]
    dh_m = w_m // H_M
    dh_f = w_f // H_F
    vmem = cfg["vmem_limit"]

    wa, wb, wc, wd, ba, bb, bc, bd = _prep_in_weights(p, vmem)

    tm = cfg["tm_in"]
    chunk = cfg["chunk"]
    nc = seq // chunk
    n_steps = t // tm
    assert seq % tm == 0 and tm % chunk == 0

    later = [p["w_br_mlstm"], p["w_br_fox"], p["w_out"], p["w_gate"], p["w_up"], p["w_down"]]

    def slice_spec(rows, cols):
        hold = 1
        while (rows * hold) % (n_steps * BF16_ROWS):
            hold *= 2
        assert hold <= n_steps
        return pl.BlockSpec((rows * hold // n_steps, cols), lambda i: (i // hold, 0))

    later_specs = [slice_spec(*w.shape) for w in later]
    outs = pl.pallas_call(
        functools.partial(_inproj_kernel, w_m=w_m, w_f=w_f, n_chunk=cfg["n_chunk"],
                          m_split=cfg["m_split"],
                          q_scale=dh_f ** -0.5 * LOG2E, k_scale=dh_m ** -0.5, chunk=chunk,
                          tiles_per_seq=seq // tm, n_cast=len(later)),
        grid=(n_steps,),
        in_specs=[pl.BlockSpec((tm, d), lambda i: (i, 0)), _const_spec((1, d)),
                  _const_spec(wa.shape), _const_spec(ba.shape),
                  _const_spec(wb.shape), _const_spec(bb.shape),
                  _const_spec(wc.shape), _const_spec(bc.shape),
                  _const_spec(wd.shape), _const_spec(bd.shape),
                  _const_spec((CONV_K, 2 * w_m)), _const_spec((1, 2 * w_m))] + later_specs,
        out_specs=[pl.BlockSpec((tm, 2 * w_m), lambda i: (i, 0)),
                   pl.BlockSpec((tm // chunk, w_m, chunk), lambda i: (i, 0, 0)),
                   pl.BlockSpec((tm, w_m), lambda i: (i, 0)),
                   pl.BlockSpec((tm, 3 * w_f), lambda i: (i, 0)),
                   pl.BlockSpec((tm, 2 * d), lambda i: (i, 0)),
                   pl.BlockSpec((N_GATE_ROWS, tm), lambda i: (0, i))] + later_specs,
        out_shape=[jax.ShapeDtypeStruct((t, 2 * w_m), BF16),
                   jax.ShapeDtypeStruct((t // chunk, w_m, chunk), BF16),
                   jax.ShapeDtypeStruct((t, w_m), BF16),
                   jax.ShapeDtypeStruct((t, 3 * w_f), BF16),
                   jax.ShapeDtypeStruct((t, 2 * d), BF16),
                   jax.ShapeDtypeStruct((N_GATE_ROWS, t), F32)]
        + [jax.ShapeDtypeStruct(w.shape, BF16) for w in later],
        scratch_shapes=[pltpu.VMEM((2 * w_m // cfg["n_chunk"], tm + SUBLANES, cfg["n_chunk"]), F32)],
        compiler_params=pltpu.CompilerParams(dimension_semantics=("arbitrary",), vmem_limit_bytes=vmem),
        name="inproj",
    )(x2d, p["norm1_g"][None, :], wa, ba, wb, bb, wc, bc, wd, bd, p["conv_w"], p["conv_b"][None, :],
      *later)
    zqk, vt, og, zb, zc, zdt, wbm, wbf, wo, wg, wu, wdn = outs

    nb = cfg["gates_batch"]
    assert batch % nb == 0
    rows, cols, csplit = pl.pallas_call(
        functools.partial(_gates_kernel, seq=seq),
        grid=(batch // nb,),
        in_specs=[pl.BlockSpec((N_GATE_ROWS, nb * seq), lambda b: (0, b))],
        out_specs=[pl.BlockSpec((nb, 3 * GATE_GROUP, seq), lambda b: (b, 0, 0)),
                   pl.BlockSpec((nb * seq, LANES), lambda b: (b, 0)),
                   pl.BlockSpec((nb * seq, LANES), lambda b: (b, 0))],
        out_shape=[jax.ShapeDtypeStruct((batch, 3 * GATE_GROUP, seq), F32),
                   jax.ShapeDtypeStruct((t, LANES), F32),
                   jax.ShapeDtypeStruct((t, LANES), BF16)],
        compiler_params=pltpu.CompilerParams(dimension_semantics=("parallel",), vmem_limit_bytes=vmem),
        name="gates",
    )(zdt)

    rows_m = rows.reshape(batch, 3 * GATE_GROUP, nc, chunk)
    ym = pl.pallas_call(
        functools.partial(_mlstm_kernel, w_m=w_m, chunk=chunk),
        grid=(batch,),
        in_specs=[pl.BlockSpec((seq, 2 * w_m), lambda b: (b, 0)),
                  pl.BlockSpec((nc, w_m, chunk), lambda b: (b, 0, 0)),
                  pl.BlockSpec((seq, w_m), lambda b: (b, 0)),
                  pl.BlockSpec((1, 3 * GATE_GROUP, nc, chunk), lambda b: (b, 0, 0, 0)),
                  pl.BlockSpec((seq, LANES), lambda b: (b, 0)),
                  _const_spec((1, w_m))],
        out_specs=pl.BlockSpec((seq, w_m), lambda b: (b, 0)),
        out_shape=jax.ShapeDtypeStruct((t, w_m), BF16),
        scratch_shapes=[pltpu.VMEM((H_M, dh_m + BF16_ROWS, dh_m), F32)],
        compiler_params=pltpu.CompilerParams(dimension_semantics=("parallel",), vmem_limit_bytes=vmem),
        name="mlstm",
    )(zqk, vt, og, rows_m, cols, p["mlstm_norm_g"][None, :])

    tq, tk = cfg["tq"], cfg["tk"]
    assert tq % tk == 0 and seq % tq == 0
    nq = seq // tq
    qsel, ksel = _bias_selectors()
    v_rows = LANES // 2 + BF16_ROWS
    yf = pl.pallas_call(
        functools.partial(_fox_kernel, tq=tq, tk=tk),
        grid=(batch, nq),
        in_specs=[pl.BlockSpec((tq, w_f), lambda b, i: (b * nq + i, 0)),
                  pl.BlockSpec((seq, w_f), lambda b, i: (b, 1)),
                  pl.BlockSpec((seq, w_f), lambda b, i: (b, 2)),
                  pl.BlockSpec((tq, LANES), lambda b, i: (b * nq + i, 0)),
                  pl.BlockSpec((seq, LANES), lambda b, i: (b, 0)),
                  _const_spec(qsel.shape), _const_spec(ksel.shape)],
        out_specs=pl.BlockSpec((tq, w_f), lambda b, i: (b * nq + i, 0)),
        out_shape=jax.ShapeDtypeStruct((t, w_f), BF16),
        scratch_shapes=[pltpu.VMEM((H_F, seq, LANES), BF16),
                        pltpu.VMEM((H_F, seq // tk, v_rows, tk), BF16),
                        pltpu.VMEM((H_F, tq, LANES), BF16),
                        pltpu.VMEM((H_F, 1, tq), F32),
                        pltpu.VMEM((H_F, v_rows, tq), F32)],
        compiler_params=pltpu.CompilerParams(dimension_semantics=("arbitrary", "arbitrary"),
                                             vmem_limit_bytes=vmem),
        name="fox",
    )(zb, zb, zb, csplit, csplit, qsel, ksel)

    tmp = cfg["tm_post"]
    return pl.pallas_call(
        functools.partial(_post_kernel, tf=cfg["tf"], m_split=cfg["m_split"]),
        grid=(t // tmp,),
        in_specs=[pl.BlockSpec((tmp, d), lambda i: (i, 0)),
                  pl.BlockSpec((tmp, w_m), lambda i: (i, 0)),
                  pl.BlockSpec((tmp, w_f), lambda i: (i, 0)),
                  pl.BlockSpec((tmp, 2 * d), lambda i: (i, 0)),
                  _const_spec(wbm.shape), _const_spec(wbf.shape), _const_spec(wo.shape),
                  _const_spec((1, d)), _const_spec(wg.shape), _const_spec(wu.shape),
                  _const_spec(wdn.shape), _const_spec((1, d))],
        out_specs=pl.BlockSpec((tmp, d), lambda i: (i, 0)),
        out_shape=jax.ShapeDtypeStruct((t, d), F32),
        compiler_params=pltpu.CompilerParams(dimension_semantics=("parallel",), vmem_limit_bytes=vmem),
        name="post",
    )(x2d, ym, yf, zc, wbm, wbf, wo, p["norm2_g"][None, :], wg, wu, wdn, p["norm_f_g"][None, :])


def kernel(x, norm1_g, w_in, b_in, conv_w, conv_b, mlstm_norm_g, w_br_mlstm, w_br_fox, w_out,
           norm2_g, w_gate, w_up, w_down, norm_f_g):
    batch, seq, d = x.shape
    depth = w_in.shape[0]
    assert depth == 1, "the final norm is fused into the single layer's last call"
    cfg = _cfg(batch, seq, d, w_gate.shape[-1])
    p = dict(norm1_g=norm1_g[0], w_in=w_in[0], b_in=b_in[0], conv_w=conv_w[0], conv_b=conv_b[0],
             mlstm_norm_g=mlstm_norm_g[0], w_br_mlstm=w_br_mlstm[0], w_br_fox=w_br_fox[0],
             w_out=w_out[0], norm2_g=norm2_g[0], w_gate=w_gate[0], w_up=w_up[0], w_down=w_down[0],
             norm_f_g=norm_f_g)
    out = _layer(x.reshape(batch * seq, d), batch, seq, p, cfg)
    return out.reshape(batch, seq, d)
```

```python
import functools

import jax
import jax.numpy as jnp
import numpy as np
from jax import lax
from jax.experimental import pallas as pl
from jax.experimental.pallas import tpu as pltpu

EPS = 1e-6
H_M = 4
H_F = 8
CONV_K = 4

LANES = 128
SUBLANES = 8
BF16_ROWS = 16
GATE_GROUP = 8
N_GATE_ROWS = 3 * GATE_GROUP
V7X_VMEM_BYTES = 64 * 1024 * 1024
NEG_BIG = -1e30
LOG2E = 1.4426950408889634

F32 = jnp.float32
BF16 = jnp.bfloat16
NT_DIMS = (((1,), (1,)), ((), ()))


def _cfg(batch, seq, d_model, d_ff):
    return dict(
        tm_in=1024,
        n_chunk=256,
        m_split=8,
        gates_batch=4,
        chunk=256,
        tq=512,
        tk=256,
        tm_post=512,
        tf=256,
        vmem_limit=V7X_VMEM_BYTES - 8 * 1024 * 1024,
    )


def _const_spec(shape):
    nd = len(shape)
    return pl.BlockSpec(shape, lambda *_: (0,) * nd, pipeline_mode=pl.Buffered(1))


def _rms(x, g):
    return x * lax.rsqrt(jnp.mean(x * x, axis=-1, keepdims=True) + EPS) * g


def _log_sigmoid(x):
    return jnp.minimum(x, 0.0) - jnp.log1p(jnp.exp(-jnp.abs(x)))


def _wprep_kernel(wint_ref, bin_ref, wa_o, wb_o, wc_o, wd_o, ba_o, bb_o, bc_o, bd_o, *, w_m, w_f):
    o_mi = 4 * w_m
    o_mf = o_mi + H_M
    o_fq = o_mf + H_M
    o_ff = o_fq + 3 * w_f
    o_g = o_ff + H_F

    def gate_block(src):
        rows = src.shape[0]
        lane = lax.broadcasted_iota(jnp.int32, (rows, LANES), 1)
        blk_m = src[:, o_mi:o_mi + LANES]
        f0 = (o_ff // LANES) * LANES
        blk_f = src[:, f0:f0 + LANES]
        mi = jnp.where(lane < H_M, blk_m, 0.0)
        mf = jnp.where((lane >= GATE_GROUP) & (lane < GATE_GROUP + H_M),
                       pltpu.roll(blk_m, GATE_GROUP - H_M, axis=1), 0.0)
        ff = jnp.where((lane >= 2 * GATE_GROUP) & (lane < 2 * GATE_GROUP + H_F),
                       pltpu.roll(blk_f, 2 * GATE_GROUP - (o_ff - f0), axis=1), 0.0)
        return mi + mf + ff

    slab = wint_ref[...]
    wa_o[...] = slab[0:o_mi].T.astype(BF16)
    wb_o[...] = slab[o_fq:o_ff].T.astype(BF16)
    wc_o[...] = slab[o_g:].T.astype(BF16)
    g_m = slab[o_mi:o_mi + GATE_GROUP]
    row = lax.broadcasted_iota(jnp.int32, g_m.shape, 0)
    gates = jnp.concatenate(
        [jnp.where(row < H_M, g_m, 0.0),
         jnp.where(row < H_M, pltpu.roll(g_m, GATE_GROUP - H_M, axis=0), 0.0),
         slab[o_ff:o_ff + GATE_GROUP],
         jnp.zeros((LANES - 3 * GATE_GROUP, slab.shape[1]), F32)], axis=0)
    wd_o[...] = gates.T.astype(BF16)
    b = bin_ref[...]
    ba_o[...] = b[:, 0:o_mi]
    bb_o[...] = b[:, o_fq:o_ff]
    bc_o[...] = b[:, o_g:]
    bd_o[...] = gate_block(b)


def _prep_in_weights(p, vmem):
    d, n_in = p["w_in"].shape
    w_m = p["w_br_mlstm"].shape[0]
    w_f = p["w_br_fox"].shape[0]
    steps = 8
    o_ff = 4 * w_m + 2 * H_M + 3 * w_f
    assert (4 * w_m) % LANES == 0 and 2 * H_M <= GATE_GROUP + H_M <= LANES
    assert o_ff % LANES <= 2 * GATE_GROUP and o_ff % LANES + H_F <= LANES
    assert n_in == o_ff + H_F + 2 * d
    srcs = [p["w_in"].T, p["b_in"][None, :]]

    def row_spec(shape, tiled=True):
        if not tiled:
            return pl.BlockSpec(shape, lambda i: (0, 0))
        assert shape[0] % (steps * BF16_ROWS) == 0
        return pl.BlockSpec((shape[0] // steps, shape[1]), lambda i: (i, 0))

    out_shapes = [((d, 4 * w_m), BF16), ((d, 3 * w_f), BF16), ((d, 2 * d), BF16), ((d, LANES), BF16),
                  ((1, 4 * w_m), F32), ((1, 3 * w_f), F32), ((1, 2 * d), F32), ((1, LANES), F32)]
    return pl.pallas_call(
        functools.partial(_wprep_kernel, w_m=w_m, w_f=w_f),
        grid=(steps,),
        in_specs=([pl.BlockSpec((n_in, d // steps), lambda i: (0, i))]
                  + [row_spec(s.shape, tiled=s.shape[0] > 1) for s in srcs[1:]]),
        out_specs=[row_spec(s, tiled=s[0] > 1) for s, _ in out_shapes],
        out_shape=[jax.ShapeDtypeStruct(s, dt) for s, dt in out_shapes],
        compiler_params=pltpu.CompilerParams(dimension_semantics=("arbitrary",), vmem_limit_bytes=vmem),
        name="wprep",
    )(*srcs)


def _inproj_kernel(x_ref, g_ref, wa_ref, ba_ref, wb_ref, bb_ref, wc_ref, bc_ref, wd_ref, bd_ref,
                   cw_ref, cb_ref, *rest,
                   w_m, w_f, n_chunk, m_split, q_scale, k_scale, chunk, tiles_per_seq, n_cast):
    cast_srcs, rest = rest[:n_cast], rest[n_cast:]
    (zqk_ref, vt_ref, og_ref, zb_ref, zc_ref, zdt_ref), rest = rest[:6], rest[6:]
    cast_dsts, (zs_sc,) = rest[:n_cast], rest[n_cast:]
    tm = x_ref.shape[0]
    seq_start = (pl.program_id(0) % tiles_per_seq) == 0

    @pl.when(seq_start)
    def _():
        zs_sc[:, 0:SUBLANES, :] = jnp.zeros((zs_sc.shape[0], SUBLANES, n_chunk), F32)

    @pl.when(jnp.logical_not(seq_start))
    def _():
        zs_sc[:, 0:SUBLANES, :] = zs_sc[:, tm:tm + SUBLANES, :]

    rows = tm // m_split
    hb = [_rms(x_ref[r0:r0 + rows, :], g_ref[...]).astype(BF16) for r0 in range(0, tm, rows)]

    def proj(w_ref, b_ref, c0, c1):
        w = w_ref[:, c0:c1]
        parts = [jnp.dot(h, w, preferred_element_type=F32) for h in hb]
        return jnp.concatenate(parts, axis=0) + b_ref[:, c0:c1]

    def qk_chunk(ci):
        c0 = ci * n_chunk
        cs = slice(c0, c0 + n_chunk)
        zs_sc[ci, SUBLANES:tm + SUBLANES, :] = proj(wa_ref, ba_ref, c0, c0 + n_chunk)
        w = cw_ref[:, cs]
        y = cb_ref[:, cs]
        for s in range(CONV_K):
            y = y + zs_sc[ci, SUBLANES - s:SUBLANES - s + tm, :] * w[CONV_K - 1 - s:CONV_K - s]
        act = y * jax.nn.sigmoid(y)
        if c0 >= w_m:
            act = act * k_scale
        zqk_ref[:, cs] = act.astype(BF16)

    def v_chunk(i):
        c0 = 2 * w_m + i * n_chunk
        z = proj(wa_ref, ba_ref, c0, c0 + n_chunk)
        for cc in range(tm // chunk):
            for f0 in range(0, n_chunk, LANES):
                blk = z[cc * chunk:(cc + 1) * chunk, f0:f0 + LANES]
                r0 = i * n_chunk + f0
                vt_ref[cc, r0:r0 + LANES, :] = blk.T.astype(BF16)

    def og_chunk(i):
        c0 = 3 * w_m + i * n_chunk
        z = proj(wa_ref, ba_ref, c0, c0 + n_chunk)
        og_ref[:, i * n_chunk:(i + 1) * n_chunk] = jax.nn.sigmoid(z).astype(BF16)

    def b_chunk(i):
        c0 = i * n_chunk
        z = proj(wb_ref, bb_ref, c0, c0 + n_chunk)
        if c0 < w_f:
            z = z * q_scale
        zb_ref[:, c0:c0 + n_chunk] = z.astype(BF16)

    def c_chunk(i):
        c0 = i * n_chunk
        z = proj(wc_ref, bc_ref, c0, c0 + n_chunk)
        zc_ref[:, c0:c0 + n_chunk] = jax.nn.sigmoid(z).astype(BF16)

    def d_chunk(_):
        zd = proj(wd_ref, bd_ref, 0, LANES)
        zdt_ref[...] = zd.T[:N_GATE_ROWS, :]

    heavy = [(qk_chunk, i) for i in range(2 * w_m // n_chunk)]
    light = ([(c_chunk, i) for i in range(wc_ref.shape[1] // n_chunk)]
             + [(b_chunk, i) for i in range(3 * w_f // n_chunk)]
             + [(v_chunk, i) for i in range(w_m // n_chunk)]
             + [(og_chunk, i) for i in range(w_m // n_chunk)] + [(d_chunk, 0)])
    per_heavy = len(light) // len(heavy)
    order = []
    for hi, task in enumerate(heavy):
        order.append(task)
        order.extend(light[hi * per_heavy:(hi + 1) * per_heavy])
    order.extend(light[len(heavy) * per_heavy:])
    for fn, i in order:
        fn(i)
    for src, dst in zip(cast_srcs, cast_dsts):
        dst[...] = src[...].astype(BF16)


def _scan_lanes(x, op, fill, seg):
    pos = lax.rem(lax.broadcasted_iota(jnp.int32, x.shape, 1), seg)
    s = 1
    while s < seg:
        shifted = pltpu.roll(x, s, axis=1)
        x = op(x, jnp.where(pos >= s, shifted, fill))
        s *= 2
    return x


def _gates_kernel(zdt_ref, rows_ref, cols_ref, csplit_ref, *, seq):
    z = zdt_ref[...]
    width = z.shape[1]
    i8 = z[0:GATE_GROUP]
    cum = _scan_lanes(_log_sigmoid(z[GATE_GROUP:3 * GATE_GROUP]), jnp.add, 0.0, seq)
    f8 = cum[0:GATE_GROUP]
    cf8 = cum[GATE_GROUP:2 * GATE_GROUP] * LOG2E
    g8 = i8 - f8
    m8 = jnp.maximum(_scan_lanes(g8, jnp.maximum, NEG_BIG, seq), 0.0)
    en8 = jnp.exp(-(f8 + m8))
    g8, m8 = g8 * LOG2E, m8 * LOG2E
    rows = jnp.concatenate([g8, m8, en8], axis=0)
    for bb in range(width // seq):
        rows_ref[bb] = rows[:, bb * seq:(bb + 1) * seq]
    stack = jnp.concatenate(
        [g8, m8, jnp.zeros((LANES - 2 * GATE_GROUP, width), F32)], axis=0)
    cols_ref[...] = stack.T
    hi = cf8.astype(BF16).astype(F32)
    r1 = cf8 - hi
    lo = r1.astype(BF16).astype(F32)
    lo2 = (r1 - lo).astype(BF16).astype(F32)
    ones = jnp.where(lax.broadcasted_iota(jnp.int32, (GATE_GROUP, width), 0) == 0, 1.0, 0.0)
    split = jnp.concatenate(
        [hi, lo, lo2, ones, jnp.zeros((LANES - 4 * GATE_GROUP, width), F32)], axis=0)
    csplit_ref[...] = split.T.astype(BF16)


def _mlstm_kernel(zqk_ref, vt_ref, og_ref, rows_ref, cols_ref, ng_ref, ym_ref, ct_sc, *, w_m, chunk):
    seq = zqk_ref.shape[0]
    dh = w_m // H_M
    nc = seq // chunk
    ct_sc[...] = jnp.zeros(ct_sc.shape, F32)
    hc = chunk // 2
    causal = (lax.broadcasted_iota(jnp.int32, (hc, hc), 0)
              <= lax.broadcasted_iota(jnp.int32, (hc, hc), 1))
    ones_rows = jnp.where(
        lax.broadcasted_iota(jnp.int32, (BF16_ROWS, chunk), 0) == 0, 1.0, 0.0).astype(BF16)

    def body(c, carry):
        r0 = pl.multiple_of(c * chunk, chunk)
        colsc = cols_ref[pl.ds(r0, chunk), :]
        last = cols_ref[pl.ds(r0 + chunk - 1, 1), :]
        prev = cols_ref[pl.ds(jnp.maximum(r0 - 1, 0), 1), :]
        prev = jnp.where(c > 0, prev, 0.0)
        first, vas = [], []
        for h in range(H_M):
            qc = zqk_ref[pl.ds(r0, chunk), h * dh:(h + 1) * dh]
            kc = zqk_ref[pl.ds(r0, chunk), w_m + h * dh:w_m + (h + 1) * dh]
            lhs = jnp.concatenate([kc, ct_sc[h].astype(BF16)], axis=0)
            first.append(lax.dot_general(lhs, qc, NT_DIMS, preferred_element_type=F32))
            vas.append(jnp.concatenate([vt_ref[c, h * dh:(h + 1) * dh, :], ones_rows], axis=0))
        for h in range(H_M):
            kc = zqk_ref[pl.ds(r0, chunk), w_m + h * dh:w_m + (h + 1) * dh]
            g_row = rows_ref[0, h, pl.ds(c, 1), :]
            m_e = last[:, GATE_GROUP + h:GATE_GROUP + h + 1]
            m_p = prev[:, GATE_GROUP + h:GATE_GROUP + h + 1]
            vaw = (vas[h].astype(F32) * jnp.exp2(g_row - m_e)).astype(BF16)
            ct_sc[h] = jnp.exp2(m_p - m_e) * ct_sc[h] + jnp.dot(vaw, kc, preferred_element_type=F32)
        for h in range(H_M):
            ch = slice(h * dh, (h + 1) * dh)
            g_col = colsc[:, h:h + 1]
            m_row = rows_ref[0, GATE_GROUP + h, pl.ds(c, 1), :]
            en_row = rows_ref[0, 2 * GATE_GROUP + h, pl.ds(c, 1), :]
            m_p = prev[:, GATE_GROUP + h:GATE_GROUP + h + 1]
            sk = first[h]
            d00 = jnp.where(causal, jnp.exp2(g_col[0:hc] - m_row[:, 0:hc]), 0.0)
            d01 = jnp.exp2(g_col[0:hc] - m_row[:, hc:chunk])
            d11 = jnp.where(causal, jnp.exp2(g_col[hc:chunk] - m_row[:, hc:chunk]), 0.0)
            top = jnp.concatenate([sk[0:hc, 0:hc] * d00, sk[0:hc, hc:chunk] * d01], axis=1)
            bot = jnp.concatenate([jnp.zeros((hc, hc), F32), sk[hc:chunk, hc:chunk] * d11], axis=1)
            sqk = jnp.concatenate([top, bot], axis=0).astype(BF16)
            nd = (jnp.exp2(m_p - m_row) * first[h][chunk:]
                  + jnp.dot(vas[h], sqk, preferred_element_type=F32))
            den = nd[dh:dh + 1]
            ht = nd[0:dh] * (1.0 / jnp.maximum(jnp.abs(den), en_row))
            hn = ht * lax.rsqrt(jnp.mean(ht * ht, axis=0, keepdims=True) + EPS)
            og = og_ref[pl.ds(r0, chunk), ch].astype(F32)
            ym_ref[pl.ds(r0, chunk), ch] = (hn.T * ng_ref[:, ch] * og).astype(BF16)
        return carry

    lax.fori_loop(0, nc, body, 0, unroll=True)


def _fox_kernel(q_ref, k_ref, v_ref, csq_ref, csk_ref, qsel_ref, ksel_ref, yf_ref,
                kaug_sc, vt_sc, qaug_sc, m_sc, acc_sc, *, tq, tk):
    qi = pl.program_id(1)

    seq = k_ref.shape[0]
    dhp = LANES
    half = dhp // 2
    v_rows = half + BF16_ROWS

    def own_lanes(head, rows):
        lane = lax.broadcasted_iota(jnp.int32, (rows, dhp), 1)
        return (lane >= half) if head % 2 else (lane < half)

    @pl.when(qi == 0)
    def _():
        csk = csk_ref[...]
        ones_rows = jnp.where(
            lax.broadcasted_iota(jnp.int32, (BF16_ROWS, tk), 0) == 0, 1.0, 0.0).astype(BF16)
        for p in range(H_F // 2):
            cp = slice(p * dhp, (p + 1) * dhp)
            kp = k_ref[:, cp]
            vt = v_ref[:, cp].astype(F32).T
            for head in (2 * p, 2 * p + 1):
                kb = jnp.dot(csk, ksel_ref[head], preferred_element_type=F32).astype(BF16)
                kaug_sc[head] = jnp.where(own_lanes(head, seq), kp, kb)
                r0 = (head % 2) * half
                for jb in range(seq // tk):
                    vt_sc[head, jb, 0:half, :] = vt[r0:r0 + half, jb * tk:(jb + 1) * tk].astype(BF16)
                    vt_sc[head, jb, half:v_rows, :] = ones_rows

    csq = csq_ref[...]
    for head in range(H_F):
        qp = q_ref[:, (head // 2) * dhp:(head // 2 + 1) * dhp]
        qb = jnp.dot(csq, qsel_ref[head], preferred_element_type=F32).astype(BF16)
        qaug_sc[head] = jnp.where(own_lanes(head, tq), qp, qb)
    m_sc[...] = jnp.full(m_sc.shape, NEG_BIG, F32)
    acc_sc[...] = jnp.zeros(acc_sc.shape, F32)

    ratio = tq // tk

    def step(j, diag):
        q0 = 0 if diag is None else diag * tk
        nq = tq - q0
        k0 = pl.multiple_of(j * tk, tk)
        sts = [lax.dot_general(kaug_sc[head, pl.ds(k0, tk), :], qaug_sc[head, q0:tq, :], NT_DIMS,
                               preferred_element_type=F32) for head in range(H_F)]
        if diag is not None:
            causal = (lax.broadcasted_iota(jnp.int32, (tk, nq), 0)
                      <= lax.broadcasted_iota(jnp.int32, (tk, nq), 1))
        for head in range(H_F):
            st = sts[head]
            if diag is not None:
                st = jnp.where(causal, st, NEG_BIG)
            m = m_sc[head, :, q0:tq]
            m_new = jnp.maximum(m, jnp.max(st, axis=0, keepdims=True))
            alpha = jnp.exp2(m - m_new)
            pt = jnp.exp2(st - m_new).astype(BF16)
            m_sc[head, :, q0:tq] = m_new
            acc_sc[head, :, q0:tq] = (alpha * acc_sc[head, :, q0:tq]
                                      + jnp.dot(vt_sc[head, j], pt, preferred_element_type=F32))

    def loop_body(jj, carry):
        for r in range(ratio):
            step(jj * ratio + r, None)
        return carry

    lax.fori_loop(0, qi, loop_body, 0)
    for diag in range(ratio):
        step(qi * ratio + diag, diag)
    for p in range(H_F // 2):
        outs = []
        for head in (2 * p, 2 * p + 1):
            acc = acc_sc[head]
            outs.append(acc[0:half] * (1.0 / acc[half:half + 1]))
        yf_ref[:, p * dhp:(p + 1) * dhp] = jnp.concatenate(outs, axis=0).T.astype(BF16)


def _post_kernel(x_ref, ym_ref, yf_ref, zc_ref, wbm_ref, wbf_ref, wo_ref, g2_ref,
                 wg_ref, wu_ref, wd_ref, gfin_ref, o_ref, *, tf):
    d = x_ref.shape[1]
    d_ff = wg_ref.shape[1]
    bm = jnp.dot(ym_ref[...], wbm_ref[...], preferred_element_type=F32)
    bf = jnp.dot(yf_ref[...], wbf_ref[...], preferred_element_type=F32)
    mix = zc_ref[:, 0:d].astype(F32) * bm + zc_ref[:, d:2 * d].astype(F32) * bf
    x1 = x_ref[...] + jnp.dot(mix.astype(BF16), wo_ref[...], preferred_element_type=F32)
    h2 = _rms(x1, g2_ref[...]).astype(BF16)
    acc = jnp.zeros(x1.shape, F32)
    for f0 in range(0, d_ff, tf):
        g = jnp.dot(h2, wg_ref[:, f0:f0 + tf], preferred_element_type=F32)
        u = jnp.dot(h2, wu_ref[:, f0:f0 + tf], preferred_element_type=F32)
        act = (g * jax.nn.sigmoid(g) * u).astype(BF16)
        acc = acc + jnp.dot(act, wd_ref[f0:f0 + tf, :], preferred_element_type=F32)
    o_ref[...] = _rms(x1 + acc, gfin_ref[...])


def _bias_selectors():
    ones_lane = 3 * GATE_GROUP
    qsel = np.zeros((H_F, LANES, LANES), np.float32)
    ksel = np.zeros((H_F, LANES, LANES), np.float32)
    for h in range(H_F):
        p0 = LANES // 2 if h % 2 == 0 else 0
        for c in range(3):
            qsel[h, GATE_GROUP * c + h, p0 + c] = 1.0
            qsel[h, ones_lane, p0 + 3 + c] = 1.0
            ksel[h, ones_lane, p0 + c] = 1.0
            ksel[h, GATE_GROUP * c + h, p0 + 3 + c] = -1.0
    return jnp.asarray(qsel, BF16), jnp.asarray(ksel, BF16)


def _layer(x2d, batch, seq, p, cfg):
    t, d = x2d.shape
    w_m = p["w_br_mlstm"].shape[0]
    w_f = p["w_br_fox"].shape[0]
    dh_m = w_m // H_M
    dh_f = w_f // H_F
    vmem = cfg["vmem_limit"]

    wa, wb, wc, wd, ba, bb, bc, bd = _prep_in_weights(p, vmem)

    tm = cfg["tm_in"]
    chunk = cfg["chunk"]
    nc = seq // chunk
    n_steps = t // tm
    assert seq % tm == 0 and tm % chunk == 0

    later = [p["w_br_mlstm"], p["w_br_fox"], p["w_out"], p["w_gate"], p["w_up"], p["w_down"]]

    def slice_spec(rows, cols):
        hold = 1
        while (rows * hold) % (n_steps * BF16_ROWS):
            hold *= 2
        assert hold <= n_steps
        return pl.BlockSpec((rows * hold // n_steps, cols), lambda i: (i // hold, 0))

    later_specs = [slice_spec(*w.shape) for w in later]
    outs = pl.pallas_call(
        functools.partial(_inproj_kernel, w_m=w_m, w_f=w_f, n_chunk=cfg["n_chunk"],
                          m_split=cfg["m_split"],
                          q_scale=dh_f ** -0.5 * LOG2E, k_scale=dh_m ** -0.5, chunk=chunk,
                          tiles_per_seq=seq // tm, n_cast=len(later)),
        grid=(n_steps,),
        in_specs=[pl.BlockSpec((tm, d), lambda i: (i, 0)), _const_spec((1, d)),
                  _const_spec(wa.shape), _const_spec(ba.shape),
                  _const_spec(wb.shape), _const_spec(bb.shape),
                  _const_spec(wc.shape), _const_spec(bc.shape),
                  _const_spec(wd.shape), _const_spec(bd.shape),
                  _const_spec((CONV_K, 2 * w_m)), _const_spec((1, 2 * w_m))] + later_specs,
        out_specs=[pl.BlockSpec((tm, 2 * w_m), lambda i: (i, 0)),
                   pl.BlockSpec((tm // chunk, w_m, chunk), lambda i: (i, 0, 0)),
                   pl.BlockSpec((tm, w_m), lambda i: (i, 0)),
                   pl.BlockSpec((tm, 3 * w_f), lambda i: (i, 0)),
                   pl.BlockSpec((tm, 2 * d), lambda i: (i, 0)),
                   pl.BlockSpec((N_GATE_ROWS, tm), lambda i: (0, i))] + later_specs,
        out_shape=[jax.ShapeDtypeStruct((t, 2 * w_m), BF16),
                   jax.ShapeDtypeStruct((t // chunk, w_m, chunk), BF16),
                   jax.ShapeDtypeStruct((t, w_m), BF16),
                   jax.ShapeDtypeStruct((t, 3 * w_f), BF16),
                   jax.ShapeDtypeStruct((t, 2 * d), BF16),
                   jax.ShapeDtypeStruct((N_GATE_ROWS, t), F32)]
        + [jax.ShapeDtypeStruct(w.shape, BF16) for w in later],
        scratch_shapes=[pltpu.VMEM((2 * w_m // cfg["n_chunk"], tm + SUBLANES, cfg["n_chunk"]), F32)],
        compiler_params=pltpu.CompilerParams(dimension_semantics=("arbitrary",), vmem_limit_bytes=vmem),
        name="inproj",
    )(x2d, p["norm1_g"][None, :], wa, ba, wb, bb, wc, bc, wd, bd, p["conv_w"], p["conv_b"][None, :],
      *later)
    zqk, vt, og, zb, zc, zdt, wbm, wbf, wo, wg, wu, wdn = outs

    nb = cfg["gates_batch"]
    assert batch % nb == 0
    rows, cols, csplit = pl.pallas_call(
        functools.partial(_gates_kernel, seq=seq),
        grid=(batch // nb,),
        in_specs=[pl.BlockSpec((N_GATE_ROWS, nb * seq), lambda b: (0, b))],
        out_specs=[pl.BlockSpec((nb, 3 * GATE_GROUP, seq), lambda b: (b, 0, 0)),
                   pl.BlockSpec((nb * seq, LANES), lambda b: (b, 0)),
                   pl.BlockSpec((nb * seq, LANES), lambda b: (b, 0))],
        out_shape=[jax.ShapeDtypeStruct((batch, 3 * GATE_GROUP, seq), F32),
                   jax.ShapeDtypeStruct((t, LANES), F32),
                   jax.ShapeDtypeStruct((t, LANES), BF16)],
        compiler_params=pltpu.CompilerParams(dimension_semantics=("parallel",), vmem_limit_bytes=vmem),
        name="gates",
    )(zdt)

    rows_m = rows.reshape(batch, 3 * GATE_GROUP, nc, chunk)
    ym = pl.pallas_call(
        functools.partial(_mlstm_kernel, w_m=w_m, chunk=chunk),
        grid=(batch,),
        in_specs=[pl.BlockSpec((seq, 2 * w_m), lambda b: (b, 0)),
                  pl.BlockSpec((nc, w_m, chunk), lambda b: (b, 0, 0)),
                  pl.BlockSpec((seq, w_m), lambda b: (b, 0)),
                  pl.BlockSpec((1, 3 * GATE_GROUP, nc, chunk), lambda b: (b, 0, 0, 0)),
                  pl.BlockSpec((seq, LANES), lambda b: (b, 0)),
                  _const_spec((1, w_m))],
        out_specs=pl.BlockSpec((seq, w_m), lambda b: (b, 0)),
        out_shape=jax.ShapeDtypeStruct((t, w_m), BF16),
        scratch_shapes=[pltpu.VMEM((H_M, dh_m + BF16_ROWS, dh_m), F32)],
        compiler_params=pltpu.CompilerParams(dimension_semantics=("parallel",), vmem_limit_bytes=vmem),
        name="mlstm",
    )(zqk, vt, og, rows_m, cols, p["mlstm_norm_g"][None, :])

    tq, tk = cfg["tq"], cfg["tk"]
    assert tq % tk == 0 and seq % tq == 0
    nq = seq // tq
    qsel, ksel = _bias_selectors()
    v_rows = LANES // 2 + BF16_ROWS
    yf = pl.pallas_call(
        functools.partial(_fox_kernel, tq=tq, tk=tk),
        grid=(batch, nq),
        in_specs=[pl.BlockSpec((tq, w_f), lambda b, i: (b * nq + i, 0)),
                  pl.BlockSpec((seq, w_f), lambda b, i: (b, 1)),
                  pl.BlockSpec((seq, w_f), lambda b, i: (b, 2)),
                  pl.BlockSpec((tq, LANES), lambda b, i: (b * nq + i, 0)),
                  pl.BlockSpec((seq, LANES), lambda b, i: (b, 0)),
                  _const_spec(qsel.shape), _const_spec(ksel.shape)],
        out_specs=pl.BlockSpec((tq, w_f), lambda b, i: (b * nq + i, 0)),
        out_shape=jax.ShapeDtypeStruct((t, w_f), BF16),
        scratch_shapes=[pltpu.VMEM((H_F, seq, LANES), BF16),
                        pltpu.VMEM((H_F, seq // tk, v_rows, tk), BF16),
                        pltpu.VMEM((H_F, tq, LANES), BF16),
                        pltpu.VMEM((H_F, 1, tq), F32),
                        pltpu.VMEM((H_F, v_rows, tq), F32)],
        compiler_params=pltpu.CompilerParams(dimension_semantics=("arbitrary", "arbitrary"),
                                             vmem_limit_bytes=vmem),
        name="fox",
    )(zb, zb, zb, csplit, csplit, qsel, ksel)

    tmp = cfg["tm_post"]
    return pl.pallas_call(
        functools.partial(_post_kernel, tf=cfg["tf"]),
        grid=(t // tmp,),
        in_specs=[pl.BlockSpec((tmp, d), lambda i: (i, 0)),
                  pl.BlockSpec((tmp, w_m), lambda i: (i, 0)),
                  pl.BlockSpec((tmp, w_f), lambda i: (i, 0)),
                  pl.BlockSpec((tmp, 2 * d), lambda i: (i, 0)),
                  _const_spec(wbm.shape), _const_spec(wbf.shape), _const_spec(wo.shape),
                  _const_spec((1, d)), _const_spec(wg.shape), _const_spec(wu.shape),
                  _const_spec(wdn.shape), _const_spec((1, d))],
        out_specs=pl.BlockSpec((tmp, d), lambda i: (i, 0)),
        out_shape=jax.ShapeDtypeStruct((t, d), F32),
        compiler_params=pltpu.CompilerParams(dimension_semantics=("parallel",), vmem_limit_bytes=vmem),
        name="post",
    )(x2d, ym, yf, zc, wbm, wbf, wo, p["norm2_g"][None, :], wg, wu, wdn, p["norm_f_g"][None, :])


def kernel(x, norm1_g, w_in, b_in, conv_w, conv_b, mlstm_norm_g, w_br_mlstm, w_br_fox, w_out,
           norm2_g, w_gate, w_up, w_down, norm_f_g):
    batch, seq, d = x.shape
    depth = w_in.shape[0]
    assert depth == 1, "the final norm is fused into the single layer's last call"
    cfg = _cfg(batch, seq, d, w_gate.shape[-1])
    p = dict(norm1_g=norm1_g[0], w_in=w_in[0], b_in=b_in[0], conv_w=conv_w[0], conv_b=conv_b[0],
             mlstm_norm_g=mlstm_norm_g[0], w_br_mlstm=w_br_mlstm[0], w_br_fox=w_br_fox[0],
             w_out=w_out[0], norm2_g=norm2_g[0], w_gate=w_gate[0], w_up=w_up[0], w_down=w_down[0],
             norm_f_g=norm_f_g)
    out = _layer(x.reshape(batch * seq, d), batch, seq, p, cfg)
    return out.reshape(batch, seq, d)
```

```python
import functools

import jax
import jax.numpy as jnp
import numpy as np
from jax import lax
from jax.experimental import pallas as pl
from jax.experimental.pallas import tpu as pltpu

EPS = 1e-6
H_M = 4
H_F = 8
CONV_K = 4

LANES = 128
SUBLANES = 8
BF16_ROWS = 16
GATE_GROUP = 8
N_GATE_ROWS = 3 * GATE_GROUP
V7X_VMEM_BYTES = 64 * 1024 * 1024
NEG_BIG = -1e30
LOG2E = 1.4426950408889634

F32 = jnp.float32
BF16 = jnp.bfloat16
NT_DIMS = (((1,), (1,)), ((), ()))


def _cfg(batch, seq, d_model, d_ff):
    return dict(
        tm_in=1024,
        n_chunk=256,
        m_split=8,
        gates_batch=4,
        chunk=256,
        tq=512,
        tk=256,
        tm_post=512,
        tf=256,
        vmem_limit=V7X_VMEM_BYTES - 8 * 1024 * 1024,
    )


def _const_spec(shape):
    nd = len(shape)
    return pl.BlockSpec(shape, lambda *_: (0,) * nd, pipeline_mode=pl.Buffered(1))


def _rms(x, g):
    return x * lax.rsqrt(jnp.mean(x * x, axis=-1, keepdims=True) + EPS) * g


def _log_sigmoid(x):
    return jnp.minimum(x, 0.0) - jnp.log1p(jnp.exp(-jnp.abs(x)))


def _wprep_kernel(wint_ref, bin_ref, wa_o, wb_o, wc_o, wd_o, ba_o, bb_o, bc_o, bd_o, *, w_m, w_f):
    o_mi = 4 * w_m
    o_mf = o_mi + H_M
    o_fq = o_mf + H_M
    o_ff = o_fq + 3 * w_f
    o_g = o_ff + H_F

    def gate_block(src):
        rows = src.shape[0]
        lane = lax.broadcasted_iota(jnp.int32, (rows, LANES), 1)
        blk_m = src[:, o_mi:o_mi + LANES]
        f0 = (o_ff // LANES) * LANES
        blk_f = src[:, f0:f0 + LANES]
        mi = jnp.where(lane < H_M, blk_m, 0.0)
        mf = jnp.where((lane >= GATE_GROUP) & (lane < GATE_GROUP + H_M),
                       pltpu.roll(blk_m, GATE_GROUP - H_M, axis=1), 0.0)
        ff = jnp.where((lane >= 2 * GATE_GROUP) & (lane < 2 * GATE_GROUP + H_F),
                       pltpu.roll(blk_f, 2 * GATE_GROUP - (o_ff - f0), axis=1), 0.0)
        return mi + mf + ff

    slab = wint_ref[...]
    wa_o[...] = slab[0:o_mi].T.astype(BF16)
    wb_o[...] = slab[o_fq:o_ff].T.astype(BF16)
    wc_o[...] = slab[o_g:].T.astype(BF16)
    g_m = slab[o_mi:o_mi + GATE_GROUP]
    row = lax.broadcasted_iota(jnp.int32, g_m.shape, 0)
    gates = jnp.concatenate(
        [jnp.where(row < H_M, g_m, 0.0),
         jnp.where(row < H_M, pltpu.roll(g_m, GATE_GROUP - H_M, axis=0), 0.0),
         slab[o_ff:o_ff + GATE_GROUP],
         jnp.zeros((LANES - 3 * GATE_GROUP, slab.shape[1]), F32)], axis=0)
    wd_o[...] = gates.T.astype(BF16)
    b = bin_ref[...]
    ba_o[...] = b[:, 0:o_mi]
    bb_o[...] = b[:, o_fq:o_ff]
    bc_o[...] = b[:, o_g:]
    bd_o[...] = gate_block(b)


def _prep_in_weights(p, vmem):
    d, n_in = p["w_in"].shape
    w_m = p["w_br_mlstm"].shape[0]
    w_f = p["w_br_fox"].shape[0]
    steps = 8
    o_ff = 4 * w_m + 2 * H_M + 3 * w_f
    assert (4 * w_m) % LANES == 0 and 2 * H_M <= GATE_GROUP + H_M <= LANES
    assert o_ff % LANES <= 2 * GATE_GROUP and o_ff % LANES + H_F <= LANES
    assert n_in == o_ff + H_F + 2 * d
    srcs = [p["w_in"].T, p["b_in"][None, :]]

    def row_spec(shape, tiled=True):
        if not tiled:
            return pl.BlockSpec(shape, lambda i: (0, 0))
        assert shape[0] % (steps * BF16_ROWS) == 0
        return pl.BlockSpec((shape[0] // steps, shape[1]), lambda i: (i, 0))

    out_shapes = [((d, 4 * w_m), BF16), ((d, 3 * w_f), BF16), ((d, 2 * d), BF16), ((d, LANES), BF16),
                  ((1, 4 * w_m), F32), ((1, 3 * w_f), F32), ((1, 2 * d), F32), ((1, LANES), F32)]
    return pl.pallas_call(
        functools.partial(_wprep_kernel, w_m=w_m, w_f=w_f),
        grid=(steps,),
        in_specs=([pl.BlockSpec((n_in, d // steps), lambda i: (0, i))]
                  + [row_spec(s.shape, tiled=s.shape[0] > 1) for s in srcs[1:]]),
        out_specs=[row_spec(s, tiled=s[0] > 1) for s, _ in out_shapes],
        out_shape=[jax.ShapeDtypeStruct(s, dt) for s, dt in out_shapes],
        compiler_params=pltpu.CompilerParams(dimension_semantics=("arbitrary",), vmem_limit_bytes=vmem),
        name="wprep",
    )(*srcs)


def _inproj_kernel(x_ref, g_ref, wa_ref, ba_ref, wb_ref, bb_ref, wc_ref, bc_ref, wd_ref, bd_ref,
                   cw_ref, cb_ref, *rest,
                   w_m, w_f, n_chunk, m_split, q_scale, k_scale, chunk, tk, tiles_per_seq, n_cast):
    cast_srcs, rest = rest[:n_cast], rest[n_cast:]
    (zqk_ref, vt_ref, og_ref, zb_ref, fvt_ref, zc_ref, zdt_ref), rest = rest[:7], rest[7:]
    cast_dsts, (zs_sc,) = rest[:n_cast], rest[n_cast:]
    tm = x_ref.shape[0]
    seq_start = (pl.program_id(0) % tiles_per_seq) == 0

    @pl.when(seq_start)
    def _():
        zs_sc[:, 0:SUBLANES, :] = jnp.zeros((zs_sc.shape[0], SUBLANES, n_chunk), F32)

    @pl.when(jnp.logical_not(seq_start))
    def _():
        zs_sc[:, 0:SUBLANES, :] = zs_sc[:, tm:tm + SUBLANES, :]

    rows = tm // m_split
    hb = [_rms(x_ref[r0:r0 + rows, :], g_ref[...]).astype(BF16) for r0 in range(0, tm, rows)]

    def proj(w_ref, b_ref, c0, c1):
        w = w_ref[:, c0:c1]
        parts = [jnp.dot(h, w, preferred_element_type=F32) for h in hb]
        return jnp.concatenate(parts, axis=0) + b_ref[:, c0:c1]

    def qk_chunk(ci):
        c0 = ci * n_chunk
        cs = slice(c0, c0 + n_chunk)
        zs_sc[ci, SUBLANES:tm + SUBLANES, :] = proj(wa_ref, ba_ref, c0, c0 + n_chunk)
        w = cw_ref[:, cs]
        y = cb_ref[:, cs]
        for s in range(CONV_K):
            y = y + zs_sc[ci, SUBLANES - s:SUBLANES - s + tm, :] * w[CONV_K - 1 - s:CONV_K - s]
        act = y * jax.nn.sigmoid(y)
        if c0 >= w_m:
            act = act * k_scale
        zqk_ref[:, cs] = act.astype(BF16)

    def v_chunk(i):
        c0 = 2 * w_m + i * n_chunk
        z = proj(wa_ref, ba_ref, c0, c0 + n_chunk)
        for cc in range(tm // chunk):
            for f0 in range(0, n_chunk, LANES):
                blk = z[cc * chunk:(cc + 1) * chunk, f0:f0 + LANES]
                r0 = i * n_chunk + f0
                vt_ref[cc, r0:r0 + LANES, :] = blk.T.astype(BF16)

    def og_chunk(i):
        c0 = 3 * w_m + i * n_chunk
        z = proj(wa_ref, ba_ref, c0, c0 + n_chunk)
        og_ref[:, i * n_chunk:(i + 1) * n_chunk] = jax.nn.sigmoid(z).astype(BF16)

    def b_chunk(i):
        c0 = i * n_chunk
        z = proj(wb_ref, bb_ref, c0, c0 + n_chunk)
        if c0 < w_f:
            z = z * q_scale
        if c0 < 2 * w_f:
            zb_ref[:, c0:c0 + n_chunk] = z.astype(BF16)
        else:
            for jb in range(tm // tk):
                for f0 in range(0, n_chunk, LANES):
                    blk = z[jb * tk:(jb + 1) * tk, f0:f0 + LANES]
                    r0 = c0 - 2 * w_f + f0
                    fvt_ref[jb, r0:r0 + LANES, :] = blk.T.astype(BF16)

    def c_chunk(i):
        c0 = i * n_chunk
        z = proj(wc_ref, bc_ref, c0, c0 + n_chunk)
        zc_ref[:, c0:c0 + n_chunk] = jax.nn.sigmoid(z).astype(BF16)

    def d_chunk(_):
        zd = proj(wd_ref, bd_ref, 0, LANES)
        zdt_ref[...] = zd.T[:N_GATE_ROWS, :]

    heavy = [(qk_chunk, i) for i in range(2 * w_m // n_chunk)]
    light = ([(c_chunk, i) for i in range(wc_ref.shape[1] // n_chunk)]
             + [(b_chunk, i) for i in range(3 * w_f // n_chunk)]
             + [(v_chunk, i) for i in range(w_m // n_chunk)]
             + [(og_chunk, i) for i in range(w_m // n_chunk)] + [(d_chunk, 0)])
    per_heavy = len(light) // len(heavy)
    order = []
    for hi, task in enumerate(heavy):
        order.append(task)
        order.extend(light[hi * per_heavy:(hi + 1) * per_heavy])
    order.extend(light[len(heavy) * per_heavy:])
    for fn, i in order:
        fn(i)
    for src, dst in zip(cast_srcs, cast_dsts):
        dst[...] = src[...].astype(BF16)


def _scan_lanes(x, op, fill, seg):
    pos = lax.rem(lax.broadcasted_iota(jnp.int32, x.shape, 1), seg)
    s = 1
    while s < seg:
        shifted = pltpu.roll(x, s, axis=1)
        x = op(x, jnp.where(pos >= s, shifted, fill))
        s *= 2
    return x


def _gates_kernel(zdt_ref, rows_ref, cols_ref, csplit_ref, *, seq):
    z = zdt_ref[...]
    width = z.shape[1]
    i8 = z[0:GATE_GROUP]
    cum = _scan_lanes(_log_sigmoid(z[GATE_GROUP:3 * GATE_GROUP]), jnp.add, 0.0, seq)
    f8 = cum[0:GATE_GROUP]
    cf8 = cum[GATE_GROUP:2 * GATE_GROUP] * LOG2E
    g8 = i8 - f8
    m8 = jnp.maximum(_scan_lanes(g8, jnp.maximum, NEG_BIG, seq), 0.0)
    en8 = jnp.exp(-(f8 + m8))
    g8, m8 = g8 * LOG2E, m8 * LOG2E
    rows = jnp.concatenate([g8, m8, en8], axis=0)
    for bb in range(width // seq):
        rows_ref[bb] = rows[:, bb * seq:(bb + 1) * seq]
    stack = jnp.concatenate(
        [g8, m8, jnp.zeros((LANES - 2 * GATE_GROUP, width), F32)], axis=0)
    cols_ref[...] = stack.T
    hi = cf8.astype(BF16).astype(F32)
    r1 = cf8 - hi
    lo = r1.astype(BF16).astype(F32)
    lo2 = (r1 - lo).astype(BF16).astype(F32)
    ones = jnp.where(lax.broadcasted_iota(jnp.int32, (GATE_GROUP, width), 0) == 0, 1.0, 0.0)
    split = jnp.concatenate(
        [hi, lo, lo2, ones, jnp.zeros((LANES - 4 * GATE_GROUP, width), F32)], axis=0)
    csplit_ref[...] = split.T.astype(BF16)


def _mlstm_kernel(zqk_ref, vt_ref, og_ref, rows_ref, cols_ref, ng_ref, ym_ref, ct_sc, *, w_m, chunk):
    seq = zqk_ref.shape[0]
    dh = w_m // H_M
    nc = seq // chunk
    ct_sc[...] = jnp.zeros(ct_sc.shape, F32)
    hc = chunk // 2
    causal = (lax.broadcasted_iota(jnp.int32, (hc, hc), 0)
              <= lax.broadcasted_iota(jnp.int32, (hc, hc), 1))
    ones_rows = jnp.where(
        lax.broadcasted_iota(jnp.int32, (BF16_ROWS, chunk), 0) == 0, 1.0, 0.0).astype(BF16)

    def body(c, carry):
        r0 = pl.multiple_of(c * chunk, chunk)
        colsc = cols_ref[pl.ds(r0, chunk), :]
        last = cols_ref[pl.ds(r0 + chunk - 1, 1), :]
        prev = cols_ref[pl.ds(jnp.maximum(r0 - 1, 0), 1), :]
        prev = jnp.where(c > 0, prev, 0.0)
        first, vas = [], []
        for h in range(H_M):
            qc = zqk_ref[pl.ds(r0, chunk), h * dh:(h + 1) * dh]
            kc = zqk_ref[pl.ds(r0, chunk), w_m + h * dh:w_m + (h + 1) * dh]
            lhs = jnp.concatenate([kc, ct_sc[h].astype(BF16)], axis=0)
            first.append(lax.dot_general(lhs, qc, NT_DIMS, preferred_element_type=F32))
            vas.append(jnp.concatenate([vt_ref[c, h * dh:(h + 1) * dh, :], ones_rows], axis=0))
        for h in range(H_M):
            kc = zqk_ref[pl.ds(r0, chunk), w_m + h * dh:w_m + (h + 1) * dh]
            g_row = rows_ref[0, h, pl.ds(c, 1), :]
            m_e = last[:, GATE_GROUP + h:GATE_GROUP + h + 1]
            m_p = prev[:, GATE_GROUP + h:GATE_GROUP + h + 1]
            vaw = (vas[h].astype(F32) * jnp.exp2(g_row - m_e)).astype(BF16)
            ct_sc[h] = jnp.exp2(m_p - m_e) * ct_sc[h] + jnp.dot(vaw, kc, preferred_element_type=F32)
        for h in range(H_M):
            ch = slice(h * dh, (h + 1) * dh)
            g_col = colsc[:, h:h + 1]
            m_row = rows_ref[0, GATE_GROUP + h, pl.ds(c, 1), :]
            en_row = rows_ref[0, 2 * GATE_GROUP + h, pl.ds(c, 1), :]
            m_p = prev[:, GATE_GROUP + h:GATE_GROUP + h + 1]
            sk = first[h]
            d00 = jnp.where(causal, jnp.exp2(g_col[0:hc] - m_row[:, 0:hc]), 0.0)
            d01 = jnp.exp2(g_col[0:hc] - m_row[:, hc:chunk])
            d11 = jnp.where(causal, jnp.exp2(g_col[hc:chunk] - m_row[:, hc:chunk]), 0.0)
            top = jnp.concatenate([sk[0:hc, 0:hc] * d00, sk[0:hc, hc:chunk] * d01], axis=1)
            bot = jnp.concatenate([jnp.zeros((hc, hc), F32), sk[hc:chunk, hc:chunk] * d11], axis=1)
            sqk = jnp.concatenate([top, bot], axis=0).astype(BF16)
            nd = (jnp.exp2(m_p - m_row) * first[h][chunk:]
                  + jnp.dot(vas[h], sqk, preferred_element_type=F32))
            den = nd[dh:dh + 1]
            ht = nd[0:dh] * (1.0 / jnp.maximum(jnp.abs(den), en_row))
            hn = ht * lax.rsqrt(jnp.mean(ht * ht, axis=0, keepdims=True) + EPS)
            og = og_ref[pl.ds(r0, chunk), ch].astype(F32)
            ym_ref[pl.ds(r0, chunk), ch] = (hn.T * ng_ref[:, ch] * og).astype(BF16)
        return carry

    lax.fori_loop(0, nc, body, 0, unroll=True)


def _fox_kernel(q_ref, k_ref, fvt_ref, csq_ref, csk_ref, qsel_ref, ksel_ref, yf_ref,
                kaug_sc, qaug_sc, m_sc, acc_sc, *, tq, tk):
    qi = pl.program_id(1)

    seq = k_ref.shape[0]
    dhp = LANES
    half = dhp // 2
    v_rows = half + BF16_ROWS

    def own_lanes(head, rows):
        lane = lax.broadcasted_iota(jnp.int32, (rows, dhp), 1)
        return (lane >= half) if head % 2 else (lane < half)

    @pl.when(qi == 0)
    def _():
        csk = csk_ref[...]
        for head in range(H_F):
            kp = k_ref[:, (head // 2) * dhp:(head // 2 + 1) * dhp]
            kb = jnp.dot(csk, ksel_ref[head], preferred_element_type=F32).astype(BF16)
            kaug_sc[head] = jnp.where(own_lanes(head, seq), kp, kb)

    csq = csq_ref[...]
    for head in range(H_F):
        qp = q_ref[:, (head // 2) * dhp:(head // 2 + 1) * dhp]
        qb = jnp.dot(csq, qsel_ref[head], preferred_element_type=F32).astype(BF16)
        qaug_sc[head] = jnp.where(own_lanes(head, tq), qp, qb)
    m_sc[...] = jnp.full(m_sc.shape, NEG_BIG, F32)
    acc_sc[...] = jnp.zeros(acc_sc.shape, F32)

    ratio = tq // tk
    ones_rows = jnp.where(
        lax.broadcasted_iota(jnp.int32, (BF16_ROWS, tk), 0) == 0, 1.0, 0.0).astype(BF16)

    def vaug(head, j):
        return jnp.concatenate([fvt_ref[j, head * half:(head + 1) * half, :], ones_rows], axis=0)

    def step(j, diag):
        q0 = 0 if diag is None else diag * tk
        nq = tq - q0
        k0 = pl.multiple_of(j * tk, tk)
        sts = [lax.dot_general(kaug_sc[head, pl.ds(k0, tk), :], qaug_sc[head, q0:tq, :], NT_DIMS,
                               preferred_element_type=F32) for head in range(H_F)]
        if diag is not None:
            causal = (lax.broadcasted_iota(jnp.int32, (tk, nq), 0)
                      <= lax.broadcasted_iota(jnp.int32, (tk, nq), 1))
        for head in range(H_F):
            st = sts[head]
            if diag is not None:
                st = jnp.where(causal, st, NEG_BIG)
            m = m_sc[head, :, q0:tq]
            m_new = jnp.maximum(m, jnp.max(st, axis=0, keepdims=True))
            alpha = jnp.exp2(m - m_new)
            pt = jnp.exp2(st - m_new).astype(BF16)
            m_sc[head, :, q0:tq] = m_new
            acc_sc[head, :, q0:tq] = (alpha * acc_sc[head, :, q0:tq]
                                      + jnp.dot(vaug(head, j), pt, preferred_element_type=F32))

    def loop_body(jj, carry):
        for r in range(ratio):
            step(jj * ratio + r, None)
        return carry

    lax.fori_loop(0, qi, loop_body, 0)
    for diag in range(ratio):
        step(qi * ratio + diag, diag)
    for p in range(H_F // 2):
        outs = []
        for head in (2 * p, 2 * p + 1):
            acc = acc_sc[head]
            outs.append(acc[0:half] * (1.0 / acc[half:half + 1]))
        yf_ref[:, p * dhp:(p + 1) * dhp] = jnp.concatenate(outs, axis=0).T.astype(BF16)


def _post_kernel(x_ref, ym_ref, yf_ref, zc_ref, wbm_ref, wbf_ref, wo_ref, g2_ref,
                 wg_ref, wu_ref, wd_ref, gfin_ref, o_ref, *, tf):
    d = x_ref.shape[1]
    d_ff = wg_ref.shape[1]
    bm = jnp.dot(ym_ref[...], wbm_ref[...], preferred_element_type=F32)
    bf = jnp.dot(yf_ref[...], wbf_ref[...], preferred_element_type=F32)
    mix = zc_ref[:, 0:d].astype(F32) * bm + zc_ref[:, d:2 * d].astype(F32) * bf
    x1 = x_ref[...] + jnp.dot(mix.astype(BF16), wo_ref[...], preferred_element_type=F32)
    h2 = _rms(x1, g2_ref[...]).astype(BF16)
    acc = jnp.zeros(x1.shape, F32)
    for f0 in range(0, d_ff, tf):
        g = jnp.dot(h2, wg_ref[:, f0:f0 + tf], preferred_element_type=F32)
        u = jnp.dot(h2, wu_ref[:, f0:f0 + tf], preferred_element_type=F32)
        act = (g * jax.nn.sigmoid(g) * u).astype(BF16)
        acc = acc + jnp.dot(act, wd_ref[f0:f0 + tf, :], preferred_element_type=F32)
    o_ref[...] = _rms(x1 + acc, gfin_ref[...])


def _bias_selectors():
    ones_lane = 3 * GATE_GROUP
    qsel = np.zeros((H_F, LANES, LANES), np.float32)
    ksel = np.zeros((H_F, LANES, LANES), np.float32)
    for h in range(H_F):
        p0 = LANES // 2 if h % 2 == 0 else 0
        for c in range(3):
            qsel[h, GATE_GROUP * c + h, p0 + c] = 1.0
            qsel[h, ones_lane, p0 + 3 + c] = 1.0
            ksel[h, ones_lane, p0 + c] = 1.0
            ksel[h, GATE_GROUP * c + h, p0 + 3 + c] = -1.0
    return jnp.asarray(qsel, BF16), jnp.asarray(ksel, BF16)


def _layer(x2d, batch, seq, p, cfg):
    t, d = x2d.shape
    w_m = p["w_br_mlstm"].shape[0]
    w_f = p["w_br_fox"].shape[0]
    dh_m = w_m // H_M
    dh_f = w_f // H_F
    vmem = cfg["vmem_limit"]

    wa, wb, wc, wd, ba, bb, bc, bd = _prep_in_weights(p, vmem)

    tm = cfg["tm_in"]
    chunk = cfg["chunk"]
    tq, tk = cfg["tq"], cfg["tk"]
    nc = seq // chunk
    n_steps = t // tm
    assert seq % tm == 0 and tm % chunk == 0 and tm % tk == 0

    later = [p["w_br_mlstm"], p["w_br_fox"], p["w_out"], p["w_gate"], p["w_up"], p["w_down"]]

    def slice_spec(rows, cols):
        hold = 1
        while (rows * hold) % (n_steps * BF16_ROWS):
            hold *= 2
        assert hold <= n_steps
        return pl.BlockSpec((rows * hold // n_steps, cols), lambda i: (i // hold, 0))

    later_specs = [slice_spec(*w.shape) for w in later]
    outs = pl.pallas_call(
        functools.partial(_inproj_kernel, w_m=w_m, w_f=w_f, n_chunk=cfg["n_chunk"],
                          m_split=cfg["m_split"],
                          q_scale=dh_f ** -0.5 * LOG2E, k_scale=dh_m ** -0.5, chunk=chunk,
                          tk=tk, tiles_per_seq=seq // tm, n_cast=len(later)),
        grid=(n_steps,),
        in_specs=[pl.BlockSpec((tm, d), lambda i: (i, 0)), _const_spec((1, d)),
                  _const_spec(wa.shape), _const_spec(ba.shape),
                  _const_spec(wb.shape), _const_spec(bb.shape),
                  _const_spec(wc.shape), _const_spec(bc.shape),
                  _const_spec(wd.shape), _const_spec(bd.shape),
                  _const_spec((CONV_K, 2 * w_m)), _const_spec((1, 2 * w_m))] + later_specs,
        out_specs=[pl.BlockSpec((tm, 2 * w_m), lambda i: (i, 0)),
                   pl.BlockSpec((tm // chunk, w_m, chunk), lambda i: (i, 0, 0)),
                   pl.BlockSpec((tm, w_m), lambda i: (i, 0)),
                   pl.BlockSpec((tm, 2 * w_f), lambda i: (i, 0)),
                   pl.BlockSpec((tm // tk, w_f, tk), lambda i: (i, 0, 0)),
                   pl.BlockSpec((tm, 2 * d), lambda i: (i, 0)),
                   pl.BlockSpec((N_GATE_ROWS, tm), lambda i: (0, i))] + later_specs,
        out_shape=[jax.ShapeDtypeStruct((t, 2 * w_m), BF16),
                   jax.ShapeDtypeStruct((t // chunk, w_m, chunk), BF16),
                   jax.ShapeDtypeStruct((t, w_m), BF16),
                   jax.ShapeDtypeStruct((t, 2 * w_f), BF16),
                   jax.ShapeDtypeStruct((t // tk, w_f, tk), BF16),
                   jax.ShapeDtypeStruct((t, 2 * d), BF16),
                   jax.ShapeDtypeStruct((N_GATE_ROWS, t), F32)]
        + [jax.ShapeDtypeStruct(w.shape, BF16) for w in later],
        scratch_shapes=[pltpu.VMEM((2 * w_m // cfg["n_chunk"], tm + SUBLANES, cfg["n_chunk"]), F32)],
        compiler_params=pltpu.CompilerParams(dimension_semantics=("arbitrary",), vmem_limit_bytes=vmem),
        name="inproj",
    )(x2d, p["norm1_g"][None, :], wa, ba, wb, bb, wc, bc, wd, bd, p["conv_w"], p["conv_b"][None, :],
      *later)
    zqk, vt, og, zb, fvt, zc, zdt, wbm, wbf, wo, wg, wu, wdn = outs

    nb = cfg["gates_batch"]
    assert batch % nb == 0
    rows, cols, csplit = pl.pallas_call(
        functools.partial(_gates_kernel, seq=seq),
        grid=(batch // nb,),
        in_specs=[pl.BlockSpec((N_GATE_ROWS, nb * seq), lambda b: (0, b))],
        out_specs=[pl.BlockSpec((nb, 3 * GATE_GROUP, seq), lambda b: (b, 0, 0)),
                   pl.BlockSpec((nb * seq, LANES), lambda b: (b, 0)),
                   pl.BlockSpec((nb * seq, LANES), lambda b: (b, 0))],
        out_shape=[jax.ShapeDtypeStruct((batch, 3 * GATE_GROUP, seq), F32),
                   jax.ShapeDtypeStruct((t, LANES), F32),
                   jax.ShapeDtypeStruct((t, LANES), BF16)],
        compiler_params=pltpu.CompilerParams(dimension_semantics=("parallel",), vmem_limit_bytes=vmem),
        name="gates",
    )(zdt)

    rows_m = rows.reshape(batch, 3 * GATE_GROUP, nc, chunk)
    ym = pl.pallas_call(
        functools.partial(_mlstm_kernel, w_m=w_m, chunk=chunk),
        grid=(batch,),
        in_specs=[pl.BlockSpec((seq, 2 * w_m), lambda b: (b, 0)),
                  pl.BlockSpec((nc, w_m, chunk), lambda b: (b, 0, 0)),
                  pl.BlockSpec((seq, w_m), lambda b: (b, 0)),
                  pl.BlockSpec((1, 3 * GATE_GROUP, nc, chunk), lambda b: (b, 0, 0, 0)),
                  pl.BlockSpec((seq, LANES), lambda b: (b, 0)),
                  _const_spec((1, w_m))],
        out_specs=pl.BlockSpec((seq, w_m), lambda b: (b, 0)),
        out_shape=jax.ShapeDtypeStruct((t, w_m), BF16),
        scratch_shapes=[pltpu.VMEM((H_M, dh_m + BF16_ROWS, dh_m), F32)],
        compiler_params=pltpu.CompilerParams(dimension_semantics=("parallel",), vmem_limit_bytes=vmem),
        name="mlstm",
    )(zqk, vt, og, rows_m, cols, p["mlstm_norm_g"][None, :])

    assert tq % tk == 0 and seq % tq == 0
    nq = seq // tq
    qsel, ksel = _bias_selectors()
    v_rows = LANES // 2 + BF16_ROWS
    yf = pl.pallas_call(
        functools.partial(_fox_kernel, tq=tq, tk=tk),
        grid=(batch, nq),
        in_specs=[pl.BlockSpec((tq, w_f), lambda b, i: (b * nq + i, 0)),
                  pl.BlockSpec((seq, w_f), lambda b, i: (b, 1)),
                  pl.BlockSpec((seq // tk, w_f, tk), lambda b, i: (b, 0, 0)),
                  pl.BlockSpec((tq, LANES), lambda b, i: (b * nq + i, 0)),
                  pl.BlockSpec((seq, LANES), lambda b, i: (b, 0)),
                  _const_spec(qsel.shape), _const_spec(ksel.shape)],
        out_specs=pl.BlockSpec((tq, w_f), lambda b, i: (b * nq + i, 0)),
        out_shape=jax.ShapeDtypeStruct((t, w_f), BF16),
        scratch_shapes=[pltpu.VMEM((H_F, seq, LANES), BF16),
                        pltpu.VMEM((H_F, tq, LANES), BF16),
                        pltpu.VMEM((H_F, 1, tq), F32),
                        pltpu.VMEM((H_F, v_rows, tq), F32)],
        compiler_params=pltpu.CompilerParams(dimension_semantics=("arbitrary", "arbitrary"),
                                             vmem_limit_bytes=vmem),
        name="fox",
    )(zb, zb, fvt, csplit, csplit, qsel, ksel)

    tmp = cfg["tm_post"]
    return pl.pallas_call(
        functools.partial(_post_kernel, tf=cfg["tf"]),
        grid=(t // tmp,),
        in_specs=[pl.BlockSpec((tmp, d), lambda i: (i, 0)),
                  pl.BlockSpec((tmp, w_m), lambda i: (i, 0)),
                  pl.BlockSpec((tmp, w_f), lambda i: (i, 0)),
                  pl.BlockSpec((tmp, 2 * d), lambda i: (i, 0)),
                  _const_spec(wbm.shape), _const_spec(wbf.shape), _const_spec(wo.shape),
                  _const_spec((1, d)), _const_spec(wg.shape), _const_spec(wu.shape),
                  _const_spec(wdn.shape), _const_spec((1, d))],
        out_specs=pl.BlockSpec((tmp, d), lambda i: (i, 0)),
        out_shape=jax.ShapeDtypeStruct((t, d), F32),
        compiler_params=pltpu.CompilerParams(dimension_semantics=("parallel",), vmem_limit_bytes=vmem),
        name="post",
    )(x2d, ym, yf, zc, wbm, wbf, wo, p["norm2_g"][None, :], wg, wu, wdn, p["norm_f_g"][None, :])


def kernel(x, norm1_g, w_in, b_in, conv_w, conv_b, mlstm_norm_g, w_br_mlstm, w_br_fox, w_out,
           norm2_g, w_gate, w_up, w_down, norm_f_g):
    batch, seq, d = x.shape
    depth = w_in.shape[0]
    assert depth == 1, "the final norm is fused into the single layer's last call"
    cfg = _cfg(batch, seq, d, w_gate.shape[-1])
    p = dict(norm1_g=norm1_g[0], w_in=w_in[0], b_in=b_in[0], conv_w=conv_w[0], conv_b=conv_b[0],
             mlstm_norm_g=mlstm_norm_g[0], w_br_mlstm=w_br_mlstm[0], w_br_fox=w_br_fox[0],
             w_out=w_out[0], norm2_g=norm2_g[0], w_gate=w_gate[0], w_up=w_up[0], w_down=w_down[0],
             norm_f_g=norm_f_g)
    out = _layer(x.reshape(batch * seq, d), batch, seq, p, cfg)
    return out.reshape(batch, seq, d)
```

```python
import functools

import jax
import jax.numpy as jnp
import numpy as np
from jax import lax
from jax.experimental import pallas as pl
from jax.experimental.pallas import tpu as pltpu

EPS = 1e-6
H_M = 4
H_F = 8
CONV_K = 4

LANES = 128
SUBLANES = 8
BF16_ROWS = 16
GATE_GROUP = 8
N_GATE_ROWS = 3 * GATE_GROUP
V7X_VMEM_BYTES = 64 * 1024 * 1024
VMEM_COMPILER_RESERVE = 8 * 1024 * 1024
NEG_BIG = -1e30
LOG2E = 1.4426950408889634

F32 = jnp.float32
BF16 = jnp.bfloat16
NT_DIMS = (((1,), (1,)), ((), ()))


def _cfg(batch, seq, d_model, d_ff):
    return dict(
        tm_in=1024,
        n_chunk=256,
        m_split=8,
        gates_batch=4,
        chunk=256,
        tq=512,
        tk=256,
        tm_post=512,
        tf=256,
        vmem_limit=V7X_VMEM_BYTES - VMEM_COMPILER_RESERVE,
    )


def _const_spec(shape):
    nd = len(shape)
    return pl.BlockSpec(shape, lambda *_: (0,) * nd, pipeline_mode=pl.Buffered(1))


def _rms(x, g):
    return x * lax.rsqrt(jnp.mean(x * x, axis=-1, keepdims=True) + EPS) * g


def _log_sigmoid(x):
    return jnp.minimum(x, 0.0) - jnp.log1p(jnp.exp(-jnp.abs(x)))


def _wprep_kernel(wint_ref, bin_ref, wa_o, wb_o, wc_o, wd_o, ba_o, bb_o, bc_o, bd_o, *, w_m, w_f):
    o_mi = 4 * w_m
    o_mf = o_mi + H_M
    o_fq = o_mf + H_M
    o_ff = o_fq + 3 * w_f
    o_g = o_ff + H_F

    def gate_block(src):
        rows = src.shape[0]
        lane = lax.broadcasted_iota(jnp.int32, (rows, LANES), 1)
        blk_m = src[:, o_mi:o_mi + LANES]
        f0 = (o_ff // LANES) * LANES
        blk_f = src[:, f0:f0 + LANES]
        mi = jnp.where(lane < H_M, blk_m, 0.0)
        mf = jnp.where((lane >= GATE_GROUP) & (lane < GATE_GROUP + H_M),
                       pltpu.roll(blk_m, GATE_GROUP - H_M, axis=1), 0.0)
        ff = jnp.where((lane >= 2 * GATE_GROUP) & (lane < 2 * GATE_GROUP + H_F),
                       pltpu.roll(blk_f, 2 * GATE_GROUP - (o_ff - f0), axis=1), 0.0)
        return mi + mf + ff

    slab = wint_ref[...]
    wa_o[...] = slab[0:o_mi].T.astype(BF16)
    wb_o[...] = slab[o_fq:o_ff].T.astype(BF16)
    wc_o[...] = slab[o_g:].T.astype(BF16)
    g_m = slab[o_mi:o_mi + GATE_GROUP]
    row = lax.broadcasted_iota(jnp.int32, g_m.shape, 0)
    gates = jnp.concatenate(
        [jnp.where(row < H_M, g_m, 0.0),
         jnp.where(row < H_M, pltpu.roll(g_m, GATE_GROUP - H_M, axis=0), 0.0),
         slab[o_ff:o_ff + GATE_GROUP],
         jnp.zeros((LANES - 3 * GATE_GROUP, slab.shape[1]), F32)], axis=0)
    wd_o[...] = gates.T.astype(BF16)
    b = bin_ref[...]
    ba_o[...] = b[:, 0:o_mi]
    bb_o[...] = b[:, o_fq:o_ff]
    bc_o[...] = b[:, o_g:]
    bd_o[...] = gate_block(b)


def _prep_in_weights(p, vmem):
    d, n_in = p["w_in"].shape
    w_m = p["w_br_mlstm"].shape[0]
    w_f = p["w_br_fox"].shape[0]
    steps = 8
    o_ff = 4 * w_m + 2 * H_M + 3 * w_f
    assert (4 * w_m) % LANES == 0 and 2 * H_M <= GATE_GROUP + H_M <= LANES
    assert o_ff % LANES <= 2 * GATE_GROUP and o_ff % LANES + H_F <= LANES
    assert n_in == o_ff + H_F + 2 * d
    srcs = [p["w_in"].T, p["b_in"][None, :]]

    def row_spec(shape, tiled=True):
        if not tiled:
            return pl.BlockSpec(shape, lambda i: (0, 0))
        assert shape[0] % (steps * BF16_ROWS) == 0
        return pl.BlockSpec((shape[0] // steps, shape[1]), lambda i: (i, 0))

    out_shapes = [((d, 4 * w_m), BF16), ((d, 3 * w_f), BF16), ((d, 2 * d), BF16), ((d, LANES), BF16),
                  ((1, 4 * w_m), F32), ((1, 3 * w_f), F32), ((1, 2 * d), F32), ((1, LANES), F32)]
    return pl.pallas_call(
        functools.partial(_wprep_kernel, w_m=w_m, w_f=w_f),
        grid=(steps,),
        in_specs=([pl.BlockSpec((n_in, d // steps), lambda i: (0, i))]
                  + [row_spec(s.shape, tiled=s.shape[0] > 1) for s in srcs[1:]]),
        out_specs=[row_spec(s, tiled=s[0] > 1) for s, _ in out_shapes],
        out_shape=[jax.ShapeDtypeStruct(s, dt) for s, dt in out_shapes],
        compiler_params=pltpu.CompilerParams(dimension_semantics=("arbitrary",), vmem_limit_bytes=vmem),
        name="wprep",
    )(*srcs)


def _inproj_kernel(x_ref, g_ref, wa_ref, ba_ref, wb_ref, bb_ref, wc_ref, bc_ref, wd_ref, bd_ref,
                   cw_ref, cb_ref, *rest,
                   w_m, w_f, n_chunk, m_split, q_scale, k_scale, chunk, tk, tiles_per_seq, n_cast):
    cast_srcs, rest = rest[:n_cast], rest[n_cast:]
    (zqk_ref, vt_ref, og_ref, zb_ref, fvt_ref, zc_ref, zdt_ref), rest = rest[:7], rest[7:]
    cast_dsts, (zs_sc,) = rest[:n_cast], rest[n_cast:]
    tm = x_ref.shape[0]
    seq_start = (pl.program_id(0) % tiles_per_seq) == 0

    @pl.when(seq_start)
    def _():
        zs_sc[:, 0:SUBLANES, :] = jnp.zeros((zs_sc.shape[0], SUBLANES, n_chunk), F32)

    @pl.when(jnp.logical_not(seq_start))
    def _():
        zs_sc[:, 0:SUBLANES, :] = zs_sc[:, tm:tm + SUBLANES, :]

    rows = tm // m_split
    hb = [_rms(x_ref[r0:r0 + rows, :], g_ref[...]).astype(BF16) for r0 in range(0, tm, rows)]

    def proj(w_ref, b_ref, c0, c1):
        w = w_ref[:, c0:c1]
        parts = [jnp.dot(h, w, preferred_element_type=F32) for h in hb]
        return jnp.concatenate(parts, axis=0) + b_ref[:, c0:c1]

    def qk_chunk(ci):
        c0 = ci * n_chunk
        cs = slice(c0, c0 + n_chunk)
        zs_sc[ci, SUBLANES:tm + SUBLANES, :] = proj(wa_ref, ba_ref, c0, c0 + n_chunk)
        w = cw_ref[:, cs]
        y = cb_ref[:, cs]
        for s in range(CONV_K):
            y = y + zs_sc[ci, SUBLANES - s:SUBLANES - s + tm, :] * w[CONV_K - 1 - s:CONV_K - s]
        act = y * jax.nn.sigmoid(y)
        if c0 >= w_m:
            act = act * k_scale
        zqk_ref[:, cs] = act.astype(BF16)

    def v_chunk(i):
        c0 = 2 * w_m + i * n_chunk
        z = proj(wa_ref, ba_ref, c0, c0 + n_chunk)
        for cc in range(tm // chunk):
            for f0 in range(0, n_chunk, LANES):
                blk = z[cc * chunk:(cc + 1) * chunk, f0:f0 + LANES]
                r0 = i * n_chunk + f0
                vt_ref[cc, r0:r0 + LANES, :] = blk.T.astype(BF16)

    def og_chunk(i):
        c0 = 3 * w_m + i * n_chunk
        z = proj(wa_ref, ba_ref, c0, c0 + n_chunk)
        og_ref[:, i * n_chunk:(i + 1) * n_chunk] = jax.nn.sigmoid(z).astype(BF16)

    def b_chunk(i):
        c0 = i * n_chunk
        z = proj(wb_ref, bb_ref, c0, c0 + n_chunk)
        if c0 < w_f:
            z = z * q_scale
        if c0 < 2 * w_f:
            zb_ref[:, c0:c0 + n_chunk] = z.astype(BF16)
        else:
            for jb in range(tm // tk):
                for f0 in range(0, n_chunk, LANES):
                    blk = z[jb * tk:(jb + 1) * tk, f0:f0 + LANES]
                    r0 = c0 - 2 * w_f + f0
                    fvt_ref[jb, r0:r0 + LANES, :] = blk.T.astype(BF16)

    def c_chunk(i):
        c0 = i * n_chunk
        z = proj(wc_ref, bc_ref, c0, c0 + n_chunk)
        zc_ref[:, c0:c0 + n_chunk] = jax.nn.sigmoid(z).astype(BF16)

    def d_chunk(_):
        zd = proj(wd_ref, bd_ref, 0, LANES)
        zdt_ref[...] = zd.T[:N_GATE_ROWS, :]

    heavy = [(qk_chunk, i) for i in range(2 * w_m // n_chunk)]
    light = ([(c_chunk, i) for i in range(wc_ref.shape[1] // n_chunk)]
             + [(b_chunk, i) for i in range(3 * w_f // n_chunk)]
             + [(v_chunk, i) for i in range(w_m // n_chunk)]
             + [(og_chunk, i) for i in range(w_m // n_chunk)] + [(d_chunk, 0)])
    per_heavy = len(light) // len(heavy)
    order = []
    for hi, task in enumerate(heavy):
        order.append(task)
        order.extend(light[hi * per_heavy:(hi + 1) * per_heavy])
    order.extend(light[len(heavy) * per_heavy:])
    for fn, i in order:
        fn(i)
    for src, dst in zip(cast_srcs, cast_dsts):
        dst[...] = src[...].astype(BF16)


def _scan_lanes(x, op, fill, seg):
    pos = lax.rem(lax.broadcasted_iota(jnp.int32, x.shape, 1), seg)
    s = 1
    while s < seg:
        shifted = pltpu.roll(x, s, axis=1)
        x = op(x, jnp.where(pos >= s, shifted, fill))
        s *= 2
    return x


def _gates_kernel(zdt_ref, rows_ref, cols_ref, csplit_ref, *, seq):
    z = zdt_ref[...]
    width = z.shape[1]
    i8 = z[0:GATE_GROUP]
    cum = _scan_lanes(_log_sigmoid(z[GATE_GROUP:3 * GATE_GROUP]), jnp.add, 0.0, seq)
    f8 = cum[0:GATE_GROUP]
    cf8 = cum[GATE_GROUP:2 * GATE_GROUP] * LOG2E
    g8 = i8 - f8
    m8 = jnp.maximum(_scan_lanes(g8, jnp.maximum, NEG_BIG, seq), 0.0)
    en8 = jnp.exp(-(f8 + m8))
    g8, m8 = g8 * LOG2E, m8 * LOG2E
    rows = jnp.concatenate([g8, m8, en8], axis=0)
    for bb in range(width // seq):
        rows_ref[bb] = rows[:, bb * seq:(bb + 1) * seq]
    stack = jnp.concatenate(
        [g8, m8, jnp.zeros((LANES - 2 * GATE_GROUP, width), F32)], axis=0)
    cols_ref[...] = stack.T
    hi = cf8.astype(BF16).astype(F32)
    r1 = cf8 - hi
    lo = r1.astype(BF16).astype(F32)
    lo2 = (r1 - lo).astype(BF16).astype(F32)
    ones = jnp.where(lax.broadcasted_iota(jnp.int32, (GATE_GROUP, width), 0) == 0, 1.0, 0.0)
    split = jnp.concatenate(
        [hi, lo, lo2, ones, jnp.zeros((LANES - 4 * GATE_GROUP, width), F32)], axis=0)
    csplit_ref[...] = split.T.astype(BF16)


def _mlstm_kernel(zqk_ref, vt_ref, og_ref, rows_ref, cols_ref, ng_ref, ym_ref, ct_sc, *, w_m, chunk):
    seq = zqk_ref.shape[0]
    dh = w_m // H_M
    nc = seq // chunk
    ct_sc[...] = jnp.zeros(ct_sc.shape, F32)
    hc = chunk // 2
    causal = (lax.broadcasted_iota(jnp.int32, (hc, hc), 0)
              <= lax.broadcasted_iota(jnp.int32, (hc, hc), 1))
    ones_rows = jnp.where(
        lax.broadcasted_iota(jnp.int32, (BF16_ROWS, chunk), 0) == 0, 1.0, 0.0).astype(BF16)

    def body(c, carry):
        r0 = pl.multiple_of(c * chunk, chunk)
        colsc = cols_ref[pl.ds(r0, chunk), :]
        last = cols_ref[pl.ds(r0 + chunk - 1, 1), :]
        prev = cols_ref[pl.ds(jnp.maximum(r0 - 1, 0), 1), :]
        prev = jnp.where(c > 0, prev, 0.0)
        first, vas = [], []
        for h in range(H_M):
            qc = zqk_ref[pl.ds(r0, chunk), h * dh:(h + 1) * dh]
            kc = zqk_ref[pl.ds(r0, chunk), w_m + h * dh:w_m + (h + 1) * dh]
            lhs = jnp.concatenate([kc, ct_sc[h].astype(BF16)], axis=0)
            first.append(lax.dot_general(lhs, qc, NT_DIMS, preferred_element_type=F32))
            vas.append(jnp.concatenate([vt_ref[c, h * dh:(h + 1) * dh, :], ones_rows], axis=0))
        for h in range(H_M):
            kc = zqk_ref[pl.ds(r0, chunk), w_m + h * dh:w_m + (h + 1) * dh]
            g_row = rows_ref[0, h, pl.ds(c, 1), :]
            m_e = last[:, GATE_GROUP + h:GATE_GROUP + h + 1]
            m_p = prev[:, GATE_GROUP + h:GATE_GROUP + h + 1]
            vaw = (vas[h].astype(F32) * jnp.exp2(g_row - m_e)).astype(BF16)
            ct_sc[h] = jnp.exp2(m_p - m_e) * ct_sc[h] + jnp.dot(vaw, kc, preferred_element_type=F32)
        for h in range(H_M):
            ch = slice(h * dh, (h + 1) * dh)
            g_col = colsc[:, h:h + 1]
            m_row = rows_ref[0, GATE_GROUP + h, pl.ds(c, 1), :]
            en_row = rows_ref[0, 2 * GATE_GROUP + h, pl.ds(c, 1), :]
            m_p = prev[:, GATE_GROUP + h:GATE_GROUP + h + 1]
            sk = first[h]
            d00 = jnp.where(causal, jnp.exp2(g_col[0:hc] - m_row[:, 0:hc]), 0.0)
            d01 = jnp.exp2(g_col[0:hc] - m_row[:, hc:chunk])
            d11 = jnp.where(causal, jnp.exp2(g_col[hc:chunk] - m_row[:, hc:chunk]), 0.0)
            top = jnp.concatenate([sk[0:hc, 0:hc] * d00, sk[0:hc, hc:chunk] * d01], axis=1)
            bot = jnp.concatenate([jnp.zeros((hc, hc), F32), sk[hc:chunk, hc:chunk] * d11], axis=1)
            sqk = jnp.concatenate([top, bot], axis=0).astype(BF16)
            nd = (jnp.exp2(m_p - m_row) * first[h][chunk:]
                  + jnp.dot(vas[h], sqk, preferred_element_type=F32))
            den = nd[dh:dh + 1]
            ht = nd[0:dh] * (1.0 / jnp.maximum(jnp.abs(den), en_row))
            hn = ht * lax.rsqrt(jnp.mean(ht * ht, axis=0, keepdims=True) + EPS)
            og = og_ref[pl.ds(r0, chunk), ch].astype(F32)
            ym_ref[pl.ds(r0, chunk), ch] = (hn.T * ng_ref[:, ch] * og).astype(BF16)
        return carry

    lax.fori_loop(0, nc, body, 0, unroll=True)


def _fox_kernel(q_ref, k_ref, fvt_ref, csq_ref, csk_ref, qsel_ref, ksel_ref, yf_ref,
                kaug_sc, qaug_sc, m_sc, acc_sc, *, tq, tk):
    qi = pl.program_id(1)

    seq = k_ref.shape[0]
    dhp = LANES
    half = dhp // 2
    v_rows = half + BF16_ROWS

    def own_lanes(head, rows):
        lane = lax.broadcasted_iota(jnp.int32, (rows, dhp), 1)
        return (lane >= half) if head % 2 else (lane < half)

    def augment(x_ref, cs_ref, sel_ref, dst_sc, rows):
        cs = cs_ref[...]
        for p in range(H_F // 2):
            xp = x_ref[:, p * dhp:(p + 1) * dhp]
            bias = jnp.dot(cs, sel_ref[p], preferred_element_type=F32).astype(BF16)
            for head in (2 * p, 2 * p + 1):
                b0 = (head % 2) * dhp
                dst_sc[head] = jnp.where(own_lanes(head, rows), xp, bias[:, b0:b0 + dhp])

    @pl.when(qi == 0)
    def _():
        augment(k_ref, csk_ref, ksel_ref, kaug_sc, seq)

    augment(q_ref, csq_ref, qsel_ref, qaug_sc, tq)
    m_sc[...] = jnp.full(m_sc.shape, NEG_BIG, F32)
    acc_sc[...] = jnp.zeros(acc_sc.shape, F32)

    ratio = tq // tk
    ones_rows = jnp.where(
        lax.broadcasted_iota(jnp.int32, (BF16_ROWS, tk), 0) == 0, 1.0, 0.0).astype(BF16)

    def vaug(head, j):
        return jnp.concatenate([fvt_ref[j, head * half:(head + 1) * half, :], ones_rows], axis=0)

    def step(j, diag):
        q0 = 0 if diag is None else diag * tk
        nq = tq - q0
        k0 = pl.multiple_of(j * tk, tk)
        sts = [lax.dot_general(kaug_sc[head, pl.ds(k0, tk), :], qaug_sc[head, q0:tq, :], NT_DIMS,
                               preferred_element_type=F32) for head in range(H_F)]
        if diag is not None:
            causal = (lax.broadcasted_iota(jnp.int32, (tk, nq), 0)
                      <= lax.broadcasted_iota(jnp.int32, (tk, nq), 1))
        for head in range(H_F):
            st = sts[head]
            if diag is not None:
                st = jnp.where(causal, st, NEG_BIG)
            m = m_sc[head, :, q0:tq]
            m_new = jnp.maximum(m, jnp.max(st, axis=0, keepdims=True))
            alpha = jnp.exp2(m - m_new)
            pt = jnp.exp2(st - m_new).astype(BF16)
            m_sc[head, :, q0:tq] = m_new
            acc_sc[head, :, q0:tq] = (alpha * acc_sc[head, :, q0:tq]
                                      + jnp.dot(vaug(head, j), pt, preferred_element_type=F32))

    def loop_body(jj, carry):
        for r in range(ratio):
            step(jj * ratio + r, None)
        return carry

    lax.fori_loop(0, qi, loop_body, 0)
    for diag in range(ratio):
        step(qi * ratio + diag, diag)
    for p in range(H_F // 2):
        outs = []
        for head in (2 * p, 2 * p + 1):
            acc = acc_sc[head]
            outs.append(acc[0:half] * (1.0 / acc[half:half + 1]))
        yf_ref[:, p * dhp:(p + 1) * dhp] = jnp.concatenate(outs, axis=0).T.astype(BF16)


def _post_kernel(x_ref, ym_ref, yf_ref, zc_ref, wbm_ref, wbf_ref, wo_ref, g2_ref,
                 wg_ref, wu_ref, wd_ref, gfin_ref, o_ref, *, tf):
    d = x_ref.shape[1]
    d_ff = wg_ref.shape[1]
    bm = jnp.dot(ym_ref[...], wbm_ref[...], preferred_element_type=F32)
    bf = jnp.dot(yf_ref[...], wbf_ref[...], preferred_element_type=F32)
    mix = zc_ref[:, 0:d].astype(F32) * bm + zc_ref[:, d:2 * d].astype(F32) * bf
    x1 = x_ref[...] + jnp.dot(mix.astype(BF16), wo_ref[...], preferred_element_type=F32)
    h2 = _rms(x1, g2_ref[...]).astype(BF16)
    acc = jnp.zeros(x1.shape, F32)
    for f0 in range(0, d_ff, tf):
        g = jnp.dot(h2, wg_ref[:, f0:f0 + tf], preferred_element_type=F32)
        u = jnp.dot(h2, wu_ref[:, f0:f0 + tf], preferred_element_type=F32)
        act = (g * jax.nn.sigmoid(g) * u).astype(BF16)
        acc = acc + jnp.dot(act, wd_ref[f0:f0 + tf, :], preferred_element_type=F32)
    o_ref[...] = _rms(x1 + acc, gfin_ref[...])


def _bias_selectors():
    ones_lane = 3 * GATE_GROUP
    qsel = np.zeros((H_F // 2, LANES, 2 * LANES), np.float32)
    ksel = np.zeros((H_F // 2, LANES, 2 * LANES), np.float32)
    for h in range(H_F):
        p0 = (h % 2) * LANES + (LANES // 2 if h % 2 == 0 else 0)
        for c in range(3):
            qsel[h // 2, GATE_GROUP * c + h, p0 + c] = 1.0
            qsel[h // 2, ones_lane, p0 + 3 + c] = 1.0
            ksel[h // 2, ones_lane, p0 + c] = 1.0
            ksel[h // 2, GATE_GROUP * c + h, p0 + 3 + c] = -1.0
    return jnp.asarray(qsel, BF16), jnp.asarray(ksel, BF16)


def _layer(x2d, batch, seq, p, cfg):
    t, d = x2d.shape
    w_m = p["w_br_mlstm"].shape[0]
    w_f = p["w_br_fox"].shape[0]
    dh_m = w_m // H_M
    dh_f = w_f // H_F
    vmem = cfg["vmem_limit"]

    wa, wb, wc, wd, ba, bb, bc, bd = _prep_in_weights(p, vmem)

    tm = cfg["tm_in"]
    chunk = cfg["chunk"]
    tq, tk = cfg["tq"], cfg["tk"]
    nc = seq // chunk
    n_steps = t // tm
    assert seq % tm == 0 and tm % chunk == 0 and tm % tk == 0

    later = [p["w_br_mlstm"], p["w_br_fox"], p["w_out"], p["w_gate"], p["w_up"], p["w_down"]]

    def slice_spec(rows, cols):
        hold = 1
        while (rows * hold) % (n_steps * BF16_ROWS):
            hold *= 2
        assert hold <= n_steps
        return pl.BlockSpec((rows * hold // n_steps, cols), lambda i: (i // hold, 0))

    later_specs = [slice_spec(*w.shape) for w in later]
    outs = pl.pallas_call(
        functools.partial(_inproj_kernel, w_m=w_m, w_f=w_f, n_chunk=cfg["n_chunk"],
                          m_split=cfg["m_split"],
                          q_scale=dh_f ** -0.5 * LOG2E, k_scale=dh_m ** -0.5, chunk=chunk,
                          tk=tk, tiles_per_seq=seq // tm, n_cast=len(later)),
        grid=(n_steps,),
        in_specs=[pl.BlockSpec((tm, d), lambda i: (i, 0)), _const_spec((1, d)),
                  _const_spec(wa.shape), _const_spec(ba.shape),
                  _const_spec(wb.shape), _const_spec(bb.shape),
                  _const_spec(wc.shape), _const_spec(bc.shape),
                  _const_spec(wd.shape), _const_spec(bd.shape),
                  _const_spec((CONV_K, 2 * w_m)), _const_spec((1, 2 * w_m))] + later_specs,
        out_specs=[pl.BlockSpec((tm, 2 * w_m), lambda i: (i, 0)),
                   pl.BlockSpec((tm // chunk, w_m, chunk), lambda i: (i, 0, 0)),
                   pl.BlockSpec((tm, w_m), lambda i: (i, 0)),
                   pl.BlockSpec((tm, 2 * w_f), lambda i: (i, 0)),
                   pl.BlockSpec((tm // tk, w_f, tk), lambda i: (i, 0, 0)),
                   pl.BlockSpec((tm, 2 * d), lambda i: (i, 0)),
                   pl.BlockSpec((N_GATE_ROWS, tm), lambda i: (0, i))] + later_specs,
        out_shape=[jax.ShapeDtypeStruct((t, 2 * w_m), BF16),
                   jax.ShapeDtypeStruct((t // chunk, w_m, chunk), BF16),
                   jax.ShapeDtypeStruct((t, w_m), BF16),
                   jax.ShapeDtypeStruct((t, 2 * w_f), BF16),
                   jax.ShapeDtypeStruct((t // tk, w_f, tk), BF16),
                   jax.ShapeDtypeStruct((t, 2 * d), BF16),
                   jax.ShapeDtypeStruct((N_GATE_ROWS, t), F32)]
        + [jax.ShapeDtypeStruct(w.shape, BF16) for w in later],
        scratch_shapes=[pltpu.VMEM((2 * w_m // cfg["n_chunk"], tm + SUBLANES, cfg["n_chunk"]), F32)],
        compiler_params=pltpu.CompilerParams(dimension_semantics=("arbitrary",), vmem_limit_bytes=vmem),
        name="inproj",
    )(x2d, p["norm1_g"][None, :], wa, ba, wb, bb, wc, bc, wd, bd, p["conv_w"], p["conv_b"][None, :],
      *later)
    zqk, vt, og, zb, fvt, zc, zdt, wbm, wbf, wo, wg, wu, wdn = outs

    nb = cfg["gates_batch"]
    assert batch % nb == 0
    rows, cols, csplit = pl.pallas_call(
        functools.partial(_gates_kernel, seq=seq),
        grid=(batch // nb,),
        in_specs=[pl.BlockSpec((N_GATE_ROWS, nb * seq), lambda b: (0, b))],
        out_specs=[pl.BlockSpec((nb, 3 * GATE_GROUP, seq), lambda b: (b, 0, 0)),
                   pl.BlockSpec((nb * seq, LANES), lambda b: (b, 0)),
                   pl.BlockSpec((nb * seq, LANES), lambda b: (b, 0))],
        out_shape=[jax.ShapeDtypeStruct((batch, 3 * GATE_GROUP, seq), F32),
                   jax.ShapeDtypeStruct((t, LANES), F32),
                   jax.ShapeDtypeStruct((t, LANES), BF16)],
        compiler_params=pltpu.CompilerParams(dimension_semantics=("parallel",), vmem_limit_bytes=vmem),
        name="gates",
    )(zdt)

    rows_m = rows.reshape(batch, 3 * GATE_GROUP, nc, chunk)
    ym = pl.pallas_call(
        functools.partial(_mlstm_kernel, w_m=w_m, chunk=chunk),
        grid=(batch,),
        in_specs=[pl.BlockSpec((seq, 2 * w_m), lambda b: (b, 0)),
                  pl.BlockSpec((nc, w_m, chunk), lambda b: (b, 0, 0)),
                  pl.BlockSpec((seq, w_m), lambda b: (b, 0)),
                  pl.BlockSpec((1, 3 * GATE_GROUP, nc, chunk), lambda b: (b, 0, 0, 0)),
                  pl.BlockSpec((seq, LANES), lambda b: (b, 0)),
                  _const_spec((1, w_m))],
        out_specs=pl.BlockSpec((seq, w_m), lambda b: (b, 0)),
        out_shape=jax.ShapeDtypeStruct((t, w_m), BF16),
        scratch_shapes=[pltpu.VMEM((H_M, dh_m + BF16_ROWS, dh_m), F32)],
        compiler_params=pltpu.CompilerParams(dimension_semantics=("parallel",), vmem_limit_bytes=vmem),
        name="mlstm",
    )(zqk, vt, og, rows_m, cols, p["mlstm_norm_g"][None, :])

    assert tq % tk == 0 and seq % tq == 0
    nq = seq // tq
    qsel, ksel = _bias_selectors()
    v_rows = LANES // 2 + BF16_ROWS
    yf = pl.pallas_call(
        functools.partial(_fox_kernel, tq=tq, tk=tk),
        grid=(batch, nq),
        in_specs=[pl.BlockSpec((tq, w_f), lambda b, i: (b * nq + i, 0)),
                  pl.BlockSpec((seq, w_f), lambda b, i: (b, 1)),
                  pl.BlockSpec((seq // tk, w_f, tk), lambda b, i: (b, 0, 0)),
                  pl.BlockSpec((tq, LANES), lambda b, i: (b * nq + i, 0)),
                  pl.BlockSpec((seq, LANES), lambda b, i: (b, 0)),
                  _const_spec(qsel.shape), _const_spec(ksel.shape)],
        out_specs=pl.BlockSpec((tq, w_f), lambda b, i: (b * nq + i, 0)),
        out_shape=jax.ShapeDtypeStruct((t, w_f), BF16),
        scratch_shapes=[pltpu.VMEM((H_F, seq, LANES), BF16),
                        pltpu.VMEM((H_F, tq, LANES), BF16),
                        pltpu.VMEM((H_F, 1, tq), F32),
                        pltpu.VMEM((H_F, v_rows, tq), F32)],
        compiler_params=pltpu.CompilerParams(dimension_semantics=("arbitrary", "arbitrary"),
                                             vmem_limit_bytes=vmem),
        name="fox",
    )(zb, zb, fvt, csplit, csplit, qsel, ksel)

    tmp = cfg["tm_post"]
    return pl.pallas_call(
        functools.partial(_post_kernel, tf=cfg["tf"]),
        grid=(t // tmp,),
        in_specs=[pl.BlockSpec((tmp, d), lambda i: (i, 0)),
                  pl.BlockSpec((tmp, w_m), lambda i: (i, 0)),
                  pl.BlockSpec((tmp, w_f), lambda i: (i, 0)),
                  pl.BlockSpec((tmp, 2 * d), lambda i: (i, 0)),
                  _const_spec(wbm.shape), _const_spec(wbf.shape), _const_spec(wo.shape),
                  _const_spec((1, d)), _const_spec(wg.shape), _const_spec(wu.shape),
                  _const_spec(wdn.shape), _const_spec((1, d))],
        out_specs=pl.BlockSpec((tmp, d), lambda i: (i, 0)),
        out_shape=jax.ShapeDtypeStruct((t, d), F32),
        compiler_params=pltpu.CompilerParams(dimension_semantics=("parallel",), vmem_limit_bytes=vmem),
        name="post",
    )(x2d, ym, yf, zc, wbm, wbf, wo, p["norm2_g"][None, :], wg, wu, wdn, p["norm_f_g"][None, :])


def kernel(x, norm1_g, w_in, b_in, conv_w, conv_b, mlstm_norm_g, w_br_mlstm, w_br_fox, w_out,
           norm2_g, w_gate, w_up, w_down, norm_f_g):
    batch, seq, d = x.shape
    depth = w_in.shape[0]
    assert depth == 1, "the final norm is fused into the single layer's last call"
    cfg = _cfg(batch, seq, d, w_gate.shape[-1])
    p = dict(norm1_g=norm1_g[0], w_in=w_in[0], b_in=b_in[0], conv_w=conv_w[0], conv_b=conv_b[0],
             mlstm_norm_g=mlstm_norm_g[0], w_br_mlstm=w_br_mlstm[0], w_br_fox=w_br_fox[0],
             w_out=w_out[0], norm2_g=norm2_g[0], w_gate=w_gate[0], w_up=w_up[0], w_down=w_down[0],
             norm_f_g=norm_f_g)
    out = _layer(x.reshape(batch * seq, d), batch, seq, p, cfg)
    return out.reshape(batch, seq, d)
```

```python
import functools

import jax
import jax.numpy as jnp
import numpy as np
from jax import lax
from jax.experimental import pallas as pl
from jax.experimental.pallas import tpu as pltpu

EPS = 1e-6
H_M = 4
H_F = 8
CONV_K = 4

LANES = 128
SUBLANES = 8
BF16_ROWS = 16
GATE_GROUP = 8
N_GATE_ROWS = 3 * GATE_GROUP
V7X_VMEM_BYTES = 64 * 1024 * 1024
VMEM_COMPILER_RESERVE = 8 * 1024 * 1024
NEG_BIG = -1e30
LOG2E = 1.4426950408889634

F32 = jnp.float32
BF16 = jnp.bfloat16
NT_DIMS = (((1,), (1,)), ((), ()))


def _cfg(batch, seq, d_model, d_ff):
    return dict(
        tm_in=1024,
        n_chunk=256,
        m_split=8,
        gates_batch=4,
        chunk=256,
        tq=512,
        tk=256,
        tm_post=512,
        tf=256,
        vmem_limit=V7X_VMEM_BYTES - VMEM_COMPILER_RESERVE,
    )


def _const_spec(shape):
    nd = len(shape)
    return pl.BlockSpec(shape, lambda *_: (0,) * nd, pipeline_mode=pl.Buffered(1))


def _rms(x, g):
    return x * lax.rsqrt(jnp.mean(x * x, axis=-1, keepdims=True) + EPS) * g


def _log_sigmoid(x):
    return jnp.minimum(x, 0.0) - jnp.log1p(jnp.exp(-jnp.abs(x)))


def _wprep_kernel(wint_ref, bin_ref, wa_o, wb_o, wc_o, wd_o, ba_o, bb_o, bc_o, bd_o, *, w_m, w_f):
    o_mi = 4 * w_m
    o_mf = o_mi + H_M
    o_fq = o_mf + H_M
    o_ff = o_fq + 3 * w_f
    o_g = o_ff + H_F

    def gate_block(src):
        rows = src.shape[0]
        lane = lax.broadcasted_iota(jnp.int32, (rows, LANES), 1)
        blk_m = src[:, o_mi:o_mi + LANES]
        f0 = (o_ff // LANES) * LANES
        blk_f = src[:, f0:f0 + LANES]
        mi = jnp.where(lane < H_M, blk_m, 0.0)
        mf = jnp.where((lane >= GATE_GROUP) & (lane < GATE_GROUP + H_M),
                       pltpu.roll(blk_m, GATE_GROUP - H_M, axis=1), 0.0)
        ff = jnp.where((lane >= 2 * GATE_GROUP) & (lane < 2 * GATE_GROUP + H_F),
                       pltpu.roll(blk_f, 2 * GATE_GROUP - (o_ff - f0), axis=1), 0.0)
        return mi + mf + ff

    slab = wint_ref[...]
    wa_o[...] = slab[0:o_mi].T.astype(BF16)
    wb_o[...] = slab[o_fq:o_ff].T.astype(BF16)
    wc_o[...] = slab[o_g:].T.astype(BF16)
    g_m = slab[o_mi:o_mi + GATE_GROUP]
    row = lax.broadcasted_iota(jnp.int32, g_m.shape, 0)
    gates = jnp.concatenate(
        [jnp.where(row < H_M, g_m, 0.0),
         jnp.where(row < H_M, pltpu.roll(g_m, GATE_GROUP - H_M, axis=0), 0.0),
         slab[o_ff:o_ff + GATE_GROUP],
         jnp.zeros((LANES - 3 * GATE_GROUP, slab.shape[1]), F32)], axis=0)
    wd_o[...] = gates.T.astype(BF16)
    b = bin_ref[...]
    ba_o[...] = b[:, 0:o_mi]
    bb_o[...] = b[:, o_fq:o_ff]
    bc_o[...] = b[:, o_g:]
    bd_o[...] = gate_block(b)


def _prep_in_weights(p, vmem):
    d, n_in = p["w_in"].shape
    w_m = p["w_br_mlstm"].shape[0]
    w_f = p["w_br_fox"].shape[0]
    steps = 8
    o_ff = 4 * w_m + 2 * H_M + 3 * w_f
    assert (4 * w_m) % LANES == 0 and 2 * H_M <= GATE_GROUP + H_M <= LANES
    assert o_ff % LANES <= 2 * GATE_GROUP and o_ff % LANES + H_F <= LANES
    assert n_in == o_ff + H_F + 2 * d
    srcs = [p["w_in"].T, p["b_in"][None, :]]

    def row_spec(shape, tiled=True):
        if not tiled:
            return pl.BlockSpec(shape, lambda i: (0, 0))
        assert shape[0] % (steps * BF16_ROWS) == 0
        return pl.BlockSpec((shape[0] // steps, shape[1]), lambda i: (i, 0))

    out_shapes = [((d, 4 * w_m), BF16), ((d, 3 * w_f), BF16), ((d, 2 * d), BF16), ((d, LANES), BF16),
                  ((1, 4 * w_m), F32), ((1, 3 * w_f), F32), ((1, 2 * d), F32), ((1, LANES), F32)]
    return pl.pallas_call(
        functools.partial(_wprep_kernel, w_m=w_m, w_f=w_f),
        grid=(steps,),
        in_specs=([pl.BlockSpec((n_in, d // steps), lambda i: (0, i))]
                  + [row_spec(s.shape, tiled=s.shape[0] > 1) for s in srcs[1:]]),
        out_specs=[row_spec(s, tiled=s[0] > 1) for s, _ in out_shapes],
        out_shape=[jax.ShapeDtypeStruct(s, dt) for s, dt in out_shapes],
        compiler_params=pltpu.CompilerParams(dimension_semantics=("arbitrary",), vmem_limit_bytes=vmem),
        name="wprep",
    )(*srcs)


def _inproj_kernel(x_ref, g_ref, wa_ref, ba_ref, wb_ref, bb_ref, wc_ref, bc_ref, wd_ref, bd_ref,
                   cw_ref, cb_ref, *rest,
                   w_m, w_f, n_chunk, m_split, q_scale, k_scale, chunk, tk, tiles_per_seq, n_cast):
    cast_srcs, rest = rest[:n_cast], rest[n_cast:]
    (zqk_ref, vt_ref, og_ref, zb_ref, fvt_ref, zc_ref, zdt_ref), rest = rest[:7], rest[7:]
    cast_dsts, (zs_sc,) = rest[:n_cast], rest[n_cast:]
    tm = x_ref.shape[0]
    seq_start = (pl.program_id(0) % tiles_per_seq) == 0

    @pl.when(seq_start)
    def _():
        zs_sc[...] = jnp.zeros(zs_sc.shape, F32)

    rows = tm // m_split
    hb = [_rms(x_ref[r0:r0 + rows, :], g_ref[...]).astype(BF16) for r0 in range(0, tm, rows)]

    def proj(w_ref, b_ref, c0, c1):
        w = w_ref[:, c0:c1]
        parts = [jnp.dot(h, w, preferred_element_type=F32) for h in hb]
        return jnp.concatenate(parts, axis=0) + b_ref[:, c0:c1]

    def qk_chunk(ci):
        c0 = ci * n_chunk
        cs = slice(c0, c0 + n_chunk)
        z = proj(wa_ref, ba_ref, c0, c0 + n_chunk)
        prev = zs_sc[ci]
        zs_sc[ci] = z[tm - SUBLANES:tm]
        zg = z.reshape(tm // SUBLANES, SUBLANES, n_chunk)
        row = lax.broadcasted_iota(jnp.int32, zg.shape, 1)
        w = cw_ref[:, cs]
        y = zg * w[CONV_K - 1:CONV_K] + cb_ref[:, cs]
        for s in range(1, CONV_K):
            rz = pltpu.roll(zg, s, axis=1)
            before = jnp.concatenate([pltpu.roll(prev, s, axis=0)[None], rz[:-1]], axis=0)
            y = y + jnp.where(row < s, before, rz) * w[CONV_K - 1 - s:CONV_K - s]
        act = y * jax.nn.sigmoid(y)
        if c0 >= w_m:
            act = act * k_scale
        zqk_ref[:, cs] = act.reshape(tm, n_chunk).astype(BF16)

    def v_chunk(i):
        c0 = 2 * w_m + i * n_chunk
        z = proj(wa_ref, ba_ref, c0, c0 + n_chunk)
        for cc in range(tm // chunk):
            for f0 in range(0, n_chunk, LANES):
                blk = z[cc * chunk:(cc + 1) * chunk, f0:f0 + LANES]
                r0 = i * n_chunk + f0
                vt_ref[cc, r0:r0 + LANES, :] = blk.T.astype(BF16)

    def og_chunk(i):
        c0 = 3 * w_m + i * n_chunk
        z = proj(wa_ref, ba_ref, c0, c0 + n_chunk)
        og_ref[:, i * n_chunk:(i + 1) * n_chunk] = jax.nn.sigmoid(z).astype(BF16)

    def b_chunk(i):
        c0 = i * n_chunk
        z = proj(wb_ref, bb_ref, c0, c0 + n_chunk)
        if c0 < w_f:
            z = z * q_scale
        if c0 < 2 * w_f:
            zb_ref[:, c0:c0 + n_chunk] = z.astype(BF16)
        else:
            for jb in range(tm // tk):
                for f0 in range(0, n_chunk, LANES):
                    blk = z[jb * tk:(jb + 1) * tk, f0:f0 + LANES]
                    r0 = c0 - 2 * w_f + f0
                    fvt_ref[jb, r0:r0 + LANES, :] = blk.T.astype(BF16)

    def c_chunk(i):
        c0 = i * n_chunk
        z = proj(wc_ref, bc_ref, c0, c0 + n_chunk)
        zc_ref[:, c0:c0 + n_chunk] = jax.nn.sigmoid(z).astype(BF16)

    def d_chunk(_):
        zd = proj(wd_ref, bd_ref, 0, LANES)
        zdt_ref[...] = zd.T[:N_GATE_ROWS, :]

    heavy = [(qk_chunk, i) for i in range(2 * w_m // n_chunk)]
    light = ([(c_chunk, i) for i in range(wc_ref.shape[1] // n_chunk)]
             + [(b_chunk, i) for i in range(3 * w_f // n_chunk)]
             + [(v_chunk, i) for i in range(w_m // n_chunk)]
             + [(og_chunk, i) for i in range(w_m // n_chunk)] + [(d_chunk, 0)])
    per_heavy = len(light) // len(heavy)
    order = []
    for hi, task in enumerate(heavy):
        order.append(task)
        order.extend(light[hi * per_heavy:(hi + 1) * per_heavy])
    order.extend(light[len(heavy) * per_heavy:])
    for fn, i in order:
        fn(i)
    for src, dst in zip(cast_srcs, cast_dsts):
        dst[...] = src[...].astype(BF16)


def _scan_lanes(x, op, fill, seg):
    pos = lax.rem(lax.broadcasted_iota(jnp.int32, x.shape, 1), seg)
    s = 1
    while s < seg:
        shifted = pltpu.roll(x, s, axis=1)
        x = op(x, jnp.where(pos >= s, shifted, fill))
        s *= 2
    return x


def _gates_kernel(zdt_ref, rows_ref, cols_ref, csplit_ref, *, seq):
    z = zdt_ref[...]
    width = z.shape[1]
    i8 = z[0:GATE_GROUP]
    cum = _scan_lanes(_log_sigmoid(z[GATE_GROUP:3 * GATE_GROUP]), jnp.add, 0.0, seq)
    f8 = cum[0:GATE_GROUP]
    cf8 = cum[GATE_GROUP:2 * GATE_GROUP] * LOG2E
    g8 = i8 - f8
    m8 = jnp.maximum(_scan_lanes(g8, jnp.maximum, NEG_BIG, seq), 0.0)
    en8 = jnp.exp(-(f8 + m8))
    g8, m8 = g8 * LOG2E, m8 * LOG2E
    rows = jnp.concatenate([g8, m8, en8], axis=0)
    for bb in range(width // seq):
        rows_ref[bb] = rows[:, bb * seq:(bb + 1) * seq]
    stack = jnp.concatenate(
        [g8, m8, jnp.zeros((LANES - 2 * GATE_GROUP, width), F32)], axis=0)
    cols_ref[...] = stack.T
    hi = cf8.astype(BF16).astype(F32)
    r1 = cf8 - hi
    lo = r1.astype(BF16).astype(F32)
    lo2 = (r1 - lo).astype(BF16).astype(F32)
    ones = jnp.where(lax.broadcasted_iota(jnp.int32, (GATE_GROUP, width), 0) == 0, 1.0, 0.0)
    split = jnp.concatenate(
        [hi, lo, lo2, ones, jnp.zeros((LANES - 4 * GATE_GROUP, width), F32)], axis=0)
    csplit_ref[...] = split.T.astype(BF16)


def _mlstm_kernel(zqk_ref, vt_ref, og_ref, rows_ref, cols_ref, ng_ref, ym_ref, ct_sc, *, w_m, chunk):
    seq = zqk_ref.shape[0]
    dh = w_m // H_M
    nc = seq // chunk
    ct_sc[...] = jnp.zeros(ct_sc.shape, F32)
    hc = chunk // 2
    causal = (lax.broadcasted_iota(jnp.int32, (hc, hc), 0)
              <= lax.broadcasted_iota(jnp.int32, (hc, hc), 1))
    ones_rows = jnp.where(
        lax.broadcasted_iota(jnp.int32, (BF16_ROWS, chunk), 0) == 0, 1.0, 0.0).astype(BF16)

    def body(c, carry):
        r0 = pl.multiple_of(c * chunk, chunk)
        colsc = cols_ref[pl.ds(r0, chunk), :]
        last = cols_ref[pl.ds(r0 + chunk - 1, 1), :]
        prev = cols_ref[pl.ds(jnp.maximum(r0 - 1, 0), 1), :]
        prev = jnp.where(c > 0, prev, 0.0)
        first, vas = [], []
        for h in range(H_M):
            qc = zqk_ref[pl.ds(r0, chunk), h * dh:(h + 1) * dh]
            kc = zqk_ref[pl.ds(r0, chunk), w_m + h * dh:w_m + (h + 1) * dh]
            lhs = jnp.concatenate([kc, ct_sc[h].astype(BF16)], axis=0)
            first.append(lax.dot_general(lhs, qc, NT_DIMS, preferred_element_type=F32))
            vas.append(jnp.concatenate([vt_ref[c, h * dh:(h + 1) * dh, :], ones_rows], axis=0))
        for h in range(H_M):
            kc = zqk_ref[pl.ds(r0, chunk), w_m + h * dh:w_m + (h + 1) * dh]
            g_row = rows_ref[0, h, pl.ds(c, 1), :]
            m_e = last[:, GATE_GROUP + h:GATE_GROUP + h + 1]
            m_p = prev[:, GATE_GROUP + h:GATE_GROUP + h + 1]
            vaw = (vas[h].astype(F32) * jnp.exp2(g_row - m_e)).astype(BF16)
            ct_sc[h] = jnp.exp2(m_p - m_e) * ct_sc[h] + jnp.dot(vaw, kc, preferred_element_type=F32)
        for h in range(H_M):
            ch = slice(h * dh, (h + 1) * dh)
            g_col = colsc[:, h:h + 1]
            m_row = rows_ref[0, GATE_GROUP + h, pl.ds(c, 1), :]
            en_row = rows_ref[0, 2 * GATE_GROUP + h, pl.ds(c, 1), :]
            m_p = prev[:, GATE_GROUP + h:GATE_GROUP + h + 1]
            sk = first[h]
            d00 = jnp.where(causal, jnp.exp2(g_col[0:hc] - m_row[:, 0:hc]), 0.0)
            d01 = jnp.exp2(g_col[0:hc] - m_row[:, hc:chunk])
            d11 = jnp.where(causal, jnp.exp2(g_col[hc:chunk] - m_row[:, hc:chunk]), 0.0)
            top = jnp.concatenate([sk[0:hc, 0:hc] * d00, sk[0:hc, hc:chunk] * d01], axis=1)
            bot = jnp.concatenate([jnp.zeros((hc, hc), F32), sk[hc:chunk, hc:chunk] * d11], axis=1)
            sqk = jnp.concatenate([top, bot], axis=0).astype(BF16)
            nd = (jnp.exp2(m_p - m_row) * first[h][chunk:]
                  + jnp.dot(vas[h], sqk, preferred_element_type=F32))
            den = nd[dh:dh + 1]
            ht = nd[0:dh] * (1.0 / jnp.maximum(jnp.abs(den), en_row))
            hn = ht * lax.rsqrt(jnp.mean(ht * ht, axis=0, keepdims=True) + EPS)
            og = og_ref[pl.ds(r0, chunk), ch].astype(F32)
            ym_ref[pl.ds(r0, chunk), ch] = (hn.T * ng_ref[:, ch] * og).astype(BF16)
        return carry

    lax.fori_loop(0, nc, body, 0, unroll=True)


def _fox_kernel(q_ref, k_ref, fvt_ref, csq_ref, csk_ref, qsel_ref, ksel_ref, yf_ref,
                kaug_sc, qaug_sc, m_sc, acc_sc, *, tq, tk):
    qi = pl.program_id(1)

    seq = k_ref.shape[0]
    dhp = LANES
    half = dhp // 2
    v_rows = half + BF16_ROWS

    def own_lanes(head, rows):
        lane = lax.broadcasted_iota(jnp.int32, (rows, dhp), 1)
        return (lane >= half) if head % 2 else (lane < half)

    def augment(x_ref, cs_ref, sel_ref, dst_sc, rows):
        cs = cs_ref[...]
        for p in range(H_F // 2):
            xp = x_ref[:, p * dhp:(p + 1) * dhp]
            bias = jnp.dot(cs, sel_ref[p], preferred_element_type=F32).astype(BF16)
            for head in (2 * p, 2 * p + 1):
                b0 = (head % 2) * dhp
                dst_sc[head] = jnp.where(own_lanes(head, rows), xp, bias[:, b0:b0 + dhp])

    @pl.when(qi == 0)
    def _():
        augment(k_ref, csk_ref, ksel_ref, kaug_sc, seq)

    augment(q_ref, csq_ref, qsel_ref, qaug_sc, tq)
    m_sc[...] = jnp.full(m_sc.shape, NEG_BIG, F32)
    acc_sc[...] = jnp.zeros(acc_sc.shape, F32)

    ratio = tq // tk
    ones_rows = jnp.where(
        lax.broadcasted_iota(jnp.int32, (BF16_ROWS, tk), 0) == 0, 1.0, 0.0).astype(BF16)

    def vaug(head, j):
        return jnp.concatenate([fvt_ref[j, head * half:(head + 1) * half, :], ones_rows], axis=0)

    def step(j, diag):
        q0 = 0 if diag is None else diag * tk
        nq = tq - q0
        k0 = pl.multiple_of(j * tk, tk)
        sts = [lax.dot_general(kaug_sc[head, pl.ds(k0, tk), :], qaug_sc[head, q0:tq, :], NT_DIMS,
                               preferred_element_type=F32) for head in range(H_F)]
        if diag is not None:
            causal = (lax.broadcasted_iota(jnp.int32, (tk, nq), 0)
                      <= lax.broadcasted_iota(jnp.int32, (tk, nq), 1))
        for head in range(H_F):
            st = sts[head]
            if diag is not None:
                st = jnp.where(causal, st, NEG_BIG)
            m = m_sc[head, :, q0:tq]
            m_new = jnp.maximum(m, jnp.max(st, axis=0, keepdims=True))
            alpha = jnp.exp2(m - m_new)
            pt = jnp.exp2(st - m_new).astype(BF16)
            m_sc[head, :, q0:tq] = m_new
            acc_sc[head, :, q0:tq] = (alpha * acc_sc[head, :, q0:tq]
                                      + jnp.dot(vaug(head, j), pt, preferred_element_type=F32))

    def loop_body(jj, carry):
        for r in range(ratio):
            step(jj * ratio + r, None)
        return carry

    lax.fori_loop(0, qi, loop_body, 0)
    for diag in range(ratio):
        step(qi * ratio + diag, diag)
    for p in range(H_F // 2):
        outs = []
        for head in (2 * p, 2 * p + 1):
            acc = acc_sc[head]
            outs.append(acc[0:half] * (1.0 / acc[half:half + 1]))
        yf_ref[:, p * dhp:(p + 1) * dhp] = jnp.concatenate(outs, axis=0).T.astype(BF16)


def _post_kernel(x_ref, ym_ref, yf_ref, zc_ref, wbm_ref, wbf_ref, wo_ref, g2_ref,
                 wg_ref, wu_ref, wd_ref, gfin_ref, o_ref, *, tf):
    d = x_ref.shape[1]
    d_ff = wg_ref.shape[1]
    bm = jnp.dot(ym_ref[...], wbm_ref[...], preferred_element_type=F32)
    bf = jnp.dot(yf_ref[...], wbf_ref[...], preferred_element_type=F32)
    mix = zc_ref[:, 0:d].astype(F32) * bm + zc_ref[:, d:2 * d].astype(F32) * bf
    x1 = x_ref[...] + jnp.dot(mix.astype(BF16), wo_ref[...], preferred_element_type=F32)
    h2 = _rms(x1, g2_ref[...]).astype(BF16)
    acc = jnp.zeros(x1.shape, F32)
    for f0 in range(0, d_ff, tf):
        g = jnp.dot(h2, wg_ref[:, f0:f0 + tf], preferred_element_type=F32)
        u = jnp.dot(h2, wu_ref[:, f0:f0 + tf], preferred_element_type=F32)
        act = (g * jax.nn.sigmoid(g) * u).astype(BF16)
        acc = acc + jnp.dot(act, wd_ref[f0:f0 + tf, :], preferred_element_type=F32)
    o_ref[...] = _rms(x1 + acc, gfin_ref[...])


def _bias_selectors():
    ones_lane = 3 * GATE_GROUP
    qsel = np.zeros((H_F // 2, LANES, 2 * LANES), np.float32)
    ksel = np.zeros((H_F // 2, LANES, 2 * LANES), np.float32)
    for h in range(H_F):
        p0 = (h % 2) * LANES + (LANES // 2 if h % 2 == 0 else 0)
        for c in range(3):
            qsel[h // 2, GATE_GROUP * c + h, p0 + c] = 1.0
            qsel[h // 2, ones_lane, p0 + 3 + c] = 1.0
            ksel[h // 2, ones_lane, p0 + c] = 1.0
            ksel[h // 2, GATE_GROUP * c + h, p0 + 3 + c] = -1.0
    return jnp.asarray(qsel, BF16), jnp.asarray(ksel, BF16)


def _layer(x2d, batch, seq, p, cfg):
    t, d = x2d.shape
    w_m = p["w_br_mlstm"].shape[0]
    w_f = p["w_br_fox"].shape[0]
    dh_m = w_m // H_M
    dh_f = w_f // H_F
    vmem = cfg["vmem_limit"]

    wa, wb, wc, wd, ba, bb, bc, bd = _prep_in_weights(p, vmem)

    tm = cfg["tm_in"]
    chunk = cfg["chunk"]
    tq, tk = cfg["tq"], cfg["tk"]
    nc = seq // chunk
    n_steps = t // tm
    assert seq % tm == 0 and tm % chunk == 0 and tm % tk == 0

    later = [p["w_br_mlstm"], p["w_br_fox"], p["w_out"], p["w_gate"], p["w_up"], p["w_down"]]

    def slice_spec(rows, cols):
        hold = 1
        while (rows * hold) % (n_steps * BF16_ROWS):
            hold *= 2
        assert hold <= n_steps
        return pl.BlockSpec((rows * hold // n_steps, cols), lambda i: (i // hold, 0))

    later_specs = [slice_spec(*w.shape) for w in later]
    outs = pl.pallas_call(
        functools.partial(_inproj_kernel, w_m=w_m, w_f=w_f, n_chunk=cfg["n_chunk"],
                          m_split=cfg["m_split"],
                          q_scale=dh_f ** -0.5 * LOG2E, k_scale=dh_m ** -0.5, chunk=chunk,
                          tk=tk, tiles_per_seq=seq // tm, n_cast=len(later)),
        grid=(n_steps,),
        in_specs=[pl.BlockSpec((tm, d), lambda i: (i, 0)), _const_spec((1, d)),
                  _const_spec(wa.shape), _const_spec(ba.shape),
                  _const_spec(wb.shape), _const_spec(bb.shape),
                  _const_spec(wc.shape), _const_spec(bc.shape),
                  _const_spec(wd.shape), _const_spec(bd.shape),
                  _const_spec((CONV_K, 2 * w_m)), _const_spec((1, 2 * w_m))] + later_specs,
        out_specs=[pl.BlockSpec((tm, 2 * w_m), lambda i: (i, 0)),
                   pl.BlockSpec((tm // chunk, w_m, chunk), lambda i: (i, 0, 0)),
                   pl.BlockSpec((tm, w_m), lambda i: (i, 0)),
                   pl.BlockSpec((tm, 2 * w_f), lambda i: (i, 0)),
                   pl.BlockSpec((tm // tk, w_f, tk), lambda i: (i, 0, 0)),
                   pl.BlockSpec((tm, 2 * d), lambda i: (i, 0)),
                   pl.BlockSpec((N_GATE_ROWS, tm), lambda i: (0, i))] + later_specs,
        out_shape=[jax.ShapeDtypeStruct((t, 2 * w_m), BF16),
                   jax.ShapeDtypeStruct((t // chunk, w_m, chunk), BF16),
                   jax.ShapeDtypeStruct((t, w_m), BF16),
                   jax.ShapeDtypeStruct((t, 2 * w_f), BF16),
                   jax.ShapeDtypeStruct((t // tk, w_f, tk), BF16),
                   jax.ShapeDtypeStruct((t, 2 * d), BF16),
                   jax.ShapeDtypeStruct((N_GATE_ROWS, t), F32)]
        + [jax.ShapeDtypeStruct(w.shape, BF16) for w in later],
        scratch_shapes=[pltpu.VMEM((2 * w_m // cfg["n_chunk"], SUBLANES, cfg["n_chunk"]), F32)],
        compiler_params=pltpu.CompilerParams(dimension_semantics=("arbitrary",), vmem_limit_bytes=vmem),
        name="inproj",
    )(x2d, p["norm1_g"][None, :], wa, ba, wb, bb, wc, bc, wd, bd, p["conv_w"], p["conv_b"][None, :],
      *later)
    zqk, vt, og, zb, fvt, zc, zdt, wbm, wbf, wo, wg, wu, wdn = outs

    nb = cfg["gates_batch"]
    assert batch % nb == 0
    rows, cols, csplit = pl.pallas_call(
        functools.partial(_gates_kernel, seq=seq),
        grid=(batch // nb,),
        in_specs=[pl.BlockSpec((N_GATE_ROWS, nb * seq), lambda b: (0, b))],
        out_specs=[pl.BlockSpec((nb, 3 * GATE_GROUP, seq), lambda b: (b, 0, 0)),
                   pl.BlockSpec((nb * seq, LANES), lambda b: (b, 0)),
                   pl.BlockSpec((nb * seq, LANES), lambda b: (b, 0))],
        out_shape=[jax.ShapeDtypeStruct((batch, 3 * GATE_GROUP, seq), F32),
                   jax.ShapeDtypeStruct((t, LANES), F32),
                   jax.ShapeDtypeStruct((t, LANES), BF16)],
        compiler_params=pltpu.CompilerParams(dimension_semantics=("parallel",), vmem_limit_bytes=vmem),
        name="gates",
    )(zdt)

    rows_m = rows.reshape(batch, 3 * GATE_GROUP, nc, chunk)
    ym = pl.pallas_call(
        functools.partial(_mlstm_kernel, w_m=w_m, chunk=chunk),
        grid=(batch,),
        in_specs=[pl.BlockSpec((seq, 2 * w_m), lambda b: (b, 0)),
                  pl.BlockSpec((nc, w_m, chunk), lambda b: (b, 0, 0)),
                  pl.BlockSpec((seq, w_m), lambda b: (b, 0)),
                  pl.BlockSpec((1, 3 * GATE_GROUP, nc, chunk), lambda b: (b, 0, 0, 0)),
                  pl.BlockSpec((seq, LANES), lambda b: (b, 0)),
                  _const_spec((1, w_m))],
        out_specs=pl.BlockSpec((seq, w_m), lambda b: (b, 0)),
        out_shape=jax.ShapeDtypeStruct((t, w_m), BF16),
        scratch_shapes=[pltpu.VMEM((H_M, dh_m + BF16_ROWS, dh_m), F32)],
        compiler_params=pltpu.CompilerParams(dimension_semantics=("parallel",), vmem_limit_bytes=vmem),
        name="mlstm",
    )(zqk, vt, og, rows_m, cols, p["mlstm_norm_g"][None, :])

    assert tq % tk == 0 and seq % tq == 0
    nq = seq // tq
    qsel, ksel = _bias_selectors()
    v_rows = LANES // 2 + BF16_ROWS
    yf = pl.pallas_call(
        functools.partial(_fox_kernel, tq=tq, tk=tk),
        grid=(batch, nq),
        in_specs=[pl.BlockSpec((tq, w_f), lambda b, i: (b * nq + i, 0)),
                  pl.BlockSpec((seq, w_f), lambda b, i: (b, 1)),
                  pl.BlockSpec((seq // tk, w_f, tk), lambda b, i: (b, 0, 0)),
                  pl.BlockSpec((tq, LANES), lambda b, i: (b * nq + i, 0)),
                  pl.BlockSpec((seq, LANES), lambda b, i: (b, 0)),
                  _const_spec(qsel.shape), _const_spec(ksel.shape)],
        out_specs=pl.BlockSpec((tq, w_f), lambda b, i: (b * nq + i, 0)),
        out_shape=jax.ShapeDtypeStruct((t, w_f), BF16),
        scratch_shapes=[pltpu.VMEM((H_F, seq, LANES), BF16),
                        pltpu.VMEM((H_F, tq, LANES), BF16),
                        pltpu.VMEM((H_F, 1, tq), F32),
                        pltpu.VMEM((H_F, v_rows, tq), F32)],
        compiler_params=pltpu.CompilerParams(dimension_semantics=("arbitrary", "arbitrary"),
                                             vmem_limit_bytes=vmem),
        name="fox",
    )(zb, zb, fvt, csplit, csplit, qsel, ksel)

    tmp = cfg["tm_post"]
    return pl.pallas_call(
        functools.partial(_post_kernel, tf=cfg["tf"]),
        grid=(t // tmp,),
        in_specs=[pl.BlockSpec((tmp, d), lambda i: (i, 0)),
                  pl.BlockSpec((tmp, w_m), lambda i: (i, 0)),
                  pl.BlockSpec((tmp, w_f), lambda i: (i, 0)),
                  pl.BlockSpec((tmp, 2 * d), lambda i: (i, 0)),
                  _const_spec(wbm.shape), _const_spec(wbf.shape), _const_spec(wo.shape),
                  _const_spec((1, d)), _const_spec(wg.shape), _const_spec(wu.shape),
                  _const_spec(wdn.shape), _const_spec((1, d))],
        out_specs=pl.BlockSpec((tmp, d), lambda i: (i, 0)),
        out_shape=jax.ShapeDtypeStruct((t, d), F32),
        compiler_params=pltpu.CompilerParams(dimension_semantics=("parallel",), vmem_limit_bytes=vmem),
        name="post",
    )(x2d, ym, yf, zc, wbm, wbf, wo, p["norm2_g"][None, :], wg, wu, wdn, p["norm_f_g"][None, :])


def kernel(x, norm1_g, w_in, b_in, conv_w, conv_b, mlstm_norm_g, w_br_mlstm, w_br_fox, w_out,
           norm2_g, w_gate, w_up, w_down, norm_f_g):
    batch, seq, d = x.shape
    depth = w_in.shape[0]
    assert depth == 1, "the final norm is fused into the single layer's last call"
    cfg = _cfg(batch, seq, d, w_gate.shape[-1])
    p = dict(norm1_g=norm1_g[0], w_in=w_in[0], b_in=b_in[0], conv_w=conv_w[0], conv_b=conv_b[0],
             mlstm_norm_g=mlstm_norm_g[0], w_br_mlstm=w_br_mlstm[0], w_br_fox=w_br_fox[0],
             w_out=w_out[0], norm2_g=norm2_g[0], w_gate=w_gate[0], w_up=w_up[0], w_down=w_down[0],
             norm_f_g=norm_f_g)
    out = _layer(x.reshape(batch * seq, d), batch, seq, p, cfg)
    return out.reshape(batch, seq, d)
```

```python
import functools

import jax
import jax.numpy as jnp
import numpy as np
from jax import lax
from jax.experimental import pallas as pl
from jax.experimental.pallas import tpu as pltpu

EPS = 1e-6
H_M = 4
H_F = 8
CONV_K = 4

LANES = 128
SUBLANES = 8
BF16_ROWS = 16
GATE_GROUP = 8
N_GATE_ROWS = 3 * GATE_GROUP
V7X_VMEM_BYTES = 64 * 1024 * 1024
VMEM_COMPILER_RESERVE = 8 * 1024 * 1024
NEG_BIG = -1e30
LOG2E = 1.4426950408889634

F32 = jnp.float32
BF16 = jnp.bfloat16
NT_DIMS = (((1,), (1,)), ((), ()))


def _cfg(batch, seq, d_model, d_ff):
    return dict(
        tm_in=1024,
        n_chunk=256,
        m_split=8,
        chunk=256,
        tq=512,
        tk=256,
        tm_post=512,
        tf=256,
        vmem_limit=V7X_VMEM_BYTES - VMEM_COMPILER_RESERVE,
    )


def _const_spec(shape):
    nd = len(shape)
    return pl.BlockSpec(shape, lambda *_: (0,) * nd, pipeline_mode=pl.Buffered(1))


def _rms(x, g):
    return x * lax.rsqrt(jnp.mean(x * x, axis=-1, keepdims=True) + EPS) * g


def _log_sigmoid(x):
    return jnp.minimum(x, 0.0) - jnp.log1p(jnp.exp(-jnp.abs(x)))


def _wprep_kernel(wint_ref, bin_ref, wa_o, wb_o, wc_o, wd_o, ba_o, bb_o, bc_o, bd_o, *, w_m, w_f):
    o_mi = 4 * w_m
    o_mf = o_mi + H_M
    o_fq = o_mf + H_M
    o_ff = o_fq + 3 * w_f
    o_g = o_ff + H_F

    def gate_block(src):
        rows = src.shape[0]
        lane = lax.broadcasted_iota(jnp.int32, (rows, LANES), 1)
        blk_m = src[:, o_mi:o_mi + LANES]
        f0 = (o_ff // LANES) * LANES
        blk_f = src[:, f0:f0 + LANES]
        mi = jnp.where(lane < H_M, blk_m, 0.0)
        mf = jnp.where((lane >= GATE_GROUP) & (lane < GATE_GROUP + H_M),
                       pltpu.roll(blk_m, GATE_GROUP - H_M, axis=1), 0.0)
        ff = jnp.where((lane >= 2 * GATE_GROUP) & (lane < 2 * GATE_GROUP + H_F),
                       pltpu.roll(blk_f, 2 * GATE_GROUP - (o_ff - f0), axis=1), 0.0)
        return mi + mf + ff

    slab = wint_ref[...]
    wa_o[...] = slab[0:o_mi].T.astype(BF16)
    wb_o[...] = slab[o_fq:o_ff].T.astype(BF16)
    wc_o[...] = slab[o_g:].T.astype(BF16)
    g_m = slab[o_mi:o_mi + GATE_GROUP]
    row = lax.broadcasted_iota(jnp.int32, g_m.shape, 0)
    gates = jnp.concatenate(
        [jnp.where(row < H_M, g_m, 0.0),
         jnp.where(row < H_M, pltpu.roll(g_m, GATE_GROUP - H_M, axis=0), 0.0),
         slab[o_ff:o_ff + GATE_GROUP],
         jnp.zeros((LANES - 3 * GATE_GROUP, slab.shape[1]), F32)], axis=0)
    wd_o[...] = gates.T.astype(BF16)
    b = bin_ref[...]
    ba_o[...] = b[:, 0:o_mi]
    bb_o[...] = b[:, o_fq:o_ff]
    bc_o[...] = b[:, o_g:]
    bd_o[...] = gate_block(b)


def _prep_in_weights(p, vmem):
    d, n_in = p["w_in"].shape
    w_m = p["w_br_mlstm"].shape[0]
    w_f = p["w_br_fox"].shape[0]
    steps = 8
    o_ff = 4 * w_m + 2 * H_M + 3 * w_f
    assert (4 * w_m) % LANES == 0 and 2 * H_M <= GATE_GROUP + H_M <= LANES
    assert o_ff % LANES <= 2 * GATE_GROUP and o_ff % LANES + H_F <= LANES
    assert n_in == o_ff + H_F + 2 * d
    srcs = [p["w_in"].T, p["b_in"][None, :]]

    def row_spec(shape, tiled=True):
        if not tiled:
            return pl.BlockSpec(shape, lambda i: (0, 0))
        assert shape[0] % (steps * BF16_ROWS) == 0
        return pl.BlockSpec((shape[0] // steps, shape[1]), lambda i: (i, 0))

    out_shapes = [((d, 4 * w_m), BF16), ((d, 3 * w_f), BF16), ((d, 2 * d), BF16), ((d, LANES), BF16),
                  ((1, 4 * w_m), F32), ((1, 3 * w_f), F32), ((1, 2 * d), F32), ((1, LANES), F32)]
    return pl.pallas_call(
        functools.partial(_wprep_kernel, w_m=w_m, w_f=w_f),
        grid=(steps,),
        in_specs=([pl.BlockSpec((n_in, d // steps), lambda i: (0, i))]
                  + [row_spec(s.shape, tiled=s.shape[0] > 1) for s in srcs[1:]]),
        out_specs=[row_spec(s, tiled=s[0] > 1) for s, _ in out_shapes],
        out_shape=[jax.ShapeDtypeStruct(s, dt) for s, dt in out_shapes],
        compiler_params=pltpu.CompilerParams(dimension_semantics=("arbitrary",), vmem_limit_bytes=vmem),
        name="wprep",
    )(*srcs)


def _inproj_kernel(x_ref, g_ref, wa_ref, ba_ref, wb_ref, bb_ref, wc_ref, bc_ref, wd_ref, bd_ref,
                   cw_ref, cb_ref, *rest,
                   w_m, w_f, n_chunk, m_split, q_scale, k_scale, chunk, tk, tiles_per_seq, n_cast):
    cast_srcs, rest = rest[:n_cast], rest[n_cast:]
    (zqk_ref, vt_ref, og_ref, zb_ref, fvt_ref, zc_ref, rows_ref, cols_ref, csplit_ref), rest = (
        rest[:9], rest[9:])
    cast_dsts, (zs_sc, gc_sc) = rest[:n_cast], rest[n_cast:]
    tm = x_ref.shape[0]
    seq_start = (pl.program_id(0) % tiles_per_seq) == 0

    @pl.when(seq_start)
    def _():
        zs_sc[...] = jnp.zeros(zs_sc.shape, F32)
        gc_sc[0:2 * GATE_GROUP] = jnp.zeros((2 * GATE_GROUP, LANES), F32)
        gc_sc[2 * GATE_GROUP:] = jnp.full((GATE_GROUP, LANES), NEG_BIG, F32)

    rows = tm // m_split
    hb = [_rms(x_ref[r0:r0 + rows, :], g_ref[...]).astype(BF16) for r0 in range(0, tm, rows)]

    def proj(w_ref, b_ref, c0, c1):
        w = w_ref[:, c0:c1]
        parts = [jnp.dot(h, w, preferred_element_type=F32) for h in hb]
        return jnp.concatenate(parts, axis=0) + b_ref[:, c0:c1]

    def qk_chunk(ci):
        c0 = ci * n_chunk
        cs = slice(c0, c0 + n_chunk)
        z = proj(wa_ref, ba_ref, c0, c0 + n_chunk)
        prev = zs_sc[ci]
        zs_sc[ci] = z[tm - SUBLANES:tm]
        zg = z.reshape(tm // SUBLANES, SUBLANES, n_chunk)
        row = lax.broadcasted_iota(jnp.int32, zg.shape, 1)
        w = cw_ref[:, cs]
        y = zg * w[CONV_K - 1:CONV_K] + cb_ref[:, cs]
        for s in range(1, CONV_K):
            rz = pltpu.roll(zg, s, axis=1)
            before = jnp.concatenate([pltpu.roll(prev, s, axis=0)[None], rz[:-1]], axis=0)
            y = y + jnp.where(row < s, before, rz) * w[CONV_K - 1 - s:CONV_K - s]
        act = y * jax.nn.sigmoid(y)
        if c0 >= w_m:
            act = act * k_scale
        zqk_ref[:, cs] = act.reshape(tm, n_chunk).astype(BF16)

    def v_chunk(i):
        c0 = 2 * w_m + i * n_chunk
        z = proj(wa_ref, ba_ref, c0, c0 + n_chunk)
        for cc in range(tm // chunk):
            for f0 in range(0, n_chunk, LANES):
                blk = z[cc * chunk:(cc + 1) * chunk, f0:f0 + LANES]
                r0 = i * n_chunk + f0
                vt_ref[cc, r0:r0 + LANES, :] = blk.T.astype(BF16)

    def og_chunk(i):
        c0 = 3 * w_m + i * n_chunk
        z = proj(wa_ref, ba_ref, c0, c0 + n_chunk)
        og_ref[:, i * n_chunk:(i + 1) * n_chunk] = jax.nn.sigmoid(z).astype(BF16)

    def b_chunk(i):
        c0 = i * n_chunk
        z = proj(wb_ref, bb_ref, c0, c0 + n_chunk)
        if c0 < w_f:
            z = z * q_scale
        if c0 < 2 * w_f:
            zb_ref[:, c0:c0 + n_chunk] = z.astype(BF16)
        else:
            for jb in range(tm // tk):
                for f0 in range(0, n_chunk, LANES):
                    blk = z[jb * tk:(jb + 1) * tk, f0:f0 + LANES]
                    r0 = c0 - 2 * w_f + f0
                    fvt_ref[jb, r0:r0 + LANES, :] = blk.T.astype(BF16)

    def c_chunk(i):
        c0 = i * n_chunk
        z = proj(wc_ref, bc_ref, c0, c0 + n_chunk)
        zc_ref[:, c0:c0 + n_chunk] = jax.nn.sigmoid(z).astype(BF16)

    def d_chunk(_):
        zd = proj(wd_ref, bd_ref, 0, LANES)
        rows, cols, csplit, gc_sc[...] = _gate_tables(zd.T[:N_GATE_ROWS, :], gc_sc[...])
        rows_ref[0] = rows
        cols_ref[...] = cols
        csplit_ref[...] = csplit

    heavy = [(qk_chunk, i) for i in range(2 * w_m // n_chunk)]
    light = ([(c_chunk, i) for i in range(wc_ref.shape[1] // n_chunk)]
             + [(b_chunk, i) for i in range(3 * w_f // n_chunk)]
             + [(v_chunk, i) for i in range(w_m // n_chunk)]
             + [(og_chunk, i) for i in range(w_m // n_chunk)])
    per_heavy = len(light) // len(heavy)
    order = [(d_chunk, 0)]
    for hi, task in enumerate(heavy):
        order.append(task)
        order.extend(light[hi * per_heavy:(hi + 1) * per_heavy])
    order.extend(light[len(heavy) * per_heavy:])
    for fn, i in order:
        fn(i)
    for src, dst in zip(cast_srcs, cast_dsts):
        dst[...] = src[...].astype(BF16)


def _scan_lanes(x, op, fill):
    n = x.shape[-1]
    pos = lax.broadcasted_iota(jnp.int32, x.shape, 1)
    s = 1
    while s < n:
        shifted = pltpu.roll(x, s, axis=1)
        x = op(x, jnp.where(pos >= s, shifted, fill))
        s *= 2
    return x


def _gate_tables(z, carry):
    width = z.shape[1]
    i8 = z[0:GATE_GROUP]
    cum = (_scan_lanes(_log_sigmoid(z[GATE_GROUP:3 * GATE_GROUP]), jnp.add, 0.0)
           + carry[0:2 * GATE_GROUP, 0:1])
    f8 = cum[0:GATE_GROUP]
    g8 = i8 - f8
    cmax = jnp.maximum(_scan_lanes(g8, jnp.maximum, NEG_BIG), carry[2 * GATE_GROUP:, 0:1])
    new_carry = jnp.concatenate(
        [jnp.broadcast_to(cum[:, width - 1:width], (2 * GATE_GROUP, LANES)),
         jnp.broadcast_to(cmax[:, width - 1:width], (GATE_GROUP, LANES))], axis=0)
    m8 = jnp.maximum(cmax, 0.0)
    en8 = jnp.exp(-(f8 + m8))
    cf8 = cum[GATE_GROUP:2 * GATE_GROUP] * LOG2E
    g8, m8 = g8 * LOG2E, m8 * LOG2E
    rows = jnp.concatenate([g8, m8, en8], axis=0)
    stack = jnp.concatenate(
        [g8, m8, jnp.zeros((LANES - 2 * GATE_GROUP, width), F32)], axis=0)
    cols = stack.T
    hi = cf8.astype(BF16).astype(F32)
    r1 = cf8 - hi
    lo = r1.astype(BF16).astype(F32)
    lo2 = (r1 - lo).astype(BF16).astype(F32)
    ones = jnp.where(lax.broadcasted_iota(jnp.int32, (GATE_GROUP, width), 0) == 0, 1.0, 0.0)
    split = jnp.concatenate(
        [hi, lo, lo2, ones, jnp.zeros((LANES - 4 * GATE_GROUP, width), F32)], axis=0)
    return rows, cols, split.T.astype(BF16), new_carry


def _mlstm_kernel(zqk_ref, vt_ref, og_ref, rows_ref, cols_ref, ng_ref, ym_ref, ct_sc, *, w_m, chunk):
    seq = zqk_ref.shape[0]
    dh = w_m // H_M
    nc = seq // chunk
    ct_sc[...] = jnp.zeros(ct_sc.shape, F32)
    hc = chunk // 2
    causal = (lax.broadcasted_iota(jnp.int32, (hc, hc), 0)
              <= lax.broadcasted_iota(jnp.int32, (hc, hc), 1))
    ones_rows = jnp.where(
        lax.broadcasted_iota(jnp.int32, (BF16_ROWS, chunk), 0) == 0, 1.0, 0.0).astype(BF16)

    def body(c, carry):
        r0 = pl.multiple_of(c * chunk, chunk)
        colsc = cols_ref[pl.ds(r0, chunk), :]
        last = cols_ref[pl.ds(r0 + chunk - 1, 1), :]
        prev = cols_ref[pl.ds(jnp.maximum(r0 - 1, 0), 1), :]
        prev = jnp.where(c > 0, prev, 0.0)
        first, vas = [], []
        for h in range(H_M):
            qc = zqk_ref[pl.ds(r0, chunk), h * dh:(h + 1) * dh]
            kc = zqk_ref[pl.ds(r0, chunk), w_m + h * dh:w_m + (h + 1) * dh]
            lhs = jnp.concatenate([kc, ct_sc[h].astype(BF16)], axis=0)
            first.append(lax.dot_general(lhs, qc, NT_DIMS, preferred_element_type=F32))
            vas.append(jnp.concatenate([vt_ref[c, h * dh:(h + 1) * dh, :], ones_rows], axis=0))
        for h in range(H_M):
            kc = zqk_ref[pl.ds(r0, chunk), w_m + h * dh:w_m + (h + 1) * dh]
            g_row = rows_ref[0, h, pl.ds(c, 1), :]
            m_e = last[:, GATE_GROUP + h:GATE_GROUP + h + 1]
            m_p = prev[:, GATE_GROUP + h:GATE_GROUP + h + 1]
            vaw = (vas[h].astype(F32) * jnp.exp2(g_row - m_e)).astype(BF16)
            ct_sc[h] = jnp.exp2(m_p - m_e) * ct_sc[h] + jnp.dot(vaw, kc, preferred_element_type=F32)
        for h in range(H_M):
            ch = slice(h * dh, (h + 1) * dh)
            g_col = colsc[:, h:h + 1]
            m_row = rows_ref[0, GATE_GROUP + h, pl.ds(c, 1), :]
            en_row = rows_ref[0, 2 * GATE_GROUP + h, pl.ds(c, 1), :]
            m_p = prev[:, GATE_GROUP + h:GATE_GROUP + h + 1]
            sk = first[h]
            d00 = jnp.where(causal, jnp.exp2(g_col[0:hc] - m_row[:, 0:hc]), 0.0)
            d01 = jnp.exp2(g_col[0:hc] - m_row[:, hc:chunk])
            d11 = jnp.where(causal, jnp.exp2(g_col[hc:chunk] - m_row[:, hc:chunk]), 0.0)
            top = jnp.concatenate([sk[0:hc, 0:hc] * d00, sk[0:hc, hc:chunk] * d01], axis=1)
            bot = jnp.concatenate([jnp.zeros((hc, hc), F32), sk[hc:chunk, hc:chunk] * d11], axis=1)
            sqk = jnp.concatenate([top, bot], axis=0).astype(BF16)
            nd = (jnp.exp2(m_p - m_row) * first[h][chunk:]
                  + jnp.dot(vas[h], sqk, preferred_element_type=F32))
            den = nd[dh:dh + 1]
            ht = nd[0:dh] * (1.0 / jnp.maximum(jnp.abs(den), en_row))
            hn = ht * lax.rsqrt(jnp.mean(ht * ht, axis=0, keepdims=True) + EPS)
            og = og_ref[pl.ds(r0, chunk), ch].astype(F32)
            ym_ref[pl.ds(r0, chunk), ch] = (hn.T * ng_ref[:, ch] * og).astype(BF16)
        return carry

    lax.fori_loop(0, nc, body, 0, unroll=True)


def _fox_kernel(q_ref, k_ref, fvt_ref, csq_ref, csk_ref, qsel_ref, ksel_ref, yf_ref,
                kaug_sc, qaug_sc, m_sc, acc_sc, *, tq, tk):
    qi = pl.program_id(1)

    seq = k_ref.shape[0]
    dhp = LANES
    half = dhp // 2
    v_rows = half + BF16_ROWS

    def own_lanes(head, rows):
        lane = lax.broadcasted_iota(jnp.int32, (rows, dhp), 1)
        return (lane >= half) if head % 2 else (lane < half)

    def augment(x_ref, cs_ref, sel_ref, dst_sc, rows):
        cs = cs_ref[...]
        for p in range(H_F // 2):
            xp = x_ref[:, p * dhp:(p + 1) * dhp]
            bias = jnp.dot(cs, sel_ref[p], preferred_element_type=F32).astype(BF16)
            for head in (2 * p, 2 * p + 1):
                b0 = (head % 2) * dhp
                dst_sc[head] = jnp.where(own_lanes(head, rows), xp, bias[:, b0:b0 + dhp])

    @pl.when(qi == 0)
    def _():
        augment(k_ref, csk_ref, ksel_ref, kaug_sc, seq)

    augment(q_ref, csq_ref, qsel_ref, qaug_sc, tq)
    m_sc[...] = jnp.full(m_sc.shape, NEG_BIG, F32)
    acc_sc[...] = jnp.zeros(acc_sc.shape, F32)

    ratio = tq // tk
    ones_rows = jnp.where(
        lax.broadcasted_iota(jnp.int32, (BF16_ROWS, tk), 0) == 0, 1.0, 0.0).astype(BF16)

    def vaug(head, j):
        return jnp.concatenate([fvt_ref[j, head * half:(head + 1) * half, :], ones_rows], axis=0)

    def step(j, diag):
        q0 = 0 if diag is None else diag * tk
        nq = tq - q0
        k0 = pl.multiple_of(j * tk, tk)
        sts = [lax.dot_general(kaug_sc[head, pl.ds(k0, tk), :], qaug_sc[head, q0:tq, :], NT_DIMS,
                               preferred_element_type=F32) for head in range(H_F)]
        if diag is not None:
            causal = (lax.broadcasted_iota(jnp.int32, (tk, nq), 0)
                      <= lax.broadcasted_iota(jnp.int32, (tk, nq), 1))
        for head in range(H_F):
            st = sts[head]
            if diag is not None:
                st = jnp.where(causal, st, NEG_BIG)
            m = m_sc[head, :, q0:tq]
            m_new = jnp.maximum(m, jnp.max(st, axis=0, keepdims=True))
            alpha = jnp.exp2(m - m_new)
            pt = jnp.exp2(st - m_new).astype(BF16)
            m_sc[head, :, q0:tq] = m_new
            acc_sc[head, :, q0:tq] = (alpha * acc_sc[head, :, q0:tq]
                                      + jnp.dot(vaug(head, j), pt, preferred_element_type=F32))

    def loop_body(jj, carry):
        for r in range(ratio):
            step(jj * ratio + r, None)
        return carry

    lax.fori_loop(0, qi, loop_body, 0)
    for diag in range(ratio):
        step(qi * ratio + diag, diag)
    for p in range(H_F // 2):
        outs = []
        for head in (2 * p, 2 * p + 1):
            acc = acc_sc[head]
            outs.append(acc[0:half] * (1.0 / acc[half:half + 1]))
        yf_ref[:, p * dhp:(p + 1) * dhp] = jnp.concatenate(outs, axis=0).T.astype(BF16)


def _post_kernel(x_ref, ym_ref, yf_ref, zc_ref, wbm_ref, wbf_ref, wo_ref, g2_ref,
                 wg_ref, wu_ref, wd_ref, gfin_ref, o_ref, *, tf):
    d = x_ref.shape[1]
    d_ff = wg_ref.shape[1]
    bm = jnp.dot(ym_ref[...], wbm_ref[...], preferred_element_type=F32)
    bf = jnp.dot(yf_ref[...], wbf_ref[...], preferred_element_type=F32)
    mix = zc_ref[:, 0:d].astype(F32) * bm + zc_ref[:, d:2 * d].astype(F32) * bf
    x1 = x_ref[...] + jnp.dot(mix.astype(BF16), wo_ref[...], preferred_element_type=F32)
    h2 = _rms(x1, g2_ref[...]).astype(BF16)
    acc = jnp.zeros(x1.shape, F32)
    for f0 in range(0, d_ff, tf):
        g = jnp.dot(h2, wg_ref[:, f0:f0 + tf], preferred_element_type=F32)
        u = jnp.dot(h2, wu_ref[:, f0:f0 + tf], preferred_element_type=F32)
        act = (g * jax.nn.sigmoid(g) * u).astype(BF16)
        acc = acc + jnp.dot(act, wd_ref[f0:f0 + tf, :], preferred_element_type=F32)
    o_ref[...] = _rms(x1 + acc, gfin_ref[...])


def _bias_selectors():
    ones_lane = 3 * GATE_GROUP
    qsel = np.zeros((H_F // 2, LANES, 2 * LANES), np.float32)
    ksel = np.zeros((H_F // 2, LANES, 2 * LANES), np.float32)
    for h in range(H_F):
        p0 = (h % 2) * LANES + (LANES // 2 if h % 2 == 0 else 0)
        for c in range(3):
            qsel[h // 2, GATE_GROUP * c + h, p0 + c] = 1.0
            qsel[h // 2, ones_lane, p0 + 3 + c] = 1.0
            ksel[h // 2, ones_lane, p0 + c] = 1.0
            ksel[h // 2, GATE_GROUP * c + h, p0 + 3 + c] = -1.0
    return jnp.asarray(qsel, BF16), jnp.asarray(ksel, BF16)


def _layer(x2d, batch, seq, p, cfg):
    t, d = x2d.shape
    w_m = p["w_br_mlstm"].shape[0]
    w_f = p["w_br_fox"].shape[0]
    dh_m = w_m // H_M
    dh_f = w_f // H_F
    vmem = cfg["vmem_limit"]

    wa, wb, wc, wd, ba, bb, bc, bd = _prep_in_weights(p, vmem)

    tm = cfg["tm_in"]
    chunk = cfg["chunk"]
    tq, tk = cfg["tq"], cfg["tk"]
    nc = seq // chunk
    n_steps = t // tm
    tps = seq // tm
    assert seq % tm == 0 and tm % chunk == 0 and tm % tk == 0

    later = [p["w_br_mlstm"], p["w_br_fox"], p["w_out"], p["w_gate"], p["w_up"], p["w_down"]]

    def slice_spec(rows, cols):
        hold = 1
        while (rows * hold) % (n_steps * BF16_ROWS):
            hold *= 2
        assert hold <= n_steps
        return pl.BlockSpec((rows * hold // n_steps, cols), lambda i: (i // hold, 0))

    later_specs = [slice_spec(*w.shape) for w in later]
    outs = pl.pallas_call(
        functools.partial(_inproj_kernel, w_m=w_m, w_f=w_f, n_chunk=cfg["n_chunk"],
                          m_split=cfg["m_split"],
                          q_scale=dh_f ** -0.5 * LOG2E, k_scale=dh_m ** -0.5, chunk=chunk,
                          tk=tk, tiles_per_seq=tps, n_cast=len(later)),
        grid=(n_steps,),
        in_specs=[pl.BlockSpec((tm, d), lambda i: (i, 0)), _const_spec((1, d)),
                  _const_spec(wa.shape), _const_spec(ba.shape),
                  _const_spec(wb.shape), _const_spec(bb.shape),
                  _const_spec(wc.shape), _const_spec(bc.shape),
                  _const_spec(wd.shape), _const_spec(bd.shape),
                  _const_spec((CONV_K, 2 * w_m)), _const_spec((1, 2 * w_m))] + later_specs,
        out_specs=[pl.BlockSpec((tm, 2 * w_m), lambda i: (i, 0)),
                   pl.BlockSpec((tm // chunk, w_m, chunk), lambda i: (i, 0, 0)),
                   pl.BlockSpec((tm, w_m), lambda i: (i, 0)),
                   pl.BlockSpec((tm, 2 * w_f), lambda i: (i, 0)),
                   pl.BlockSpec((tm // tk, w_f, tk), lambda i: (i, 0, 0)),
                   pl.BlockSpec((tm, 2 * d), lambda i: (i, 0)),
                   pl.BlockSpec((1, N_GATE_ROWS, tm), lambda i: (i // tps, 0, i % tps)),
                   pl.BlockSpec((tm, LANES), lambda i: (i, 0)),
                   pl.BlockSpec((tm, LANES), lambda i: (i, 0))] + later_specs,
        out_shape=[jax.ShapeDtypeStruct((t, 2 * w_m), BF16),
                   jax.ShapeDtypeStruct((t // chunk, w_m, chunk), BF16),
                   jax.ShapeDtypeStruct((t, w_m), BF16),
                   jax.ShapeDtypeStruct((t, 2 * w_f), BF16),
                   jax.ShapeDtypeStruct((t // tk, w_f, tk), BF16),
                   jax.ShapeDtypeStruct((t, 2 * d), BF16),
                   jax.ShapeDtypeStruct((batch, N_GATE_ROWS, seq), F32),
                   jax.ShapeDtypeStruct((t, LANES), F32),
                   jax.ShapeDtypeStruct((t, LANES), BF16)]
        + [jax.ShapeDtypeStruct(w.shape, BF16) for w in later],
        scratch_shapes=[pltpu.VMEM((2 * w_m // cfg["n_chunk"], SUBLANES, cfg["n_chunk"]), F32),
                        pltpu.VMEM((N_GATE_ROWS, LANES), F32)],
        compiler_params=pltpu.CompilerParams(dimension_semantics=("arbitrary",), vmem_limit_bytes=vmem),
        name="inproj",
    )(x2d, p["norm1_g"][None, :], wa, ba, wb, bb, wc, bc, wd, bd, p["conv_w"], p["conv_b"][None, :],
      *later)
    zqk, vt, og, zb, fvt, zc, rows, cols, csplit, wbm, wbf, wo, wg, wu, wdn = outs

    rows_m = rows.reshape(batch, 3 * GATE_GROUP, nc, chunk)
    ym = pl.pallas_call(
        functools.partial(_mlstm_kernel, w_m=w_m, chunk=chunk),
        grid=(batch,),
        in_specs=[pl.BlockSpec((seq, 2 * w_m), lambda b: (b, 0)),
                  pl.BlockSpec((nc, w_m, chunk), lambda b: (b, 0, 0)),
                  pl.BlockSpec((seq, w_m), lambda b: (b, 0)),
                  pl.BlockSpec((1, 3 * GATE_GROUP, nc, chunk), lambda b: (b, 0, 0, 0)),
                  pl.BlockSpec((seq, LANES), lambda b: (b, 0)),
                  _const_spec((1, w_m))],
        out_specs=pl.BlockSpec((seq, w_m), lambda b: (b, 0)),
        out_shape=jax.ShapeDtypeStruct((t, w_m), BF16),
        scratch_shapes=[pltpu.VMEM((H_M, dh_m + BF16_ROWS, dh_m), F32)],
        compiler_params=pltpu.CompilerParams(dimension_semantics=("parallel",), vmem_limit_bytes=vmem),
        name="mlstm",
    )(zqk, vt, og, rows_m, cols, p["mlstm_norm_g"][None, :])

    assert tq % tk == 0 and seq % tq == 0
    nq = seq // tq
    qsel, ksel = _bias_selectors()
    v_rows = LANES // 2 + BF16_ROWS
    yf = pl.pallas_call(
        functools.partial(_fox_kernel, tq=tq, tk=tk),
        grid=(batch, nq),
        in_specs=[pl.BlockSpec((tq, w_f), lambda b, i: (b * nq + i, 0)),
                  pl.BlockSpec((seq, w_f), lambda b, i: (b, 1)),
                  pl.BlockSpec((seq // tk, w_f, tk), lambda b, i: (b, 0, 0)),
                  pl.BlockSpec((tq, LANES), lambda b, i: (b * nq + i, 0)),
                  pl.BlockSpec((seq, LANES), lambda b, i: (b, 0)),
                  _const_spec(qsel.shape), _const_spec(ksel.shape)],
        out_specs=pl.BlockSpec((tq, w_f), lambda b, i: (b * nq + i, 0)),
        out_shape=jax.ShapeDtypeStruct((t, w_f), BF16),
        scratch_shapes=[pltpu.VMEM((H_F, seq, LANES), BF16),
                        pltpu.VMEM((H_F, tq, LANES), BF16),
                        pltpu.VMEM((H_F, 1, tq), F32),
                        pltpu.VMEM((H_F, v_rows, tq), F32)],
        compiler_params=pltpu.CompilerParams(dimension_semantics=("arbitrary", "arbitrary"),
                                             vmem_limit_bytes=vmem),
        name="fox",
    )(zb, zb, fvt, csplit, csplit, qsel, ksel)

    tmp = cfg["tm_post"]
    return pl.pallas_call(
        functools.partial(_post_kernel, tf=cfg["tf"]),
        grid=(t // tmp,),
        in_specs=[pl.BlockSpec((tmp, d), lambda i: (i, 0)),
                  pl.BlockSpec((tmp, w_m), lambda i: (i, 0)),
                  pl.BlockSpec((tmp, w_f), lambda i: (i, 0)),
                  pl.BlockSpec((tmp, 2 * d), lambda i: (i, 0)),
                  _const_spec(wbm.shape), _const_spec(wbf.shape), _const_spec(wo.shape),
                  _const_spec((1, d)), _const_spec(wg.shape), _const_spec(wu.shape),
                  _const_spec(wdn.shape), _const_spec((1, d))],
        out_specs=pl.BlockSpec((tmp, d), lambda i: (i, 0)),
        out_shape=jax.ShapeDtypeStruct((t, d), F32),
        compiler_params=pltpu.CompilerParams(dimension_semantics=("parallel",), vmem_limit_bytes=vmem),
        name="post",
    )(x2d, ym, yf, zc, wbm, wbf, wo, p["norm2_g"][None, :], wg, wu, wdn, p["norm_f_g"][None, :])


def kernel(x, norm1_g, w_in, b_in, conv_w, conv_b, mlstm_norm_g, w_br_mlstm, w_br_fox, w_out,
           norm2_g, w_gate, w_up, w_down, norm_f_g):
    batch, seq, d = x.shape
    depth = w_in.shape[0]
    assert depth == 1, "the final norm is fused into the single layer's last call"
    cfg = _cfg(batch, seq, d, w_gate.shape[-1])
    p = dict(norm1_g=norm1_g[0], w_in=w_in[0], b_in=b_in[0], conv_w=conv_w[0], conv_b=conv_b[0],
             mlstm_norm_g=mlstm_norm_g[0], w_br_mlstm=w_br_mlstm[0], w_br_fox=w_br_fox[0],
             w_out=w_out[0], norm2_g=norm2_g[0], w_gate=w_gate[0], w_up=w_up[0], w_down=w_down[0],
             norm_f_g=norm_f_g)
    out = _layer(x.reshape(batch * seq, d), batch, seq, p, cfg)
    return out.reshape(batch, seq, d)
```

```python
import functools

import jax
import jax.numpy as jnp
import numpy as np
from jax import lax
from jax.experimental import pallas as pl
from jax.experimental.pallas import tpu as pltpu

EPS = 1e-6
H_M = 4
H_F = 8
CONV_K = 4

LANES = 128
SUBLANES = 8
BF16_ROWS = 16
GATE_GROUP = 8
N_GATE_ROWS = 3 * GATE_GROUP
V7X_VMEM_BYTES = 64 * 1024 * 1024
VMEM_COMPILER_RESERVE = 8 * 1024 * 1024
NEG_BIG = -1e30
LOG2E = 1.4426950408889634

F32 = jnp.float32
BF16 = jnp.bfloat16
NT_DIMS = (((1,), (1,)), ((), ()))


def _cfg(batch, seq, d_model, d_ff):
    return dict(
        tm_in=1024,
        n_chunk=256,
        m_split=8,
        chunk=256,
        tq=512,
        tk=256,
        tm_post=512,
        tf=256,
        vmem_limit=V7X_VMEM_BYTES - VMEM_COMPILER_RESERVE,
    )


def _const_spec(shape):
    nd = len(shape)
    return pl.BlockSpec(shape, lambda *_: (0,) * nd, pipeline_mode=pl.Buffered(1))


def _rms(x, g):
    return x * lax.rsqrt(jnp.mean(x * x, axis=-1, keepdims=True) + EPS) * g


def _log_sigmoid(x):
    return jnp.minimum(x, 0.0) - jnp.log1p(jnp.exp(-jnp.abs(x)))


def _wprep_kernel(wint_ref, bin_ref, wa_o, wb_o, wc_o, wd_o, ba_o, bb_o, bc_o, bd_o, *, w_m, w_f):
    o_mi = 4 * w_m
    o_mf = o_mi + H_M
    o_fq = o_mf + H_M
    o_ff = o_fq + 3 * w_f
    o_g = o_ff + H_F

    def gate_block(src):
        rows = src.shape[0]
        lane = lax.broadcasted_iota(jnp.int32, (rows, LANES), 1)
        blk_m = src[:, o_mi:o_mi + LANES]
        f0 = (o_ff // LANES) * LANES
        blk_f = src[:, f0:f0 + LANES]
        mi = jnp.where(lane < H_M, blk_m, 0.0)
        mf = jnp.where((lane >= GATE_GROUP) & (lane < GATE_GROUP + H_M),
                       pltpu.roll(blk_m, GATE_GROUP - H_M, axis=1), 0.0)
        ff = jnp.where((lane >= 2 * GATE_GROUP) & (lane < 2 * GATE_GROUP + H_F),
                       pltpu.roll(blk_f, 2 * GATE_GROUP - (o_ff - f0), axis=1), 0.0)
        return mi + mf + ff

    slab = wint_ref[...]
    wa_o[...] = slab[0:o_mi].T.astype(BF16)
    wb_o[...] = slab[o_fq:o_ff].T.astype(BF16)
    wc_o[...] = slab[o_g:].T.astype(BF16)
    g_m = slab[o_mi:o_mi + GATE_GROUP]
    row = lax.broadcasted_iota(jnp.int32, g_m.shape, 0)
    gates = jnp.concatenate(
        [jnp.where(row < H_M, g_m, 0.0),
         jnp.where(row < H_M, pltpu.roll(g_m, GATE_GROUP - H_M, axis=0), 0.0),
         slab[o_ff:o_ff + GATE_GROUP],
         jnp.zeros((LANES - 3 * GATE_GROUP, slab.shape[1]), F32)], axis=0)
    wd_o[...] = gates.T.astype(BF16)
    b = bin_ref[...]
    ba_o[...] = b[:, 0:o_mi]
    bb_o[...] = b[:, o_fq:o_ff]
    bc_o[...] = b[:, o_g:]
    bd_o[...] = gate_block(b)


def _prep_in_weights(p, vmem):
    d, n_in = p["w_in"].shape
    w_m = p["w_br_mlstm"].shape[0]
    w_f = p["w_br_fox"].shape[0]
    steps = 8
    o_ff = 4 * w_m + 2 * H_M + 3 * w_f
    assert (4 * w_m) % LANES == 0 and 2 * H_M <= GATE_GROUP + H_M <= LANES
    assert o_ff % LANES <= 2 * GATE_GROUP and o_ff % LANES + H_F <= LANES
    assert n_in == o_ff + H_F + 2 * d
    srcs = [p["w_in"].T, p["b_in"][None, :]]

    def row_spec(shape, tiled=True):
        if not tiled:
            return pl.BlockSpec(shape, lambda i: (0, 0))
        assert shape[0] % (steps * BF16_ROWS) == 0
        return pl.BlockSpec((shape[0] // steps, shape[1]), lambda i: (i, 0))

    out_shapes = [((d, 4 * w_m), BF16), ((d, 3 * w_f), BF16), ((d, 2 * d), BF16), ((d, LANES), BF16),
                  ((1, 4 * w_m), F32), ((1, 3 * w_f), F32), ((1, 2 * d), F32), ((1, LANES), F32)]
    return pl.pallas_call(
        functools.partial(_wprep_kernel, w_m=w_m, w_f=w_f),
        grid=(steps,),
        in_specs=([pl.BlockSpec((n_in, d // steps), lambda i: (0, i))]
                  + [row_spec(s.shape, tiled=s.shape[0] > 1) for s in srcs[1:]]),
        out_specs=[row_spec(s, tiled=s[0] > 1) for s, _ in out_shapes],
        out_shape=[jax.ShapeDtypeStruct(s, dt) for s, dt in out_shapes],
        compiler_params=pltpu.CompilerParams(dimension_semantics=("arbitrary",), vmem_limit_bytes=vmem),
        name="wprep",
    )(*srcs)


def _inproj_kernel(x_ref, g_ref, wa_ref, ba_ref, wb_ref, bb_ref, wc_ref, bc_ref, wd_ref, bd_ref,
                   cw_ref, cb_ref, *rest,
                   w_m, w_f, n_chunk, m_split, q_scale, k_scale, chunk, tk, tiles_per_seq, n_cast):
    cast_srcs, rest = rest[:n_cast], rest[n_cast:]
    (zqk_ref, vt_ref, og_ref, zb_ref, fvt_ref, zc_ref, rows_ref, cols_ref, csplit_ref), rest = (
        rest[:9], rest[9:])
    cast_dsts, (zs_sc, gc_sc) = rest[:n_cast], rest[n_cast:]
    tm = x_ref.shape[0]
    seq_start = (pl.program_id(0) % tiles_per_seq) == 0

    @pl.when(seq_start)
    def _():
        zs_sc[...] = jnp.zeros(zs_sc.shape, F32)
        gc_sc[0:2 * GATE_GROUP] = jnp.zeros((2 * GATE_GROUP, LANES), F32)
        gc_sc[2 * GATE_GROUP:] = jnp.full((GATE_GROUP, LANES), NEG_BIG, F32)

    rows = tm // m_split
    hb = [_rms(x_ref[r0:r0 + rows, :], g_ref[...]).astype(BF16) for r0 in range(0, tm, rows)]

    def proj(w_ref, b_ref, c0, c1):
        w = w_ref[:, c0:c1]
        parts = [jnp.dot(h, w, preferred_element_type=F32) for h in hb]
        return jnp.concatenate(parts, axis=0) + b_ref[:, c0:c1]

    def qk_chunk(ci):
        c0 = ci * n_chunk
        cs = slice(c0, c0 + n_chunk)
        z = proj(wa_ref, ba_ref, c0, c0 + n_chunk)
        prev = zs_sc[ci]
        zs_sc[ci] = z[tm - SUBLANES:tm]
        zg = z.reshape(tm // SUBLANES, SUBLANES, n_chunk)
        row = lax.broadcasted_iota(jnp.int32, zg.shape, 1)
        w = cw_ref[:, cs]
        y = zg * w[CONV_K - 1:CONV_K] + cb_ref[:, cs]
        for s in range(1, CONV_K):
            rz = pltpu.roll(zg, s, axis=1)
            before = jnp.concatenate([pltpu.roll(prev, s, axis=0)[None], rz[:-1]], axis=0)
            y = y + jnp.where(row < s, before, rz) * w[CONV_K - 1 - s:CONV_K - s]
        act = y * jax.nn.sigmoid(y)
        if c0 >= w_m:
            act = act * k_scale
        zqk_ref[:, cs] = act.reshape(tm, n_chunk).astype(BF16)

    def v_chunk(i):
        c0 = 2 * w_m + i * n_chunk
        z = proj(wa_ref, ba_ref, c0, c0 + n_chunk)
        for cc in range(tm // chunk):
            for f0 in range(0, n_chunk, LANES):
                blk = z[cc * chunk:(cc + 1) * chunk, f0:f0 + LANES]
                r0 = i * n_chunk + f0
                vt_ref[cc, r0:r0 + LANES, :] = blk.T.astype(BF16)

    def og_chunk(i):
        c0 = 3 * w_m + i * n_chunk
        z = proj(wa_ref, ba_ref, c0, c0 + n_chunk)
        og_ref[:, i * n_chunk:(i + 1) * n_chunk] = jax.nn.sigmoid(z).astype(BF16)

    def b_chunk(i):
        c0 = i * n_chunk
        z = proj(wb_ref, bb_ref, c0, c0 + n_chunk)
        if c0 < w_f:
            z = z * q_scale
        if c0 < 2 * w_f:
            zb_ref[:, c0:c0 + n_chunk] = z.astype(BF16)
        else:
            for jb in range(tm // tk):
                for f0 in range(0, n_chunk, LANES):
                    blk = z[jb * tk:(jb + 1) * tk, f0:f0 + LANES]
                    r0 = c0 - 2 * w_f + f0
                    fvt_ref[jb, r0:r0 + LANES, :] = blk.T.astype(BF16)

    def c_chunk(i):
        c0 = i * n_chunk
        z = proj(wc_ref, bc_ref, c0, c0 + n_chunk)
        zc_ref[:, c0:c0 + n_chunk] = jax.nn.sigmoid(z).astype(BF16)

    def d_chunk(_):
        zd = proj(wd_ref, bd_ref, 0, LANES)
        rows, cols, csplit, gc_sc[...] = _gate_tables(zd.T[:N_GATE_ROWS, :], gc_sc[...])
        rows_ref[0] = rows
        cols_ref[...] = cols
        csplit_ref[...] = csplit

    heavy = [(qk_chunk, i) for i in range(2 * w_m // n_chunk)]
    light = ([(c_chunk, i) for i in range(wc_ref.shape[1] // n_chunk)]
             + [(b_chunk, i) for i in range(3 * w_f // n_chunk)]
             + [(v_chunk, i) for i in range(w_m // n_chunk)]
             + [(og_chunk, i) for i in range(w_m // n_chunk)])
    per_heavy = len(light) // len(heavy)
    order = [(d_chunk, 0)]
    for hi, task in enumerate(heavy):
        order.append(task)
        order.extend(light[hi * per_heavy:(hi + 1) * per_heavy])
    order.extend(light[len(heavy) * per_heavy:])
    for fn, i in order:
        fn(i)
    for src, dst in zip(cast_srcs, cast_dsts):
        dst[...] = src[...].astype(BF16)


def _scan_lanes(x, op, fill):
    n = x.shape[-1]
    pos = lax.broadcasted_iota(jnp.int32, x.shape, 1)
    s = 1
    while s < n:
        shifted = pltpu.roll(x, s, axis=1)
        x = op(x, jnp.where(pos >= s, shifted, fill))
        s *= 2
    return x


def _gate_tables(z, carry):
    width = z.shape[1]
    i8 = z[0:GATE_GROUP]
    cum = (_scan_lanes(_log_sigmoid(z[GATE_GROUP:3 * GATE_GROUP]), jnp.add, 0.0)
           + carry[0:2 * GATE_GROUP, 0:1])
    f8 = cum[0:GATE_GROUP]
    g8 = i8 - f8
    cmax = jnp.maximum(_scan_lanes(g8, jnp.maximum, NEG_BIG), carry[2 * GATE_GROUP:, 0:1])
    new_carry = jnp.concatenate(
        [jnp.broadcast_to(cum[:, width - 1:width], (2 * GATE_GROUP, LANES)),
         jnp.broadcast_to(cmax[:, width - 1:width], (GATE_GROUP, LANES))], axis=0)
    m8 = jnp.maximum(cmax, 0.0)
    en8 = jnp.exp(-(f8 + m8))
    cf8 = cum[GATE_GROUP:2 * GATE_GROUP] * LOG2E
    g8, m8 = g8 * LOG2E, m8 * LOG2E
    rows = jnp.concatenate([g8, m8, en8], axis=0)
    stack = jnp.concatenate(
        [g8, m8, jnp.zeros((LANES - 2 * GATE_GROUP, width), F32)], axis=0)
    cols = stack.T
    hi = cf8.astype(BF16).astype(F32)
    r1 = cf8 - hi
    lo = r1.astype(BF16).astype(F32)
    lo2 = (r1 - lo).astype(BF16).astype(F32)
    ones = jnp.where(lax.broadcasted_iota(jnp.int32, (GATE_GROUP, width), 0) == 0, 1.0, 0.0)
    split = jnp.concatenate(
        [hi, lo, lo2, ones, jnp.zeros((LANES - 4 * GATE_GROUP, width), F32)], axis=0)
    return rows, cols, split.T.astype(BF16), new_carry


def _mlstm_kernel(zqk_ref, vt_ref, og_ref, rows_ref, cols_ref, ng_ref, ym_ref, ct_sc, *, w_m, chunk):
    seq = zqk_ref.shape[0]
    dh = w_m // H_M
    nc = seq // chunk
    ct_sc[...] = jnp.zeros(ct_sc.shape, F32)
    hc = chunk // 2
    causal = (lax.broadcasted_iota(jnp.int32, (hc, hc), 0)
              <= lax.broadcasted_iota(jnp.int32, (hc, hc), 1))
    ones_rows = jnp.where(
        lax.broadcasted_iota(jnp.int32, (BF16_ROWS, chunk), 0) == 0, 1.0, 0.0).astype(BF16)

    def chunk_step(c):
        r0 = c * chunk
        tc = slice(r0, r0 + chunk)
        colsc = cols_ref[tc, :]
        last = cols_ref[r0 + chunk - 1:r0 + chunk, :]
        prev = cols_ref[r0 - 1:r0, :] if c else jnp.zeros((1, LANES), F32)
        first, vas = [], []
        for h in range(H_M):
            qc = zqk_ref[tc, h * dh:(h + 1) * dh]
            kc = zqk_ref[tc, w_m + h * dh:w_m + (h + 1) * dh]
            lhs = jnp.concatenate([kc, ct_sc[h].astype(BF16)], axis=0)
            first.append(lax.dot_general(lhs, qc, NT_DIMS, preferred_element_type=F32))
            vas.append(jnp.concatenate([vt_ref[c, h * dh:(h + 1) * dh, :], ones_rows], axis=0))
        for h in range(H_M):
            kc = zqk_ref[tc, w_m + h * dh:w_m + (h + 1) * dh]
            g_row = rows_ref[0, h:h + 1, tc]
            m_e = last[:, GATE_GROUP + h:GATE_GROUP + h + 1]
            m_p = prev[:, GATE_GROUP + h:GATE_GROUP + h + 1]
            vaw = (vas[h].astype(F32) * jnp.exp2(g_row - m_e)).astype(BF16)
            ct_sc[h] = jnp.exp2(m_p - m_e) * ct_sc[h] + jnp.dot(vaw, kc, preferred_element_type=F32)
        for h in range(H_M):
            ch = slice(h * dh, (h + 1) * dh)
            g_col = colsc[:, h:h + 1]
            m_row = rows_ref[0, GATE_GROUP + h:GATE_GROUP + h + 1, tc]
            en_row = rows_ref[0, 2 * GATE_GROUP + h:2 * GATE_GROUP + h + 1, tc]
            m_p = prev[:, GATE_GROUP + h:GATE_GROUP + h + 1]
            sk = first[h]
            d00 = jnp.where(causal, jnp.exp2(g_col[0:hc] - m_row[:, 0:hc]), 0.0)
            d01 = jnp.exp2(g_col[0:hc] - m_row[:, hc:chunk])
            d11 = jnp.where(causal, jnp.exp2(g_col[hc:chunk] - m_row[:, hc:chunk]), 0.0)
            top = jnp.concatenate([sk[0:hc, 0:hc] * d00, sk[0:hc, hc:chunk] * d01], axis=1)
            bot = jnp.concatenate([jnp.zeros((hc, hc), F32), sk[hc:chunk, hc:chunk] * d11], axis=1)
            sqk = jnp.concatenate([top, bot], axis=0).astype(BF16)
            nd = (jnp.exp2(m_p - m_row) * first[h][chunk:]
                  + jnp.dot(vas[h], sqk, preferred_element_type=F32))
            den = nd[dh:dh + 1]
            ht = nd[0:dh] * (1.0 / jnp.maximum(jnp.abs(den), en_row))
            hn = ht * lax.rsqrt(jnp.mean(ht * ht, axis=0, keepdims=True) + EPS)
            og = og_ref[tc, ch].astype(F32)
            ym_ref[tc, ch] = (hn.T * ng_ref[:, ch] * og).astype(BF16)

    for c in range(nc):
        chunk_step(c)


def _fox_kernel(q_ref, k_ref, fvt_ref, csq_ref, csk_ref, qsel_ref, ksel_ref, yf_ref,
                kaug_sc, qaug_sc, m_sc, acc_sc, *, tq, tk):
    qi = pl.program_id(1)

    seq = k_ref.shape[0]
    dhp = LANES
    half = dhp // 2
    v_rows = half + BF16_ROWS

    def own_lanes(head, rows):
        lane = lax.broadcasted_iota(jnp.int32, (rows, dhp), 1)
        return (lane >= half) if head % 2 else (lane < half)

    def augment(x_ref, cs_ref, sel_ref, dst_sc, rows):
        cs = cs_ref[...]
        for p in range(H_F // 2):
            xp = x_ref[:, p * dhp:(p + 1) * dhp]
            bias = jnp.dot(cs, sel_ref[p], preferred_element_type=F32).astype(BF16)
            for head in (2 * p, 2 * p + 1):
                b0 = (head % 2) * dhp
                dst_sc[head] = jnp.where(own_lanes(head, rows), xp, bias[:, b0:b0 + dhp])

    @pl.when(qi == 0)
    def _():
        augment(k_ref, csk_ref, ksel_ref, kaug_sc, seq)

    augment(q_ref, csq_ref, qsel_ref, qaug_sc, tq)
    m_sc[...] = jnp.full(m_sc.shape, NEG_BIG, F32)
    acc_sc[...] = jnp.zeros(acc_sc.shape, F32)

    ratio = tq // tk
    ones_rows = jnp.where(
        lax.broadcasted_iota(jnp.int32, (BF16_ROWS, tk), 0) == 0, 1.0, 0.0).astype(BF16)

    def vaug(head, j):
        return jnp.concatenate([fvt_ref[j, head * half:(head + 1) * half, :], ones_rows], axis=0)

    def step(j, diag):
        q0 = 0 if diag is None else diag * tk
        nq = tq - q0
        k0 = pl.multiple_of(j * tk, tk)
        sts = [lax.dot_general(kaug_sc[head, pl.ds(k0, tk), :], qaug_sc[head, q0:tq, :], NT_DIMS,
                               preferred_element_type=F32) for head in range(H_F)]
        if diag is not None:
            causal = (lax.broadcasted_iota(jnp.int32, (tk, nq), 0)
                      <= lax.broadcasted_iota(jnp.int32, (tk, nq), 1))
        for head in range(H_F):
            st = sts[head]
            if diag is not None:
                st = jnp.where(causal, st, NEG_BIG)
            m = m_sc[head, :, q0:tq]
            m_new = jnp.maximum(m, jnp.max(st, axis=0, keepdims=True))
            alpha = jnp.exp2(m - m_new)
            pt = jnp.exp2(st - m_new).astype(BF16)
            m_sc[head, :, q0:tq] = m_new
            acc_sc[head, :, q0:tq] = (alpha * acc_sc[head, :, q0:tq]
                                      + jnp.dot(vaug(head, j), pt, preferred_element_type=F32))

    def loop_body(jj, carry):
        for r in range(ratio):
            step(jj * ratio + r, None)
        return carry

    lax.fori_loop(0, qi, loop_body, 0)
    for diag in range(ratio):
        step(qi * ratio + diag, diag)
    for p in range(H_F // 2):
        outs = []
        for head in (2 * p, 2 * p + 1):
            acc = acc_sc[head]
            outs.append(acc[0:half] * (1.0 / acc[half:half + 1]))
        yf_ref[:, p * dhp:(p + 1) * dhp] = jnp.concatenate(outs, axis=0).T.astype(BF16)


def _post_kernel(x_ref, ym_ref, yf_ref, zc_ref, wbm_ref, wbf_ref, wo_ref, g2_ref,
                 wg_ref, wu_ref, wd_ref, gfin_ref, o_ref, *, tf):
    d = x_ref.shape[1]
    d_ff = wg_ref.shape[1]
    bm = jnp.dot(ym_ref[...], wbm_ref[...], preferred_element_type=F32)
    bf = jnp.dot(yf_ref[...], wbf_ref[...], preferred_element_type=F32)
    mix = zc_ref[:, 0:d].astype(F32) * bm + zc_ref[:, d:2 * d].astype(F32) * bf
    x1 = x_ref[...] + jnp.dot(mix.astype(BF16), wo_ref[...], preferred_element_type=F32)
    h2 = _rms(x1, g2_ref[...]).astype(BF16)
    acc = jnp.zeros(x1.shape, F32)
    for f0 in range(0, d_ff, tf):
        g = jnp.dot(h2, wg_ref[:, f0:f0 + tf], preferred_element_type=F32)
        u = jnp.dot(h2, wu_ref[:, f0:f0 + tf], preferred_element_type=F32)
        act = (g * jax.nn.sigmoid(g) * u).astype(BF16)
        acc = acc + jnp.dot(act, wd_ref[f0:f0 + tf, :], preferred_element_type=F32)
    o_ref[...] = _rms(x1 + acc, gfin_ref[...])


def _bias_selectors():
    ones_lane = 3 * GATE_GROUP
    qsel = np.zeros((H_F // 2, LANES, 2 * LANES), np.float32)
    ksel = np.zeros((H_F // 2, LANES, 2 * LANES), np.float32)
    for h in range(H_F):
        p0 = (h % 2) * LANES + (LANES // 2 if h % 2 == 0 else 0)
        for c in range(3):
            qsel[h // 2, GATE_GROUP * c + h, p0 + c] = 1.0
            qsel[h // 2, ones_lane, p0 + 3 + c] = 1.0
            ksel[h // 2, ones_lane, p0 + c] = 1.0
            ksel[h // 2, GATE_GROUP * c + h, p0 + 3 + c] = -1.0
    return jnp.asarray(qsel, BF16), jnp.asarray(ksel, BF16)


def _layer(x2d, batch, seq, p, cfg):
    t, d = x2d.shape
    w_m = p["w_br_mlstm"].shape[0]
    w_f = p["w_br_fox"].shape[0]
    dh_m = w_m // H_M
    dh_f = w_f // H_F
    vmem = cfg["vmem_limit"]

    wa, wb, wc, wd, ba, bb, bc, bd = _prep_in_weights(p, vmem)

    tm = cfg["tm_in"]
    chunk = cfg["chunk"]
    tq, tk = cfg["tq"], cfg["tk"]
    nc = seq // chunk
    n_steps = t // tm
    tps = seq // tm
    assert seq % tm == 0 and tm % chunk == 0 and tm % tk == 0

    later = [p["w_br_mlstm"], p["w_br_fox"], p["w_out"], p["w_gate"], p["w_up"], p["w_down"]]

    def slice_spec(rows, cols):
        hold = 1
        while (rows * hold) % (n_steps * BF16_ROWS):
            hold *= 2
        assert hold <= n_steps
        return pl.BlockSpec((rows * hold // n_steps, cols), lambda i: (i // hold, 0))

    later_specs = [slice_spec(*w.shape) for w in later]
    outs = pl.pallas_call(
        functools.partial(_inproj_kernel, w_m=w_m, w_f=w_f, n_chunk=cfg["n_chunk"],
                          m_split=cfg["m_split"],
                          q_scale=dh_f ** -0.5 * LOG2E, k_scale=dh_m ** -0.5, chunk=chunk,
                          tk=tk, tiles_per_seq=tps, n_cast=len(later)),
        grid=(n_steps,),
        in_specs=[pl.BlockSpec((tm, d), lambda i: (i, 0)), _const_spec((1, d)),
                  _const_spec(wa.shape), _const_spec(ba.shape),
                  _const_spec(wb.shape), _const_spec(bb.shape),
                  _const_spec(wc.shape), _const_spec(bc.shape),
                  _const_spec(wd.shape), _const_spec(bd.shape),
                  _const_spec((CONV_K, 2 * w_m)), _const_spec((1, 2 * w_m))] + later_specs,
        out_specs=[pl.BlockSpec((tm, 2 * w_m), lambda i: (i, 0)),
                   pl.BlockSpec((tm // chunk, w_m, chunk), lambda i: (i, 0, 0)),
                   pl.BlockSpec((tm, w_m), lambda i: (i, 0)),
                   pl.BlockSpec((tm, 2 * w_f), lambda i: (i, 0)),
                   pl.BlockSpec((tm // tk, w_f, tk), lambda i: (i, 0, 0)),
                   pl.BlockSpec((tm, 2 * d), lambda i: (i, 0)),
                   pl.BlockSpec((1, N_GATE_ROWS, tm), lambda i: (i // tps, 0, i % tps)),
                   pl.BlockSpec((tm, LANES), lambda i: (i, 0)),
                   pl.BlockSpec((tm, LANES), lambda i: (i, 0))] + later_specs,
        out_shape=[jax.ShapeDtypeStruct((t, 2 * w_m), BF16),
                   jax.ShapeDtypeStruct((t // chunk, w_m, chunk), BF16),
                   jax.ShapeDtypeStruct((t, w_m), BF16),
                   jax.ShapeDtypeStruct((t, 2 * w_f), BF16),
                   jax.ShapeDtypeStruct((t // tk, w_f, tk), BF16),
                   jax.ShapeDtypeStruct((t, 2 * d), BF16),
                   jax.ShapeDtypeStruct((batch, N_GATE_ROWS, seq), F32),
                   jax.ShapeDtypeStruct((t, LANES), F32),
                   jax.ShapeDtypeStruct((t, LANES), BF16)]
        + [jax.ShapeDtypeStruct(w.shape, BF16) for w in later],
        scratch_shapes=[pltpu.VMEM((2 * w_m // cfg["n_chunk"], SUBLANES, cfg["n_chunk"]), F32),
                        pltpu.VMEM((N_GATE_ROWS, LANES), F32)],
        compiler_params=pltpu.CompilerParams(dimension_semantics=("arbitrary",), vmem_limit_bytes=vmem),
        name="inproj",
    )(x2d, p["norm1_g"][None, :], wa, ba, wb, bb, wc, bc, wd, bd, p["conv_w"], p["conv_b"][None, :],
      *later)
    zqk, vt, og, zb, fvt, zc, rows, cols, csplit, wbm, wbf, wo, wg, wu, wdn = outs

    ym = pl.pallas_call(
        functools.partial(_mlstm_kernel, w_m=w_m, chunk=chunk),
        grid=(batch,),
        in_specs=[pl.BlockSpec((seq, 2 * w_m), lambda b: (b, 0)),
                  pl.BlockSpec((nc, w_m, chunk), lambda b: (b, 0, 0)),
                  pl.BlockSpec((seq, w_m), lambda b: (b, 0)),
                  pl.BlockSpec((1, 3 * GATE_GROUP, seq), lambda b: (b, 0, 0)),
                  pl.BlockSpec((seq, LANES), lambda b: (b, 0)),
                  _const_spec((1, w_m))],
        out_specs=pl.BlockSpec((seq, w_m), lambda b: (b, 0)),
        out_shape=jax.ShapeDtypeStruct((t, w_m), BF16),
        scratch_shapes=[pltpu.VMEM((H_M, dh_m + BF16_ROWS, dh_m), F32)],
        compiler_params=pltpu.CompilerParams(dimension_semantics=("parallel",), vmem_limit_bytes=vmem),
        name="mlstm",
    )(zqk, vt, og, rows, cols, p["mlstm_norm_g"][None, :])

    assert tq % tk == 0 and seq % tq == 0
    nq = seq // tq
    qsel, ksel = _bias_selectors()
    v_rows = LANES // 2 + BF16_ROWS
    yf = pl.pallas_call(
        functools.partial(_fox_kernel, tq=tq, tk=tk),
        grid=(batch, nq),
        in_specs=[pl.BlockSpec((tq, w_f), lambda b, i: (b * nq + i, 0)),
                  pl.BlockSpec((seq, w_f), lambda b, i: (b, 1)),
                  pl.BlockSpec((seq // tk, w_f, tk), lambda b, i: (b, 0, 0)),
                  pl.BlockSpec((tq, LANES), lambda b, i: (b * nq + i, 0)),
                  pl.BlockSpec((seq, LANES), lambda b, i: (b, 0)),
                  _const_spec(qsel.shape), _const_spec(ksel.shape)],
        out_specs=pl.BlockSpec((tq, w_f), lambda b, i: (b * nq + i, 0)),
        out_shape=jax.ShapeDtypeStruct((t, w_f), BF16),
        scratch_shapes=[pltpu.VMEM((H_F, seq, LANES), BF16),
                        pltpu.VMEM((H_F, tq, LANES), BF16),
                        pltpu.VMEM((H_F, 1, tq), F32),
                        pltpu.VMEM((H_F, v_rows, tq), F32)],
        compiler_params=pltpu.CompilerParams(dimension_semantics=("arbitrary", "arbitrary"),
                                             vmem_limit_bytes=vmem),
        name="fox",
    )(zb, zb, fvt, csplit, csplit, qsel, ksel)

    tmp = cfg["tm_post"]
    return pl.pallas_call(
        functools.partial(_post_kernel, tf=cfg["tf"]),
        grid=(t // tmp,),
        in_specs=[pl.BlockSpec((tmp, d), lambda i: (i, 0)),
                  pl.BlockSpec((tmp, w_m), lambda i: (i, 0)),
                  pl.BlockSpec((tmp, w_f), lambda i: (i, 0)),
                  pl.BlockSpec((tmp, 2 * d), lambda i: (i, 0)),
                  _const_spec(wbm.shape), _const_spec(wbf.shape), _const_spec(wo.shape),
                  _const_spec((1, d)), _const_spec(wg.shape), _const_spec(wu.shape),
                  _const_spec(wdn.shape), _const_spec((1, d))],
        out_specs=pl.BlockSpec((tmp, d), lambda i: (i, 0)),
        out_shape=jax.ShapeDtypeStruct((t, d), F32),
        compiler_params=pltpu.CompilerParams(dimension_semantics=("parallel",), vmem_limit_bytes=vmem),
        name="post",
    )(x2d, ym, yf, zc, wbm, wbf, wo, p["norm2_g"][None, :], wg, wu, wdn, p["norm_f_g"][None, :])


def kernel(x, norm1_g, w_in, b_in, conv_w, conv_b, mlstm_norm_g, w_br_mlstm, w_br_fox, w_out,
           norm2_g, w_gate, w_up, w_down, norm_f_g):
    batch, seq, d = x.shape
    depth = w_in.shape[0]
    assert depth == 1, "the final norm is fused into the single layer's last call"
    cfg = _cfg(batch, seq, d, w_gate.shape[-1])
    p = dict(norm1_g=norm1_g[0], w_in=w_in[0], b_in=b_in[0], conv_w=conv_w[0], conv_b=conv_b[0],
             mlstm_norm_g=mlstm_norm_g[0], w_br_mlstm=w_br_mlstm[0], w_br_fox=w_br_fox[0],
             w_out=w_out[0], norm2_g=norm2_g[0], w_gate=w_gate[0], w_up=w_up[0], w_down=w_down[0],
             norm_f_g=norm_f_g)
    out = _layer(x.reshape(batch * seq, d), batch, seq, p, cfg)
    return out.reshape(batch, seq, d)
```

```python
import functools

import jax
import jax.numpy as jnp
import numpy as np
from jax import lax
from jax.experimental import pallas as pl
from jax.experimental.pallas import tpu as pltpu

EPS = 1e-6
H_M = 4
H_F = 8
CONV_K = 4

LANES = 128
SUBLANES = 8
BF16_ROWS = 16
GATE_GROUP = 8
N_GATE_ROWS = 3 * GATE_GROUP
V7X_VMEM_BYTES = 64 * 1024 * 1024
VMEM_COMPILER_RESERVE = 8 * 1024 * 1024
NEG_BIG = -1e30
LOG2E = 1.4426950408889634

F32 = jnp.float32
BF16 = jnp.bfloat16
NT_DIMS = (((1,), (1,)), ((), ()))


def _cfg(batch, seq, d_model, d_ff):
    return dict(
        tm_in=1024,
        n_chunk=256,
        m_split=8,
        chunk=256,
        tq=512,
        tk=256,
        tm_post=512,
        tf=256,
        vmem_limit=V7X_VMEM_BYTES - VMEM_COMPILER_RESERVE,
    )


def _const_spec(shape):
    nd = len(shape)
    return pl.BlockSpec(shape, lambda *_: (0,) * nd, pipeline_mode=pl.Buffered(1))


def _rms(x, g):
    return x * lax.rsqrt(jnp.mean(x * x, axis=-1, keepdims=True) + EPS) * g


def _log_sigmoid(x):
    return jnp.minimum(x, 0.0) - jnp.log1p(jnp.exp(-jnp.abs(x)))


def _wprep_kernel(wint_ref, bin_ref, wa_o, wb_o, wc_o, wd_o, ba_o, bb_o, bc_o, bd_o, *, w_m, w_f):
    o_mi = 4 * w_m
    o_mf = o_mi + H_M
    o_fq = o_mf + H_M
    o_ff = o_fq + 3 * w_f
    o_g = o_ff + H_F

    def gate_block(src):
        rows = src.shape[0]
        lane = lax.broadcasted_iota(jnp.int32, (rows, LANES), 1)
        blk_m = src[:, o_mi:o_mi + LANES]
        f0 = (o_ff // LANES) * LANES
        blk_f = src[:, f0:f0 + LANES]
        mi = jnp.where(lane < H_M, blk_m, 0.0)
        mf = jnp.where((lane >= GATE_GROUP) & (lane < GATE_GROUP + H_M),
                       pltpu.roll(blk_m, GATE_GROUP - H_M, axis=1), 0.0)
        ff = jnp.where((lane >= 2 * GATE_GROUP) & (lane < 2 * GATE_GROUP + H_F),
                       pltpu.roll(blk_f, 2 * GATE_GROUP - (o_ff - f0), axis=1), 0.0)
        return mi + mf + ff

    slab = wint_ref[...]
    wa_o[...] = slab[0:o_mi].T.astype(BF16)
    wb_o[...] = slab[o_fq:o_ff].T.astype(BF16)
    wc_o[...] = slab[o_g:].T.astype(BF16)
    g_m = slab[o_mi:o_mi + GATE_GROUP]
    row = lax.broadcasted_iota(jnp.int32, g_m.shape, 0)
    gates = jnp.concatenate(
        [jnp.where(row < H_M, g_m, 0.0),
         jnp.where(row < H_M, pltpu.roll(g_m, GATE_GROUP - H_M, axis=0), 0.0),
         slab[o_ff:o_ff + GATE_GROUP],
         jnp.zeros((LANES - 3 * GATE_GROUP, slab.shape[1]), F32)], axis=0)
    wd_o[...] = gates.T.astype(BF16)
    b = bin_ref[...]
    ba_o[...] = b[:, 0:o_mi]
    bb_o[...] = b[:, o_fq:o_ff]
    bc_o[...] = b[:, o_g:]
    bd_o[...] = gate_block(b)


def _prep_in_weights(p, vmem):
    d, n_in = p["w_in"].shape
    w_m = p["w_br_mlstm"].shape[0]
    w_f = p["w_br_fox"].shape[0]
    steps = 8
    o_ff = 4 * w_m + 2 * H_M + 3 * w_f
    assert (4 * w_m) % LANES == 0 and 2 * H_M <= GATE_GROUP + H_M <= LANES
    assert o_ff % LANES <= 2 * GATE_GROUP and o_ff % LANES + H_F <= LANES
    assert n_in == o_ff + H_F + 2 * d
    srcs = [p["w_in"].T, p["b_in"][None, :]]

    def row_spec(shape, tiled=True):
        if not tiled:
            return pl.BlockSpec(shape, lambda i: (0, 0))
        assert shape[0] % (steps * BF16_ROWS) == 0
        return pl.BlockSpec((shape[0] // steps, shape[1]), lambda i: (i, 0))

    out_shapes = [((d, 4 * w_m), BF16), ((d, 3 * w_f), BF16), ((d, 2 * d), BF16), ((d, LANES), BF16),
                  ((1, 4 * w_m), F32), ((1, 3 * w_f), F32), ((1, 2 * d), F32), ((1, LANES), F32)]
    return pl.pallas_call(
        functools.partial(_wprep_kernel, w_m=w_m, w_f=w_f),
        grid=(steps,),
        in_specs=([pl.BlockSpec((n_in, d // steps), lambda i: (0, i))]
                  + [row_spec(s.shape, tiled=s.shape[0] > 1) for s in srcs[1:]]),
        out_specs=[row_spec(s, tiled=s[0] > 1) for s, _ in out_shapes],
        out_shape=[jax.ShapeDtypeStruct(s, dt) for s, dt in out_shapes],
        compiler_params=pltpu.CompilerParams(dimension_semantics=("arbitrary",), vmem_limit_bytes=vmem),
        name="wprep",
    )(*srcs)


def _inproj_kernel(x_ref, g_ref, wa_ref, ba_ref, wb_ref, bb_ref, wc_ref, bc_ref, wd_ref, bd_ref,
                   cw_ref, cb_ref, ng_ref, *rest,
                   w_m, w_f, n_chunk, m_split, q_scale, k_scale, chunk, tk, tiles_per_seq, n_cast):
    cast_srcs, rest = rest[:n_cast], rest[n_cast:]
    (ym_ref, zb_ref, fvt_ref, zc_ref, csplit_ref), rest = rest[:5], rest[5:]
    cast_dsts, rest = rest[:n_cast], rest[n_cast:]
    zs_sc, gc_sc, zqk_ref, vt_ref, og_ref, rows_ref, cols_ref, colprev_sc, ct_sc = rest
    tm = x_ref.shape[0]
    seq_start = (pl.program_id(0) % tiles_per_seq) == 0

    @pl.when(seq_start)
    def _():
        zs_sc[...] = jnp.zeros(zs_sc.shape, F32)
        gc_sc[0:2 * GATE_GROUP] = jnp.zeros((2 * GATE_GROUP, LANES), F32)
        gc_sc[2 * GATE_GROUP:] = jnp.full((GATE_GROUP, LANES), NEG_BIG, F32)
        colprev_sc[...] = jnp.zeros(colprev_sc.shape, F32)
        ct_sc[...] = jnp.zeros(ct_sc.shape, F32)

    rows = tm // m_split
    hb = [_rms(x_ref[r0:r0 + rows, :], g_ref[...]).astype(BF16) for r0 in range(0, tm, rows)]

    def proj(w_ref, b_ref, c0, c1):
        w = w_ref[:, c0:c1]
        parts = [jnp.dot(h, w, preferred_element_type=F32) for h in hb]
        return jnp.concatenate(parts, axis=0) + b_ref[:, c0:c1]

    def qk_chunk(ci):
        c0 = ci * n_chunk
        cs = slice(c0, c0 + n_chunk)
        z = proj(wa_ref, ba_ref, c0, c0 + n_chunk)
        prev = zs_sc[ci]
        zs_sc[ci] = z[tm - SUBLANES:tm]
        zg = z.reshape(tm // SUBLANES, SUBLANES, n_chunk)
        row = lax.broadcasted_iota(jnp.int32, zg.shape, 1)
        w = cw_ref[:, cs]
        y = zg * w[CONV_K - 1:CONV_K] + cb_ref[:, cs]
        for s in range(1, CONV_K):
            rz = pltpu.roll(zg, s, axis=1)
            before = jnp.concatenate([pltpu.roll(prev, s, axis=0)[None], rz[:-1]], axis=0)
            y = y + jnp.where(row < s, before, rz) * w[CONV_K - 1 - s:CONV_K - s]
        act = y * jax.nn.sigmoid(y)
        if c0 >= w_m:
            act = act * k_scale
        zqk_ref[:, cs] = act.reshape(tm, n_chunk).astype(BF16)

    def v_chunk(i):
        c0 = 2 * w_m + i * n_chunk
        z = proj(wa_ref, ba_ref, c0, c0 + n_chunk)
        for cc in range(tm // chunk):
            for f0 in range(0, n_chunk, LANES):
                blk = z[cc * chunk:(cc + 1) * chunk, f0:f0 + LANES]
                r0 = i * n_chunk + f0
                vt_ref[cc, r0:r0 + LANES, :] = blk.T.astype(BF16)

    def og_chunk(i):
        c0 = 3 * w_m + i * n_chunk
        z = proj(wa_ref, ba_ref, c0, c0 + n_chunk)
        og_ref[:, i * n_chunk:(i + 1) * n_chunk] = jax.nn.sigmoid(z).astype(BF16)

    def b_chunk(i):
        c0 = i * n_chunk
        z = proj(wb_ref, bb_ref, c0, c0 + n_chunk)
        if c0 < w_f:
            z = z * q_scale
        if c0 < 2 * w_f:
            zb_ref[:, c0:c0 + n_chunk] = z.astype(BF16)
        else:
            for jb in range(tm // tk):
                for f0 in range(0, n_chunk, LANES):
                    blk = z[jb * tk:(jb + 1) * tk, f0:f0 + LANES]
                    r0 = c0 - 2 * w_f + f0
                    fvt_ref[jb, r0:r0 + LANES, :] = blk.T.astype(BF16)

    def c_chunk(i):
        c0 = i * n_chunk
        z = proj(wc_ref, bc_ref, c0, c0 + n_chunk)
        zc_ref[:, c0:c0 + n_chunk] = jax.nn.sigmoid(z).astype(BF16)

    def d_chunk(_):
        zd = proj(wd_ref, bd_ref, 0, LANES)
        rows, cols, csplit, gc_sc[...] = _gate_tables(zd.T[:N_GATE_ROWS, :], gc_sc[...])
        rows_ref[...] = rows
        cols_ref[...] = cols
        csplit_ref[...] = csplit

    mlstm = _MlstmTile(zqk_ref, vt_ref, og_ref, rows_ref, cols_ref, colprev_sc, ng_ref, ym_ref,
                       ct_sc, w_m=w_m, chunk=chunk)
    order = [(d_chunk, 0)]
    order += [(qk_chunk, i) for i in range(2 * w_m // n_chunk)]
    order += [(v_chunk, i) for i in range(w_m // n_chunk)]
    order += [(og_chunk, i) for i in range(w_m // n_chunk)]
    fill = ([(c_chunk, i) for i in range(wc_ref.shape[1] // n_chunk)]
            + [(b_chunk, i) for i in range(3 * w_f // n_chunk)])
    for c in range(tm // chunk):
        order += [(mlstm.state_stage, c), fill.pop(0), (mlstm.output_stage, c), fill.pop(0)]
    order += fill
    for fn, i in order:
        fn(i)
    colprev_sc[...] = cols_ref[tm - SUBLANES:tm, :]
    for src, dst in zip(cast_srcs, cast_dsts):
        dst[...] = src[...].astype(BF16)


def _scan_lanes(x, op, fill):
    n = x.shape[-1]
    pos = lax.broadcasted_iota(jnp.int32, x.shape, 1)
    s = 1
    while s < n:
        shifted = pltpu.roll(x, s, axis=1)
        x = op(x, jnp.where(pos >= s, shifted, fill))
        s *= 2
    return x


def _gate_tables(z, carry):
    width = z.shape[1]
    i8 = z[0:GATE_GROUP]
    cum = (_scan_lanes(_log_sigmoid(z[GATE_GROUP:3 * GATE_GROUP]), jnp.add, 0.0)
           + carry[0:2 * GATE_GROUP, 0:1])
    f8 = cum[0:GATE_GROUP]
    g8 = i8 - f8
    cmax = jnp.maximum(_scan_lanes(g8, jnp.maximum, NEG_BIG), carry[2 * GATE_GROUP:, 0:1])
    new_carry = jnp.concatenate(
        [jnp.broadcast_to(cum[:, width - 1:width], (2 * GATE_GROUP, LANES)),
         jnp.broadcast_to(cmax[:, width - 1:width], (GATE_GROUP, LANES))], axis=0)
    m8 = jnp.maximum(cmax, 0.0)
    en8 = jnp.exp(-(f8 + m8))
    cf8 = cum[GATE_GROUP:2 * GATE_GROUP] * LOG2E
    g8, m8 = g8 * LOG2E, m8 * LOG2E
    rows = jnp.concatenate([g8, m8, en8], axis=0)
    stack = jnp.concatenate(
        [g8, m8, jnp.zeros((LANES - 2 * GATE_GROUP, width), F32)], axis=0)
    cols = stack.T
    hi = cf8.astype(BF16).astype(F32)
    r1 = cf8 - hi
    lo = r1.astype(BF16).astype(F32)
    lo2 = (r1 - lo).astype(BF16).astype(F32)
    ones = jnp.where(lax.broadcasted_iota(jnp.int32, (GATE_GROUP, width), 0) == 0, 1.0, 0.0)
    split = jnp.concatenate(
        [hi, lo, lo2, ones, jnp.zeros((LANES - 4 * GATE_GROUP, width), F32)], axis=0)
    return rows, cols, split.T.astype(BF16), new_carry


class _MlstmTile:
    def __init__(self, zqk_ref, vt_ref, og_ref, rows_ref, cols_ref, colprev_ref, ng_ref, ym_ref,
                 ct_sc, *, w_m, chunk):
        self.refs = (zqk_ref, vt_ref, og_ref, rows_ref, cols_ref, colprev_ref, ng_ref, ym_ref, ct_sc)
        self.w_m, self.chunk, self.dh, self.hc = w_m, chunk, w_m // H_M, chunk // 2
        hc = self.hc
        self.causal = (lax.broadcasted_iota(jnp.int32, (hc, hc), 0)
                       <= lax.broadcasted_iota(jnp.int32, (hc, hc), 1))
        self.ones_rows = jnp.where(
            lax.broadcasted_iota(jnp.int32, (BF16_ROWS, chunk), 0) == 0, 1.0, 0.0).astype(BF16)
        self.pending = {}

    def state_stage(self, c):
        zqk_ref, vt_ref, _, rows_ref, cols_ref, colprev_ref, _, _, ct_sc = self.refs
        w_m, chunk, dh = self.w_m, self.chunk, self.dh
        r0 = c * chunk
        tc = slice(r0, r0 + chunk)
        last = cols_ref[r0 + chunk - 1:r0 + chunk, :]
        prev = cols_ref[r0 - 1:r0, :] if c else colprev_ref[SUBLANES - 1:SUBLANES, :]
        first, vas = [], []
        for h in range(H_M):
            qc = zqk_ref[tc, h * dh:(h + 1) * dh]
            kc = zqk_ref[tc, w_m + h * dh:w_m + (h + 1) * dh]
            lhs = jnp.concatenate([kc, ct_sc[h].astype(BF16)], axis=0)
            first.append(lax.dot_general(lhs, qc, NT_DIMS, preferred_element_type=F32))
            vas.append(jnp.concatenate([vt_ref[c, h * dh:(h + 1) * dh, :], self.ones_rows], axis=0))
        for h in range(H_M):
            kc = zqk_ref[tc, w_m + h * dh:w_m + (h + 1) * dh]
            g_row = rows_ref[h:h + 1, tc]
            m_e = last[:, GATE_GROUP + h:GATE_GROUP + h + 1]
            m_p = prev[:, GATE_GROUP + h:GATE_GROUP + h + 1]
            vaw = (vas[h].astype(F32) * jnp.exp2(g_row - m_e)).astype(BF16)
            ct_sc[h] = jnp.exp2(m_p - m_e) * ct_sc[h] + jnp.dot(vaw, kc, preferred_element_type=F32)
        self.pending[c] = (first, vas, prev)

    def output_stage(self, c):
        _, _, og_ref, rows_ref, cols_ref, _, ng_ref, ym_ref, _ = self.refs
        chunk, dh, hc, causal = self.chunk, self.dh, self.hc, self.causal
        first, vas, prev = self.pending.pop(c)
        r0 = c * chunk
        tc = slice(r0, r0 + chunk)
        colsc = cols_ref[tc, :]
        for h in range(H_M):
            ch = slice(h * dh, (h + 1) * dh)
            g_col = colsc[:, h:h + 1]
            m_row = rows_ref[GATE_GROUP + h:GATE_GROUP + h + 1, tc]
            en_row = rows_ref[2 * GATE_GROUP + h:2 * GATE_GROUP + h + 1, tc]
            m_p = prev[:, GATE_GROUP + h:GATE_GROUP + h + 1]
            sk = first[h]
            d00 = jnp.where(causal, jnp.exp2(g_col[0:hc] - m_row[:, 0:hc]), 0.0)
            d01 = jnp.exp2(g_col[0:hc] - m_row[:, hc:chunk])
            d11 = jnp.where(causal, jnp.exp2(g_col[hc:chunk] - m_row[:, hc:chunk]), 0.0)
            top = jnp.concatenate([sk[0:hc, 0:hc] * d00, sk[0:hc, hc:chunk] * d01], axis=1)
            bot = jnp.concatenate([jnp.zeros((hc, hc), F32), sk[hc:chunk, hc:chunk] * d11], axis=1)
            sqk = jnp.concatenate([top, bot], axis=0).astype(BF16)
            nd = (jnp.exp2(m_p - m_row) * first[h][chunk:]
                  + jnp.dot(vas[h], sqk, preferred_element_type=F32))
            den = nd[dh:dh + 1]
            ht = nd[0:dh] * (1.0 / jnp.maximum(jnp.abs(den), en_row))
            hn = ht * lax.rsqrt(jnp.mean(ht * ht, axis=0, keepdims=True) + EPS)
            og = og_ref[tc, ch].astype(F32)
            ym_ref[tc, ch] = (hn.T * ng_ref[:, ch] * og).astype(BF16)


def _fox_kernel(q_ref, k_ref, fvt_ref, csq_ref, csk_ref, qsel_ref, ksel_ref, yf_ref,
                kaug_sc, qaug_sc, m_sc, acc_sc, *, tq, tk):
    qi = pl.program_id(1)

    seq = k_ref.shape[0]
    dhp = LANES
    half = dhp // 2
    v_rows = half + BF16_ROWS

    def own_lanes(head, rows):
        lane = lax.broadcasted_iota(jnp.int32, (rows, dhp), 1)
        return (lane >= half) if head % 2 else (lane < half)

    def augment(x_ref, cs_ref, sel_ref, dst_sc, rows):
        cs = cs_ref[...]
        for p in range(H_F // 2):
            xp = x_ref[:, p * dhp:(p + 1) * dhp]
            bias = jnp.dot(cs, sel_ref[p], preferred_element_type=F32).astype(BF16)
            for head in (2 * p, 2 * p + 1):
                b0 = (head % 2) * dhp
                dst_sc[head] = jnp.where(own_lanes(head, rows), xp, bias[:, b0:b0 + dhp])

    @pl.when(qi == 0)
    def _():
        augment(k_ref, csk_ref, ksel_ref, kaug_sc, seq)

    augment(q_ref, csq_ref, qsel_ref, qaug_sc, tq)
    m_sc[...] = jnp.full(m_sc.shape, NEG_BIG, F32)
    acc_sc[...] = jnp.zeros(acc_sc.shape, F32)

    ratio = tq // tk
    ones_rows = jnp.where(
        lax.broadcasted_iota(jnp.int32, (BF16_ROWS, tk), 0) == 0, 1.0, 0.0).astype(BF16)

    def vaug(head, j):
        return jnp.concatenate([fvt_ref[j, head * half:(head + 1) * half, :], ones_rows], axis=0)

    def step(j, diag):
        q0 = 0 if diag is None else diag * tk
        nq = tq - q0
        k0 = pl.multiple_of(j * tk, tk)
        sts = [lax.dot_general(kaug_sc[head, pl.ds(k0, tk), :], qaug_sc[head, q0:tq, :], NT_DIMS,
                               preferred_element_type=F32) for head in range(H_F)]
        if diag is not None:
            causal = (lax.broadcasted_iota(jnp.int32, (tk, nq), 0)
                      <= lax.broadcasted_iota(jnp.int32, (tk, nq), 1))
        for head in range(H_F):
            st = sts[head]
            if diag is not None:
                st = jnp.where(causal, st, NEG_BIG)
            m = m_sc[head, :, q0:tq]
            m_new = jnp.maximum(m, jnp.max(st, axis=0, keepdims=True))
            alpha = jnp.exp2(m - m_new)
            pt = jnp.exp2(st - m_new).astype(BF16)
            m_sc[head, :, q0:tq] = m_new
            acc_sc[head, :, q0:tq] = (alpha * acc_sc[head, :, q0:tq]
                                      + jnp.dot(vaug(head, j), pt, preferred_element_type=F32))

    def loop_body(jj, carry):
        for r in range(ratio):
            step(jj * ratio + r, None)
        return carry

    lax.fori_loop(0, qi, loop_body, 0)
    for diag in range(ratio):
        step(qi * ratio + diag, diag)
    for p in range(H_F // 2):
        outs = []
        for head in (2 * p, 2 * p + 1):
            acc = acc_sc[head]
            outs.append(acc[0:half] * (1.0 / acc[half:half + 1]))
        yf_ref[:, p * dhp:(p + 1) * dhp] = jnp.concatenate(outs, axis=0).T.astype(BF16)


def _post_kernel(x_ref, ym_ref, yf_ref, zc_ref, wbm_ref, wbf_ref, wo_ref, g2_ref,
                 wg_ref, wu_ref, wd_ref, gfin_ref, o_ref, *, tf):
    d = x_ref.shape[1]
    d_ff = wg_ref.shape[1]
    bm = jnp.dot(ym_ref[...], wbm_ref[...], preferred_element_type=F32)
    bf = jnp.dot(yf_ref[...], wbf_ref[...], preferred_element_type=F32)
    mix = zc_ref[:, 0:d].astype(F32) * bm + zc_ref[:, d:2 * d].astype(F32) * bf
    x1 = x_ref[...] + jnp.dot(mix.astype(BF16), wo_ref[...], preferred_element_type=F32)
    h2 = _rms(x1, g2_ref[...]).astype(BF16)
    acc = jnp.zeros(x1.shape, F32)
    for f0 in range(0, d_ff, tf):
        g = jnp.dot(h2, wg_ref[:, f0:f0 + tf], preferred_element_type=F32)
        u = jnp.dot(h2, wu_ref[:, f0:f0 + tf], preferred_element_type=F32)
        act = (g * jax.nn.sigmoid(g) * u).astype(BF16)
        acc = acc + jnp.dot(act, wd_ref[f0:f0 + tf, :], preferred_element_type=F32)
    o_ref[...] = _rms(x1 + acc, gfin_ref[...])


def _bias_selectors():
    ones_lane = 3 * GATE_GROUP
    qsel = np.zeros((H_F // 2, LANES, 2 * LANES), np.float32)
    ksel = np.zeros((H_F // 2, LANES, 2 * LANES), np.float32)
    for h in range(H_F):
        p0 = (h % 2) * LANES + (LANES // 2 if h % 2 == 0 else 0)
        for c in range(3):
            qsel[h // 2, GATE_GROUP * c + h, p0 + c] = 1.0
            qsel[h // 2, ones_lane, p0 + 3 + c] = 1.0
            ksel[h // 2, ones_lane, p0 + c] = 1.0
            ksel[h // 2, GATE_GROUP * c + h, p0 + 3 + c] = -1.0
    return jnp.asarray(qsel, BF16), jnp.asarray(ksel, BF16)


def _layer(x2d, batch, seq, p, cfg):
    t, d = x2d.shape
    w_m = p["w_br_mlstm"].shape[0]
    w_f = p["w_br_fox"].shape[0]
    dh_m = w_m // H_M
    dh_f = w_f // H_F
    vmem = cfg["vmem_limit"]

    wa, wb, wc, wd, ba, bb, bc, bd = _prep_in_weights(p, vmem)

    tm = cfg["tm_in"]
    chunk = cfg["chunk"]
    tq, tk = cfg["tq"], cfg["tk"]
    n_steps = t // tm
    tps = seq // tm
    assert seq % tm == 0 and tm % chunk == 0 and tm % tk == 0

    later = [p["w_br_mlstm"], p["w_br_fox"], p["w_out"], p["w_gate"], p["w_up"], p["w_down"]]

    def slice_spec(rows, cols):
        hold = 1
        while (rows * hold) % (n_steps * BF16_ROWS):
            hold *= 2
        assert hold <= n_steps
        return pl.BlockSpec((rows * hold // n_steps, cols), lambda i: (i // hold, 0))

    later_specs = [slice_spec(*w.shape) for w in later]
    outs = pl.pallas_call(
        functools.partial(_inproj_kernel, w_m=w_m, w_f=w_f, n_chunk=cfg["n_chunk"],
                          m_split=cfg["m_split"],
                          q_scale=dh_f ** -0.5 * LOG2E, k_scale=dh_m ** -0.5, chunk=chunk,
                          tk=tk, tiles_per_seq=tps, n_cast=len(later)),
        grid=(n_steps,),
        in_specs=[pl.BlockSpec((tm, d), lambda i: (i, 0)), _const_spec((1, d)),
                  _const_spec(wa.shape), _const_spec(ba.shape),
                  _const_spec(wb.shape), _const_spec(bb.shape),
                  _const_spec(wc.shape), _const_spec(bc.shape),
                  _const_spec(wd.shape), _const_spec(bd.shape),
                  _const_spec((CONV_K, 2 * w_m)), _const_spec((1, 2 * w_m)),
                  _const_spec((1, w_m))] + later_specs,
        out_specs=[pl.BlockSpec((tm, w_m), lambda i: (i, 0)),
                   pl.BlockSpec((tm, 2 * w_f), lambda i: (i, 0)),
                   pl.BlockSpec((tm // tk, w_f, tk), lambda i: (i, 0, 0)),
                   pl.BlockSpec((tm, 2 * d), lambda i: (i, 0)),
                   pl.BlockSpec((tm, LANES), lambda i: (i, 0))] + later_specs,
        out_shape=[jax.ShapeDtypeStruct((t, w_m), BF16),
                   jax.ShapeDtypeStruct((t, 2 * w_f), BF16),
                   jax.ShapeDtypeStruct((t // tk, w_f, tk), BF16),
                   jax.ShapeDtypeStruct((t, 2 * d), BF16),
                   jax.ShapeDtypeStruct((t, LANES), BF16)]
        + [jax.ShapeDtypeStruct(w.shape, BF16) for w in later],
        scratch_shapes=[pltpu.VMEM((2 * w_m // cfg["n_chunk"], SUBLANES, cfg["n_chunk"]), F32),
                        pltpu.VMEM((N_GATE_ROWS, LANES), F32),
                        pltpu.VMEM((tm, 2 * w_m), BF16),
                        pltpu.VMEM((tm // chunk, w_m, chunk), BF16),
                        pltpu.VMEM((tm, w_m), BF16),
                        pltpu.VMEM((N_GATE_ROWS, tm), F32),
                        pltpu.VMEM((tm, LANES), F32),
                        pltpu.VMEM((SUBLANES, LANES), F32),
                        pltpu.VMEM((H_M, dh_m + BF16_ROWS, dh_m), F32)],
        compiler_params=pltpu.CompilerParams(dimension_semantics=("arbitrary",), vmem_limit_bytes=vmem),
        name="inproj",
    )(x2d, p["norm1_g"][None, :], wa, ba, wb, bb, wc, bc, wd, bd, p["conv_w"], p["conv_b"][None, :],
      p["mlstm_norm_g"][None, :], *later)
    ym, zb, fvt, zc, csplit, wbm, wbf, wo, wg, wu, wdn = outs

    assert tq % tk == 0 and seq % tq == 0
    nq = seq // tq
    qsel, ksel = _bias_selectors()
    v_rows = LANES // 2 + BF16_ROWS
    yf = pl.pallas_call(
        functools.partial(_fox_kernel, tq=tq, tk=tk),
        grid=(batch, nq),
        in_specs=[pl.BlockSpec((tq, w_f), lambda b, i: (b * nq + i, 0)),
                  pl.BlockSpec((seq, w_f), lambda b, i: (b, 1)),
                  pl.BlockSpec((seq // tk, w_f, tk), lambda b, i: (b, 0, 0)),
                  pl.BlockSpec((tq, LANES), lambda b, i: (b * nq + i, 0)),
                  pl.BlockSpec((seq, LANES), lambda b, i: (b, 0)),
                  _const_spec(qsel.shape), _const_spec(ksel.shape)],
        out_specs=pl.BlockSpec((tq, w_f), lambda b, i: (b * nq + i, 0)),
        out_shape=jax.ShapeDtypeStruct((t, w_f), BF16),
        scratch_shapes=[pltpu.VMEM((H_F, seq, LANES), BF16),
                        pltpu.VMEM((H_F, tq, LANES), BF16),
                        pltpu.VMEM((H_F, 1, tq), F32),
                        pltpu.VMEM((H_F, v_rows, tq), F32)],
        compiler_params=pltpu.CompilerParams(dimension_semantics=("arbitrary", "arbitrary"),
                                             vmem_limit_bytes=vmem),
        name="fox",
    )(zb, zb, fvt, csplit, csplit, qsel, ksel)

    tmp = cfg["tm_post"]
    return pl.pallas_call(
        functools.partial(_post_kernel, tf=cfg["tf"]),
        grid=(t // tmp,),
        in_specs=[pl.BlockSpec((tmp, d), lambda i: (i, 0)),
                  pl.BlockSpec((tmp, w_m), lambda i: (i, 0)),
                  pl.BlockSpec((tmp, w_f), lambda i: (i, 0)),
                  pl.BlockSpec((tmp, 2 * d), lambda i: (i, 0)),
                  _const_spec(wbm.shape), _const_spec(wbf.shape), _const_spec(wo.shape),
                  _const_spec((1, d)), _const_spec(wg.shape), _const_spec(wu.shape),
                  _const_spec(wdn.shape), _const_spec((1, d))],
        out_specs=pl.BlockSpec((tmp, d), lambda i: (i, 0)),
        out_shape=jax.ShapeDtypeStruct((t, d), F32),
        compiler_params=pltpu.CompilerParams(dimension_semantics=("parallel",), vmem_limit_bytes=vmem),
        name="post",
    )(x2d, ym, yf, zc, wbm, wbf, wo, p["norm2_g"][None, :], wg, wu, wdn, p["norm_f_g"][None, :])


def kernel(x, norm1_g, w_in, b_in, conv_w, conv_b, mlstm_norm_g, w_br_mlstm, w_br_fox, w_out,
           norm2_g, w_gate, w_up, w_down, norm_f_g):
    batch, seq, d = x.shape
    depth = w_in.shape[0]
    assert depth == 1, "the final norm is fused into the single layer's last call"
    cfg = _cfg(batch, seq, d, w_gate.shape[-1])
    p = dict(norm1_g=norm1_g[0], w_in=w_in[0], b_in=b_in[0], conv_w=conv_w[0], conv_b=conv_b[0],
             mlstm_norm_g=mlstm_norm_g[0], w_br_mlstm=w_br_mlstm[0], w_br_fox=w_br_fox[0],
             w_out=w_out[0], norm2_g=norm2_g[0], w_gate=w_gate[0], w_up=w_up[0], w_down=w_down[0],
             norm_f_g=norm_f_g)
    out = _layer(x.reshape(batch * seq, d), batch, seq, p, cfg)
    return out.reshape(batch, seq, d)
```

```python
import functools

import jax
import jax.numpy as jnp
import numpy as np
from jax import lax
from jax.experimental import pallas as pl
from jax.experimental.pallas import tpu as pltpu

EPS = 1e-6
H_M = 4
H_F = 8
CONV_K = 4

LANES = 128
SUBLANES = 8
BF16_ROWS = 16
GATE_GROUP = 8
N_GATE_ROWS = 3 * GATE_GROUP
V7X_VMEM_BYTES = 64 * 1024 * 1024
VMEM_COMPILER_RESERVE = 8 * 1024 * 1024
NEG_BIG = -1e30
LOG2E = 1.4426950408889634

F32 = jnp.float32
BF16 = jnp.bfloat16
NT_DIMS = (((1,), (1,)), ((), ()))


def _cfg(batch, seq, d_model, d_ff):
    return dict(
        tm_in=512,
        n_chunk=256,
        m_split=4,
        chunk=256,
        tq=512,
        tk=256,
        tm_post=512,
        tf=256,
        vmem_limit=V7X_VMEM_BYTES - VMEM_COMPILER_RESERVE,
    )


def _const_spec(shape):
    nd = len(shape)
    return pl.BlockSpec(shape, lambda *_: (0,) * nd, pipeline_mode=pl.Buffered(1))


def _rms(x, g):
    return x * lax.rsqrt(jnp.mean(x * x, axis=-1, keepdims=True) + EPS) * g


def _log_sigmoid(x):
    return jnp.minimum(x, 0.0) - jnp.log1p(jnp.exp(-jnp.abs(x)))


def _wprep_kernel(wint_ref, bin_ref, wa_o, wb_o, wc_o, wd_o, ba_o, bb_o, bc_o, bd_o, *, w_m, w_f):
    o_mi = 4 * w_m
    o_mf = o_mi + H_M
    o_fq = o_mf + H_M
    o_ff = o_fq + 3 * w_f
    o_g = o_ff + H_F

    def gate_block(src):
        rows = src.shape[0]
        lane = lax.broadcasted_iota(jnp.int32, (rows, LANES), 1)
        blk_m = src[:, o_mi:o_mi + LANES]
        f0 = (o_ff // LANES) * LANES
        blk_f = src[:, f0:f0 + LANES]
        mi = jnp.where(lane < H_M, blk_m, 0.0)
        mf = jnp.where((lane >= GATE_GROUP) & (lane < GATE_GROUP + H_M),
                       pltpu.roll(blk_m, GATE_GROUP - H_M, axis=1), 0.0)
        ff = jnp.where((lane >= 2 * GATE_GROUP) & (lane < 2 * GATE_GROUP + H_F),
                       pltpu.roll(blk_f, 2 * GATE_GROUP - (o_ff - f0), axis=1), 0.0)
        return mi + mf + ff

    slab = wint_ref[...]
    wa_o[...] = slab[0:o_mi].T.astype(BF16)
    wb_o[...] = slab[o_fq:o_ff].T.astype(BF16)
    wc_o[...] = slab[o_g:].T.astype(BF16)
    g_m = slab[o_mi:o_mi + GATE_GROUP]
    row = lax.broadcasted_iota(jnp.int32, g_m.shape, 0)
    gates = jnp.concatenate(
        [jnp.where(row < H_M, g_m, 0.0),
         jnp.where(row < H_M, pltpu.roll(g_m, GATE_GROUP - H_M, axis=0), 0.0),
         slab[o_ff:o_ff + GATE_GROUP],
         jnp.zeros((LANES - 3 * GATE_GROUP, slab.shape[1]), F32)], axis=0)
    wd_o[...] = gates.T.astype(BF16)
    b = bin_ref[...]
    ba_o[...] = b[:, 0:o_mi]
    bb_o[...] = b[:, o_fq:o_ff]
    bc_o[...] = b[:, o_g:]
    bd_o[...] = gate_block(b)


def _prep_in_weights(p, vmem):
    d, n_in = p["w_in"].shape
    w_m = p["w_br_mlstm"].shape[0]
    w_f = p["w_br_fox"].shape[0]
    steps = 8
    o_ff = 4 * w_m + 2 * H_M + 3 * w_f
    assert (4 * w_m) % LANES == 0 and 2 * H_M <= GATE_GROUP + H_M <= LANES
    assert o_ff % LANES <= 2 * GATE_GROUP and o_ff % LANES + H_F <= LANES
    assert n_in == o_ff + H_F + 2 * d
    srcs = [p["w_in"].T, p["b_in"][None, :]]

    def row_spec(shape, tiled=True):
        if not tiled:
            return pl.BlockSpec(shape, lambda i: (0, 0))
        assert shape[0] % (steps * BF16_ROWS) == 0
        return pl.BlockSpec((shape[0] // steps, shape[1]), lambda i: (i, 0))

    out_shapes = [((d, 4 * w_m), BF16), ((d, 3 * w_f), BF16), ((d, 2 * d), BF16), ((d, LANES), BF16),
                  ((1, 4 * w_m), F32), ((1, 3 * w_f), F32), ((1, 2 * d), F32), ((1, LANES), F32)]
    return pl.pallas_call(
        functools.partial(_wprep_kernel, w_m=w_m, w_f=w_f),
        grid=(steps,),
        in_specs=([pl.BlockSpec((n_in, d // steps), lambda i: (0, i))]
                  + [row_spec(s.shape, tiled=s.shape[0] > 1) for s in srcs[1:]]),
        out_specs=[row_spec(s, tiled=s[0] > 1) for s, _ in out_shapes],
        out_shape=[jax.ShapeDtypeStruct(s, dt) for s, dt in out_shapes],
        compiler_params=pltpu.CompilerParams(dimension_semantics=("arbitrary",), vmem_limit_bytes=vmem),
        name="wprep",
    )(*srcs)


def _inproj_kernel(x_ref, g_ref, wa_ref, ba_ref, wb_ref, bb_ref, wc_ref, bc_ref, wd_ref, bd_ref,
                   cw_ref, cb_ref, ng_ref, *rest,
                   w_m, w_f, n_chunk, m_split, q_scale, k_scale, chunk, tk, tiles_per_seq, n_cast):
    cast_srcs, rest = rest[:n_cast], rest[n_cast:]
    (ym_ref, zb_ref, fvt_ref, zc_ref, csplit_ref), rest = rest[:5], rest[5:]
    cast_dsts, rest = rest[:n_cast], rest[n_cast:]
    zs_sc, gc_sc, zqk_ref, vt_ref, og_ref, rows_ref, cols_ref, colprev_sc, ct_sc = rest
    tm = x_ref.shape[0]
    seq_start = (pl.program_id(0) % tiles_per_seq) == 0

    @pl.when(seq_start)
    def _():
        zs_sc[...] = jnp.zeros(zs_sc.shape, F32)
        gc_sc[0:2 * GATE_GROUP] = jnp.zeros((2 * GATE_GROUP, LANES), F32)
        gc_sc[2 * GATE_GROUP:] = jnp.full((GATE_GROUP, LANES), NEG_BIG, F32)
        colprev_sc[...] = jnp.zeros(colprev_sc.shape, F32)
        ct_sc[...] = jnp.zeros(ct_sc.shape, F32)

    rows = tm // m_split
    hb = [_rms(x_ref[r0:r0 + rows, :], g_ref[...]).astype(BF16) for r0 in range(0, tm, rows)]

    def proj(w_ref, b_ref, c0, c1):
        w = w_ref[:, c0:c1]
        parts = [jnp.dot(h, w, preferred_element_type=F32) for h in hb]
        return jnp.concatenate(parts, axis=0) + b_ref[:, c0:c1]

    def qk_chunk(ci):
        c0 = ci * n_chunk
        cs = slice(c0, c0 + n_chunk)
        z = proj(wa_ref, ba_ref, c0, c0 + n_chunk)
        prev = zs_sc[ci]
        zs_sc[ci] = z[tm - SUBLANES:tm]
        zg = z.reshape(tm // SUBLANES, SUBLANES, n_chunk)
        row = lax.broadcasted_iota(jnp.int32, zg.shape, 1)
        w = cw_ref[:, cs]
        y = zg * w[CONV_K - 1:CONV_K] + cb_ref[:, cs]
        for s in range(1, CONV_K):
            rz = pltpu.roll(zg, s, axis=1)
            before = jnp.concatenate([pltpu.roll(prev, s, axis=0)[None], rz[:-1]], axis=0)
            y = y + jnp.where(row < s, before, rz) * w[CONV_K - 1 - s:CONV_K - s]
        act = y * jax.nn.sigmoid(y)
        if c0 >= w_m:
            act = act * k_scale
        zqk_ref[:, cs] = act.reshape(tm, n_chunk).astype(BF16)

    def v_chunk(i):
        c0 = 2 * w_m + i * n_chunk
        z = proj(wa_ref, ba_ref, c0, c0 + n_chunk)
        for cc in range(tm // chunk):
            for f0 in range(0, n_chunk, LANES):
                blk = z[cc * chunk:(cc + 1) * chunk, f0:f0 + LANES]
                r0 = i * n_chunk + f0
                vt_ref[cc, r0:r0 + LANES, :] = blk.T.astype(BF16)

    def og_chunk(i):
        c0 = 3 * w_m + i * n_chunk
        z = proj(wa_ref, ba_ref, c0, c0 + n_chunk)
        og_ref[:, i * n_chunk:(i + 1) * n_chunk] = jax.nn.sigmoid(z).astype(BF16)

    def b_chunk(i):
        c0 = i * n_chunk
        z = proj(wb_ref, bb_ref, c0, c0 + n_chunk)
        if c0 < w_f:
            z = z * q_scale
        if c0 < 2 * w_f:
            zb_ref[:, c0:c0 + n_chunk] = z.astype(BF16)
        else:
            for jb in range(tm // tk):
                for f0 in range(0, n_chunk, LANES):
                    blk = z[jb * tk:(jb + 1) * tk, f0:f0 + LANES]
                    r0 = c0 - 2 * w_f + f0
                    fvt_ref[jb, r0:r0 + LANES, :] = blk.T.astype(BF16)

    def c_chunk(i):
        c0 = i * n_chunk
        z = proj(wc_ref, bc_ref, c0, c0 + n_chunk)
        zc_ref[:, c0:c0 + n_chunk] = jax.nn.sigmoid(z).astype(BF16)

    def d_chunk(_):
        zd = proj(wd_ref, bd_ref, 0, LANES)
        rows, cols, csplit, gc_sc[...] = _gate_tables(zd.T[:N_GATE_ROWS, :], gc_sc[...])
        rows_ref[...] = rows
        cols_ref[...] = cols
        csplit_ref[...] = csplit

    mlstm = _MlstmTile(zqk_ref, vt_ref, og_ref, rows_ref, cols_ref, colprev_sc, ng_ref, ym_ref,
                       ct_sc, w_m=w_m, chunk=chunk)
    fill = ([(c_chunk, i) for i in range(wc_ref.shape[1] // n_chunk)]
            + [(b_chunk, i) for i in range(3 * w_f // n_chunk)])
    order = [(d_chunk, 0)]
    for i in range(2 * w_m // n_chunk):
        order += [(qk_chunk, i), fill.pop(0)]
    order += [(v_chunk, i) for i in range(w_m // n_chunk)]
    order += [(og_chunk, i) for i in range(w_m // n_chunk)]
    for c in range(tm // chunk):
        order += [(mlstm.state_stage, c), fill.pop(0), (mlstm.output_stage, c), fill.pop(0)]
    order += fill
    for fn, i in order:
        fn(i)
    colprev_sc[...] = cols_ref[tm - SUBLANES:tm, :]
    for src, dst in zip(cast_srcs, cast_dsts):
        dst[...] = src[...].astype(BF16)


def _scan_lanes(x, op, fill):
    n = x.shape[-1]
    pos = lax.broadcasted_iota(jnp.int32, x.shape, 1)
    s = 1
    while s < n:
        shifted = pltpu.roll(x, s, axis=1)
        x = op(x, jnp.where(pos >= s, shifted, fill))
        s *= 2
    return x


def _gate_tables(z, carry):
    width = z.shape[1]
    i8 = z[0:GATE_GROUP]
    cum = (_scan_lanes(_log_sigmoid(z[GATE_GROUP:3 * GATE_GROUP]), jnp.add, 0.0)
           + carry[0:2 * GATE_GROUP, 0:1])
    f8 = cum[0:GATE_GROUP]
    g8 = i8 - f8
    cmax = jnp.maximum(_scan_lanes(g8, jnp.maximum, NEG_BIG), carry[2 * GATE_GROUP:, 0:1])
    new_carry = jnp.concatenate(
        [jnp.broadcast_to(cum[:, width - 1:width], (2 * GATE_GROUP, LANES)),
         jnp.broadcast_to(cmax[:, width - 1:width], (GATE_GROUP, LANES))], axis=0)
    m8 = jnp.maximum(cmax, 0.0)
    en8 = jnp.exp(-(f8 + m8))
    cf8 = cum[GATE_GROUP:2 * GATE_GROUP] * LOG2E
    g8, m8 = g8 * LOG2E, m8 * LOG2E
    rows = jnp.concatenate([g8, m8, en8], axis=0)
    stack = jnp.concatenate(
        [g8, m8, jnp.zeros((LANES - 2 * GATE_GROUP, width), F32)], axis=0)
    cols = stack.T
    hi = cf8.astype(BF16).astype(F32)
    r1 = cf8 - hi
    lo = r1.astype(BF16).astype(F32)
    lo2 = (r1 - lo).astype(BF16).astype(F32)
    ones = jnp.where(lax.broadcasted_iota(jnp.int32, (GATE_GROUP, width), 0) == 0, 1.0, 0.0)
    split = jnp.concatenate(
        [hi, lo, lo2, ones, jnp.zeros((LANES - 4 * GATE_GROUP, width), F32)], axis=0)
    return rows, cols, split.T.astype(BF16), new_carry


class _MlstmTile:
    def __init__(self, zqk_ref, vt_ref, og_ref, rows_ref, cols_ref, colprev_ref, ng_ref, ym_ref,
                 ct_sc, *, w_m, chunk):
        self.refs = (zqk_ref, vt_ref, og_ref, rows_ref, cols_ref, colprev_ref, ng_ref, ym_ref, ct_sc)
        self.w_m, self.chunk, self.dh, self.hc = w_m, chunk, w_m // H_M, chunk // 2
        hc = self.hc
        self.causal = (lax.broadcasted_iota(jnp.int32, (hc, hc), 0)
                       <= lax.broadcasted_iota(jnp.int32, (hc, hc), 1))
        self.ones_rows = jnp.where(
            lax.broadcasted_iota(jnp.int32, (BF16_ROWS, chunk), 0) == 0, 1.0, 0.0).astype(BF16)
        self.pending = {}

    def state_stage(self, c):
        zqk_ref, vt_ref, _, rows_ref, cols_ref, colprev_ref, _, _, ct_sc = self.refs
        w_m, chunk, dh = self.w_m, self.chunk, self.dh
        r0 = c * chunk
        tc = slice(r0, r0 + chunk)
        last = cols_ref[r0 + chunk - 1:r0 + chunk, :]
        prev = cols_ref[r0 - 1:r0, :] if c else colprev_ref[SUBLANES - 1:SUBLANES, :]
        first, vas = [], []
        for h in range(H_M):
            qc = zqk_ref[tc, h * dh:(h + 1) * dh]
            kc = zqk_ref[tc, w_m + h * dh:w_m + (h + 1) * dh]
            lhs = jnp.concatenate([kc, ct_sc[h].astype(BF16)], axis=0)
            first.append(lax.dot_general(lhs, qc, NT_DIMS, preferred_element_type=F32))
            vas.append(jnp.concatenate([vt_ref[c, h * dh:(h + 1) * dh, :], self.ones_rows], axis=0))
        for h in range(H_M):
            kc = zqk_ref[tc, w_m + h * dh:w_m + (h + 1) * dh]
            g_row = rows_ref[h:h + 1, tc]
            m_e = last[:, GATE_GROUP + h:GATE_GROUP + h + 1]
            m_p = prev[:, GATE_GROUP + h:GATE_GROUP + h + 1]
            vaw = (vas[h].astype(F32) * jnp.exp2(g_row - m_e)).astype(BF16)
            ct_sc[h] = jnp.exp2(m_p - m_e) * ct_sc[h] + jnp.dot(vaw, kc, preferred_element_type=F32)
        self.pending[c] = (first, vas, prev)

    def output_stage(self, c):
        _, _, og_ref, rows_ref, cols_ref, _, ng_ref, ym_ref, _ = self.refs
        chunk, dh, hc, causal = self.chunk, self.dh, self.hc, self.causal
        first, vas, prev = self.pending.pop(c)
        r0 = c * chunk
        tc = slice(r0, r0 + chunk)
        colsc = cols_ref[tc, :]
        for h in range(H_M):
            ch = slice(h * dh, (h + 1) * dh)
            g_col = colsc[:, h:h + 1]
            m_row = rows_ref[GATE_GROUP + h:GATE_GROUP + h + 1, tc]
            en_row = rows_ref[2 * GATE_GROUP + h:2 * GATE_GROUP + h + 1, tc]
            m_p = prev[:, GATE_GROUP + h:GATE_GROUP + h + 1]
            sk = first[h]
            d00 = jnp.where(causal, jnp.exp2(g_col[0:hc] - m_row[:, 0:hc]), 0.0)
            d01 = jnp.exp2(g_col[0:hc] - m_row[:, hc:chunk])
            d11 = jnp.where(causal, jnp.exp2(g_col[hc:chunk] - m_row[:, hc:chunk]), 0.0)
            top = jnp.concatenate([sk[0:hc, 0:hc] * d00, sk[0:hc, hc:chunk] * d01], axis=1)
            bot = jnp.concatenate([jnp.zeros((hc, hc), F32), sk[hc:chunk, hc:chunk] * d11], axis=1)
            sqk = jnp.concatenate([top, bot], axis=0).astype(BF16)
            nd = (jnp.exp2(m_p - m_row) * first[h][chunk:]
                  + jnp.dot(vas[h], sqk, preferred_element_type=F32))
            den = nd[dh:dh + 1]
            ht = nd[0:dh] * (1.0 / jnp.maximum(jnp.abs(den), en_row))
            hn = ht * lax.rsqrt(jnp.mean(ht * ht, axis=0, keepdims=True) + EPS)
            og = og_ref[tc, ch].astype(F32)
            ym_ref[tc, ch] = (hn.T * ng_ref[:, ch] * og).astype(BF16)


def _fox_kernel(q_ref, k_ref, fvt_ref, csq_ref, csk_ref, qsel_ref, ksel_ref, yf_ref,
                kaug_sc, qaug_sc, m_sc, acc_sc, *, tq, tk):
    qi = pl.program_id(1)

    seq = k_ref.shape[0]
    dhp = LANES
    half = dhp // 2
    v_rows = half + BF16_ROWS

    def own_lanes(head, rows):
        lane = lax.broadcasted_iota(jnp.int32, (rows, dhp), 1)
        return (lane >= half) if head % 2 else (lane < half)

    def augment(x_ref, cs_ref, sel_ref, dst_sc, rows):
        cs = cs_ref[...]
        for p in range(H_F // 2):
            xp = x_ref[:, p * dhp:(p + 1) * dhp]
            bias = jnp.dot(cs, sel_ref[p], preferred_element_type=F32).astype(BF16)
            for head in (2 * p, 2 * p + 1):
                b0 = (head % 2) * dhp
                dst_sc[head] = jnp.where(own_lanes(head, rows), xp, bias[:, b0:b0 + dhp])

    @pl.when(qi == 0)
    def _():
        augment(k_ref, csk_ref, ksel_ref, kaug_sc, seq)

    augment(q_ref, csq_ref, qsel_ref, qaug_sc, tq)
    m_sc[...] = jnp.full(m_sc.shape, NEG_BIG, F32)
    acc_sc[...] = jnp.zeros(acc_sc.shape, F32)

    ratio = tq // tk
    ones_rows = jnp.where(
        lax.broadcasted_iota(jnp.int32, (BF16_ROWS, tk), 0) == 0, 1.0, 0.0).astype(BF16)

    def vaug(head, j):
        return jnp.concatenate([fvt_ref[j, head * half:(head + 1) * half, :], ones_rows], axis=0)

    def step(j, diag):
        q0 = 0 if diag is None else diag * tk
        nq = tq - q0
        k0 = pl.multiple_of(j * tk, tk)
        sts = [lax.dot_general(kaug_sc[head, pl.ds(k0, tk), :], qaug_sc[head, q0:tq, :], NT_DIMS,
                               preferred_element_type=F32) for head in range(H_F)]
        if diag is not None:
            causal = (lax.broadcasted_iota(jnp.int32, (tk, nq), 0)
                      <= lax.broadcasted_iota(jnp.int32, (tk, nq), 1))
        for head in range(H_F):
            st = sts[head]
            if diag is not None:
                st = jnp.where(causal, st, NEG_BIG)
            m = m_sc[head, :, q0:tq]
            m_new = jnp.maximum(m, jnp.max(st, axis=0, keepdims=True))
            alpha = jnp.exp2(m - m_new)
            pt = jnp.exp2(st - m_new).astype(BF16)
            m_sc[head, :, q0:tq] = m_new
            acc_sc[head, :, q0:tq] = (alpha * acc_sc[head, :, q0:tq]
                                      + jnp.dot(vaug(head, j), pt, preferred_element_type=F32))

    def loop_body(jj, carry):
        for r in range(ratio):
            step(jj * ratio + r, None)
        return carry

    lax.fori_loop(0, qi, loop_body, 0)
    for diag in range(ratio):
        step(qi * ratio + diag, diag)
    for p in range(H_F // 2):
        outs = []
        for head in (2 * p, 2 * p + 1):
            acc = acc_sc[head]
            outs.append(acc[0:half] * (1.0 / acc[half:half + 1]))
        yf_ref[:, p * dhp:(p + 1) * dhp] = jnp.concatenate(outs, axis=0).T.astype(BF16)


def _post_kernel(x_ref, ym_ref, yf_ref, zc_ref, wbm_ref, wbf_ref, wo_ref, g2_ref,
                 wg_ref, wu_ref, wd_ref, gfin_ref, o_ref, *, tf):
    d = x_ref.shape[1]
    d_ff = wg_ref.shape[1]
    bm = jnp.dot(ym_ref[...], wbm_ref[...], preferred_element_type=F32)
    bf = jnp.dot(yf_ref[...], wbf_ref[...], preferred_element_type=F32)
    mix = zc_ref[:, 0:d].astype(F32) * bm + zc_ref[:, d:2 * d].astype(F32) * bf
    x1 = x_ref[...] + jnp.dot(mix.astype(BF16), wo_ref[...], preferred_element_type=F32)
    h2 = _rms(x1, g2_ref[...]).astype(BF16)
    acc = jnp.zeros(x1.shape, F32)
    for f0 in range(0, d_ff, tf):
        g = jnp.dot(h2, wg_ref[:, f0:f0 + tf], preferred_element_type=F32)
        u = jnp.dot(h2, wu_ref[:, f0:f0 + tf], preferred_element_type=F32)
        act = (g * jax.nn.sigmoid(g) * u).astype(BF16)
        acc = acc + jnp.dot(act, wd_ref[f0:f0 + tf, :], preferred_element_type=F32)
    o_ref[...] = _rms(x1 + acc, gfin_ref[...])


def _bias_selectors():
    ones_lane = 3 * GATE_GROUP
    qsel = np.zeros((H_F // 2, LANES, 2 * LANES), np.float32)
    ksel = np.zeros((H_F // 2, LANES, 2 * LANES), np.float32)
    for h in range(H_F):
        p0 = (h % 2) * LANES + (LANES // 2 if h % 2 == 0 else 0)
        for c in range(3):
            qsel[h // 2, GATE_GROUP * c + h, p0 + c] = 1.0
            qsel[h // 2, ones_lane, p0 + 3 + c] = 1.0
            ksel[h // 2, ones_lane, p0 + c] = 1.0
            ksel[h // 2, GATE_GROUP * c + h, p0 + 3 + c] = -1.0
    return jnp.asarray(qsel, BF16), jnp.asarray(ksel, BF16)


def _layer(x2d, batch, seq, p, cfg):
    t, d = x2d.shape
    w_m = p["w_br_mlstm"].shape[0]
    w_f = p["w_br_fox"].shape[0]
    dh_m = w_m // H_M
    dh_f = w_f // H_F
    vmem = cfg["vmem_limit"]

    wa, wb, wc, wd, ba, bb, bc, bd = _prep_in_weights(p, vmem)

    tm = cfg["tm_in"]
    chunk = cfg["chunk"]
    tq, tk = cfg["tq"], cfg["tk"]
    n_steps = t // tm
    tps = seq // tm
    assert seq % tm == 0 and tm % chunk == 0 and tm % tk == 0

    later = [p["w_br_mlstm"], p["w_br_fox"], p["w_out"], p["w_gate"], p["w_up"], p["w_down"]]

    def slice_spec(rows, cols):
        hold = 1
        while (rows * hold) % (n_steps * BF16_ROWS):
            hold *= 2
        assert hold <= n_steps
        return pl.BlockSpec((rows * hold // n_steps, cols), lambda i: (i // hold, 0))

    later_specs = [slice_spec(*w.shape) for w in later]
    outs = pl.pallas_call(
        functools.partial(_inproj_kernel, w_m=w_m, w_f=w_f, n_chunk=cfg["n_chunk"],
                          m_split=cfg["m_split"],
                          q_scale=dh_f ** -0.5 * LOG2E, k_scale=dh_m ** -0.5, chunk=chunk,
                          tk=tk, tiles_per_seq=tps, n_cast=len(later)),
        grid=(n_steps,),
        in_specs=[pl.BlockSpec((tm, d), lambda i: (i, 0)), _const_spec((1, d)),
                  _const_spec(wa.shape), _const_spec(ba.shape),
                  _const_spec(wb.shape), _const_spec(bb.shape),
                  _const_spec(wc.shape), _const_spec(bc.shape),
                  _const_spec(wd.shape), _const_spec(bd.shape),
                  _const_spec((CONV_K, 2 * w_m)), _const_spec((1, 2 * w_m)),
                  _const_spec((1, w_m))] + later_specs,
        out_specs=[pl.BlockSpec((tm, w_m), lambda i: (i, 0)),
                   pl.BlockSpec((tm, 2 * w_f), lambda i: (i, 0)),
                   pl.BlockSpec((tm // tk, w_f, tk), lambda i: (i, 0, 0)),
                   pl.BlockSpec((tm, 2 * d), lambda i: (i, 0)),
                   pl.BlockSpec((tm, LANES), lambda i: (i, 0))] + later_specs,
        out_shape=[jax.ShapeDtypeStruct((t, w_m), BF16),
                   jax.ShapeDtypeStruct((t, 2 * w_f), BF16),
                   jax.ShapeDtypeStruct((t // tk, w_f, tk), BF16),
                   jax.ShapeDtypeStruct((t, 2 * d), BF16),
                   jax.ShapeDtypeStruct((t, LANES), BF16)]
        + [jax.ShapeDtypeStruct(w.shape, BF16) for w in later],
        scratch_shapes=[pltpu.VMEM((2 * w_m // cfg["n_chunk"], SUBLANES, cfg["n_chunk"]), F32),
                        pltpu.VMEM((N_GATE_ROWS, LANES), F32),
                        pltpu.VMEM((tm, 2 * w_m), BF16),
                        pltpu.VMEM((tm // chunk, w_m, chunk), BF16),
                        pltpu.VMEM((tm, w_m), BF16),
                        pltpu.VMEM((N_GATE_ROWS, tm), F32),
                        pltpu.VMEM((tm, LANES), F32),
                        pltpu.VMEM((SUBLANES, LANES), F32),
                        pltpu.VMEM((H_M, dh_m + BF16_ROWS, dh_m), F32)],
        compiler_params=pltpu.CompilerParams(dimension_semantics=("arbitrary",), vmem_limit_bytes=vmem),
        name="inproj",
    )(x2d, p["norm1_g"][None, :], wa, ba, wb, bb, wc, bc, wd, bd, p["conv_w"], p["conv_b"][None, :],
      p["mlstm_norm_g"][None, :], *later)
    ym, zb, fvt, zc, csplit, wbm, wbf, wo, wg, wu, wdn = outs

    assert tq % tk == 0 and seq % tq == 0
    nq = seq // tq
    qsel, ksel = _bias_selectors()
    v_rows = LANES // 2 + BF16_ROWS
    yf = pl.pallas_call(
        functools.partial(_fox_kernel, tq=tq, tk=tk),
        grid=(batch, nq),
        in_specs=[pl.BlockSpec((tq, w_f), lambda b, i: (b * nq + i, 0)),
                  pl.BlockSpec((seq, w_f), lambda b, i: (b, 1)),
                  pl.BlockSpec((seq // tk, w_f, tk), lambda b, i: (b, 0, 0)),
                  pl.BlockSpec((tq, LANES), lambda b, i: (b * nq + i, 0)),
                  pl.BlockSpec((seq, LANES), lambda b, i: (b, 0)),
                  _const_spec(qsel.shape), _const_spec(ksel.shape)],
        out_specs=pl.BlockSpec((tq, w_f), lambda b, i: (b * nq + i, 0)),
        out_shape=jax.ShapeDtypeStruct((t, w_f), BF16),
        scratch_shapes=[pltpu.VMEM((H_F, seq, LANES), BF16),
                        pltpu.VMEM((H_F, tq, LANES), BF16),
                        pltpu.VMEM((H_F, 1, tq), F32),
                        pltpu.VMEM((H_F, v_rows, tq), F32)],
        compiler_params=pltpu.CompilerParams(dimension_semantics=("arbitrary", "arbitrary"),
                                             vmem_limit_bytes=vmem),
        name="fox",
    )(zb, zb, fvt, csplit, csplit, qsel, ksel)

    tmp = cfg["tm_post"]
    return pl.pallas_call(
        functools.partial(_post_kernel, tf=cfg["tf"]),
        grid=(t // tmp,),
        in_specs=[pl.BlockSpec((tmp, d), lambda i: (i, 0)),
                  pl.BlockSpec((tmp, w_m), lambda i: (i, 0)),
                  pl.BlockSpec((tmp, w_f), lambda i: (i, 0)),
                  pl.BlockSpec((tmp, 2 * d), lambda i: (i, 0)),
                  _const_spec(wbm.shape), _const_spec(wbf.shape), _const_spec(wo.shape),
                  _const_spec((1, d)), _const_spec(wg.shape), _const_spec(wu.shape),
                  _const_spec(wdn.shape), _const_spec((1, d))],
        out_specs=pl.BlockSpec((tmp, d), lambda i: (i, 0)),
        out_shape=jax.ShapeDtypeStruct((t, d), F32),
        compiler_params=pltpu.CompilerParams(dimension_semantics=("parallel",), vmem_limit_bytes=vmem),
        name="post",
    )(x2d, ym, yf, zc, wbm, wbf, wo, p["norm2_g"][None, :], wg, wu, wdn, p["norm_f_g"][None, :])


def kernel(x, norm1_g, w_in, b_in, conv_w, conv_b, mlstm_norm_g, w_br_mlstm, w_br_fox, w_out,
           norm2_g, w_gate, w_up, w_down, norm_f_g):
    batch, seq, d = x.shape
    depth = w_in.shape[0]
    assert depth == 1, "the final norm is fused into the single layer's last call"
    cfg = _cfg(batch, seq, d, w_gate.shape[-1])
    p = dict(norm1_g=norm1_g[0], w_in=w_in[0], b_in=b_in[0], conv_w=conv_w[0], conv_b=conv_b[0],
             mlstm_norm_g=mlstm_norm_g[0], w_br_mlstm=w_br_mlstm[0], w_br_fox=w_br_fox[0],
             w_out=w_out[0], norm2_g=norm2_g[0], w_gate=w_gate[0], w_up=w_up[0], w_down=w_down[0],
             norm_f_g=norm_f_g)
    out = _layer(x.reshape(batch * seq, d), batch, seq, p, cfg)
    return out.reshape(batch, seq, d)
```

```python
import functools

import jax
import jax.numpy as jnp
import numpy as np
from jax import lax
from jax.experimental import pallas as pl
from jax.experimental.pallas import tpu as pltpu

EPS = 1e-6
H_M = 4
H_F = 8
CONV_K = 4

LANES = 128
SUBLANES = 8
BF16_ROWS = 16
GATE_GROUP = 8
N_GATE_ROWS = 3 * GATE_GROUP
V7X_VMEM_BYTES = 64 * 1024 * 1024
VMEM_COMPILER_RESERVE = 8 * 1024 * 1024
NEG_BIG = -1e30
LOG2E = 1.4426950408889634

F32 = jnp.float32
BF16 = jnp.bfloat16
NT_DIMS = (((1,), (1,)), ((), ()))


def _cfg(batch, seq, d_model, d_ff):
    return dict(
        tm_in=1024,
        n_chunk=256,
        m_split=8,
        chunk=256,
        tq=512,
        tk=256,
        tm_post=512,
        tf=256,
        vmem_limit=V7X_VMEM_BYTES - VMEM_COMPILER_RESERVE,
    )


def _const_spec(shape):
    nd = len(shape)
    return pl.BlockSpec(shape, lambda *_: (0,) * nd, pipeline_mode=pl.Buffered(1))


def _rms(x, g):
    return x * lax.rsqrt(jnp.mean(x * x, axis=-1, keepdims=True) + EPS) * g


def _log_sigmoid(x):
    return jnp.minimum(x, 0.0) - jnp.log1p(jnp.exp(-jnp.abs(x)))


def _wprep_kernel(wint_ref, bin_ref, wa_o, wb_o, wc_o, wd_o, ba_o, bb_o, bc_o, bd_o, *, w_m, w_f):
    o_mi = 4 * w_m
    o_mf = o_mi + H_M
    o_fq = o_mf + H_M
    o_ff = o_fq + 3 * w_f
    o_g = o_ff + H_F

    def gate_block(src):
        rows = src.shape[0]
        lane = lax.broadcasted_iota(jnp.int32, (rows, LANES), 1)
        blk_m = src[:, o_mi:o_mi + LANES]
        f0 = (o_ff // LANES) * LANES
        blk_f = src[:, f0:f0 + LANES]
        mi = jnp.where(lane < H_M, blk_m, 0.0)
        mf = jnp.where((lane >= GATE_GROUP) & (lane < GATE_GROUP + H_M),
                       pltpu.roll(blk_m, GATE_GROUP - H_M, axis=1), 0.0)
        ff = jnp.where((lane >= 2 * GATE_GROUP) & (lane < 2 * GATE_GROUP + H_F),
                       pltpu.roll(blk_f, 2 * GATE_GROUP - (o_ff - f0), axis=1), 0.0)
        return mi + mf + ff

    slab = wint_ref[...]
    wa_o[...] = slab[0:o_mi].T.astype(BF16)
    wb_o[...] = slab[o_fq:o_ff].T.astype(BF16)
    wc_o[...] = slab[o_g:].T.astype(BF16)
    g_m = slab[o_mi:o_mi + GATE_GROUP]
    row = lax.broadcasted_iota(jnp.int32, g_m.shape, 0)
    gates = jnp.concatenate(
        [jnp.where(row < H_M, g_m, 0.0),
         jnp.where(row < H_M, pltpu.roll(g_m, GATE_GROUP - H_M, axis=0), 0.0),
         slab[o_ff:o_ff + GATE_GROUP],
         jnp.zeros((LANES - 3 * GATE_GROUP, slab.shape[1]), F32)], axis=0)
    wd_o[...] = gates.T.astype(BF16)
    b = bin_ref[...]
    ba_o[...] = b[:, 0:o_mi]
    bb_o[...] = b[:, o_fq:o_ff]
    bc_o[...] = b[:, o_g:]
    bd_o[...] = gate_block(b)


def _prep_in_weights(p, vmem):
    d, n_in = p["w_in"].shape
    w_m = p["w_br_mlstm"].shape[0]
    w_f = p["w_br_fox"].shape[0]
    steps = 8
    o_ff = 4 * w_m + 2 * H_M + 3 * w_f
    assert (4 * w_m) % LANES == 0 and 2 * H_M <= GATE_GROUP + H_M <= LANES
    assert o_ff % LANES <= 2 * GATE_GROUP and o_ff % LANES + H_F <= LANES
    assert n_in == o_ff + H_F + 2 * d
    srcs = [p["w_in"].T, p["b_in"][None, :]]

    def row_spec(shape, tiled=True):
        if not tiled:
            return pl.BlockSpec(shape, lambda i: (0, 0))
        assert shape[0] % (steps * BF16_ROWS) == 0
        return pl.BlockSpec((shape[0] // steps, shape[1]), lambda i: (i, 0))

    out_shapes = [((d, 4 * w_m), BF16), ((d, 3 * w_f), BF16), ((d, 2 * d), BF16), ((d, LANES), BF16),
                  ((1, 4 * w_m), F32), ((1, 3 * w_f), F32), ((1, 2 * d), F32), ((1, LANES), F32)]
    return pl.pallas_call(
        functools.partial(_wprep_kernel, w_m=w_m, w_f=w_f),
        grid=(steps,),
        in_specs=([pl.BlockSpec((n_in, d // steps), lambda i: (0, i))]
                  + [row_spec(s.shape, tiled=s.shape[0] > 1) for s in srcs[1:]]),
        out_specs=[row_spec(s, tiled=s[0] > 1) for s, _ in out_shapes],
        out_shape=[jax.ShapeDtypeStruct(s, dt) for s, dt in out_shapes],
        compiler_params=pltpu.CompilerParams(dimension_semantics=("arbitrary",), vmem_limit_bytes=vmem),
        name="wprep",
    )(*srcs)


def _inproj_kernel(x_ref, g_ref, wa_ref, ba_ref, wb_ref, bb_ref, wc_ref, bc_ref, wd_ref, bd_ref,
                   cw_ref, cb_ref, ng_ref, *rest,
                   w_m, w_f, n_chunk, m_split, q_scale, k_scale, chunk, tk, tiles_per_seq, n_cast):
    cast_srcs, rest = rest[:n_cast], rest[n_cast:]
    (ym_ref, zb_ref, fvt_ref, zc_ref, csplit_ref), rest = rest[:5], rest[5:]
    cast_dsts, rest = rest[:n_cast], rest[n_cast:]
    zs_sc, gc_sc, zqk_ref, vt_ref, og_ref, rows_ref, cols_ref, colprev_sc, ct_sc = rest
    tm = x_ref.shape[0]
    seq_start = (pl.program_id(0) % tiles_per_seq) == 0

    @pl.when(seq_start)
    def _():
        zs_sc[...] = jnp.zeros(zs_sc.shape, F32)
        gc_sc[0:2 * GATE_GROUP] = jnp.zeros((2 * GATE_GROUP, LANES), F32)
        gc_sc[2 * GATE_GROUP:] = jnp.full((GATE_GROUP, LANES), NEG_BIG, F32)
        colprev_sc[...] = jnp.zeros(colprev_sc.shape, F32)
        ct_sc[...] = jnp.zeros(ct_sc.shape, F32)

    rows = tm // m_split
    hb = [_rms(x_ref[r0:r0 + rows, :], g_ref[...]).astype(BF16) for r0 in range(0, tm, rows)]

    def proj(w_ref, b_ref, c0, c1):
        w = w_ref[:, c0:c1]
        parts = [jnp.dot(h, w, preferred_element_type=F32) for h in hb]
        return jnp.concatenate(parts, axis=0) + b_ref[:, c0:c1]

    def qk_chunk(ci):
        c0 = ci * n_chunk
        cs = slice(c0, c0 + n_chunk)
        z = proj(wa_ref, ba_ref, c0, c0 + n_chunk)
        prev = zs_sc[ci]
        zs_sc[ci] = z[tm - SUBLANES:tm]
        zg = z.reshape(tm // SUBLANES, SUBLANES, n_chunk)
        row = lax.broadcasted_iota(jnp.int32, zg.shape, 1)
        w = cw_ref[:, cs]
        y = zg * w[CONV_K - 1:CONV_K] + cb_ref[:, cs]
        for s in range(1, CONV_K):
            rz = pltpu.roll(zg, s, axis=1)
            before = jnp.concatenate([pltpu.roll(prev, s, axis=0)[None], rz[:-1]], axis=0)
            y = y + jnp.where(row < s, before, rz) * w[CONV_K - 1 - s:CONV_K - s]
        act = y * jax.nn.sigmoid(y)
        if c0 >= w_m:
            act = act * k_scale
        zqk_ref[:, cs] = act.reshape(tm, n_chunk).astype(BF16)

    def v_chunk(i):
        c0 = 2 * w_m + i * n_chunk
        z = proj(wa_ref, ba_ref, c0, c0 + n_chunk)
        for cc in range(tm // chunk):
            for f0 in range(0, n_chunk, LANES):
                blk = z[cc * chunk:(cc + 1) * chunk, f0:f0 + LANES]
                r0 = i * n_chunk + f0
                vt_ref[cc, r0:r0 + LANES, :] = blk.T.astype(BF16)

    def og_chunk(i):
        c0 = 3 * w_m + i * n_chunk
        z = proj(wa_ref, ba_ref, c0, c0 + n_chunk)
        og_ref[:, i * n_chunk:(i + 1) * n_chunk] = jax.nn.sigmoid(z).astype(BF16)

    def b_chunk(i):
        c0 = i * n_chunk
        z = proj(wb_ref, bb_ref, c0, c0 + n_chunk)
        if c0 < w_f:
            z = z * q_scale
        if c0 < 2 * w_f:
            zb_ref[:, c0:c0 + n_chunk] = z.astype(BF16)
        else:
            for jb in range(tm // tk):
                for f0 in range(0, n_chunk, LANES):
                    blk = z[jb * tk:(jb + 1) * tk, f0:f0 + LANES]
                    r0 = c0 - 2 * w_f + f0
                    fvt_ref[jb, r0:r0 + LANES, :] = blk.T.astype(BF16)

    def c_chunk(i):
        c0 = i * n_chunk
        z = proj(wc_ref, bc_ref, c0, c0 + n_chunk)
        zc_ref[:, c0:c0 + n_chunk] = jax.nn.sigmoid(z).astype(BF16)

    def d_chunk(_):
        zd = proj(wd_ref, bd_ref, 0, LANES)
        rows, cols, csplit, gc_sc[...] = _gate_tables(zd.T[:N_GATE_ROWS, :], gc_sc[...])
        rows_ref[...] = rows
        cols_ref[...] = cols
        csplit_ref[...] = csplit

    mlstm = _MlstmTile(zqk_ref, vt_ref, og_ref, rows_ref, cols_ref, colprev_sc, ng_ref, ym_ref,
                       ct_sc, w_m=w_m, chunk=chunk)
    order = [(d_chunk, 0)]
    order += [(qk_chunk, i) for i in range(2 * w_m // n_chunk)]
    order += [(v_chunk, i) for i in range(w_m // n_chunk)]
    order += [(og_chunk, i) for i in range(w_m // n_chunk)]
    fill = ([(c_chunk, i) for i in range(wc_ref.shape[1] // n_chunk)]
            + [(b_chunk, i) for i in range(3 * w_f // n_chunk)])
    for c in range(tm // chunk):
        order += [(mlstm.state_stage, c), fill.pop(0), (mlstm.output_stage, c), fill.pop(0)]
    order += fill
    for fn, i in order:
        fn(i)
    colprev_sc[...] = cols_ref[tm - SUBLANES:tm, :]
    for src, dst in zip(cast_srcs, cast_dsts):
        dst[...] = src[...].astype(BF16)


def _scan_lanes(x, op, fill):
    n = x.shape[-1]
    pos = lax.broadcasted_iota(jnp.int32, x.shape, 1)
    s = 1
    while s < n:
        shifted = pltpu.roll(x, s, axis=1)
        x = op(x, jnp.where(pos >= s, shifted, fill))
        s *= 2
    return x


def _gate_tables(z, carry):
    width = z.shape[1]
    i8 = z[0:GATE_GROUP]
    cum = (_scan_lanes(_log_sigmoid(z[GATE_GROUP:3 * GATE_GROUP]), jnp.add, 0.0)
           + carry[0:2 * GATE_GROUP, 0:1])
    f8 = cum[0:GATE_GROUP]
    g8 = i8 - f8
    cmax = jnp.maximum(_scan_lanes(g8, jnp.maximum, NEG_BIG), carry[2 * GATE_GROUP:, 0:1])
    new_carry = jnp.concatenate(
        [jnp.broadcast_to(cum[:, width - 1:width], (2 * GATE_GROUP, LANES)),
         jnp.broadcast_to(cmax[:, width - 1:width], (GATE_GROUP, LANES))], axis=0)
    m8 = jnp.maximum(cmax, 0.0)
    en8 = jnp.exp(-(f8 + m8))
    cf8 = cum[GATE_GROUP:2 * GATE_GROUP] * LOG2E
    g8, m8 = g8 * LOG2E, m8 * LOG2E
    rows = jnp.concatenate([g8, m8, en8], axis=0)
    stack = jnp.concatenate(
        [g8, m8, jnp.zeros((LANES - 2 * GATE_GROUP, width), F32)], axis=0)
    cols = stack.T
    hi = cf8.astype(BF16).astype(F32)
    r1 = cf8 - hi
    lo = r1.astype(BF16).astype(F32)
    lo2 = (r1 - lo).astype(BF16).astype(F32)
    ones = jnp.where(lax.broadcasted_iota(jnp.int32, (GATE_GROUP, width), 0) == 0, 1.0, 0.0)
    split = jnp.concatenate(
        [hi, lo, lo2, ones, jnp.zeros((LANES - 4 * GATE_GROUP, width), F32)], axis=0)
    return rows, cols, split.T.astype(BF16), new_carry


class _MlstmTile:
    def __init__(self, zqk_ref, vt_ref, og_ref, rows_ref, cols_ref, colprev_ref, ng_ref, ym_ref,
                 ct_sc, *, w_m, chunk):
        self.refs = (zqk_ref, vt_ref, og_ref, rows_ref, cols_ref, colprev_ref, ng_ref, ym_ref, ct_sc)
        self.w_m, self.chunk, self.dh, self.hc = w_m, chunk, w_m // H_M, chunk // 2
        hc = self.hc
        self.causal = (lax.broadcasted_iota(jnp.int32, (hc, hc), 0)
                       <= lax.broadcasted_iota(jnp.int32, (hc, hc), 1))
        self.ones_rows = jnp.where(
            lax.broadcasted_iota(jnp.int32, (BF16_ROWS, chunk), 0) == 0, 1.0, 0.0).astype(BF16)
        self.pending = {}

    def state_stage(self, c):
        zqk_ref, vt_ref, _, rows_ref, cols_ref, colprev_ref, _, _, ct_sc = self.refs
        w_m, chunk, dh = self.w_m, self.chunk, self.dh
        r0 = c * chunk
        tc = slice(r0, r0 + chunk)
        last = cols_ref[r0 + chunk - 1:r0 + chunk, :]
        prev = cols_ref[r0 - 1:r0, :] if c else colprev_ref[SUBLANES - 1:SUBLANES, :]
        first, vas = [], []
        for h in range(H_M):
            qc = zqk_ref[tc, h * dh:(h + 1) * dh]
            kc = zqk_ref[tc, w_m + h * dh:w_m + (h + 1) * dh]
            lhs = jnp.concatenate([kc, ct_sc[h].astype(BF16)], axis=0)
            first.append(lax.dot_general(lhs, qc, NT_DIMS, preferred_element_type=F32))
            vas.append(jnp.concatenate([vt_ref[c, h * dh:(h + 1) * dh, :], self.ones_rows], axis=0))
        for h in range(H_M):
            kc = zqk_ref[tc, w_m + h * dh:w_m + (h + 1) * dh]
            g_row = rows_ref[h:h + 1, tc]
            m_e = last[:, GATE_GROUP + h:GATE_GROUP + h + 1]
            m_p = prev[:, GATE_GROUP + h:GATE_GROUP + h + 1]
            vaw = (vas[h].astype(F32) * jnp.exp2(g_row - m_e)).astype(BF16)
            ct_sc[h] = jnp.exp2(m_p - m_e) * ct_sc[h] + jnp.dot(vaw, kc, preferred_element_type=F32)
        self.pending[c] = (first, vas, prev)

    def output_stage(self, c):
        _, _, og_ref, rows_ref, cols_ref, _, ng_ref, ym_ref, _ = self.refs
        chunk, dh, hc, causal = self.chunk, self.dh, self.hc, self.causal
        first, vas, prev = self.pending.pop(c)
        r0 = c * chunk
        tc = slice(r0, r0 + chunk)
        colsc = cols_ref[tc, :]
        for h in range(H_M):
            ch = slice(h * dh, (h + 1) * dh)
            g_col = colsc[:, h:h + 1]
            m_row = rows_ref[GATE_GROUP + h:GATE_GROUP + h + 1, tc]
            en_row = rows_ref[2 * GATE_GROUP + h:2 * GATE_GROUP + h + 1, tc]
            m_p = prev[:, GATE_GROUP + h:GATE_GROUP + h + 1]
            sk = first[h]
            d00 = jnp.where(causal, jnp.exp2(g_col[0:hc] - m_row[:, 0:hc]), 0.0)
            d01 = jnp.exp2(g_col[0:hc] - m_row[:, hc:chunk])
            d11 = jnp.where(causal, jnp.exp2(g_col[hc:chunk] - m_row[:, hc:chunk]), 0.0)
            top = jnp.concatenate([sk[0:hc, 0:hc] * d00, sk[0:hc, hc:chunk] * d01], axis=1)
            bot = jnp.concatenate([jnp.zeros((hc, hc), F32), sk[hc:chunk, hc:chunk] * d11], axis=1)
            sqk = jnp.concatenate([top, bot], axis=0).astype(BF16)
            nd = (jnp.exp2(m_p - m_row) * first[h][chunk:]
                  + jnp.dot(vas[h], sqk, preferred_element_type=F32))
            den = nd[dh:dh + 1]
            ht = nd[0:dh] * (1.0 / jnp.maximum(jnp.abs(den), en_row))
            hn = ht * lax.rsqrt(jnp.mean(ht * ht, axis=0, keepdims=True) + EPS)
            og = og_ref[tc, ch].astype(F32)
            ym_ref[tc, ch] = (hn.T * ng_ref[:, ch] * og).astype(BF16)


def _fox_kernel(q_ref, k_ref, fvt_ref, csq_ref, csk_ref, qsel_ref, ksel_ref, yf_ref,
                kaug_sc, qaug_sc, m_sc, acc_sc, *, tq, tk):
    qi = pl.program_id(1)

    seq = k_ref.shape[0]
    dhp = LANES
    half = dhp // 2
    v_rows = half + BF16_ROWS

    def own_lanes(head, rows):
        lane = lax.broadcasted_iota(jnp.int32, (rows, dhp), 1)
        return (lane >= half) if head % 2 else (lane < half)

    def augment(x_ref, cs_ref, sel_ref, dst_sc, rows, transpose):
        cs = cs_ref[...]
        for p in range(H_F // 2):
            xp = x_ref[:, p * dhp:(p + 1) * dhp]
            bias = jnp.dot(cs, sel_ref[p], preferred_element_type=F32)
            for head in (2 * p, 2 * p + 1):
                b0 = (head % 2) * dhp
                aug = jnp.where(own_lanes(head, rows), xp.astype(F32), bias[:, b0:b0 + dhp])
                dst_sc[head] = (aug.T if transpose else aug).astype(BF16)

    @pl.when(qi == 0)
    def _():
        augment(k_ref, csk_ref, ksel_ref, kaug_sc, seq, False)

    augment(q_ref, csq_ref, qsel_ref, qaug_sc, tq, True)
    m_sc[...] = jnp.full(m_sc.shape, NEG_BIG, F32)
    acc_sc[...] = jnp.zeros(acc_sc.shape, F32)

    ratio = tq // tk
    ones_rows = jnp.where(
        lax.broadcasted_iota(jnp.int32, (BF16_ROWS, tk), 0) == 0, 1.0, 0.0).astype(BF16)

    def vaug(head, j):
        return jnp.concatenate([fvt_ref[j, head * half:(head + 1) * half, :], ones_rows], axis=0)

    def step(j, diag):
        q0 = 0 if diag is None else diag * tk
        nq = tq - q0
        k0 = pl.multiple_of(j * tk, tk)
        sts = [jnp.dot(kaug_sc[head, pl.ds(k0, tk), :], qaug_sc[head, :, q0:tq],
                       preferred_element_type=F32) for head in range(H_F)]
        if diag is not None:
            causal = (lax.broadcasted_iota(jnp.int32, (tk, nq), 0)
                      <= lax.broadcasted_iota(jnp.int32, (tk, nq), 1))
        for head in range(H_F):
            st = sts[head]
            if diag is not None:
                st = jnp.where(causal, st, NEG_BIG)
            m = m_sc[head, :, q0:tq]
            m_new = jnp.maximum(m, jnp.max(st, axis=0, keepdims=True))
            alpha = jnp.exp2(m - m_new)
            pt = jnp.exp2(st - m_new).astype(BF16)
            m_sc[head, :, q0:tq] = m_new
            acc_sc[head, :, q0:tq] = (alpha * acc_sc[head, :, q0:tq]
                                      + jnp.dot(vaug(head, j), pt, preferred_element_type=F32))

    def loop_body(jj, carry):
        for r in range(ratio):
            step(jj * ratio + r, None)
        return carry

    lax.fori_loop(0, qi, loop_body, 0)
    for diag in range(ratio):
        step(qi * ratio + diag, diag)
    for p in range(H_F // 2):
        outs = []
        for head in (2 * p, 2 * p + 1):
            acc = acc_sc[head]
            outs.append(acc[0:half] * (1.0 / acc[half:half + 1]))
        yf_ref[:, p * dhp:(p + 1) * dhp] = jnp.concatenate(outs, axis=0).T.astype(BF16)


def _post_kernel(x_ref, ym_ref, yf_ref, zc_ref, wbm_ref, wbf_ref, wo_ref, g2_ref,
                 wg_ref, wu_ref, wd_ref, gfin_ref, o_ref, *, tf):
    d = x_ref.shape[1]
    d_ff = wg_ref.shape[1]
    bm = jnp.dot(ym_ref[...], wbm_ref[...], preferred_element_type=F32)
    bf = jnp.dot(yf_ref[...], wbf_ref[...], preferred_element_type=F32)
    mix = zc_ref[:, 0:d].astype(F32) * bm + zc_ref[:, d:2 * d].astype(F32) * bf
    x1 = x_ref[...] + jnp.dot(mix.astype(BF16), wo_ref[...], preferred_element_type=F32)
    h2 = _rms(x1, g2_ref[...]).astype(BF16)
    acc = jnp.zeros(x1.shape, F32)
    for f0 in range(0, d_ff, tf):
        g = jnp.dot(h2, wg_ref[:, f0:f0 + tf], preferred_element_type=F32)
        u = jnp.dot(h2, wu_ref[:, f0:f0 + tf], preferred_element_type=F32)
        act = (g * jax.nn.sigmoid(g) * u).astype(BF16)
        acc = acc + jnp.dot(act, wd_ref[f0:f0 + tf, :], preferred_element_type=F32)
    o_ref[...] = _rms(x1 + acc, gfin_ref[...])


def _bias_selectors():
    ones_lane = 3 * GATE_GROUP
    qsel = np.zeros((H_F // 2, LANES, 2 * LANES), np.float32)
    ksel = np.zeros((H_F // 2, LANES, 2 * LANES), np.float32)
    for h in range(H_F):
        p0 = (h % 2) * LANES + (LANES // 2 if h % 2 == 0 else 0)
        for c in range(3):
            qsel[h // 2, GATE_GROUP * c + h, p0 + c] = 1.0
            qsel[h // 2, ones_lane, p0 + 3 + c] = 1.0
            ksel[h // 2, ones_lane, p0 + c] = 1.0
            ksel[h // 2, GATE_GROUP * c + h, p0 + 3 + c] = -1.0
    return jnp.asarray(qsel, BF16), jnp.asarray(ksel, BF16)


def _layer(x2d, batch, seq, p, cfg):
    t, d = x2d.shape
    w_m = p["w_br_mlstm"].shape[0]
    w_f = p["w_br_fox"].shape[0]
    dh_m = w_m // H_M
    dh_f = w_f // H_F
    vmem = cfg["vmem_limit"]

    wa, wb, wc, wd, ba, bb, bc, bd = _prep_in_weights(p, vmem)

    tm = cfg["tm_in"]
    chunk = cfg["chunk"]
    tq, tk = cfg["tq"], cfg["tk"]
    n_steps = t // tm
    tps = seq // tm
    assert seq % tm == 0 and tm % chunk == 0 and tm % tk == 0

    later = [p["w_br_mlstm"], p["w_br_fox"], p["w_out"], p["w_gate"], p["w_up"], p["w_down"]]

    def slice_spec(rows, cols):
        hold = 1
        while (rows * hold) % (n_steps * BF16_ROWS):
            hold *= 2
        assert hold <= n_steps
        return pl.BlockSpec((rows * hold // n_steps, cols), lambda i: (i // hold, 0))

    later_specs = [slice_spec(*w.shape) for w in later]
    outs = pl.pallas_call(
        functools.partial(_inproj_kernel, w_m=w_m, w_f=w_f, n_chunk=cfg["n_chunk"],
                          m_split=cfg["m_split"],
                          q_scale=dh_f ** -0.5 * LOG2E, k_scale=dh_m ** -0.5, chunk=chunk,
                          tk=tk, tiles_per_seq=tps, n_cast=len(later)),
        grid=(n_steps,),
        in_specs=[pl.BlockSpec((tm, d), lambda i: (i, 0)), _const_spec((1, d)),
                  _const_spec(wa.shape), _const_spec(ba.shape),
                  _const_spec(wb.shape), _const_spec(bb.shape),
                  _const_spec(wc.shape), _const_spec(bc.shape),
                  _const_spec(wd.shape), _const_spec(bd.shape),
                  _const_spec((CONV_K, 2 * w_m)), _const_spec((1, 2 * w_m)),
                  _const_spec((1, w_m))] + later_specs,
        out_specs=[pl.BlockSpec((tm, w_m), lambda i: (i, 0)),
                   pl.BlockSpec((tm, 2 * w_f), lambda i: (i, 0)),
                   pl.BlockSpec((tm // tk, w_f, tk), lambda i: (i, 0, 0)),
                   pl.BlockSpec((tm, 2 * d), lambda i: (i, 0)),
                   pl.BlockSpec((tm, LANES), lambda i: (i, 0))] + later_specs,
        out_shape=[jax.ShapeDtypeStruct((t, w_m), BF16),
                   jax.ShapeDtypeStruct((t, 2 * w_f), BF16),
                   jax.ShapeDtypeStruct((t // tk, w_f, tk), BF16),
                   jax.ShapeDtypeStruct((t, 2 * d), BF16),
                   jax.ShapeDtypeStruct((t, LANES), BF16)]
        + [jax.ShapeDtypeStruct(w.shape, BF16) for w in later],
        scratch_shapes=[pltpu.VMEM((2 * w_m // cfg["n_chunk"], SUBLANES, cfg["n_chunk"]), F32),
                        pltpu.VMEM((N_GATE_ROWS, LANES), F32),
                        pltpu.VMEM((tm, 2 * w_m), BF16),
                        pltpu.VMEM((tm // chunk, w_m, chunk), BF16),
                        pltpu.VMEM((tm, w_m), BF16),
                        pltpu.VMEM((N_GATE_ROWS, tm), F32),
                        pltpu.VMEM((tm, LANES), F32),
                        pltpu.VMEM((SUBLANES, LANES), F32),
                        pltpu.VMEM((H_M, dh_m + BF16_ROWS, dh_m), F32)],
        compiler_params=pltpu.CompilerParams(dimension_semantics=("arbitrary",), vmem_limit_bytes=vmem),
        name="inproj",
    )(x2d, p["norm1_g"][None, :], wa, ba, wb, bb, wc, bc, wd, bd, p["conv_w"], p["conv_b"][None, :],
      p["mlstm_norm_g"][None, :], *later)
    ym, zb, fvt, zc, csplit, wbm, wbf, wo, wg, wu, wdn = outs

    assert tq % tk == 0 and seq % tq == 0
    nq = seq // tq
    qsel, ksel = _bias_selectors()
    v_rows = LANES // 2 + BF16_ROWS
    yf = pl.pallas_call(
        functools.partial(_fox_kernel, tq=tq, tk=tk),
        grid=(batch, nq),
        in_specs=[pl.BlockSpec((tq, w_f), lambda b, i: (b * nq + i, 0)),
                  pl.BlockSpec((seq, w_f), lambda b, i: (b, 1)),
                  pl.BlockSpec((seq // tk, w_f, tk), lambda b, i: (b, 0, 0)),
                  pl.BlockSpec((tq, LANES), lambda b, i: (b * nq + i, 0)),
                  pl.BlockSpec((seq, LANES), lambda b, i: (b, 0)),
                  _const_spec(qsel.shape), _const_spec(ksel.shape)],
        out_specs=pl.BlockSpec((tq, w_f), lambda b, i: (b * nq + i, 0)),
        out_shape=jax.ShapeDtypeStruct((t, w_f), BF16),
        scratch_shapes=[pltpu.VMEM((H_F, seq, LANES), BF16),
                        pltpu.VMEM((H_F, LANES, tq), BF16),
                        pltpu.VMEM((H_F, 1, tq), F32),
                        pltpu.VMEM((H_F, v_rows, tq), F32)],
        compiler_params=pltpu.CompilerParams(dimension_semantics=("arbitrary", "arbitrary"),
                                             vmem_limit_bytes=vmem),
        name="fox",
    )(zb, zb, fvt, csplit, csplit, qsel, ksel)

    tmp = cfg["tm_post"]
    return pl.pallas_call(
        functools.partial(_post_kernel, tf=cfg["tf"]),
        grid=(t // tmp,),
        in_specs=[pl.BlockSpec((tmp, d), lambda i: (i, 0)),
                  pl.BlockSpec((tmp, w_m), lambda i: (i, 0)),
                  pl.BlockSpec((tmp, w_f), lambda i: (i, 0)),
                  pl.BlockSpec((tmp, 2 * d), lambda i: (i, 0)),
                  _const_spec(wbm.shape), _const_spec(wbf.shape), _const_spec(wo.shape),
                  _const_spec((1, d)), _const_spec(wg.shape), _const_spec(wu.shape),
                  _const_spec(wdn.shape), _const_spec((1, d))],
        out_specs=pl.BlockSpec((tmp, d), lambda i: (i, 0)),
        out_shape=jax.ShapeDtypeStruct((t, d), F32),
        compiler_params=pltpu.CompilerParams(dimension_semantics=("parallel",), vmem_limit_bytes=vmem),
        name="post",
    )(x2d, ym, yf, zc, wbm, wbf, wo, p["norm2_g"][None, :], wg, wu, wdn, p["norm_f_g"][None, :])


def kernel(x, norm1_g, w_in, b_in, conv_w, conv_b, mlstm_norm_g, w_br_mlstm, w_br_fox, w_out,
           norm2_g, w_gate, w_up, w_down, norm_f_g):
    batch, seq, d = x.shape
    depth = w_in.shape[0]
    assert depth == 1, "the final norm is fused into the single layer's last call"
    cfg = _cfg(batch, seq, d, w_gate.shape[-1])
    p = dict(norm1_g=norm1_g[0], w_in=w_in[0], b_in=b_in[0], conv_w=conv_w[0], conv_b=conv_b[0],
             mlstm_norm_g=mlstm_norm_g[0], w_br_mlstm=w_br_mlstm[0], w_br_fox=w_br_fox[0],
             w_out=w_out[0], norm2_g=norm2_g[0], w_gate=w_gate[0], w_up=w_up[0], w_down=w_down[0],
             norm_f_g=norm_f_g)
    out = _layer(x.reshape(batch * seq, d), batch, seq, p, cfg)
    return out.reshape(batch, seq, d)
```

```python
import functools

import jax
import jax.numpy as jnp
import numpy as np
from jax import lax
from jax.experimental import pallas as pl
from jax.experimental.pallas import tpu as pltpu

EPS = 1e-6
H_M = 4
H_F = 8
CONV_K = 4

LANES = 128
SUBLANES = 8
BF16_ROWS = 16
GATE_GROUP = 8
N_GATE_ROWS = 3 * GATE_GROUP
V7X_VMEM_BYTES = 64 * 1024 * 1024
VMEM_COMPILER_RESERVE = 8 * 1024 * 1024
NEG_BIG = -1e30
LOG2E = 1.4426950408889634

F32 = jnp.float32
BF16 = jnp.bfloat16
NT_DIMS = (((1,), (1,)), ((), ()))


def _cfg(batch, seq, d_model, d_ff):
    return dict(
        tm_in=1024,
        n_chunk=256,
        m_split=8,
        chunk=256,
        tq=512,
        tk=256,
        tm_post=512,
        tf=256,
        vmem_limit=V7X_VMEM_BYTES - VMEM_COMPILER_RESERVE,
    )


def _const_spec(shape):
    nd = len(shape)
    return pl.BlockSpec(shape, lambda *_: (0,) * nd, pipeline_mode=pl.Buffered(1))


def _rms(x, g):
    return x * lax.rsqrt(jnp.mean(x * x, axis=-1, keepdims=True) + EPS) * g


def _log_sigmoid(x):
    return jnp.minimum(x, 0.0) - jnp.log1p(jnp.exp(-jnp.abs(x)))


def _wprep_kernel(wint_ref, bin_ref, wa_o, wb_o, wc_o, wd_o, ba_o, bb_o, bc_o, bd_o, *, w_m, w_f):
    o_mi = 4 * w_m
    o_mf = o_mi + H_M
    o_fq = o_mf + H_M
    o_ff = o_fq + 3 * w_f
    o_g = o_ff + H_F

    def gate_block(src):
        rows = src.shape[0]
        lane = lax.broadcasted_iota(jnp.int32, (rows, LANES), 1)
        blk_m = src[:, o_mi:o_mi + LANES]
        f0 = (o_ff // LANES) * LANES
        blk_f = src[:, f0:f0 + LANES]
        mi = jnp.where(lane < H_M, blk_m, 0.0)
        mf = jnp.where((lane >= GATE_GROUP) & (lane < GATE_GROUP + H_M),
                       pltpu.roll(blk_m, GATE_GROUP - H_M, axis=1), 0.0)
        ff = jnp.where((lane >= 2 * GATE_GROUP) & (lane < 2 * GATE_GROUP + H_F),
                       pltpu.roll(blk_f, 2 * GATE_GROUP - (o_ff - f0), axis=1), 0.0)
        return mi + mf + ff

    slab = wint_ref[...]
    wa_o[...] = slab[0:o_mi].T.astype(BF16)
    wb_o[...] = slab[o_fq:o_ff].T.astype(BF16)
    wc_o[...] = slab[o_g:].T.astype(BF16)
    g_m = slab[o_mi:o_mi + GATE_GROUP]
    row = lax.broadcasted_iota(jnp.int32, g_m.shape, 0)
    gates = jnp.concatenate(
        [jnp.where(row < H_M, g_m, 0.0),
         jnp.where(row < H_M, pltpu.roll(g_m, GATE_GROUP - H_M, axis=0), 0.0),
         slab[o_ff:o_ff + GATE_GROUP],
         jnp.zeros((LANES - 3 * GATE_GROUP, slab.shape[1]), F32)], axis=0)
    wd_o[...] = gates.T.astype(BF16)
    b = bin_ref[...]
    ba_o[...] = b[:, 0:o_mi]
    bb_o[...] = b[:, o_fq:o_ff]
    bc_o[...] = b[:, o_g:]
    bd_o[...] = gate_block(b)


def _prep_in_weights(p, vmem):
    d, n_in = p["w_in"].shape
    w_m = p["w_br_mlstm"].shape[0]
    w_f = p["w_br_fox"].shape[0]
    steps = 8
    o_ff = 4 * w_m + 2 * H_M + 3 * w_f
    assert (4 * w_m) % LANES == 0 and 2 * H_M <= GATE_GROUP + H_M <= LANES
    assert o_ff % LANES <= 2 * GATE_GROUP and o_ff % LANES + H_F <= LANES
    assert n_in == o_ff + H_F + 2 * d
    srcs = [p["w_in"].T, p["b_in"][None, :]]

    def row_spec(shape, tiled=True):
        if not tiled:
            return pl.BlockSpec(shape, lambda i: (0, 0))
        assert shape[0] % (steps * BF16_ROWS) == 0
        return pl.BlockSpec((shape[0] // steps, shape[1]), lambda i: (i, 0))

    out_shapes = [((d, 4 * w_m), BF16), ((d, 3 * w_f), BF16), ((d, 2 * d), BF16), ((d, LANES), BF16),
                  ((1, 4 * w_m), F32), ((1, 3 * w_f), F32), ((1, 2 * d), F32), ((1, LANES), F32)]
    return pl.pallas_call(
        functools.partial(_wprep_kernel, w_m=w_m, w_f=w_f),
        grid=(steps,),
        in_specs=([pl.BlockSpec((n_in, d // steps), lambda i: (0, i))]
                  + [row_spec(s.shape, tiled=s.shape[0] > 1) for s in srcs[1:]]),
        out_specs=[row_spec(s, tiled=s[0] > 1) for s, _ in out_shapes],
        out_shape=[jax.ShapeDtypeStruct(s, dt) for s, dt in out_shapes],
        compiler_params=pltpu.CompilerParams(dimension_semantics=("arbitrary",), vmem_limit_bytes=vmem),
        name="wprep",
    )(*srcs)


def _inproj_kernel(x_ref, g_ref, wa_ref, ba_ref, wb_ref, bb_ref, wc_ref, bc_ref, wd_ref, bd_ref,
                   cw_ref, cb_ref, ng_ref, *rest,
                   w_m, w_f, n_chunk, m_split, q_scale, k_scale, chunk, tk, tiles_per_seq, n_cast):
    cast_srcs, rest = rest[:n_cast], rest[n_cast:]
    (ym_ref, zb_ref, fvt_ref, zc_ref, csplit_ref), rest = rest[:5], rest[5:]
    cast_dsts, rest = rest[:n_cast], rest[n_cast:]
    zs_sc, gc_sc, zqk_ref, vt_ref, og_ref, rows_ref, cols_ref, colprev_sc, ct_sc = rest
    tm = x_ref.shape[0]
    seq_start = (pl.program_id(0) % tiles_per_seq) == 0

    @pl.when(seq_start)
    def _():
        zs_sc[...] = jnp.zeros(zs_sc.shape, F32)
        gc_sc[0:2 * GATE_GROUP] = jnp.zeros((2 * GATE_GROUP, LANES), F32)
        gc_sc[2 * GATE_GROUP:] = jnp.full((GATE_GROUP, LANES), NEG_BIG, F32)
        colprev_sc[...] = jnp.zeros(colprev_sc.shape, F32)
        ct_sc[...] = jnp.zeros(ct_sc.shape, F32)

    rows = tm // m_split
    hb = [_rms(x_ref[r0:r0 + rows, :], g_ref[...]).astype(BF16) for r0 in range(0, tm, rows)]

    def proj(w_ref, b_ref, c0, c1):
        w = w_ref[:, c0:c1]
        parts = [jnp.dot(h, w, preferred_element_type=F32) for h in hb]
        return jnp.concatenate(parts, axis=0) + b_ref[:, c0:c1]

    def qk_chunk(ci):
        c0 = ci * n_chunk
        cs = slice(c0, c0 + n_chunk)
        z = proj(wa_ref, ba_ref, c0, c0 + n_chunk)
        prev = zs_sc[ci]
        zs_sc[ci] = z[tm - SUBLANES:tm]
        zg = z.reshape(tm // SUBLANES, SUBLANES, n_chunk)
        row = lax.broadcasted_iota(jnp.int32, zg.shape, 1)
        w = cw_ref[:, cs]
        y = zg * w[CONV_K - 1:CONV_K] + cb_ref[:, cs]
        for s in range(1, CONV_K):
            rz = pltpu.roll(zg, s, axis=1)
            before = jnp.concatenate([pltpu.roll(prev, s, axis=0)[None], rz[:-1]], axis=0)
            y = y + jnp.where(row < s, before, rz) * w[CONV_K - 1 - s:CONV_K - s]
        act = y * jax.nn.sigmoid(y)
        if c0 >= w_m:
            act = act * k_scale
        zqk_ref[:, cs] = act.reshape(tm, n_chunk).astype(BF16)

    def v_chunk(i):
        c0 = 2 * w_m + i * n_chunk
        z = proj(wa_ref, ba_ref, c0, c0 + n_chunk)
        for cc in range(tm // chunk):
            for f0 in range(0, n_chunk, LANES):
                blk = z[cc * chunk:(cc + 1) * chunk, f0:f0 + LANES]
                r0 = i * n_chunk + f0
                vt_ref[cc, r0:r0 + LANES, :] = blk.T.astype(BF16)

    def og_chunk(i):
        c0 = 3 * w_m + i * n_chunk
        z = proj(wa_ref, ba_ref, c0, c0 + n_chunk)
        og_ref[:, i * n_chunk:(i + 1) * n_chunk] = jax.nn.sigmoid(z).astype(BF16)

    def b_chunk(i):
        c0 = i * n_chunk
        z = proj(wb_ref, bb_ref, c0, c0 + n_chunk)
        if c0 < w_f:
            z = z * q_scale
        if c0 < 2 * w_f:
            zb_ref[:, c0:c0 + n_chunk] = z.astype(BF16)
        else:
            for jb in range(tm // tk):
                for f0 in range(0, n_chunk, LANES):
                    blk = z[jb * tk:(jb + 1) * tk, f0:f0 + LANES]
                    r0 = c0 - 2 * w_f + f0
                    fvt_ref[jb, r0:r0 + LANES, :] = blk.T.astype(BF16)

    def c_chunk(i):
        c0 = i * n_chunk
        z = proj(wc_ref, bc_ref, c0, c0 + n_chunk)
        zc_ref[:, c0:c0 + n_chunk] = jax.nn.sigmoid(z).astype(BF16)

    def d_chunk(_):
        zd = proj(wd_ref, bd_ref, 0, LANES)
        rows, cols, csplit, gc_sc[...] = _gate_tables(zd.T[:N_GATE_ROWS, :], gc_sc[...])
        rows_ref[...] = rows
        cols_ref[...] = cols
        csplit_ref[...] = csplit

    mlstm = _MlstmTile(zqk_ref, vt_ref, og_ref, rows_ref, cols_ref, colprev_sc, ng_ref, ym_ref,
                       ct_sc, w_m=w_m, chunk=chunk)
    order = [(d_chunk, 0)]
    order += [(qk_chunk, i) for i in range(2 * w_m // n_chunk)]
    order += [(v_chunk, i) for i in range(w_m // n_chunk)]
    order += [(og_chunk, i) for i in range(w_m // n_chunk)]
    fill = ([(c_chunk, i) for i in range(wc_ref.shape[1] // n_chunk)]
            + [(b_chunk, i) for i in range(3 * w_f // n_chunk)])
    for c in range(tm // chunk):
        order += [(mlstm.state_stage, c), fill.pop(0), (mlstm.output_stage, c), fill.pop(0)]
    order += fill
    for fn, i in order:
        fn(i)
    colprev_sc[...] = cols_ref[tm - SUBLANES:tm, :]
    for src, dst in zip(cast_srcs, cast_dsts):
        dst[...] = src[...].astype(BF16)


def _scan_lanes(x, op, fill):
    n = x.shape[-1]
    pos = lax.broadcasted_iota(jnp.int32, x.shape, 1)
    s = 1
    while s < n:
        shifted = pltpu.roll(x, s, axis=1)
        x = op(x, jnp.where(pos >= s, shifted, fill))
        s *= 2
    return x


def _gate_tables(z, carry):
    width = z.shape[1]
    i8 = z[0:GATE_GROUP]
    cum = (_scan_lanes(_log_sigmoid(z[GATE_GROUP:3 * GATE_GROUP]), jnp.add, 0.0)
           + carry[0:2 * GATE_GROUP, 0:1])
    f8 = cum[0:GATE_GROUP]
    g8 = i8 - f8
    cmax = jnp.maximum(_scan_lanes(g8, jnp.maximum, NEG_BIG), carry[2 * GATE_GROUP:, 0:1])
    new_carry = jnp.concatenate(
        [jnp.broadcast_to(cum[:, width - 1:width], (2 * GATE_GROUP, LANES)),
         jnp.broadcast_to(cmax[:, width - 1:width], (GATE_GROUP, LANES))], axis=0)
    m8 = jnp.maximum(cmax, 0.0)
    en8 = jnp.exp(-(f8 + m8))
    cf8 = cum[GATE_GROUP:2 * GATE_GROUP] * LOG2E
    g8, m8 = g8 * LOG2E, m8 * LOG2E
    rows = jnp.concatenate([g8, m8, en8], axis=0)
    stack = jnp.concatenate(
        [g8, m8, jnp.zeros((LANES - 2 * GATE_GROUP, width), F32)], axis=0)
    cols = stack.T
    hi = cf8.astype(BF16).astype(F32)
    r1 = cf8 - hi
    lo = r1.astype(BF16).astype(F32)
    lo2 = (r1 - lo).astype(BF16).astype(F32)
    ones = jnp.where(lax.broadcasted_iota(jnp.int32, (GATE_GROUP, width), 0) == 0, 1.0, 0.0)
    split = jnp.concatenate(
        [hi, lo, lo2, ones, jnp.zeros((LANES - 4 * GATE_GROUP, width), F32)], axis=0)
    return rows, cols, split.T.astype(BF16), new_carry


class _MlstmTile:
    def __init__(self, zqk_ref, vt_ref, og_ref, rows_ref, cols_ref, colprev_ref, ng_ref, ym_ref,
                 ct_sc, *, w_m, chunk):
        self.refs = (zqk_ref, vt_ref, og_ref, rows_ref, cols_ref, colprev_ref, ng_ref, ym_ref, ct_sc)
        self.w_m, self.chunk, self.dh, self.hc = w_m, chunk, w_m // H_M, chunk // 2
        hc = self.hc
        self.causal = (lax.broadcasted_iota(jnp.int32, (hc, hc), 0)
                       <= lax.broadcasted_iota(jnp.int32, (hc, hc), 1))
        self.ones_rows = jnp.where(
            lax.broadcasted_iota(jnp.int32, (BF16_ROWS, chunk), 0) == 0, 1.0, 0.0).astype(BF16)
        self.pending = {}

    def state_stage(self, c):
        zqk_ref, vt_ref, _, rows_ref, cols_ref, colprev_ref, _, _, ct_sc = self.refs
        w_m, chunk, dh = self.w_m, self.chunk, self.dh
        r0 = c * chunk
        tc = slice(r0, r0 + chunk)
        last = cols_ref[r0 + chunk - 1:r0 + chunk, :]
        prev = cols_ref[r0 - 1:r0, :] if c else colprev_ref[SUBLANES - 1:SUBLANES, :]
        first, vas = [], []
        for h in range(H_M):
            qc = zqk_ref[tc, h * dh:(h + 1) * dh]
            kc = zqk_ref[tc, w_m + h * dh:w_m + (h + 1) * dh]
            lhs = jnp.concatenate([kc, ct_sc[h].astype(BF16)], axis=0)
            first.append(lax.dot_general(lhs, qc, NT_DIMS, preferred_element_type=F32))
            vas.append(jnp.concatenate([vt_ref[c, h * dh:(h + 1) * dh, :], self.ones_rows], axis=0))
        for h in range(H_M):
            kc = zqk_ref[tc, w_m + h * dh:w_m + (h + 1) * dh]
            g_row = rows_ref[h:h + 1, tc]
            m_e = last[:, GATE_GROUP + h:GATE_GROUP + h + 1]
            m_p = prev[:, GATE_GROUP + h:GATE_GROUP + h + 1]
            vaw = (vas[h].astype(F32) * jnp.exp2(g_row - m_e)).astype(BF16)
            ct_sc[h] = jnp.exp2(m_p - m_e) * ct_sc[h] + jnp.dot(vaw, kc, preferred_element_type=F32)
        self.pending[c] = (first, vas, prev)

    def output_stage(self, c):
        _, _, og_ref, rows_ref, cols_ref, _, ng_ref, ym_ref, _ = self.refs
        chunk, dh, hc, causal = self.chunk, self.dh, self.hc, self.causal
        first, vas, prev = self.pending.pop(c)
        r0 = c * chunk
        tc = slice(r0, r0 + chunk)
        colsc = cols_ref[tc, :]
        for h in range(H_M):
            ch = slice(h * dh, (h + 1) * dh)
            g_col = colsc[:, h:h + 1]
            m_row = rows_ref[GATE_GROUP + h:GATE_GROUP + h + 1, tc]
            en_row = rows_ref[2 * GATE_GROUP + h:2 * GATE_GROUP + h + 1, tc]
            m_p = prev[:, GATE_GROUP + h:GATE_GROUP + h + 1]
            sk = first[h]
            d00 = jnp.where(causal, jnp.exp2(g_col[0:hc] - m_row[:, 0:hc]), 0.0)
            d01 = jnp.exp2(g_col[0:hc] - m_row[:, hc:chunk])
            d11 = jnp.where(causal, jnp.exp2(g_col[hc:chunk] - m_row[:, hc:chunk]), 0.0)
            top = jnp.concatenate([sk[0:hc, 0:hc] * d00, sk[0:hc, hc:chunk] * d01], axis=1)
            bot = jnp.concatenate([jnp.zeros((hc, hc), F32), sk[hc:chunk, hc:chunk] * d11], axis=1)
            sqk = jnp.concatenate([top, bot], axis=0).astype(BF16)
            nd = (jnp.exp2(m_p - m_row) * first[h][chunk:]
                  + jnp.dot(vas[h], sqk, preferred_element_type=F32))
            den = nd[dh:dh + 1]
            ht = nd[0:dh] * (1.0 / jnp.maximum(jnp.abs(den), en_row))
            hn = ht * lax.rsqrt(jnp.mean(ht * ht, axis=0, keepdims=True) + EPS)
            og = og_ref[tc, ch].astype(F32)
            ym_ref[tc, ch] = (hn.T * ng_ref[:, ch] * og).astype(BF16)


def _fox_kernel(q_ref, k_ref, fvt_ref, csq_ref, csk_ref, qsel_ref, ksel_ref, yf_ref,
                kaug_sc, qaug_sc, m_sc, acc_sc, *, tq, tk):
    qi = pl.program_id(1)

    seq = k_ref.shape[0]
    dhp = LANES
    half = dhp // 2
    v_rows = half + BF16_ROWS

    def own_lanes(head, rows):
        lane = lax.broadcasted_iota(jnp.int32, (rows, dhp), 1)
        return (lane >= half) if head % 2 else (lane < half)

    def augment(x_ref, cs_ref, sel_ref, dst_sc, rows):
        cs = cs_ref[...]
        for p in range(H_F // 2):
            xp = x_ref[:, p * dhp:(p + 1) * dhp]
            bias = jnp.dot(cs, sel_ref[p], preferred_element_type=F32).astype(BF16)
            for head in (2 * p, 2 * p + 1):
                b0 = (head % 2) * dhp
                dst_sc[head] = jnp.where(own_lanes(head, rows), xp, bias[:, b0:b0 + dhp])

    @pl.when(qi == 0)
    def _():
        augment(k_ref, csk_ref, ksel_ref, kaug_sc, seq)

    augment(q_ref, csq_ref, qsel_ref, qaug_sc, tq)
    m_sc[...] = jnp.full(m_sc.shape, NEG_BIG, F32)
    acc_sc[...] = jnp.zeros(acc_sc.shape, F32)

    ratio = tq // tk
    ones_rows = jnp.where(
        lax.broadcasted_iota(jnp.int32, (BF16_ROWS, tk), 0) == 0, 1.0, 0.0).astype(BF16)

    def vaug(head, j):
        return jnp.concatenate([fvt_ref[j, head * half:(head + 1) * half, :], ones_rows], axis=0)

    def step(j, diag):
        q0 = 0 if diag is None else diag * tk
        nq = tq - q0
        k0 = pl.multiple_of(j * tk, tk)
        if diag is not None:
            causal = (lax.broadcasted_iota(jnp.int32, (tk, nq), 0)
                      <= lax.broadcasted_iota(jnp.int32, (tk, nq), 1))
        group = H_F // 2
        sts = {}
        for head in range(H_F):
            if head % group == 0:
                for g in range(head, head + group):
                    sts[g] = lax.dot_general(kaug_sc[g, pl.ds(k0, tk), :], qaug_sc[g, q0:tq, :],
                                             NT_DIMS, preferred_element_type=F32)
            st = sts.pop(head)
            if diag is not None:
                st = jnp.where(causal, st, NEG_BIG)
            m = m_sc[head, :, q0:tq]
            m_new = jnp.maximum(m, jnp.max(st, axis=0, keepdims=True))
            alpha = jnp.exp2(m - m_new)
            pt = jnp.exp2(st - m_new).astype(BF16)
            m_sc[head, :, q0:tq] = m_new
            acc_sc[head, :, q0:tq] = (alpha * acc_sc[head, :, q0:tq]
                                      + jnp.dot(vaug(head, j), pt, preferred_element_type=F32))

    def loop_body(jj, carry):
        for r in range(ratio):
            step(jj * ratio + r, None)
        return carry

    lax.fori_loop(0, qi, loop_body, 0)
    for diag in range(ratio):
        step(qi * ratio + diag, diag)
    for p in range(H_F // 2):
        outs = []
        for head in (2 * p, 2 * p + 1):
            acc = acc_sc[head]
            outs.append(acc[0:half] * (1.0 / acc[half:half + 1]))
        yf_ref[:, p * dhp:(p + 1) * dhp] = jnp.concatenate(outs, axis=0).T.astype(BF16)


def _post_kernel(x_ref, ym_ref, yf_ref, zc_ref, wbm_ref, wbf_ref, wo_ref, g2_ref,
                 wg_ref, wu_ref, wd_ref, gfin_ref, o_ref, *, tf):
    d = x_ref.shape[1]
    d_ff = wg_ref.shape[1]
    bm = jnp.dot(ym_ref[...], wbm_ref[...], preferred_element_type=F32)
    bf = jnp.dot(yf_ref[...], wbf_ref[...], preferred_element_type=F32)
    mix = zc_ref[:, 0:d].astype(F32) * bm + zc_ref[:, d:2 * d].astype(F32) * bf
    x1 = x_ref[...] + jnp.dot(mix.astype(BF16), wo_ref[...], preferred_element_type=F32)
    h2 = _rms(x1, g2_ref[...]).astype(BF16)
    acc = jnp.zeros(x1.shape, F32)
    for f0 in range(0, d_ff, tf):
        g = jnp.dot(h2, wg_ref[:, f0:f0 + tf], preferred_element_type=F32)
        u = jnp.dot(h2, wu_ref[:, f0:f0 + tf], preferred_element_type=F32)
        act = (g * jax.nn.sigmoid(g) * u).astype(BF16)
        acc = acc + jnp.dot(act, wd_ref[f0:f0 + tf, :], preferred_element_type=F32)
    o_ref[...] = _rms(x1 + acc, gfin_ref[...])


def _bias_selectors():
    ones_lane = 3 * GATE_GROUP
    qsel = np.zeros((H_F // 2, LANES, 2 * LANES), np.float32)
    ksel = np.zeros((H_F // 2, LANES, 2 * LANES), np.float32)
    for h in range(H_F):
        p0 = (h % 2) * LANES + (LANES // 2 if h % 2 == 0 else 0)
        for c in range(3):
            qsel[h // 2, GATE_GROUP * c + h, p0 + c] = 1.0
            qsel[h // 2, ones_lane, p0 + 3 + c] = 1.0
            ksel[h // 2, ones_lane, p0 + c] = 1.0
            ksel[h // 2, GATE_GROUP * c + h, p0 + 3 + c] = -1.0
    return jnp.asarray(qsel, BF16), jnp.asarray(ksel, BF16)


def _layer(x2d, batch, seq, p, cfg):
    t, d = x2d.shape
    w_m = p["w_br_mlstm"].shape[0]
    w_f = p["w_br_fox"].shape[0]
    dh_m = w_m // H_M
    dh_f = w_f // H_F
    vmem = cfg["vmem_limit"]

    wa, wb, wc, wd, ba, bb, bc, bd = _prep_in_weights(p, vmem)

    tm = cfg["tm_in"]
    chunk = cfg["chunk"]
    tq, tk = cfg["tq"], cfg["tk"]
    n_steps = t // tm
    tps = seq // tm
    assert seq % tm == 0 and tm % chunk == 0 and tm % tk == 0

    later = [p["w_br_mlstm"], p["w_br_fox"], p["w_out"], p["w_gate"], p["w_up"], p["w_down"]]

    def slice_spec(rows, cols):
        hold = 1
        while (rows * hold) % (n_steps * BF16_ROWS):
            hold *= 2
        assert hold <= n_steps
        return pl.BlockSpec((rows * hold // n_steps, cols), lambda i: (i // hold, 0))

    later_specs = [slice_spec(*w.shape) for w in later]
    outs = pl.pallas_call(
        functools.partial(_inproj_kernel, w_m=w_m, w_f=w_f, n_chunk=cfg["n_chunk"],
                          m_split=cfg["m_split"],
                          q_scale=dh_f ** -0.5 * LOG2E, k_scale=dh_m ** -0.5, chunk=chunk,
                          tk=tk, tiles_per_seq=tps, n_cast=len(later)),
        grid=(n_steps,),
        in_specs=[pl.BlockSpec((tm, d), lambda i: (i, 0)), _const_spec((1, d)),
                  _const_spec(wa.shape), _const_spec(ba.shape),
                  _const_spec(wb.shape), _const_spec(bb.shape),
                  _const_spec(wc.shape), _const_spec(bc.shape),
                  _const_spec(wd.shape), _const_spec(bd.shape),
                  _const_spec((CONV_K, 2 * w_m)), _const_spec((1, 2 * w_m)),
                  _const_spec((1, w_m))] + later_specs,
        out_specs=[pl.BlockSpec((tm, w_m), lambda i: (i, 0)),
                   pl.BlockSpec((tm, 2 * w_f), lambda i: (i, 0)),
                   pl.BlockSpec((tm // tk, w_f, tk), lambda i: (i, 0, 0)),
                   pl.BlockSpec((tm, 2 * d), lambda i: (i, 0)),
                   pl.BlockSpec((tm, LANES), lambda i: (i, 0))] + later_specs,
        out_shape=[jax.ShapeDtypeStruct((t, w_m), BF16),
                   jax.ShapeDtypeStruct((t, 2 * w_f), BF16),
                   jax.ShapeDtypeStruct((t // tk, w_f, tk), BF16),
                   jax.ShapeDtypeStruct((t, 2 * d), BF16),
                   jax.ShapeDtypeStruct((t, LANES), BF16)]
        + [jax.ShapeDtypeStruct(w.shape, BF16) for w in later],
        scratch_shapes=[pltpu.VMEM((2 * w_m // cfg["n_chunk"], SUBLANES, cfg["n_chunk"]), F32),
                        pltpu.VMEM((N_GATE_ROWS, LANES), F32),
                        pltpu.VMEM((tm, 2 * w_m), BF16),
                        pltpu.VMEM((tm // chunk, w_m, chunk), BF16),
                        pltpu.VMEM((tm, w_m), BF16),
                        pltpu.VMEM((N_GATE_ROWS, tm), F32),
                        pltpu.VMEM((tm, LANES), F32),
                        pltpu.VMEM((SUBLANES, LANES), F32),
                        pltpu.VMEM((H_M, dh_m + BF16_ROWS, dh_m), F32)],
        compiler_params=pltpu.CompilerParams(dimension_semantics=("arbitrary",), vmem_limit_bytes=vmem),
        name="inproj",
    )(x2d, p["norm1_g"][None, :], wa, ba, wb, bb, wc, bc, wd, bd, p["conv_w"], p["conv_b"][None, :],
      p["mlstm_norm_g"][None, :], *later)
    ym, zb, fvt, zc, csplit, wbm, wbf, wo, wg, wu, wdn = outs

    assert tq % tk == 0 and seq % tq == 0
    nq = seq // tq
    qsel, ksel = _bias_selectors()
    v_rows = LANES // 2 + BF16_ROWS
    yf = pl.pallas_call(
        functools.partial(_fox_kernel, tq=tq, tk=tk),
        grid=(batch, nq),
        in_specs=[pl.BlockSpec((tq, w_f), lambda b, i: (b * nq + i, 0)),
                  pl.BlockSpec((seq, w_f), lambda b, i: (b, 1)),
                  pl.BlockSpec((seq // tk, w_f, tk), lambda b, i: (b, 0, 0)),
                  pl.BlockSpec((tq, LANES), lambda b, i: (b * nq + i, 0)),
                  pl.BlockSpec((seq, LANES), lambda b, i: (b, 0)),
                  _const_spec(qsel.shape), _const_spec(ksel.shape)],
        out_specs=pl.BlockSpec((tq, w_f), lambda b, i: (b * nq + i, 0)),
        out_shape=jax.ShapeDtypeStruct((t, w_f), BF16),
        scratch_shapes=[pltpu.VMEM((H_F, seq, LANES), BF16),
                        pltpu.VMEM((H_F, tq, LANES), BF16),
                        pltpu.VMEM((H_F, 1, tq), F32),
                        pltpu.VMEM((H_F, v_rows, tq), F32)],
        compiler_params=pltpu.CompilerParams(dimension_semantics=("arbitrary", "arbitrary"),
                                             vmem_limit_bytes=vmem),
        name="fox",
    )(zb, zb, fvt, csplit, csplit, qsel, ksel)

    tmp = cfg["tm_post"]
    return pl.pallas_call(
        functools.partial(_post_kernel, tf=cfg["tf"]),
        grid=(t // tmp,),
        in_specs=[pl.BlockSpec((tmp, d), lambda i: (i, 0)),
                  pl.BlockSpec((tmp, w_m), lambda i: (i, 0)),
                  pl.BlockSpec((tmp, w_f), lambda i: (i, 0)),
                  pl.BlockSpec((tmp, 2 * d), lambda i: (i, 0)),
                  _const_spec(wbm.shape), _const_spec(wbf.shape), _const_spec(wo.shape),
                  _const_spec((1, d)), _const_spec(wg.shape), _const_spec(wu.shape),
                  _const_spec(wdn.shape), _const_spec((1, d))],
        out_specs=pl.BlockSpec((tmp, d), lambda i: (i, 0)),
        out_shape=jax.ShapeDtypeStruct((t, d), F32),
        compiler_params=pltpu.CompilerParams(dimension_semantics=("parallel",), vmem_limit_bytes=vmem),
        name="post",
    )(x2d, ym, yf, zc, wbm, wbf, wo, p["norm2_g"][None, :], wg, wu, wdn, p["norm_f_g"][None, :])


def kernel(x, norm1_g, w_in, b_in, conv_w, conv_b, mlstm_norm_g, w_br_mlstm, w_br_fox, w_out,
           norm2_g, w_gate, w_up, w_down, norm_f_g):
    batch, seq, d = x.shape
    depth = w_in.shape[0]
    assert depth == 1, "the final norm is fused into the single layer's last call"
    cfg = _cfg(batch, seq, d, w_gate.shape[-1])
    p = dict(norm1_g=norm1_g[0], w_in=w_in[0], b_in=b_in[0], conv_w=conv_w[0], conv_b=conv_b[0],
             mlstm_norm_g=mlstm_norm_g[0], w_br_mlstm=w_br_mlstm[0], w_br_fox=w_br_fox[0],
             w_out=w_out[0], norm2_g=norm2_g[0], w_gate=w_gate[0], w_up=w_up[0], w_down=w_down[0],
             norm_f_g=norm_f_g)
    out = _layer(x.reshape(batch * seq, d), batch, seq, p, cfg)
    return out.reshape(batch, seq, d)
```

```python
import functools

import jax
import jax.numpy as jnp
import numpy as np
from jax import lax
from jax.experimental import pallas as pl
from jax.experimental.pallas import tpu as pltpu

EPS = 1e-6
H_M = 4
H_F = 8
CONV_K = 4

LANES = 128
SUBLANES = 8
BF16_ROWS = 16
GATE_GROUP = 8
N_GATE_ROWS = 3 * GATE_GROUP
V7X_VMEM_BYTES = 64 * 1024 * 1024
VMEM_COMPILER_RESERVE = 8 * 1024 * 1024
NEG_BIG = -1e30
LOG2E = 1.4426950408889634

F32 = jnp.float32
BF16 = jnp.bfloat16
NT_DIMS = (((1,), (1,)), ((), ()))


def _cfg(batch, seq, d_model, d_ff):
    return dict(
        tm_in=1024,
        n_chunk=256,
        m_split=8,
        chunk=256,
        tq=512,
        tk=256,
        tm_post=512,
        tf=256,
        vmem_limit=V7X_VMEM_BYTES - VMEM_COMPILER_RESERVE,
    )


def _const_spec(shape):
    nd = len(shape)
    return pl.BlockSpec(shape, lambda *_: (0,) * nd, pipeline_mode=pl.Buffered(1))


def _rms(x, g):
    return x * lax.rsqrt(jnp.mean(x * x, axis=-1, keepdims=True) + EPS) * g


def _log_sigmoid(x):
    return jnp.minimum(x, 0.0) - jnp.log1p(jnp.exp(-jnp.abs(x)))


def _wprep_kernel(wint_ref, bin_ref, wa_o, wb_o, wc_o, wd_o, ba_o, bb_o, bc_o, bd_o, *, w_m, w_f):
    o_mi = 4 * w_m
    o_mf = o_mi + H_M
    o_fq = o_mf + H_M
    o_ff = o_fq + 3 * w_f
    o_g = o_ff + H_F

    def gate_block(src):
        rows = src.shape[0]
        lane = lax.broadcasted_iota(jnp.int32, (rows, LANES), 1)
        blk_m = src[:, o_mi:o_mi + LANES]
        f0 = (o_ff // LANES) * LANES
        blk_f = src[:, f0:f0 + LANES]
        mi = jnp.where(lane < H_M, blk_m, 0.0)
        mf = jnp.where((lane >= GATE_GROUP) & (lane < GATE_GROUP + H_M),
                       pltpu.roll(blk_m, GATE_GROUP - H_M, axis=1), 0.0)
        ff = jnp.where((lane >= 2 * GATE_GROUP) & (lane < 2 * GATE_GROUP + H_F),
                       pltpu.roll(blk_f, 2 * GATE_GROUP - (o_ff - f0), axis=1), 0.0)
        return mi + mf + ff

    slab = wint_ref[...]
    wa_o[...] = slab[0:o_mi].T.astype(BF16)
    wb_o[...] = slab[o_fq:o_ff].T.astype(BF16)
    wc_o[...] = slab[o_g:].T.astype(BF16)
    g_m = slab[o_mi:o_mi + GATE_GROUP]
    row = lax.broadcasted_iota(jnp.int32, g_m.shape, 0)
    gates = jnp.concatenate(
        [jnp.where(row < H_M, g_m, 0.0),
         jnp.where(row < H_M, pltpu.roll(g_m, GATE_GROUP - H_M, axis=0), 0.0),
         slab[o_ff:o_ff + GATE_GROUP],
         jnp.zeros((LANES - 3 * GATE_GROUP, slab.shape[1]), F32)], axis=0)
    wd_o[...] = gates.T.astype(BF16)
    b = bin_ref[...]
    ba_o[...] = b[:, 0:o_mi]
    bb_o[...] = b[:, o_fq:o_ff]
    bc_o[...] = b[:, o_g:]
    bd_o[...] = gate_block(b)


def _prep_in_weights(p, vmem):
    d, n_in = p["w_in"].shape
    w_m = p["w_br_mlstm"].shape[0]
    w_f = p["w_br_fox"].shape[0]
    steps = 8
    o_ff = 4 * w_m + 2 * H_M + 3 * w_f
    assert (4 * w_m) % LANES == 0 and 2 * H_M <= GATE_GROUP + H_M <= LANES
    assert o_ff % LANES <= 2 * GATE_GROUP and o_ff % LANES + H_F <= LANES
    assert n_in == o_ff + H_F + 2 * d
    srcs = [p["w_in"].T, p["b_in"][None, :]]

    def row_spec(shape, tiled=True):
        if not tiled:
            return pl.BlockSpec(shape, lambda i: (0, 0))
        assert shape[0] % (steps * BF16_ROWS) == 0
        return pl.BlockSpec((shape[0] // steps, shape[1]), lambda i: (i, 0))

    out_shapes = [((d, 4 * w_m), BF16), ((d, 3 * w_f), BF16), ((d, 2 * d), BF16), ((d, LANES), BF16),
                  ((1, 4 * w_m), F32), ((1, 3 * w_f), F32), ((1, 2 * d), F32), ((1, LANES), F32)]
    return pl.pallas_call(
        functools.partial(_wprep_kernel, w_m=w_m, w_f=w_f),
        grid=(steps,),
        in_specs=([pl.BlockSpec((n_in, d // steps), lambda i: (0, i))]
                  + [row_spec(s.shape, tiled=s.shape[0] > 1) for s in srcs[1:]]),
        out_specs=[row_spec(s, tiled=s[0] > 1) for s, _ in out_shapes],
        out_shape=[jax.ShapeDtypeStruct(s, dt) for s, dt in out_shapes],
        compiler_params=pltpu.CompilerParams(dimension_semantics=("arbitrary",), vmem_limit_bytes=vmem),
        name="wprep",
    )(*srcs)


def _inproj_kernel(x_ref, g_ref, wa_ref, ba_ref, wb_ref, bb_ref, wc_ref, bc_ref, wd_ref, bd_ref,
                   cw_ref, cb_ref, ng_ref, *rest,
                   w_m, w_f, n_chunk, m_split, q_scale, k_scale, chunk, tk, tiles_per_seq, n_cast):
    cast_srcs, rest = rest[:n_cast], rest[n_cast:]
    (ym_ref, zb_ref, fvt_ref, zc_ref, csplit_ref), rest = rest[:5], rest[5:]
    cast_dsts, rest = rest[:n_cast], rest[n_cast:]
    zs_sc, gc_sc, zqk_ref, vt_ref, og_ref, rows_ref, cols_ref, colprev_sc, ct_sc = rest
    tm = x_ref.shape[0]
    seq_start = (pl.program_id(0) % tiles_per_seq) == 0

    @pl.when(seq_start)
    def _():
        zs_sc[...] = jnp.zeros(zs_sc.shape, F32)
        gc_sc[0:2 * GATE_GROUP] = jnp.zeros((2 * GATE_GROUP, LANES), F32)
        gc_sc[2 * GATE_GROUP:] = jnp.full((GATE_GROUP, LANES), NEG_BIG, F32)
        colprev_sc[...] = jnp.zeros(colprev_sc.shape, F32)
        ct_sc[...] = jnp.zeros(ct_sc.shape, F32)

    rows = tm // m_split
    hb = [_rms(x_ref[r0:r0 + rows, :], g_ref[...]).astype(BF16) for r0 in range(0, tm, rows)]

    def proj(w_ref, b_ref, c0, c1):
        w = w_ref[:, c0:c1]
        parts = [jnp.dot(h, w, preferred_element_type=F32) for h in hb]
        return jnp.concatenate(parts, axis=0) + b_ref[:, c0:c1]

    def qk_chunk(ci):
        c0 = ci * n_chunk
        cs = slice(c0, c0 + n_chunk)
        z = proj(wa_ref, ba_ref, c0, c0 + n_chunk)
        prev = zs_sc[ci]
        zs_sc[ci] = z[tm - SUBLANES:tm]
        zg = z.reshape(tm // SUBLANES, SUBLANES, n_chunk)
        row = lax.broadcasted_iota(jnp.int32, zg.shape, 1)
        w = cw_ref[:, cs]
        y = zg * w[CONV_K - 1:CONV_K] + cb_ref[:, cs]
        for s in range(1, CONV_K):
            rz = pltpu.roll(zg, s, axis=1)
            before = jnp.concatenate([pltpu.roll(prev, s, axis=0)[None], rz[:-1]], axis=0)
            y = y + jnp.where(row < s, before, rz) * w[CONV_K - 1 - s:CONV_K - s]
        act = y * jax.nn.sigmoid(y)
        if c0 >= w_m:
            act = act * k_scale
        zqk_ref[:, cs] = act.reshape(tm, n_chunk).astype(BF16)

    def v_chunk(i):
        c0 = 2 * w_m + i * n_chunk
        z = proj(wa_ref, ba_ref, c0, c0 + n_chunk)
        for cc in range(tm // chunk):
            for f0 in range(0, n_chunk, LANES):
                blk = z[cc * chunk:(cc + 1) * chunk, f0:f0 + LANES]
                r0 = i * n_chunk + f0
                vt_ref[cc, r0:r0 + LANES, :] = blk.T.astype(BF16)

    def og_chunk(i):
        c0 = 3 * w_m + i * n_chunk
        z = proj(wa_ref, ba_ref, c0, c0 + n_chunk)
        og_ref[:, i * n_chunk:(i + 1) * n_chunk] = jax.nn.sigmoid(z).astype(BF16)

    def b_chunk(i):
        c0 = i * n_chunk
        z = proj(wb_ref, bb_ref, c0, c0 + n_chunk)
        if c0 < w_f:
            z = z * q_scale
        if c0 < 2 * w_f:
            zb_ref[:, c0:c0 + n_chunk] = z.astype(BF16)
        else:
            for jb in range(tm // tk):
                for f0 in range(0, n_chunk, LANES):
                    blk = z[jb * tk:(jb + 1) * tk, f0:f0 + LANES]
                    r0 = c0 - 2 * w_f + f0
                    fvt_ref[jb, r0:r0 + LANES, :] = blk.T.astype(BF16)

    def c_chunk(i):
        c0 = i * n_chunk
        z = proj(wc_ref, bc_ref, c0, c0 + n_chunk)
        zc_ref[:, c0:c0 + n_chunk] = jax.nn.sigmoid(z).astype(BF16)

    def d_chunk(_):
        zd = proj(wd_ref, bd_ref, 0, LANES)
        rows, cols, csplit, gc_sc[...] = _gate_tables(zd.T[:N_GATE_ROWS, :], gc_sc[...])
        rows_ref[...] = rows
        cols_ref[...] = cols
        csplit_ref[...] = csplit

    mlstm = _MlstmTile(zqk_ref, vt_ref, og_ref, rows_ref, cols_ref, colprev_sc, ng_ref, ym_ref,
                       ct_sc, w_m=w_m, chunk=chunk)
    order = [(d_chunk, 0)]
    order += [(qk_chunk, i) for i in range(2 * w_m // n_chunk)]
    order += [(v_chunk, i) for i in range(w_m // n_chunk)]
    order += [(og_chunk, i) for i in range(w_m // n_chunk)]
    fill = ([(c_chunk, i) for i in range(wc_ref.shape[1] // n_chunk)]
            + [(b_chunk, i) for i in range(3 * w_f // n_chunk)])
    for c in range(tm // chunk):
        order += [(mlstm.state_stage, c), fill.pop(0), (mlstm.output_stage, c), fill.pop(0)]
    order += fill
    for fn, i in order:
        fn(i)
    colprev_sc[...] = cols_ref[tm - SUBLANES:tm, :]
    for src, dst in zip(cast_srcs, cast_dsts):
        dst[...] = src[...].astype(BF16)


def _scan_lanes(x, op, fill):
    n = x.shape[-1]
    pos = lax.broadcasted_iota(jnp.int32, x.shape, 1)
    s = 1
    while s < n:
        shifted = pltpu.roll(x, s, axis=1)
        x = op(x, jnp.where(pos >= s, shifted, fill))
        s *= 2
    return x


def _gate_tables(z, carry):
    width = z.shape[1]
    i8 = z[0:GATE_GROUP]
    cum = (_scan_lanes(_log_sigmoid(z[GATE_GROUP:3 * GATE_GROUP]), jnp.add, 0.0)
           + carry[0:2 * GATE_GROUP, 0:1])
    f8 = cum[0:GATE_GROUP]
    g8 = i8 - f8
    cmax = jnp.maximum(_scan_lanes(g8, jnp.maximum, NEG_BIG), carry[2 * GATE_GROUP:, 0:1])
    new_carry = jnp.concatenate(
        [jnp.broadcast_to(cum[:, width - 1:width], (2 * GATE_GROUP, LANES)),
         jnp.broadcast_to(cmax[:, width - 1:width], (GATE_GROUP, LANES))], axis=0)
    m8 = jnp.maximum(cmax, 0.0)
    en8 = jnp.exp(-(f8 + m8))
    cf8 = cum[GATE_GROUP:2 * GATE_GROUP] * LOG2E
    g8, m8 = g8 * LOG2E, m8 * LOG2E
    rows = jnp.concatenate([g8, m8, en8], axis=0)
    stack = jnp.concatenate(
        [g8, m8, jnp.zeros((LANES - 2 * GATE_GROUP, width), F32)], axis=0)
    cols = stack.T
    hi = cf8.astype(BF16).astype(F32)
    r1 = cf8 - hi
    lo = r1.astype(BF16).astype(F32)
    lo2 = (r1 - lo).astype(BF16).astype(F32)
    ones = jnp.where(lax.broadcasted_iota(jnp.int32, (GATE_GROUP, width), 0) == 0, 1.0, 0.0)
    split = jnp.concatenate(
        [hi, lo, lo2, ones, jnp.zeros((LANES - 4 * GATE_GROUP, width), F32)], axis=0)
    return rows, cols, split.T.astype(BF16), new_carry


class _MlstmTile:
    def __init__(self, zqk_ref, vt_ref, og_ref, rows_ref, cols_ref, colprev_ref, ng_ref, ym_ref,
                 ct_sc, *, w_m, chunk):
        self.refs = (zqk_ref, vt_ref, og_ref, rows_ref, cols_ref, colprev_ref, ng_ref, ym_ref, ct_sc)
        self.w_m, self.chunk, self.dh, self.hc = w_m, chunk, w_m // H_M, chunk // 2
        hc = self.hc
        self.causal = (lax.broadcasted_iota(jnp.int32, (hc, hc), 0)
                       <= lax.broadcasted_iota(jnp.int32, (hc, hc), 1))
        self.ones_rows = jnp.where(
            lax.broadcasted_iota(jnp.int32, (BF16_ROWS, chunk), 0) == 0, 1.0, 0.0).astype(BF16)
        self.pending = {}

    def state_stage(self, c):
        zqk_ref, vt_ref, _, rows_ref, cols_ref, colprev_ref, _, _, ct_sc = self.refs
        w_m, chunk, dh = self.w_m, self.chunk, self.dh
        r0 = c * chunk
        tc = slice(r0, r0 + chunk)
        last = cols_ref[r0 + chunk - 1:r0 + chunk, :]
        prev = cols_ref[r0 - 1:r0, :] if c else colprev_ref[SUBLANES - 1:SUBLANES, :]
        first, vas = [], []
        for h in range(H_M):
            qc = zqk_ref[tc, h * dh:(h + 1) * dh]
            kc = zqk_ref[tc, w_m + h * dh:w_m + (h + 1) * dh]
            lhs = jnp.concatenate([kc, ct_sc[h].astype(BF16)], axis=0)
            first.append(lax.dot_general(lhs, qc, NT_DIMS, preferred_element_type=F32))
            vas.append(jnp.concatenate([vt_ref[c, h * dh:(h + 1) * dh, :], self.ones_rows], axis=0))
        for h in range(H_M):
            kc = zqk_ref[tc, w_m + h * dh:w_m + (h + 1) * dh]
            g_row = rows_ref[h:h + 1, tc]
            m_e = last[:, GATE_GROUP + h:GATE_GROUP + h + 1]
            m_p = prev[:, GATE_GROUP + h:GATE_GROUP + h + 1]
            vaw = (vas[h].astype(F32) * jnp.exp2(g_row - m_e)).astype(BF16)
            ct_sc[h] = jnp.exp2(m_p - m_e) * ct_sc[h] + jnp.dot(vaw, kc, preferred_element_type=F32)
        self.pending[c] = (first, vas, prev)

    def output_stage(self, c):
        _, _, og_ref, rows_ref, cols_ref, _, ng_ref, ym_ref, _ = self.refs
        chunk, dh, hc, causal = self.chunk, self.dh, self.hc, self.causal
        first, vas, prev = self.pending.pop(c)
        r0 = c * chunk
        tc = slice(r0, r0 + chunk)
        colsc = cols_ref[tc, :]
        for h in range(H_M):
            ch = slice(h * dh, (h + 1) * dh)
            g_col = colsc[:, h:h + 1]
            m_row = rows_ref[GATE_GROUP + h:GATE_GROUP + h + 1, tc]
            en_row = rows_ref[2 * GATE_GROUP + h:2 * GATE_GROUP + h + 1, tc]
            m_p = prev[:, GATE_GROUP + h:GATE_GROUP + h + 1]
            sk = first[h]
            d00 = jnp.where(causal, jnp.exp2(g_col[0:hc] - m_row[:, 0:hc]), 0.0)
            d01 = jnp.exp2(g_col[0:hc] - m_row[:, hc:chunk])
            d11 = jnp.where(causal, jnp.exp2(g_col[hc:chunk] - m_row[:, hc:chunk]), 0.0)
            top = jnp.concatenate([sk[0:hc, 0:hc] * d00, sk[0:hc, hc:chunk] * d01], axis=1)
            bot = jnp.concatenate([jnp.zeros((hc, hc), F32), sk[hc:chunk, hc:chunk] * d11], axis=1)
            sqk = jnp.concatenate([top, bot], axis=0).astype(BF16)
            nd = (jnp.exp2(m_p - m_row) * first[h][chunk:]
                  + jnp.dot(vas[h], sqk, preferred_element_type=F32))
            den = nd[dh:dh + 1]
            ht = nd[0:dh] * (1.0 / jnp.maximum(jnp.abs(den), en_row))
            hn = ht * lax.rsqrt(jnp.mean(ht * ht, axis=0, keepdims=True) + EPS)
            og = og_ref[tc, ch].astype(F32)
            ym_ref[tc, ch] = (hn.T * ng_ref[:, ch] * og).astype(BF16)


def _fox_kernel(q_ref, k_ref, fvt_ref, csq_ref, csk_ref, qsel_ref, ksel_ref, yf_ref,
                kaug_sc, qaug_sc, m_sc, acc_sc, *, tq, tk):
    qi = pl.program_id(1)

    seq = k_ref.shape[0]
    dhp = LANES
    half = dhp // 2
    v_rows = half + BF16_ROWS

    def own_lanes(head, rows):
        lane = lax.broadcasted_iota(jnp.int32, (rows, dhp), 1)
        return (lane >= half) if head % 2 else (lane < half)

    def augment(x_ref, cs_ref, sel_ref, dst_sc, rows):
        cs = cs_ref[...]
        for p in range(H_F // 2):
            xp = x_ref[:, p * dhp:(p + 1) * dhp]
            bias = jnp.dot(cs, sel_ref[p], preferred_element_type=F32).astype(BF16)
            for head in (2 * p, 2 * p + 1):
                b0 = (head % 2) * dhp
                dst_sc[head] = jnp.where(own_lanes(head, rows), xp, bias[:, b0:b0 + dhp])

    @pl.when(qi == 0)
    def _():
        augment(k_ref, csk_ref, ksel_ref, kaug_sc, seq)

    augment(q_ref, csq_ref, qsel_ref, qaug_sc, tq)
    m_sc[...] = jnp.full(m_sc.shape, NEG_BIG, F32)
    acc_sc[...] = jnp.zeros(acc_sc.shape, F32)

    ratio = tq // tk
    ones_rows = jnp.where(
        lax.broadcasted_iota(jnp.int32, (BF16_ROWS, tk), 0) == 0, 1.0, 0.0).astype(BF16)

    def vaug(head, j):
        return jnp.concatenate([fvt_ref[j, head * half:(head + 1) * half, :], ones_rows], axis=0)

    def step(j, diag):
        q0 = 0 if diag is None else diag * tk
        nq = tq - q0
        k0 = pl.multiple_of(j * tk, tk)
        sts = [lax.dot_general(kaug_sc[head, pl.ds(k0, tk), :], qaug_sc[head, q0:tq, :], NT_DIMS,
                               preferred_element_type=F32) for head in range(H_F)]
        if diag is not None:
            causal = (lax.broadcasted_iota(jnp.int32, (tk, nq), 0)
                      <= lax.broadcasted_iota(jnp.int32, (tk, nq), 1))
        for head in range(H_F):
            st = sts[head]
            if diag is not None:
                st = jnp.where(causal, st, NEG_BIG)
            m = m_sc[head, :, q0:tq]
            m_new = jnp.maximum(m, jnp.max(st, axis=0, keepdims=True))
            alpha = jnp.exp2(m - m_new)
            pt = jnp.exp2(st - m_new).astype(BF16)
            m_sc[head, :, q0:tq] = m_new
            acc_sc[head, :, q0:tq] = (alpha * acc_sc[head, :, q0:tq]
                                      + jnp.dot(vaug(head, j), pt, preferred_element_type=F32))

    def loop_body(jj, carry):
        for r in range(ratio):
            step(jj * ratio + r, None)
        return carry

    lax.fori_loop(0, qi, loop_body, 0)
    for diag in range(ratio):
        step(qi * ratio + diag, diag)
    for p in range(H_F // 2):
        outs = []
        for head in (2 * p, 2 * p + 1):
            acc = acc_sc[head]
            outs.append(acc[0:half] * (1.0 / acc[half:half + 1]))
        yf_ref[:, p * dhp:(p + 1) * dhp] = jnp.concatenate(outs, axis=0).T.astype(BF16)


def _post_kernel(x_ref, ym_ref, yf_ref, zc_ref, wbm_ref, wbf_ref, wo_ref, g2_ref,
                 wg_ref, wu_ref, wd_ref, gfin_ref, o_ref, *, tf):
    d = x_ref.shape[1]
    d_ff = wg_ref.shape[1]
    bm = jnp.dot(ym_ref[...], wbm_ref[...], preferred_element_type=F32)
    bf = jnp.dot(yf_ref[...], wbf_ref[...], preferred_element_type=F32)
    mix = zc_ref[:, 0:d].astype(F32) * bm + zc_ref[:, d:2 * d].astype(F32) * bf
    x1 = x_ref[...] + jnp.dot(mix.astype(BF16), wo_ref[...], preferred_element_type=F32)
    h2 = _rms(x1, g2_ref[...]).astype(BF16)
    acc = jnp.zeros(x1.shape, F32)
    for f0 in range(0, d_ff, tf):
        g = jnp.dot(h2, wg_ref[:, f0:f0 + tf], preferred_element_type=F32)
        u = jnp.dot(h2, wu_ref[:, f0:f0 + tf], preferred_element_type=F32)
        act = (g * jax.nn.sigmoid(g) * u).astype(BF16)
        acc = acc + jnp.dot(act, wd_ref[f0:f0 + tf, :], preferred_element_type=F32)
    o_ref[...] = _rms(x1 + acc, gfin_ref[...])


def _bias_selectors():
    ones_lane = 3 * GATE_GROUP
    qsel = np.zeros((H_F // 2, LANES, 2 * LANES), np.float32)
    ksel = np.zeros((H_F // 2, LANES, 2 * LANES), np.float32)
    for h in range(H_F):
        p0 = (h % 2) * LANES + (LANES // 2 if h % 2 == 0 else 0)
        for c in range(3):
            qsel[h // 2, GATE_GROUP * c + h, p0 + c] = 1.0
            qsel[h // 2, ones_lane, p0 + 3 + c] = 1.0
            ksel[h // 2, ones_lane, p0 + c] = 1.0
            ksel[h // 2, GATE_GROUP * c + h, p0 + 3 + c] = -1.0
    return jnp.asarray(qsel, BF16), jnp.asarray(ksel, BF16)


def _layer(x2d, batch, seq, p, cfg):
    t, d = x2d.shape
    w_m = p["w_br_mlstm"].shape[0]
    w_f = p["w_br_fox"].shape[0]
    dh_m = w_m // H_M
    dh_f = w_f // H_F
    vmem = cfg["vmem_limit"]

    wa, wb, wc, wd, ba, bb, bc, bd = _prep_in_weights(p, vmem)

    tm = cfg["tm_in"]
    chunk = cfg["chunk"]
    tq, tk = cfg["tq"], cfg["tk"]
    n_steps = t // tm
    tps = seq // tm
    assert seq % tm == 0 and tm % chunk == 0 and tm % tk == 0

    later = [p["w_br_mlstm"], p["w_br_fox"], p["w_out"], p["w_gate"], p["w_up"], p["w_down"]]

    def slice_spec(rows, cols):
        hold = 1
        while (rows * hold) % (n_steps * BF16_ROWS):
            hold *= 2
        assert hold <= n_steps
        return pl.BlockSpec((rows * hold // n_steps, cols), lambda i: (i // hold, 0))

    later_specs = [slice_spec(*w.shape) for w in later]
    outs = pl.pallas_call(
        functools.partial(_inproj_kernel, w_m=w_m, w_f=w_f, n_chunk=cfg["n_chunk"],
                          m_split=cfg["m_split"],
                          q_scale=dh_f ** -0.5 * LOG2E, k_scale=dh_m ** -0.5, chunk=chunk,
                          tk=tk, tiles_per_seq=tps, n_cast=len(later)),
        grid=(n_steps,),
        in_specs=[pl.BlockSpec((tm, d), lambda i: (i, 0)), _const_spec((1, d)),
                  _const_spec(wa.shape), _const_spec(ba.shape),
                  _const_spec(wb.shape), _const_spec(bb.shape),
                  _const_spec(wc.shape), _const_spec(bc.shape),
                  _const_spec(wd.shape), _const_spec(bd.shape),
                  _const_spec((CONV_K, 2 * w_m)), _const_spec((1, 2 * w_m)),
                  _const_spec((1, w_m))] + later_specs,
        out_specs=[pl.BlockSpec((tm, w_m), lambda i: (i, 0)),
                   pl.BlockSpec((tm, 2 * w_f), lambda i: (i, 0)),
                   pl.BlockSpec((tm // tk, w_f, tk), lambda i: (i, 0, 0)),
                   pl.BlockSpec((tm, 2 * d), lambda i: (i, 0)),
                   pl.BlockSpec((tm, LANES), lambda i: (i, 0))] + later_specs,
        out_shape=[jax.ShapeDtypeStruct((t, w_m), BF16),
                   jax.ShapeDtypeStruct((t, 2 * w_f), BF16),
                   jax.ShapeDtypeStruct((t // tk, w_f, tk), BF16),
                   jax.ShapeDtypeStruct((t, 2 * d), BF16),
                   jax.ShapeDtypeStruct((t, LANES), BF16)]
        + [jax.ShapeDtypeStruct(w.shape, BF16) for w in later],
        scratch_shapes=[pltpu.VMEM((2 * w_m // cfg["n_chunk"], SUBLANES, cfg["n_chunk"]), F32),
                        pltpu.VMEM((N_GATE_ROWS, LANES), F32),
                        pltpu.VMEM((tm, 2 * w_m), BF16),
                        pltpu.VMEM((tm // chunk, w_m, chunk), BF16),
                        pltpu.VMEM((tm, w_m), BF16),
                        pltpu.VMEM((N_GATE_ROWS, tm), F32),
                        pltpu.VMEM((tm, LANES), F32),
                        pltpu.VMEM((SUBLANES, LANES), F32),
                        pltpu.VMEM((H_M, dh_m + BF16_ROWS, dh_m), F32)],
        compiler_params=pltpu.CompilerParams(dimension_semantics=("arbitrary",), vmem_limit_bytes=vmem),
        name="inproj",
    )(x2d, p["norm1_g"][None, :], wa, ba, wb, bb, wc, bc, wd, bd, p["conv_w"], p["conv_b"][None, :],
      p["mlstm_norm_g"][None, :], *later)
    ym, zb, fvt, zc, csplit, wbm, wbf, wo, wg, wu, wdn = outs

    assert tq % tk == 0 and seq % tq == 0
    nq = seq // tq
    qsel, ksel = _bias_selectors()
    v_rows = LANES // 2 + BF16_ROWS
    yf = pl.pallas_call(
        functools.partial(_fox_kernel, tq=tq, tk=tk),
        grid=(batch, nq),
        in_specs=[pl.BlockSpec((tq, w_f), lambda b, i: (b * nq + i, 0)),
                  pl.BlockSpec((seq, w_f), lambda b, i: (b, 1)),
                  pl.BlockSpec((seq // tk, w_f, tk), lambda b, i: (b, 0, 0)),
                  pl.BlockSpec((tq, LANES), lambda b, i: (b * nq + i, 0)),
                  pl.BlockSpec((seq, LANES), lambda b, i: (b, 0)),
                  _const_spec(qsel.shape), _const_spec(ksel.shape)],
        out_specs=pl.BlockSpec((tq, w_f), lambda b, i: (b * nq + i, 0)),
        out_shape=jax.ShapeDtypeStruct((t, w_f), BF16),
        scratch_shapes=[pltpu.VMEM((H_F, seq, LANES), BF16),
                        pltpu.VMEM((H_F, tq, LANES), BF16),
                        pltpu.VMEM((H_F, 1, tq), F32),
                        pltpu.VMEM((H_F, v_rows, tq), F32)],
        compiler_params=pltpu.CompilerParams(dimension_semantics=("arbitrary", "arbitrary"),
                                             vmem_limit_bytes=vmem),
        name="fox",
    )(zb, zb, fvt, csplit, csplit, qsel, ksel)

    tmp = cfg["tm_post"]
    return pl.pallas_call(
        functools.partial(_post_kernel, tf=cfg["tf"]),
        grid=(t // tmp,),
        in_specs=[pl.BlockSpec((tmp, d), lambda i: (i, 0)),
                  pl.BlockSpec((tmp, w_m), lambda i: (i, 0)),
                  pl.BlockSpec((tmp, w_f), lambda i: (i, 0)),
                  pl.BlockSpec((tmp, 2 * d), lambda i: (i, 0)),
                  _const_spec(wbm.shape), _const_spec(wbf.shape), _const_spec(wo.shape),
                  _const_spec((1, d)), _const_spec(wg.shape), _const_spec(wu.shape),
                  _const_spec(wdn.shape), _const_spec((1, d))],
        out_specs=pl.BlockSpec((tmp, d), lambda i: (i, 0)),
        out_shape=jax.ShapeDtypeStruct((t, d), F32),
        compiler_params=pltpu.CompilerParams(dimension_semantics=("parallel",), vmem_limit_bytes=vmem),
        name="post",
    )(x2d, ym, yf, zc, wbm, wbf, wo, p["norm2_g"][None, :], wg, wu, wdn, p["norm_f_g"][None, :])


def kernel(x, norm1_g, w_in, b_in, conv_w, conv_b, mlstm_norm_g, w_br_mlstm, w_br_fox, w_out,
           norm2_g, w_gate, w_up, w_down, norm_f_g):
    batch, seq, d = x.shape
    depth = w_in.shape[0]
    assert depth == 1, "the final norm is fused into the single layer's last call"
    cfg = _cfg(batch, seq, d, w_gate.shape[-1])
    p = dict(norm1_g=norm1_g[0], w_in=w_in[0], b_in=b_in[0], conv_w=conv_w[0], conv_b=conv_b[0],
             mlstm_norm_g=mlstm_norm_g[0], w_br_mlstm=w_br_mlstm[0], w_br_fox=w_br_fox[0],
             w_out=w_out[0], norm2_g=norm2_g[0], w_gate=w_gate[0], w_up=w_up[0], w_down=w_down[0],
             norm_f_g=norm_f_g)
    out = _layer(x.reshape(batch * seq, d), batch, seq, p, cfg)
    return out.reshape(batch, seq, d)
```

```python
import functools

import jax
import jax.numpy as jnp
import numpy as np
from jax import lax
from jax.experimental import pallas as pl
from jax.experimental.pallas import tpu as pltpu

EPS = 1e-6
H_M = 4
H_F = 8
CONV_K = 4

LANES = 128
SUBLANES = 8
BF16_ROWS = 16
F8_ROWS = 32
F8_SAFE = 256.0
TINY = 1e-30
GATE_GROUP = 8
N_GATE_ROWS = 3 * GATE_GROUP
V7X_VMEM_BYTES = 64 * 1024 * 1024
VMEM_COMPILER_RESERVE = 8 * 1024 * 1024
NEG_BIG = -1e30
LOG2E = 1.4426950408889634

F32 = jnp.float32
BF16 = jnp.bfloat16
F8 = jnp.float8_e4m3fn
NT_DIMS = (((1,), (1,)), ((), ()))


def _cfg(batch, seq, d_model, d_ff):
    return dict(
        tm_in=1024,
        n_chunk=256,
        m_split=8,
        chunk=256,
        tq=512,
        tk=256,
        tm_post=512,
        tf=256,
        vmem_limit=V7X_VMEM_BYTES - VMEM_COMPILER_RESERVE,
    )


def _const_spec(shape):
    nd = len(shape)
    return pl.BlockSpec(shape, lambda *_: (0,) * nd, pipeline_mode=pl.Buffered(1))


def _rms(x, g):
    return x * lax.rsqrt(jnp.mean(x * x, axis=-1, keepdims=True) + EPS) * g


def _log_sigmoid(x):
    return jnp.minimum(x, 0.0) - jnp.log1p(jnp.exp(-jnp.abs(x)))


def _wprep_kernel(wint_ref, bin_ref, wa_o, wb_o, wc_o, wd_o, ba_o, bb_o, bc_o, bd_o, *, w_m, w_f):
    o_mi = 4 * w_m
    o_mf = o_mi + H_M
    o_fq = o_mf + H_M
    o_ff = o_fq + 3 * w_f
    o_g = o_ff + H_F

    def gate_block(src):
        rows = src.shape[0]
        lane = lax.broadcasted_iota(jnp.int32, (rows, LANES), 1)
        blk_m = src[:, o_mi:o_mi + LANES]
        f0 = (o_ff // LANES) * LANES
        blk_f = src[:, f0:f0 + LANES]
        mi = jnp.where(lane < H_M, blk_m, 0.0)
        mf = jnp.where((lane >= GATE_GROUP) & (lane < GATE_GROUP + H_M),
                       pltpu.roll(blk_m, GATE_GROUP - H_M, axis=1), 0.0)
        ff = jnp.where((lane >= 2 * GATE_GROUP) & (lane < 2 * GATE_GROUP + H_F),
                       pltpu.roll(blk_f, 2 * GATE_GROUP - (o_ff - f0), axis=1), 0.0)
        return mi + mf + ff

    slab = wint_ref[...]
    wa_o[...] = slab[0:o_mi].T.astype(BF16)
    wb_o[...] = slab[o_fq:o_ff].T.astype(BF16)
    wc_o[...] = slab[o_g:].T.astype(BF16)
    g_m = slab[o_mi:o_mi + GATE_GROUP]
    row = lax.broadcasted_iota(jnp.int32, g_m.shape, 0)
    gates = jnp.concatenate(
        [jnp.where(row < H_M, g_m, 0.0),
         jnp.where(row < H_M, pltpu.roll(g_m, GATE_GROUP - H_M, axis=0), 0.0),
         slab[o_ff:o_ff + GATE_GROUP],
         jnp.zeros((LANES - 3 * GATE_GROUP, slab.shape[1]), F32)], axis=0)
    wd_o[...] = gates.T.astype(BF16)
    b = bin_ref[...]
    ba_o[...] = b[:, 0:o_mi]
    bb_o[...] = b[:, o_fq:o_ff]
    bc_o[...] = b[:, o_g:]
    bd_o[...] = gate_block(b)


def _prep_in_weights(p, vmem):
    d, n_in = p["w_in"].shape
    w_m = p["w_br_mlstm"].shape[0]
    w_f = p["w_br_fox"].shape[0]
    steps = 8
    o_ff = 4 * w_m + 2 * H_M + 3 * w_f
    assert (4 * w_m) % LANES == 0 and 2 * H_M <= GATE_GROUP + H_M <= LANES
    assert o_ff % LANES <= 2 * GATE_GROUP and o_ff % LANES + H_F <= LANES
    assert n_in == o_ff + H_F + 2 * d
    srcs = [p["w_in"].T, p["b_in"][None, :]]

    def row_spec(shape, tiled=True):
        if not tiled:
            return pl.BlockSpec(shape, lambda i: (0, 0))
        assert shape[0] % (steps * BF16_ROWS) == 0
        return pl.BlockSpec((shape[0] // steps, shape[1]), lambda i: (i, 0))

    out_shapes = [((d, 4 * w_m), BF16), ((d, 3 * w_f), BF16), ((d, 2 * d), BF16), ((d, LANES), BF16),
                  ((1, 4 * w_m), F32), ((1, 3 * w_f), F32), ((1, 2 * d), F32), ((1, LANES), F32)]
    return pl.pallas_call(
        functools.partial(_wprep_kernel, w_m=w_m, w_f=w_f),
        grid=(steps,),
        in_specs=([pl.BlockSpec((n_in, d // steps), lambda i: (0, i))]
                  + [row_spec(s.shape, tiled=s.shape[0] > 1) for s in srcs[1:]]),
        out_specs=[row_spec(s, tiled=s[0] > 1) for s, _ in out_shapes],
        out_shape=[jax.ShapeDtypeStruct(s, dt) for s, dt in out_shapes],
        compiler_params=pltpu.CompilerParams(dimension_semantics=("arbitrary",), vmem_limit_bytes=vmem),
        name="wprep",
    )(*srcs)


def _inproj_kernel(x_ref, g_ref, wa_ref, ba_ref, wb_ref, bb_ref, wc_ref, bc_ref, wd_ref, bd_ref,
                   cw_ref, cb_ref, ng_ref, *rest,
                   w_m, w_f, n_chunk, m_split, q_scale, k_scale, chunk, tk, tiles_per_seq, n_cast):
    cast_srcs, rest = rest[:n_cast], rest[n_cast:]
    (ym_ref, zb_ref, fvt_ref, zc_ref, csplit_ref), rest = rest[:5], rest[5:]
    cast_dsts, rest = rest[:n_cast], rest[n_cast:]
    zs_sc, gc_sc, zqk_ref, vt_ref, og_ref, rows_ref, cols_ref, colprev_sc, ct_sc = rest
    tm = x_ref.shape[0]
    seq_start = (pl.program_id(0) % tiles_per_seq) == 0

    @pl.when(seq_start)
    def _():
        zs_sc[...] = jnp.zeros(zs_sc.shape, F32)
        gc_sc[0:2 * GATE_GROUP] = jnp.zeros((2 * GATE_GROUP, LANES), F32)
        gc_sc[2 * GATE_GROUP:] = jnp.full((GATE_GROUP, LANES), NEG_BIG, F32)
        colprev_sc[...] = jnp.zeros(colprev_sc.shape, F32)
        ct_sc[...] = jnp.zeros(ct_sc.shape, F32)

    rows = tm // m_split
    hb = [_rms(x_ref[r0:r0 + rows, :], g_ref[...]).astype(BF16) for r0 in range(0, tm, rows)]

    def proj(w_ref, b_ref, c0, c1):
        w = w_ref[:, c0:c1]
        parts = [jnp.dot(h, w, preferred_element_type=F32) for h in hb]
        return jnp.concatenate(parts, axis=0) + b_ref[:, c0:c1]

    def qk_chunk(ci):
        c0 = ci * n_chunk
        cs = slice(c0, c0 + n_chunk)
        z = proj(wa_ref, ba_ref, c0, c0 + n_chunk)
        prev = zs_sc[ci]
        zs_sc[ci] = z[tm - SUBLANES:tm]
        zg = z.reshape(tm // SUBLANES, SUBLANES, n_chunk)
        row = lax.broadcasted_iota(jnp.int32, zg.shape, 1)
        w = cw_ref[:, cs]
        y = zg * w[CONV_K - 1:CONV_K] + cb_ref[:, cs]
        for s in range(1, CONV_K):
            rz = pltpu.roll(zg, s, axis=1)
            before = jnp.concatenate([pltpu.roll(prev, s, axis=0)[None], rz[:-1]], axis=0)
            y = y + jnp.where(row < s, before, rz) * w[CONV_K - 1 - s:CONV_K - s]
        act = y * jax.nn.sigmoid(y)
        if c0 >= w_m:
            act = act * k_scale
        zqk_ref[:, cs] = act.reshape(tm, n_chunk).astype(BF16)

    def v_chunk(i):
        c0 = 2 * w_m + i * n_chunk
        z = proj(wa_ref, ba_ref, c0, c0 + n_chunk)
        for cc in range(tm // chunk):
            for f0 in range(0, n_chunk, LANES):
                blk = z[cc * chunk:(cc + 1) * chunk, f0:f0 + LANES]
                r0 = i * n_chunk + f0
                vt_ref[cc, r0:r0 + LANES, :] = blk.T.astype(BF16)

    def og_chunk(i):
        c0 = 3 * w_m + i * n_chunk
        z = proj(wa_ref, ba_ref, c0, c0 + n_chunk)
        og_ref[:, i * n_chunk:(i + 1) * n_chunk] = jax.nn.sigmoid(z).astype(BF16)

    def b_chunk(i):
        c0 = i * n_chunk
        z = proj(wb_ref, bb_ref, c0, c0 + n_chunk)
        if c0 < w_f:
            z = z * q_scale
        if c0 < 2 * w_f:
            zb_ref[:, c0:c0 + n_chunk] = z.astype(BF16)
        else:
            for jb in range(tm // tk):
                for f0 in range(0, n_chunk, LANES):
                    blk = z[jb * tk:(jb + 1) * tk, f0:f0 + LANES]
                    r0 = c0 - 2 * w_f + f0
                    fvt_ref[jb, r0:r0 + LANES, :] = blk.T.astype(BF16)

    def c_chunk(i):
        c0 = i * n_chunk
        z = proj(wc_ref, bc_ref, c0, c0 + n_chunk)
        zc_ref[:, c0:c0 + n_chunk] = jax.nn.sigmoid(z).astype(BF16)

    def d_chunk(_):
        zd = proj(wd_ref, bd_ref, 0, LANES)
        rows, cols, csplit, gc_sc[...] = _gate_tables(zd.T[:N_GATE_ROWS, :], gc_sc[...])
        rows_ref[...] = rows
        cols_ref[...] = cols
        csplit_ref[...] = csplit

    mlstm = _MlstmTile(zqk_ref, vt_ref, og_ref, rows_ref, cols_ref, colprev_sc, ng_ref, ym_ref,
                       ct_sc, w_m=w_m, chunk=chunk)
    order = [(d_chunk, 0)]
    order += [(qk_chunk, i) for i in range(2 * w_m // n_chunk)]
    order += [(v_chunk, i) for i in range(w_m // n_chunk)]
    order += [(og_chunk, i) for i in range(w_m // n_chunk)]
    fill = ([(c_chunk, i) for i in range(wc_ref.shape[1] // n_chunk)]
            + [(b_chunk, i) for i in range(3 * w_f // n_chunk)])
    for c in range(tm // chunk):
        order += [(mlstm.state_stage, c), fill.pop(0), (mlstm.output_stage, c), fill.pop(0)]
    order += fill
    for fn, i in order:
        fn(i)
    colprev_sc[...] = cols_ref[tm - SUBLANES:tm, :]
    for src, dst in zip(cast_srcs, cast_dsts):
        dst[...] = src[...].astype(BF16)


def _scan_lanes(x, op, fill):
    n = x.shape[-1]
    pos = lax.broadcasted_iota(jnp.int32, x.shape, 1)
    s = 1
    while s < n:
        shifted = pltpu.roll(x, s, axis=1)
        x = op(x, jnp.where(pos >= s, shifted, fill))
        s *= 2
    return x


def _gate_tables(z, carry):
    width = z.shape[1]
    i8 = z[0:GATE_GROUP]
    cum = (_scan_lanes(_log_sigmoid(z[GATE_GROUP:3 * GATE_GROUP]), jnp.add, 0.0)
           + carry[0:2 * GATE_GROUP, 0:1])
    f8 = cum[0:GATE_GROUP]
    g8 = i8 - f8
    cmax = jnp.maximum(_scan_lanes(g8, jnp.maximum, NEG_BIG), carry[2 * GATE_GROUP:, 0:1])
    new_carry = jnp.concatenate(
        [jnp.broadcast_to(cum[:, width - 1:width], (2 * GATE_GROUP, LANES)),
         jnp.broadcast_to(cmax[:, width - 1:width], (GATE_GROUP, LANES))], axis=0)
    m8 = jnp.maximum(cmax, 0.0)
    en8 = jnp.exp(-(f8 + m8))
    cf8 = cum[GATE_GROUP:2 * GATE_GROUP] * LOG2E
    g8, m8 = g8 * LOG2E, m8 * LOG2E
    rows = jnp.concatenate([g8, m8, en8], axis=0)
    stack = jnp.concatenate(
        [g8, m8, jnp.zeros((LANES - 2 * GATE_GROUP, width), F32)], axis=0)
    cols = stack.T
    hi = cf8.astype(BF16).astype(F32)
    r1 = cf8 - hi
    lo = r1.astype(BF16).astype(F32)
    lo2 = (r1 - lo).astype(BF16).astype(F32)
    ones = jnp.where(lax.broadcasted_iota(jnp.int32, (GATE_GROUP, width), 0) == 0, 1.0, 0.0)
    split = jnp.concatenate(
        [hi, lo, lo2, ones, jnp.zeros((LANES - 4 * GATE_GROUP, width), F32)], axis=0)
    return rows, cols, split.T.astype(BF16), new_carry


class _MlstmTile:
    def __init__(self, zqk_ref, vt_ref, og_ref, rows_ref, cols_ref, colprev_ref, ng_ref, ym_ref,
                 ct_sc, *, w_m, chunk):
        self.refs = (zqk_ref, vt_ref, og_ref, rows_ref, cols_ref, colprev_ref, ng_ref, ym_ref, ct_sc)
        self.w_m, self.chunk, self.dh, self.hc = w_m, chunk, w_m // H_M, chunk // 2
        hc = self.hc
        self.causal = (lax.broadcasted_iota(jnp.int32, (hc, hc), 0)
                       <= lax.broadcasted_iota(jnp.int32, (hc, hc), 1))
        self.ones_rows = jnp.where(
            lax.broadcasted_iota(jnp.int32, (BF16_ROWS, chunk), 0) == 0, 1.0, 0.0).astype(BF16)
        self.pending = {}

    def state_stage(self, c):
        zqk_ref, vt_ref, _, rows_ref, cols_ref, colprev_ref, _, _, ct_sc = self.refs
        w_m, chunk, dh = self.w_m, self.chunk, self.dh
        r0 = c * chunk
        tc = slice(r0, r0 + chunk)
        last = cols_ref[r0 + chunk - 1:r0 + chunk, :]
        prev = cols_ref[r0 - 1:r0, :] if c else colprev_ref[SUBLANES - 1:SUBLANES, :]
        first, vas = [], []
        for h in range(H_M):
            qc = zqk_ref[tc, h * dh:(h + 1) * dh]
            kc = zqk_ref[tc, w_m + h * dh:w_m + (h + 1) * dh]
            lhs = jnp.concatenate([kc, ct_sc[h].astype(BF16)], axis=0)
            first.append(lax.dot_general(lhs, qc, NT_DIMS, preferred_element_type=F32))
            vas.append(jnp.concatenate([vt_ref[c, h * dh:(h + 1) * dh, :], self.ones_rows], axis=0))
        for h in range(H_M):
            kc = zqk_ref[tc, w_m + h * dh:w_m + (h + 1) * dh]
            g_row = rows_ref[h:h + 1, tc]
            m_e = last[:, GATE_GROUP + h:GATE_GROUP + h + 1]
            m_p = prev[:, GATE_GROUP + h:GATE_GROUP + h + 1]
            vaw = (vas[h].astype(F32) * jnp.exp2(g_row - m_e)).astype(BF16)
            ct_sc[h] = jnp.exp2(m_p - m_e) * ct_sc[h] + jnp.dot(vaw, kc, preferred_element_type=F32)
        self.pending[c] = (first, vas, prev)

    def output_stage(self, c):
        _, _, og_ref, rows_ref, cols_ref, _, ng_ref, ym_ref, _ = self.refs
        chunk, dh, hc, causal = self.chunk, self.dh, self.hc, self.causal
        first, vas, prev = self.pending.pop(c)
        r0 = c * chunk
        tc = slice(r0, r0 + chunk)
        colsc = cols_ref[tc, :]
        for h in range(H_M):
            ch = slice(h * dh, (h + 1) * dh)
            g_col = colsc[:, h:h + 1]
            m_row = rows_ref[GATE_GROUP + h:GATE_GROUP + h + 1, tc]
            en_row = rows_ref[2 * GATE_GROUP + h:2 * GATE_GROUP + h + 1, tc]
            m_p = prev[:, GATE_GROUP + h:GATE_GROUP + h + 1]
            sk = first[h]
            d00 = jnp.where(causal, jnp.exp2(g_col[0:hc] - m_row[:, 0:hc]), 0.0)
            d01 = jnp.exp2(g_col[0:hc] - m_row[:, hc:chunk])
            d11 = jnp.where(causal, jnp.exp2(g_col[hc:chunk] - m_row[:, hc:chunk]), 0.0)
            top = jnp.concatenate([sk[0:hc, 0:hc] * d00, sk[0:hc, hc:chunk] * d01], axis=1)
            bot = jnp.concatenate([jnp.zeros((hc, hc), F32), sk[hc:chunk, hc:chunk] * d11], axis=1)
            sqk = jnp.concatenate([top, bot], axis=0).astype(BF16)
            nd = (jnp.exp2(m_p - m_row) * first[h][chunk:]
                  + jnp.dot(vas[h], sqk, preferred_element_type=F32))
            den = nd[dh:dh + 1]
            ht = nd[0:dh] * (1.0 / jnp.maximum(jnp.abs(den), en_row))
            hn = ht * lax.rsqrt(jnp.mean(ht * ht, axis=0, keepdims=True) + EPS)
            og = og_ref[tc, ch].astype(F32)
            ym_ref[tc, ch] = (hn.T * ng_ref[:, ch] * og).astype(BF16)


def _fox_kernel(q_ref, k_ref, fvt_ref, csq_ref, csk_ref, qsel_ref, ksel_ref, yf_ref,
                kaug_sc, qaug_sc, m_sc, acc_sc, *, tq, tk):
    qi = pl.program_id(1)

    seq = k_ref.shape[0]
    dhp = LANES
    half = dhp // 2
    v_rows = half + F8_ROWS

    def own_lanes(head, rows):
        lane = lax.broadcasted_iota(jnp.int32, (rows, dhp), 1)
        return (lane >= half) if head % 2 else (lane < half)

    def augment(x_ref, cs_ref, sel_ref, dst_sc, rows):
        cs = cs_ref[...]
        for p in range(H_F // 2):
            xp = x_ref[:, p * dhp:(p + 1) * dhp]
            bias = jnp.dot(cs, sel_ref[p], preferred_element_type=F32).astype(BF16)
            for head in (2 * p, 2 * p + 1):
                b0 = (head % 2) * dhp
                dst_sc[head] = jnp.where(own_lanes(head, rows), xp, bias[:, b0:b0 + dhp])

    @pl.when(qi == 0)
    def _():
        augment(k_ref, csk_ref, ksel_ref, kaug_sc, seq)

    augment(q_ref, csq_ref, qsel_ref, qaug_sc, tq)
    m_sc[...] = jnp.full(m_sc.shape, NEG_BIG, F32)
    acc_sc[...] = jnp.zeros(acc_sc.shape, F32)

    ratio = tq // tk
    ones_rows = jnp.where(
        lax.broadcasted_iota(jnp.int32, (F8_ROWS, tk), 0) == 0, 1.0, 0.0).astype(F8)

    def vaug(head, j):
        vt = fvt_ref[j, head * half:(head + 1) * half, :].astype(F32)
        amax = jnp.max(jnp.max(jnp.abs(vt), axis=1, keepdims=True), axis=0, keepdims=True)
        scale = jnp.maximum(amax, TINY) * (1.0 / F8_SAFE)
        v8 = (vt * (1.0 / scale)).astype(F8)
        return jnp.concatenate([v8, ones_rows], axis=0), scale

    def step(j, diag):
        q0 = 0 if diag is None else diag * tk
        nq = tq - q0
        k0 = pl.multiple_of(j * tk, tk)
        sts = [lax.dot_general(kaug_sc[head, pl.ds(k0, tk), :], qaug_sc[head, q0:tq, :], NT_DIMS,
                               preferred_element_type=F32) for head in range(H_F)]
        if diag is not None:
            causal = (lax.broadcasted_iota(jnp.int32, (tk, nq), 0)
                      <= lax.broadcasted_iota(jnp.int32, (tk, nq), 1))
        for head in range(H_F):
            st = sts[head]
            if diag is not None:
                st = jnp.where(causal, st, NEG_BIG)
            m = m_sc[head, :, q0:tq]
            m_new = jnp.maximum(m, jnp.max(st, axis=0, keepdims=True))
            alpha = jnp.exp2(m - m_new)
            pt = jnp.exp2(st - m_new).astype(F8)
            m_sc[head, :, q0:tq] = m_new
            va, vscale = vaug(head, j)
            pv = jnp.dot(va, pt, preferred_element_type=F32)
            pv = jnp.concatenate([pv[0:half] * vscale, pv[half:]], axis=0)
            acc_sc[head, :, q0:tq] = alpha * acc_sc[head, :, q0:tq] + pv

    def loop_body(jj, carry):
        for r in range(ratio):
            step(jj * ratio + r, None)
        return carry

    lax.fori_loop(0, qi, loop_body, 0)
    for diag in range(ratio):
        step(qi * ratio + diag, diag)
    for p in range(H_F // 2):
        outs = []
        for head in (2 * p, 2 * p + 1):
            acc = acc_sc[head]
            outs.append(acc[0:half] * (1.0 / acc[half:half + 1]))
        yf_ref[:, p * dhp:(p + 1) * dhp] = jnp.concatenate(outs, axis=0).T.astype(BF16)


def _post_kernel(x_ref, ym_ref, yf_ref, zc_ref, wbm_ref, wbf_ref, wo_ref, g2_ref,
                 wg_ref, wu_ref, wd_ref, gfin_ref, o_ref, *, tf):
    d = x_ref.shape[1]
    d_ff = wg_ref.shape[1]
    bm = jnp.dot(ym_ref[...], wbm_ref[...], preferred_element_type=F32)
    bf = jnp.dot(yf_ref[...], wbf_ref[...], preferred_element_type=F32)
    mix = zc_ref[:, 0:d].astype(F32) * bm + zc_ref[:, d:2 * d].astype(F32) * bf
    x1 = x_ref[...] + jnp.dot(mix.astype(BF16), wo_ref[...], preferred_element_type=F32)
    h2 = _rms(x1, g2_ref[...]).astype(BF16)
    acc = jnp.zeros(x1.shape, F32)
    for f0 in range(0, d_ff, tf):
        g = jnp.dot(h2, wg_ref[:, f0:f0 + tf], preferred_element_type=F32)
        u = jnp.dot(h2, wu_ref[:, f0:f0 + tf], preferred_element_type=F32)
        act = (g * jax.nn.sigmoid(g) * u).astype(BF16)
        acc = acc + jnp.dot(act, wd_ref[f0:f0 + tf, :], preferred_element_type=F32)
    o_ref[...] = _rms(x1 + acc, gfin_ref[...])


def _bias_selectors():
    ones_lane = 3 * GATE_GROUP
    qsel = np.zeros((H_F // 2, LANES, 2 * LANES), np.float32)
    ksel = np.zeros((H_F // 2, LANES, 2 * LANES), np.float32)
    for h in range(H_F):
        p0 = (h % 2) * LANES + (LANES // 2 if h % 2 == 0 else 0)
        for c in range(3):
            qsel[h // 2, GATE_GROUP * c + h, p0 + c] = 1.0
            qsel[h // 2, ones_lane, p0 + 3 + c] = 1.0
            ksel[h // 2, ones_lane, p0 + c] = 1.0
            ksel[h // 2, GATE_GROUP * c + h, p0 + 3 + c] = -1.0
    return jnp.asarray(qsel, BF16), jnp.asarray(ksel, BF16)


def _layer(x2d, batch, seq, p, cfg):
    t, d = x2d.shape
    w_m = p["w_br_mlstm"].shape[0]
    w_f = p["w_br_fox"].shape[0]
    dh_m = w_m // H_M
    dh_f = w_f // H_F
    vmem = cfg["vmem_limit"]

    wa, wb, wc, wd, ba, bb, bc, bd = _prep_in_weights(p, vmem)

    tm = cfg["tm_in"]
    chunk = cfg["chunk"]
    tq, tk = cfg["tq"], cfg["tk"]
    n_steps = t // tm
    tps = seq // tm
    assert seq % tm == 0 and tm % chunk == 0 and tm % tk == 0

    later = [p["w_br_mlstm"], p["w_br_fox"], p["w_out"], p["w_gate"], p["w_up"], p["w_down"]]

    def slice_spec(rows, cols):
        hold = 1
        while (rows * hold) % (n_steps * BF16_ROWS):
            hold *= 2
        assert hold <= n_steps
        return pl.BlockSpec((rows * hold // n_steps, cols), lambda i: (i // hold, 0))

    later_specs = [slice_spec(*w.shape) for w in later]
    outs = pl.pallas_call(
        functools.partial(_inproj_kernel, w_m=w_m, w_f=w_f, n_chunk=cfg["n_chunk"],
                          m_split=cfg["m_split"],
                          q_scale=dh_f ** -0.5 * LOG2E, k_scale=dh_m ** -0.5, chunk=chunk,
                          tk=tk, tiles_per_seq=tps, n_cast=len(later)),
        grid=(n_steps,),
        in_specs=[pl.BlockSpec((tm, d), lambda i: (i, 0)), _const_spec((1, d)),
                  _const_spec(wa.shape), _const_spec(ba.shape),
                  _const_spec(wb.shape), _const_spec(bb.shape),
                  _const_spec(wc.shape), _const_spec(bc.shape),
                  _const_spec(wd.shape), _const_spec(bd.shape),
                  _const_spec((CONV_K, 2 * w_m)), _const_spec((1, 2 * w_m)),
                  _const_spec((1, w_m))] + later_specs,
        out_specs=[pl.BlockSpec((tm, w_m), lambda i: (i, 0)),
                   pl.BlockSpec((tm, 2 * w_f), lambda i: (i, 0)),
                   pl.BlockSpec((tm // tk, w_f, tk), lambda i: (i, 0, 0)),
                   pl.BlockSpec((tm, 2 * d), lambda i: (i, 0)),
                   pl.BlockSpec((tm, LANES), lambda i: (i, 0))] + later_specs,
        out_shape=[jax.ShapeDtypeStruct((t, w_m), BF16),
                   jax.ShapeDtypeStruct((t, 2 * w_f), BF16),
                   jax.ShapeDtypeStruct((t // tk, w_f, tk), BF16),
                   jax.ShapeDtypeStruct((t, 2 * d), BF16),
                   jax.ShapeDtypeStruct((t, LANES), BF16)]
        + [jax.ShapeDtypeStruct(w.shape, BF16) for w in later],
        scratch_shapes=[pltpu.VMEM((2 * w_m // cfg["n_chunk"], SUBLANES, cfg["n_chunk"]), F32),
                        pltpu.VMEM((N_GATE_ROWS, LANES), F32),
                        pltpu.VMEM((tm, 2 * w_m), BF16),
                        pltpu.VMEM((tm // chunk, w_m, chunk), BF16),
                        pltpu.VMEM((tm, w_m), BF16),
                        pltpu.VMEM((N_GATE_ROWS, tm), F32),
                        pltpu.VMEM((tm, LANES), F32),
                        pltpu.VMEM((SUBLANES, LANES), F32),
                        pltpu.VMEM((H_M, dh_m + BF16_ROWS, dh_m), F32)],
        compiler_params=pltpu.CompilerParams(dimension_semantics=("arbitrary",), vmem_limit_bytes=vmem),
        name="inproj",
    )(x2d, p["norm1_g"][None, :], wa, ba, wb, bb, wc, bc, wd, bd, p["conv_w"], p["conv_b"][None, :],
      p["mlstm_norm_g"][None, :], *later)
    ym, zb, fvt, zc, csplit, wbm, wbf, wo, wg, wu, wdn = outs

    assert tq % tk == 0 and seq % tq == 0
    nq = seq // tq
    qsel, ksel = _bias_selectors()
    v_rows = LANES // 2 + F8_ROWS
    yf = pl.pallas_call(
        functools.partial(_fox_kernel, tq=tq, tk=tk),
        grid=(batch, nq),
        in_specs=[pl.BlockSpec((tq, w_f), lambda b, i: (b * nq + i, 0)),
                  pl.BlockSpec((seq, w_f), lambda b, i: (b, 1)),
                  pl.BlockSpec((seq // tk, w_f, tk), lambda b, i: (b, 0, 0)),
                  pl.BlockSpec((tq, LANES), lambda b, i: (b * nq + i, 0)),
                  pl.BlockSpec((seq, LANES), lambda b, i: (b, 0)),
                  _const_spec(qsel.shape), _const_spec(ksel.shape)],
        out_specs=pl.BlockSpec((tq, w_f), lambda b, i: (b * nq + i, 0)),
        out_shape=jax.ShapeDtypeStruct((t, w_f), BF16),
        scratch_shapes=[pltpu.VMEM((H_F, seq, LANES), BF16),
                        pltpu.VMEM((H_F, tq, LANES), BF16),
                        pltpu.VMEM((H_F, 1, tq), F32),
                        pltpu.VMEM((H_F, v_rows, tq), F32)],
        compiler_params=pltpu.CompilerParams(dimension_semantics=("arbitrary", "arbitrary"),
                                             vmem_limit_bytes=vmem),
        name="fox",
    )(zb, zb, fvt, csplit, csplit, qsel, ksel)

    tmp = cfg["tm_post"]
    return pl.pallas_call(
        functools.partial(_post_kernel, tf=cfg["tf"]),
        grid=(t // tmp,),
        in_specs=[pl.BlockSpec((tmp, d), lambda i: (i, 0)),
                  pl.BlockSpec((tmp, w_m), lambda i: (i, 0)),
                  pl.BlockSpec((tmp, w_f), lambda i: (i, 0)),
                  pl.BlockSpec((tmp, 2 * d), lambda i: (i, 0)),
                  _const_spec(wbm.shape), _const_spec(wbf.shape), _const_spec(wo.shape),
                  _const_spec((1, d)), _const_spec(wg.shape), _const_spec(wu.shape),
                  _const_spec(wdn.shape), _const_spec((1, d))],
        out_specs=pl.BlockSpec((tmp, d), lambda i: (i, 0)),
        out_shape=jax.ShapeDtypeStruct((t, d), F32),
        compiler_params=pltpu.CompilerParams(dimension_semantics=("parallel",), vmem_limit_bytes=vmem),
        name="post",
    )(x2d, ym, yf, zc, wbm, wbf, wo, p["norm2_g"][None, :], wg, wu, wdn, p["norm_f_g"][None, :])


def kernel(x, norm1_g, w_in, b_in, conv_w, conv_b, mlstm_norm_g, w_br_mlstm, w_br_fox, w_out,
           norm2_g, w_gate, w_up, w_down, norm_f_g):
    batch, seq, d = x.shape
    depth = w_in.shape[0]
    assert depth == 1, "the final norm is fused into the single layer's last call"
    cfg = _cfg(batch, seq, d, w_gate.shape[-1])
    p = dict(norm1_g=norm1_g[0], w_in=w_in[0], b_in=b_in[0], conv_w=conv_w[0], conv_b=conv_b[0],
             mlstm_norm_g=mlstm_norm_g[0], w_br_mlstm=w_br_mlstm[0], w_br_fox=w_br_fox[0],
             w_out=w_out[0], norm2_g=norm2_g[0], w_gate=w_gate[0], w_up=w_up[0], w_down=w_down[0],
             norm_f_g=norm_f_g)
    out = _layer(x.reshape(batch * seq, d), batch, seq, p, cfg)
    return out.reshape(batch, seq, d)
```

```python
import functools

import jax
import jax.numpy as jnp
import numpy as np
from jax import lax
from jax.experimental import pallas as pl
from jax.experimental.pallas import tpu as pltpu

EPS = 1e-6
H_M = 4
H_F = 8
CONV_K = 4

LANES = 128
SUBLANES = 8
BF16_ROWS = 16
F8_ROWS = 32
F8_SAFE = 256.0
TINY = 1e-30
GATE_GROUP = 8
N_GATE_ROWS = 3 * GATE_GROUP
V7X_VMEM_BYTES = 64 * 1024 * 1024
VMEM_COMPILER_RESERVE = 8 * 1024 * 1024
NEG_BIG = -1e30
LOG2E = 1.4426950408889634

F32 = jnp.float32
BF16 = jnp.bfloat16
F8 = jnp.float8_e4m3fn
NT_DIMS = (((1,), (1,)), ((), ()))


def _cfg(batch, seq, d_model, d_ff):
    return dict(
        tm_in=1024,
        n_chunk=256,
        m_split=8,
        chunk=256,
        tq=512,
        tk=256,
        tm_post=512,
        tf=256,
        vmem_limit=V7X_VMEM_BYTES - VMEM_COMPILER_RESERVE,
    )


def _const_spec(shape):
    nd = len(shape)
    return pl.BlockSpec(shape, lambda *_: (0,) * nd, pipeline_mode=pl.Buffered(1))


def _rms(x, g):
    return x * lax.rsqrt(jnp.mean(x * x, axis=-1, keepdims=True) + EPS) * g


def _log_sigmoid(x):
    return jnp.minimum(x, 0.0) - jnp.log1p(jnp.exp(-jnp.abs(x)))


def _wprep_kernel(wint_ref, bin_ref, wa_o, wb_o, wc_o, wd_o, ba_o, bb_o, bc_o, bd_o, *, w_m, w_f):
    o_mi = 4 * w_m
    o_mf = o_mi + H_M
    o_fq = o_mf + H_M
    o_ff = o_fq + 3 * w_f
    o_g = o_ff + H_F

    def gate_block(src):
        rows = src.shape[0]
        lane = lax.broadcasted_iota(jnp.int32, (rows, LANES), 1)
        blk_m = src[:, o_mi:o_mi + LANES]
        f0 = (o_ff // LANES) * LANES
        blk_f = src[:, f0:f0 + LANES]
        mi = jnp.where(lane < H_M, blk_m, 0.0)
        mf = jnp.where((lane >= GATE_GROUP) & (lane < GATE_GROUP + H_M),
                       pltpu.roll(blk_m, GATE_GROUP - H_M, axis=1), 0.0)
        ff = jnp.where((lane >= 2 * GATE_GROUP) & (lane < 2 * GATE_GROUP + H_F),
                       pltpu.roll(blk_f, 2 * GATE_GROUP - (o_ff - f0), axis=1), 0.0)
        return mi + mf + ff

    slab = wint_ref[...]
    wa_o[...] = slab[0:o_mi].T.astype(BF16)
    wb_o[...] = slab[o_fq:o_ff].T.astype(BF16)
    wc_o[...] = slab[o_g:].T.astype(BF16)
    g_m = slab[o_mi:o_mi + GATE_GROUP]
    row = lax.broadcasted_iota(jnp.int32, g_m.shape, 0)
    gates = jnp.concatenate(
        [jnp.where(row < H_M, g_m, 0.0),
         jnp.where(row < H_M, pltpu.roll(g_m, GATE_GROUP - H_M, axis=0), 0.0),
         slab[o_ff:o_ff + GATE_GROUP],
         jnp.zeros((LANES - 3 * GATE_GROUP, slab.shape[1]), F32)], axis=0)
    wd_o[...] = gates.T.astype(BF16)
    b = bin_ref[...]
    ba_o[...] = b[:, 0:o_mi]
    bb_o[...] = b[:, o_fq:o_ff]
    bc_o[...] = b[:, o_g:]
    bd_o[...] = gate_block(b)


def _prep_in_weights(p, vmem):
    d, n_in = p["w_in"].shape
    w_m = p["w_br_mlstm"].shape[0]
    w_f = p["w_br_fox"].shape[0]
    steps = 8
    o_ff = 4 * w_m + 2 * H_M + 3 * w_f
    assert (4 * w_m) % LANES == 0 and 2 * H_M <= GATE_GROUP + H_M <= LANES
    assert o_ff % LANES <= 2 * GATE_GROUP and o_ff % LANES + H_F <= LANES
    assert n_in == o_ff + H_F + 2 * d
    srcs = [p["w_in"].T, p["b_in"][None, :]]

    def row_spec(shape, tiled=True):
        if not tiled:
            return pl.BlockSpec(shape, lambda i: (0, 0))
        assert shape[0] % (steps * BF16_ROWS) == 0
        return pl.BlockSpec((shape[0] // steps, shape[1]), lambda i: (i, 0))

    out_shapes = [((d, 4 * w_m), BF16), ((d, 3 * w_f), BF16), ((d, 2 * d), BF16), ((d, LANES), BF16),
                  ((1, 4 * w_m), F32), ((1, 3 * w_f), F32), ((1, 2 * d), F32), ((1, LANES), F32)]
    return pl.pallas_call(
        functools.partial(_wprep_kernel, w_m=w_m, w_f=w_f),
        grid=(steps,),
        in_specs=([pl.BlockSpec((n_in, d // steps), lambda i: (0, i))]
                  + [row_spec(s.shape, tiled=s.shape[0] > 1) for s in srcs[1:]]),
        out_specs=[row_spec(s, tiled=s[0] > 1) for s, _ in out_shapes],
        out_shape=[jax.ShapeDtypeStruct(s, dt) for s, dt in out_shapes],
        compiler_params=pltpu.CompilerParams(dimension_semantics=("arbitrary",), vmem_limit_bytes=vmem),
        name="wprep",
    )(*srcs)


def _inproj_kernel(x_ref, g_ref, wa_ref, ba_ref, wb_ref, bb_ref, wc_ref, bc_ref, wd_ref, bd_ref,
                   cw_ref, cb_ref, ng_ref, *rest,
                   w_m, w_f, n_chunk, m_split, q_scale, k_scale, chunk, tk, tiles_per_seq, n_cast):
    cast_srcs, rest = rest[:n_cast], rest[n_cast:]
    (ym_ref, zb_ref, fvt_ref, zc_ref, csplit_ref, vscale_ref), rest = rest[:6], rest[6:]
    cast_dsts, rest = rest[:n_cast], rest[n_cast:]
    zs_sc, gc_sc, zqk_ref, vt_ref, og_ref, rows_ref, cols_ref, colprev_sc, ct_sc = rest
    tm = x_ref.shape[0]
    seq_start = (pl.program_id(0) % tiles_per_seq) == 0

    @pl.when(seq_start)
    def _():
        zs_sc[...] = jnp.zeros(zs_sc.shape, F32)
        gc_sc[0:2 * GATE_GROUP] = jnp.zeros((2 * GATE_GROUP, LANES), F32)
        gc_sc[2 * GATE_GROUP:] = jnp.full((GATE_GROUP, LANES), NEG_BIG, F32)
        colprev_sc[...] = jnp.zeros(colprev_sc.shape, F32)
        ct_sc[...] = jnp.zeros(ct_sc.shape, F32)

    rows = tm // m_split
    hb = [_rms(x_ref[r0:r0 + rows, :], g_ref[...]).astype(BF16) for r0 in range(0, tm, rows)]

    def proj(w_ref, b_ref, c0, c1):
        w = w_ref[:, c0:c1]
        parts = [jnp.dot(h, w, preferred_element_type=F32) for h in hb]
        return jnp.concatenate(parts, axis=0) + b_ref[:, c0:c1]

    def qk_chunk(ci):
        c0 = ci * n_chunk
        cs = slice(c0, c0 + n_chunk)
        z = proj(wa_ref, ba_ref, c0, c0 + n_chunk)
        prev = zs_sc[ci]
        zs_sc[ci] = z[tm - SUBLANES:tm]
        zg = z.reshape(tm // SUBLANES, SUBLANES, n_chunk)
        row = lax.broadcasted_iota(jnp.int32, zg.shape, 1)
        w = cw_ref[:, cs]
        y = zg * w[CONV_K - 1:CONV_K] + cb_ref[:, cs]
        for s in range(1, CONV_K):
            rz = pltpu.roll(zg, s, axis=1)
            before = jnp.concatenate([pltpu.roll(prev, s, axis=0)[None], rz[:-1]], axis=0)
            y = y + jnp.where(row < s, before, rz) * w[CONV_K - 1 - s:CONV_K - s]
        act = y * jax.nn.sigmoid(y)
        if c0 >= w_m:
            act = act * k_scale
        zqk_ref[:, cs] = act.reshape(tm, n_chunk).astype(BF16)

    def v_chunk(i):
        c0 = 2 * w_m + i * n_chunk
        z = proj(wa_ref, ba_ref, c0, c0 + n_chunk)
        for cc in range(tm // chunk):
            for f0 in range(0, n_chunk, LANES):
                blk = z[cc * chunk:(cc + 1) * chunk, f0:f0 + LANES]
                r0 = i * n_chunk + f0
                vt_ref[cc, r0:r0 + LANES, :] = blk.T.astype(BF16)

    def og_chunk(i):
        c0 = 3 * w_m + i * n_chunk
        z = proj(wa_ref, ba_ref, c0, c0 + n_chunk)
        og_ref[:, i * n_chunk:(i + 1) * n_chunk] = jax.nn.sigmoid(z).astype(BF16)

    def b_chunk(i):
        c0 = i * n_chunk
        z = proj(wb_ref, bb_ref, c0, c0 + n_chunk)
        if c0 < w_f:
            z = z * q_scale
        if c0 < 2 * w_f:
            zb_ref[:, c0:c0 + n_chunk] = z.astype(BF16)
        else:
            for jb in range(tm // tk):
                for f0 in range(0, n_chunk, LANES):
                    blk = z[jb * tk:(jb + 1) * tk, f0:f0 + LANES]
                    r0 = c0 - 2 * w_f + f0
                    half = LANES // 2
                    invs = []
                    for hh in range(2):
                        sub = jnp.abs(blk[:, hh * half:(hh + 1) * half])
                        amax = jnp.max(jnp.max(sub, axis=1, keepdims=True), axis=0, keepdims=True)
                        scale = jnp.maximum(amax, TINY) * (1.0 / F8_SAFE)
                        head = r0 // half + hh
                        vscale_ref[jb, head:head + 1, :] = jnp.broadcast_to(scale, (1, LANES))
                        invs.append(1.0 / scale)
                    lane = lax.broadcasted_iota(jnp.int32, (1, LANES), 1)
                    blk = blk * jnp.where(lane < half, invs[0], invs[1])
                    fvt_ref[jb, r0:r0 + LANES, :] = blk.T.astype(F8)

    def c_chunk(i):
        c0 = i * n_chunk
        z = proj(wc_ref, bc_ref, c0, c0 + n_chunk)
        zc_ref[:, c0:c0 + n_chunk] = jax.nn.sigmoid(z).astype(BF16)

    def d_chunk(_):
        zd = proj(wd_ref, bd_ref, 0, LANES)
        rows, cols, csplit, gc_sc[...] = _gate_tables(zd.T[:N_GATE_ROWS, :], gc_sc[...])
        rows_ref[...] = rows
        cols_ref[...] = cols
        csplit_ref[...] = csplit

    mlstm = _MlstmTile(zqk_ref, vt_ref, og_ref, rows_ref, cols_ref, colprev_sc, ng_ref, ym_ref,
                       ct_sc, w_m=w_m, chunk=chunk)
    order = [(d_chunk, 0)]
    order += [(qk_chunk, i) for i in range(2 * w_m // n_chunk)]
    order += [(v_chunk, i) for i in range(w_m // n_chunk)]
    order += [(og_chunk, i) for i in range(w_m // n_chunk)]
    fill = ([(c_chunk, i) for i in range(wc_ref.shape[1] // n_chunk)]
            + [(b_chunk, i) for i in range(3 * w_f // n_chunk)])
    for c in range(tm // chunk):
        order += [(mlstm.state_stage, c), fill.pop(0), (mlstm.output_stage, c), fill.pop(0)]
    order += fill
    for fn, i in order:
        fn(i)
    colprev_sc[...] = cols_ref[tm - SUBLANES:tm, :]
    for src, dst in zip(cast_srcs, cast_dsts):
        dst[...] = src[...].astype(BF16)


def _scan_lanes(x, op, fill):
    n = x.shape[-1]
    pos = lax.broadcasted_iota(jnp.int32, x.shape, 1)
    s = 1
    while s < n:
        shifted = pltpu.roll(x, s, axis=1)
        x = op(x, jnp.where(pos >= s, shifted, fill))
        s *= 2
    return x


def _gate_tables(z, carry):
    width = z.shape[1]
    i8 = z[0:GATE_GROUP]
    cum = (_scan_lanes(_log_sigmoid(z[GATE_GROUP:3 * GATE_GROUP]), jnp.add, 0.0)
           + carry[0:2 * GATE_GROUP, 0:1])
    f8 = cum[0:GATE_GROUP]
    g8 = i8 - f8
    cmax = jnp.maximum(_scan_lanes(g8, jnp.maximum, NEG_BIG), carry[2 * GATE_GROUP:, 0:1])
    new_carry = jnp.concatenate(
        [jnp.broadcast_to(cum[:, width - 1:width], (2 * GATE_GROUP, LANES)),
         jnp.broadcast_to(cmax[:, width - 1:width], (GATE_GROUP, LANES))], axis=0)
    m8 = jnp.maximum(cmax, 0.0)
    en8 = jnp.exp(-(f8 + m8))
    cf8 = cum[GATE_GROUP:2 * GATE_GROUP] * LOG2E
    g8, m8 = g8 * LOG2E, m8 * LOG2E
    rows = jnp.concatenate([g8, m8, en8], axis=0)
    stack = jnp.concatenate(
        [g8, m8, jnp.zeros((LANES - 2 * GATE_GROUP, width), F32)], axis=0)
    cols = stack.T
    hi = cf8.astype(BF16).astype(F32)
    r1 = cf8 - hi
    lo = r1.astype(BF16).astype(F32)
    lo2 = (r1 - lo).astype(BF16).astype(F32)
    ones = jnp.where(lax.broadcasted_iota(jnp.int32, (GATE_GROUP, width), 0) == 0, 1.0, 0.0)
    split = jnp.concatenate(
        [hi, lo, lo2, ones, jnp.zeros((LANES - 4 * GATE_GROUP, width), F32)], axis=0)
    return rows, cols, split.T.astype(BF16), new_carry


class _MlstmTile:
    def __init__(self, zqk_ref, vt_ref, og_ref, rows_ref, cols_ref, colprev_ref, ng_ref, ym_ref,
                 ct_sc, *, w_m, chunk):
        self.refs = (zqk_ref, vt_ref, og_ref, rows_ref, cols_ref, colprev_ref, ng_ref, ym_ref, ct_sc)
        self.w_m, self.chunk, self.dh, self.hc = w_m, chunk, w_m // H_M, chunk // 2
        hc = self.hc
        self.causal = (lax.broadcasted_iota(jnp.int32, (hc, hc), 0)
                       <= lax.broadcasted_iota(jnp.int32, (hc, hc), 1))
        self.ones_rows = jnp.where(
            lax.broadcasted_iota(jnp.int32, (BF16_ROWS, chunk), 0) == 0, 1.0, 0.0).astype(BF16)
        self.pending = {}

    def state_stage(self, c):
        zqk_ref, vt_ref, _, rows_ref, cols_ref, colprev_ref, _, _, ct_sc = self.refs
        w_m, chunk, dh = self.w_m, self.chunk, self.dh
        r0 = c * chunk
        tc = slice(r0, r0 + chunk)
        last = cols_ref[r0 + chunk - 1:r0 + chunk, :]
        prev = cols_ref[r0 - 1:r0, :] if c else colprev_ref[SUBLANES - 1:SUBLANES, :]
        first, vas = [], []
        for h in range(H_M):
            qc = zqk_ref[tc, h * dh:(h + 1) * dh]
            kc = zqk_ref[tc, w_m + h * dh:w_m + (h + 1) * dh]
            lhs = jnp.concatenate([kc, ct_sc[h].astype(BF16)], axis=0)
            first.append(lax.dot_general(lhs, qc, NT_DIMS, preferred_element_type=F32))
            vas.append(jnp.concatenate([vt_ref[c, h * dh:(h + 1) * dh, :], self.ones_rows], axis=0))
        for h in range(H_M):
            kc = zqk_ref[tc, w_m + h * dh:w_m + (h + 1) * dh]
            g_row = rows_ref[h:h + 1, tc]
            m_e = last[:, GATE_GROUP + h:GATE_GROUP + h + 1]
            m_p = prev[:, GATE_GROUP + h:GATE_GROUP + h + 1]
            vaw = (vas[h].astype(F32) * jnp.exp2(g_row - m_e)).astype(BF16)
            ct_sc[h] = jnp.exp2(m_p - m_e) * ct_sc[h] + jnp.dot(vaw, kc, preferred_element_type=F32)
        self.pending[c] = (first, vas, prev)

    def output_stage(self, c):
        _, _, og_ref, rows_ref, cols_ref, _, ng_ref, ym_ref, _ = self.refs
        chunk, dh, hc, causal = self.chunk, self.dh, self.hc, self.causal
        first, vas, prev = self.pending.pop(c)
        r0 = c * chunk
        tc = slice(r0, r0 + chunk)
        colsc = cols_ref[tc, :]
        for h in range(H_M):
            ch = slice(h * dh, (h + 1) * dh)
            g_col = colsc[:, h:h + 1]
            m_row = rows_ref[GATE_GROUP + h:GATE_GROUP + h + 1, tc]
            en_row = rows_ref[2 * GATE_GROUP + h:2 * GATE_GROUP + h + 1, tc]
            m_p = prev[:, GATE_GROUP + h:GATE_GROUP + h + 1]
            sk = first[h]
            d00 = jnp.where(causal, jnp.exp2(g_col[0:hc] - m_row[:, 0:hc]), 0.0)
            d01 = jnp.exp2(g_col[0:hc] - m_row[:, hc:chunk])
            d11 = jnp.where(causal, jnp.exp2(g_col[hc:chunk] - m_row[:, hc:chunk]), 0.0)
            top = jnp.concatenate([sk[0:hc, 0:hc] * d00, sk[0:hc, hc:chunk] * d01], axis=1)
            bot = jnp.concatenate([jnp.zeros((hc, hc), F32), sk[hc:chunk, hc:chunk] * d11], axis=1)
            sqk = jnp.concatenate([top, bot], axis=0).astype(BF16)
            nd = (jnp.exp2(m_p - m_row) * first[h][chunk:]
                  + jnp.dot(vas[h], sqk, preferred_element_type=F32))
            den = nd[dh:dh + 1]
            ht = nd[0:dh] * (1.0 / jnp.maximum(jnp.abs(den), en_row))
            hn = ht * lax.rsqrt(jnp.mean(ht * ht, axis=0, keepdims=True) + EPS)
            og = og_ref[tc, ch].astype(F32)
            ym_ref[tc, ch] = (hn.T * ng_ref[:, ch] * og).astype(BF16)


def _fox_kernel(q_ref, k_ref, fvt_ref, vscale_ref, csq_ref, csk_ref, qsel_ref, ksel_ref, yf_ref,
                kaug_sc, qaug_sc, m_sc, acc_sc, *, tq, tk):
    qi = pl.program_id(1)

    seq = k_ref.shape[0]
    dhp = LANES
    half = dhp // 2
    v_rows = half + F8_ROWS

    def own_lanes(head, rows):
        lane = lax.broadcasted_iota(jnp.int32, (rows, dhp), 1)
        return (lane >= half) if head % 2 else (lane < half)

    def augment(x_ref, cs_ref, sel_ref, dst_sc, rows):
        cs = cs_ref[...]
        for p in range(H_F // 2):
            xp = x_ref[:, p * dhp:(p + 1) * dhp]
            bias = jnp.dot(cs, sel_ref[p], preferred_element_type=F32).astype(BF16)
            for head in (2 * p, 2 * p + 1):
                b0 = (head % 2) * dhp
                dst_sc[head] = jnp.where(own_lanes(head, rows), xp, bias[:, b0:b0 + dhp])

    @pl.when(qi == 0)
    def _():
        augment(k_ref, csk_ref, ksel_ref, kaug_sc, seq)

    augment(q_ref, csq_ref, qsel_ref, qaug_sc, tq)
    m_sc[...] = jnp.full(m_sc.shape, NEG_BIG, F32)
    acc_sc[...] = jnp.zeros(acc_sc.shape, F32)

    ratio = tq // tk
    ones_rows = jnp.where(
        lax.broadcasted_iota(jnp.int32, (F8_ROWS, tk), 0) == 0, 1.0, 0.0).astype(F8)

    def vaug(head, j):
        v8 = fvt_ref[j, head * half:(head + 1) * half, :]
        return jnp.concatenate([v8, ones_rows], axis=0), vscale_ref[j, head:head + 1, 0:1]

    def step(j, diag):
        q0 = 0 if diag is None else diag * tk
        nq = tq - q0
        k0 = pl.multiple_of(j * tk, tk)
        sts = [lax.dot_general(kaug_sc[head, pl.ds(k0, tk), :], qaug_sc[head, q0:tq, :], NT_DIMS,
                               preferred_element_type=F32) for head in range(H_F)]
        if diag is not None:
            causal = (lax.broadcasted_iota(jnp.int32, (tk, nq), 0)
                      <= lax.broadcasted_iota(jnp.int32, (tk, nq), 1))
        for head in range(H_F):
            st = sts[head]
            if diag is not None:
                st = jnp.where(causal, st, NEG_BIG)
            m = m_sc[head, :, q0:tq]
            m_new = jnp.maximum(m, jnp.max(st, axis=0, keepdims=True))
            alpha = jnp.exp2(m - m_new)
            pt = jnp.exp2(st - m_new).astype(F8)
            m_sc[head, :, q0:tq] = m_new
            va, vscale = vaug(head, j)
            pv = jnp.dot(va, pt, preferred_element_type=F32)
            pv = jnp.concatenate([pv[0:half] * vscale, pv[half:]], axis=0)
            acc_sc[head, :, q0:tq] = alpha * acc_sc[head, :, q0:tq] + pv

    def loop_body(jj, carry):
        for r in range(ratio):
            step(jj * ratio + r, None)
        return carry

    lax.fori_loop(0, qi, loop_body, 0)
    for diag in range(ratio):
        step(qi * ratio + diag, diag)
    for p in range(H_F // 2):
        outs = []
        for head in (2 * p, 2 * p + 1):
            acc = acc_sc[head]
            outs.append(acc[0:half] * (1.0 / acc[half:half + 1]))
        yf_ref[:, p * dhp:(p + 1) * dhp] = jnp.concatenate(outs, axis=0).T.astype(BF16)


def _post_kernel(x_ref, ym_ref, yf_ref, zc_ref, wbm_ref, wbf_ref, wo_ref, g2_ref,
                 wg_ref, wu_ref, wd_ref, gfin_ref, o_ref, *, tf):
    d = x_ref.shape[1]
    d_ff = wg_ref.shape[1]
    bm = jnp.dot(ym_ref[...], wbm_ref[...], preferred_element_type=F32)
    bf = jnp.dot(yf_ref[...], wbf_ref[...], preferred_element_type=F32)
    mix = zc_ref[:, 0:d].astype(F32) * bm + zc_ref[:, d:2 * d].astype(F32) * bf
    x1 = x_ref[...] + jnp.dot(mix.astype(BF16), wo_ref[...], preferred_element_type=F32)
    h2 = _rms(x1, g2_ref[...]).astype(BF16)
    acc = jnp.zeros(x1.shape, F32)
    for f0 in range(0, d_ff, tf):
        g = jnp.dot(h2, wg_ref[:, f0:f0 + tf], preferred_element_type=F32)
        u = jnp.dot(h2, wu_ref[:, f0:f0 + tf], preferred_element_type=F32)
        act = (g * jax.nn.sigmoid(g) * u).astype(BF16)
        acc = acc + jnp.dot(act, wd_ref[f0:f0 + tf, :], preferred_element_type=F32)
    o_ref[...] = _rms(x1 + acc, gfin_ref[...])


def _bias_selectors():
    ones_lane = 3 * GATE_GROUP
    qsel = np.zeros((H_F // 2, LANES, 2 * LANES), np.float32)
    ksel = np.zeros((H_F // 2, LANES, 2 * LANES), np.float32)
    for h in range(H_F):
        p0 = (h % 2) * LANES + (LANES // 2 if h % 2 == 0 else 0)
        for c in range(3):
            qsel[h // 2, GATE_GROUP * c + h, p0 + c] = 1.0
            qsel[h // 2, ones_lane, p0 + 3 + c] = 1.0
            ksel[h // 2, ones_lane, p0 + c] = 1.0
            ksel[h // 2, GATE_GROUP * c + h, p0 + 3 + c] = -1.0
    return jnp.asarray(qsel, BF16), jnp.asarray(ksel, BF16)


def _layer(x2d, batch, seq, p, cfg):
    t, d = x2d.shape
    w_m = p["w_br_mlstm"].shape[0]
    w_f = p["w_br_fox"].shape[0]
    dh_m = w_m // H_M
    dh_f = w_f // H_F
    vmem = cfg["vmem_limit"]

    wa, wb, wc, wd, ba, bb, bc, bd = _prep_in_weights(p, vmem)

    tm = cfg["tm_in"]
    chunk = cfg["chunk"]
    tq, tk = cfg["tq"], cfg["tk"]
    n_steps = t // tm
    tps = seq // tm
    assert seq % tm == 0 and tm % chunk == 0 and tm % tk == 0

    later = [p["w_br_mlstm"], p["w_br_fox"], p["w_out"], p["w_gate"], p["w_up"], p["w_down"]]

    def slice_spec(rows, cols):
        hold = 1
        while (rows * hold) % (n_steps * BF16_ROWS):
            hold *= 2
        assert hold <= n_steps
        return pl.BlockSpec((rows * hold // n_steps, cols), lambda i: (i // hold, 0))

    later_specs = [slice_spec(*w.shape) for w in later]
    outs = pl.pallas_call(
        functools.partial(_inproj_kernel, w_m=w_m, w_f=w_f, n_chunk=cfg["n_chunk"],
                          m_split=cfg["m_split"],
                          q_scale=dh_f ** -0.5 * LOG2E, k_scale=dh_m ** -0.5, chunk=chunk,
                          tk=tk, tiles_per_seq=tps, n_cast=len(later)),
        grid=(n_steps,),
        in_specs=[pl.BlockSpec((tm, d), lambda i: (i, 0)), _const_spec((1, d)),
                  _const_spec(wa.shape), _const_spec(ba.shape),
                  _const_spec(wb.shape), _const_spec(bb.shape),
                  _const_spec(wc.shape), _const_spec(bc.shape),
                  _const_spec(wd.shape), _const_spec(bd.shape),
                  _const_spec((CONV_K, 2 * w_m)), _const_spec((1, 2 * w_m)),
                  _const_spec((1, w_m))] + later_specs,
        out_specs=[pl.BlockSpec((tm, w_m), lambda i: (i, 0)),
                   pl.BlockSpec((tm, 2 * w_f), lambda i: (i, 0)),
                   pl.BlockSpec((tm // tk, w_f, tk), lambda i: (i, 0, 0)),
                   pl.BlockSpec((tm, 2 * d), lambda i: (i, 0)),
                   pl.BlockSpec((tm, LANES), lambda i: (i, 0)),
                   pl.BlockSpec((tm // tk, H_F, LANES), lambda i: (i, 0, 0))] + later_specs,
        out_shape=[jax.ShapeDtypeStruct((t, w_m), BF16),
                   jax.ShapeDtypeStruct((t, 2 * w_f), BF16),
                   jax.ShapeDtypeStruct((t // tk, w_f, tk), F8),
                   jax.ShapeDtypeStruct((t, 2 * d), BF16),
                   jax.ShapeDtypeStruct((t, LANES), BF16),
                   jax.ShapeDtypeStruct((t // tk, H_F, LANES), F32)]
        + [jax.ShapeDtypeStruct(w.shape, BF16) for w in later],
        scratch_shapes=[pltpu.VMEM((2 * w_m // cfg["n_chunk"], SUBLANES, cfg["n_chunk"]), F32),
                        pltpu.VMEM((N_GATE_ROWS, LANES), F32),
                        pltpu.VMEM((tm, 2 * w_m), BF16),
                        pltpu.VMEM((tm // chunk, w_m, chunk), BF16),
                        pltpu.VMEM((tm, w_m), BF16),
                        pltpu.VMEM((N_GATE_ROWS, tm), F32),
                        pltpu.VMEM((tm, LANES), F32),
                        pltpu.VMEM((SUBLANES, LANES), F32),
                        pltpu.VMEM((H_M, dh_m + BF16_ROWS, dh_m), F32)],
        compiler_params=pltpu.CompilerParams(dimension_semantics=("arbitrary",), vmem_limit_bytes=vmem),
        name="inproj",
    )(x2d, p["norm1_g"][None, :], wa, ba, wb, bb, wc, bc, wd, bd, p["conv_w"], p["conv_b"][None, :],
      p["mlstm_norm_g"][None, :], *later)
    ym, zb, fvt, zc, csplit, vscale, wbm, wbf, wo, wg, wu, wdn = outs

    assert tq % tk == 0 and seq % tq == 0
    nq = seq // tq
    qsel, ksel = _bias_selectors()
    v_rows = LANES // 2 + F8_ROWS
    yf = pl.pallas_call(
        functools.partial(_fox_kernel, tq=tq, tk=tk),
        grid=(batch, nq),
        in_specs=[pl.BlockSpec((tq, w_f), lambda b, i: (b * nq + i, 0)),
                  pl.BlockSpec((seq, w_f), lambda b, i: (b, 1)),
                  pl.BlockSpec((seq // tk, w_f, tk), lambda b, i: (b, 0, 0)),
                  pl.BlockSpec((seq // tk, H_F, LANES), lambda b, i: (b, 0, 0)),
                  pl.BlockSpec((tq, LANES), lambda b, i: (b * nq + i, 0)),
                  pl.BlockSpec((seq, LANES), lambda b, i: (b, 0)),
                  _const_spec(qsel.shape), _const_spec(ksel.shape)],
        out_specs=pl.BlockSpec((tq, w_f), lambda b, i: (b * nq + i, 0)),
        out_shape=jax.ShapeDtypeStruct((t, w_f), BF16),
        scratch_shapes=[pltpu.VMEM((H_F, seq, LANES), BF16),
                        pltpu.VMEM((H_F, tq, LANES), BF16),
                        pltpu.VMEM((H_F, 1, tq), F32),
                        pltpu.VMEM((H_F, v_rows, tq), F32)],
        compiler_params=pltpu.CompilerParams(dimension_semantics=("arbitrary", "arbitrary"),
                                             vmem_limit_bytes=vmem),
        name="fox",
    )(zb, zb, fvt, vscale, csplit, csplit, qsel, ksel)

    tmp = cfg["tm_post"]
    return pl.pallas_call(
        functools.partial(_post_kernel, tf=cfg["tf"]),
        grid=(t // tmp,),
        in_specs=[pl.BlockSpec((tmp, d), lambda i: (i, 0)),
                  pl.BlockSpec((tmp, w_m), lambda i: (i, 0)),
                  pl.BlockSpec((tmp, w_f), lambda i: (i, 0)),
                  pl.BlockSpec((tmp, 2 * d), lambda i: (i, 0)),
                  _const_spec(wbm.shape), _const_spec(wbf.shape), _const_spec(wo.shape),
                  _const_spec((1, d)), _const_spec(wg.shape), _const_spec(wu.shape),
                  _const_spec(wdn.shape), _const_spec((1, d))],
        out_specs=pl.BlockSpec((tmp, d), lambda i: (i, 0)),
        out_shape=jax.ShapeDtypeStruct((t, d), F32),
        compiler_params=pltpu.CompilerParams(dimension_semantics=("parallel",), vmem_limit_bytes=vmem),
        name="post",
    )(x2d, ym, yf, zc, wbm, wbf, wo, p["norm2_g"][None, :], wg, wu, wdn, p["norm_f_g"][None, :])


def kernel(x, norm1_g, w_in, b_in, conv_w, conv_b, mlstm_norm_g, w_br_mlstm, w_br_fox, w_out,
           norm2_g, w_gate, w_up, w_down, norm_f_g):
    batch, seq, d = x.shape
    depth = w_in.shape[0]
    assert depth == 1, "the final norm is fused into the single layer's last call"
    cfg = _cfg(batch, seq, d, w_gate.shape[-1])
    p = dict(norm1_g=norm1_g[0], w_in=w_in[0], b_in=b_in[0], conv_w=conv_w[0], conv_b=conv_b[0],
             mlstm_norm_g=mlstm_norm_g[0], w_br_mlstm=w_br_mlstm[0], w_br_fox=w_br_fox[0],
             w_out=w_out[0], norm2_g=norm2_g[0], w_gate=w_gate[0], w_up=w_up[0], w_down=w_down[0],
             norm_f_g=norm_f_g)
    out = _layer(x.reshape(batch * seq, d), batch, seq, p, cfg)
    return out.reshape(batch, seq, d)
```
